```python
import jax
import jax.numpy as jnp
from jax import lax
import numpy as np

D_MODEL = 1024
BATCH = 2
SEQ = 8192
DEPTH = 2

GRID_W = 64
CTX_LEN = 256
NORM_EPS = 1e-6
N_BRANCH = 3
BRANCH_W = D_MODEL // 2
NA_HEAD_DIM = 64
NA_HEADS = BRANCH_W // NA_HEAD_DIM
NA_WIN_ROWS = 8
NA_WIN_COLS = 16
GLA_HEADS = 4
GLA_DV = BRANCH_W // GLA_HEADS
GLA_DK = GLA_DV // 2
GLA_GATE_RANK = 16
GLA_GATE_TAU = 16.0
RET_HEADS = 4
RET_DV = BRANCH_W // RET_HEADS
RET_DK = RET_DV // 2
RET_DECAY_FWD = 5.0
RET_DECAY_BWD = 5.5
CHUNK = 64
ROPE_BASE = 10000.0
IN_SPLITS = (
    NA_HEADS * NA_HEAD_DIM, NA_HEADS * NA_HEAD_DIM, NA_HEADS * NA_HEAD_DIM,
    GLA_HEADS * GLA_DK, GLA_HEADS * GLA_DK, GLA_HEADS * GLA_DV, 2 * GLA_GATE_RANK,
    RET_HEADS * RET_DK, RET_HEADS * RET_DK, RET_HEADS * RET_DV,
    N_BRANCH * BRANCH_W, N_BRANCH * D_MODEL,
)
N_IN = sum(IN_SPLITS)

kernel_name = 'hybrid_na_gla_retention_prefix_dit'


def rms_norm(x, w):
    xf = x.astype(jnp.float32)
    y = xf * lax.rsqrt(jnp.mean(xf * xf, axis=-1, keepdims=True) + NORM_EPS)
    return (y * w).astype(x.dtype)


def split_heads(t, n_heads):
    b, l, _ = t.shape
    return t.reshape(b, l, n_heads, -1).transpose(0, 2, 1, 3)


def merge_heads(t):
    b, h, l, d = t.shape
    return t.transpose(0, 2, 1, 3).reshape(b, l, h * d)


def head_rms_merge(o, gain):
    b, h, t, dv = o.shape
    return merge_heads(rms_norm(o, gain.reshape(h, 1, dv)))


def split_cols(t):
    idx = np.cumsum(IN_SPLITS)[:-1].tolist()
    return jnp.split(t, idx, axis=-1)


def axial_rope(t):
    n, dh = t.shape[2], t.shape[3]
    half = dh // 2
    quarter = half // 2
    pos = jnp.arange(n)
    row = (pos // GRID_W).astype(jnp.float32)
    col = (pos % GRID_W).astype(jnp.float32)
    inv = ROPE_BASE ** (-jnp.arange(quarter, dtype=jnp.float32) / quarter)

    def rot(u, p):
        ang = p[:, None] * inv[None, :]
        cos, sin = jnp.cos(ang), jnp.sin(ang)
        u1, u2 = u[..., :quarter], u[..., quarter:]
        return jnp.concatenate([u1 * cos - u2 * sin, u1 * sin + u2 * cos], axis=-1)

    return jnp.concatenate([rot(t[..., :half], row), rot(t[..., half:], col)], axis=-1).astype(t.dtype)


def chunk_scan(q, k, v, log_a, s0):
    b, h, t, _ = q.shape
    dv = v.shape[-1]
    n = t // CHUNK

    def to_chunks(u):
        return u.astype(jnp.float32).reshape(b, h, n, CHUNK, u.shape[-1]).transpose(2, 0, 1, 3, 4)

    lower = jnp.tril(jnp.ones((CHUNK, CHUNK), dtype=bool))[:, :, None]

    def step(s, inp):
        qc, kc, vc, lc = inp
        cum = jnp.cumsum(lc, axis=2)
        diff = cum[:, :, :, None, :] - cum[:, :, None, :, :]
        decay = jnp.exp(jnp.where(lower, diff, -jnp.inf))
        att = jnp.sum(qc[:, :, :, None, :] * kc[:, :, None, :, :] * decay, axis=-1)
        o = jnp.einsum('bhts,bhsv->bhtv', att, vc) + jnp.einsum('bhtd,bhdv->bhtv', qc * jnp.exp(cum), s)
        last = cum[:, :, -1:, :]
        s_new = jnp.exp(last[:, :, 0, :, None]) * s + jnp.einsum('bhsd,bhsv->bhdv', kc * jnp.exp(last - cum), vc)
        return s_new, o

    s_fin, o = lax.scan(step, s0, (to_chunks(q), to_chunks(k), to_chunks(v), to_chunks(log_a)))
    o = o.transpose(1, 2, 0, 3, 4).reshape(b, h, t, dv).astype(v.dtype)
    return o, s_fin


def bidir_scan(ctx_in, lat_in, with_ctx):
    qc, kc, vc, lfc, lbc = ctx_in
    ql, kl, vl, lfl, lbl = lat_in
    b, h, _, dk = qc.shape
    s0 = jnp.zeros((b, h, dk, vc.shape[-1]), jnp.float32)
    rev = lambda u: jnp.flip(u, axis=2)
    oc_f, s_f = chunk_scan(qc, kc, vc, lfc, s0)
    oc_b, s_b = chunk_scan(rev(qc), rev(kc), rev(vc), rev(lbc), s0)
    ol_f, _ = chunk_scan(ql, kl, vl, lfl, s_f)
    ol_b, _ = chunk_scan(rev(ql), rev(kl), rev(vl), rev(lbl), s_b)
    o_lat = ol_f + rev(ol_b)
    o_ctx = oc_f + rev(oc_b) if with_ctx else None
    return o_lat, o_ctx


def neighbourhood_attention(q, k, v, kc, vc, rpb):
    b, h, n, dh = q.shape
    rows = n // GRID_W
    kr = min(NA_WIN_ROWS, rows)
    scale = dh ** -0.5
    qg = q.reshape(b, h, rows, GRID_W, dh)
    kg = k.reshape(b, h, rows, GRID_W, dh)
    vg = v.reshape(b, h, rows, GRID_W, dh)
    r_idx = jnp.arange(rows)
    row_start = jnp.clip(r_idx - kr // 2, 0, rows - kr)
    cidx = jnp.arange(GRID_W)
    col_start = jnp.clip(cidx - NA_WIN_COLS // 2, 0, GRID_W - NA_WIN_COLS)
    col_idx = col_start[:, None] + jnp.arange(NA_WIN_COLS)
    col_bias_idx = col_idx - cidx[:, None] + NA_WIN_COLS - 1
    rpb_cols = rpb[:, :, col_bias_idx]
    n_win = kr * NA_WIN_COLS

    def row_block(args):
        r, q_r = args
        rs = row_start[r]
        k_band = lax.dynamic_slice_in_dim(kg, rs, kr, axis=2)
        v_band = lax.dynamic_slice_in_dim(vg, rs, kr, axis=2)
        k_win = k_band[:, :, :, col_idx]
        v_win = v_band[:, :, :, col_idx]
        row_bias_idx = rs + jnp.arange(kr) - r + NA_WIN_ROWS - 1
        bias = jnp.take(rpb_cols, row_bias_idx, axis=1).transpose(0, 2, 1, 3)
        s_win = jnp.einsum('bhwd,bhrwcd->bhwrc', q_r, k_win).astype(jnp.float32) * scale + bias[None]
        s_ctx = jnp.einsum('bhwd,bhnd->bhwn', q_r, kc).astype(jnp.float32) * scale
        s = jnp.concatenate([s_win.reshape(b, h, GRID_W, n_win), s_ctx], axis=-1)
        p = jax.nn.softmax(s, axis=-1).astype(v.dtype)
        p_win = p[..., :n_win].reshape(b, h, GRID_W, kr, NA_WIN_COLS)
        p_ctx = p[..., n_win:]
        return jnp.einsum('bhwrc,bhrwcd->bhwd', p_win, v_win) + jnp.einsum('bhwn,bhnd->bhwd', p_ctx, vc)

    o = lax.map(row_block, (r_idx, qg.transpose(2, 0, 1, 3, 4)))
    return o.transpose(1, 2, 0, 3, 4).reshape(b, h, n, dh)


def context_attention(qc, kc, vc):
    s = jnp.einsum('bhqd,bhkd->bhqk', qc, kc).astype(jnp.float32) * qc.shape[-1] ** -0.5
    p = jax.nn.softmax(s, axis=-1).astype(vc.dtype)
    return jnp.einsum('bhqk,bhkd->bhqd', p, vc)


def na_branch(pl, pc, q_norm, k_norm, rpb, with_ctx):
    def prep(q, k, v):
        q = rms_norm(split_heads(q, NA_HEADS), q_norm)
        k = rms_norm(split_heads(k, NA_HEADS), k_norm)
        return q, k, split_heads(v, NA_HEADS)
    ql, kl, vl = prep(*pl)
    qc, kc, vc = prep(*pc)
    o_lat = merge_heads(neighbourhood_attention(ql, kl, vl, kc, vc, rpb))
    o_ctx = merge_heads(context_attention(qc, kc, vc)) if with_ctx else None
    return o_lat, o_ctx


def gla_branch(pl, pc, w_gate, b_gate, out_norm, with_ctx):
    def prep(q, k, v, g):
        q = split_heads(q, GLA_HEADS) * GLA_DK ** -0.5
        k = split_heads(k, GLA_HEADS)
        v = split_heads(v, GLA_HEADS)
        g_f, g_b = jnp.split(g, 2, axis=-1)
        la_f = split_heads(jax.nn.log_sigmoid((g_f @ w_gate[0] + b_gate[0]).astype(jnp.float32)) / GLA_GATE_TAU, GLA_HEADS)
        la_b = split_heads(jax.nn.log_sigmoid((g_b @ w_gate[1] + b_gate[1]).astype(jnp.float32)) / GLA_GATE_TAU, GLA_HEADS)
        return q, k, v, la_f, la_b
    o_lat, o_ctx = bidir_scan(prep(*pc), prep(*pl), with_ctx)
    o_lat = head_rms_merge(o_lat, out_norm)
    o_ctx = head_rms_merge(o_ctx, out_norm) if with_ctx else None
    return o_lat, o_ctx


def log_gamma(offset):
    return jnp.log1p(-jnp.exp2(-(offset + jnp.arange(RET_HEADS, dtype=jnp.float32))))


def ret_branch(pl, pc, out_norm, with_ctx):
    def prep(q, k, v, rotate):
        q = split_heads(q, RET_HEADS) * RET_DK ** -0.5
        k = split_heads(k, RET_HEADS)
        if rotate:
            q, k = axial_rope(q), axial_rope(k)
        v = split_heads(v, RET_HEADS)
        b, h, t, _ = q.shape
        la_f = jnp.broadcast_to(log_gamma(RET_DECAY_FWD)[None, :, None, None], (b, h, t, 1))
        la_b = jnp.broadcast_to(log_gamma(RET_DECAY_BWD)[None, :, None, None], (b, h, t, 1))
        return q, k, v, la_f, la_b
    o_lat, o_ctx = bidir_scan(prep(*pc, False), prep(*pl, True), with_ctx)
    o_lat = head_rms_merge(o_lat, out_norm)
    o_ctx = head_rms_merge(o_ctx, out_norm) if with_ctx else None
    return o_lat, o_ctx


def merge_branches(outs, z, g, w_branch, w_out):
    zs = jnp.split(z, N_BRANCH, axis=-1)
    gs = jnp.split(g, N_BRANCH, axis=-1)
    y = jax.nn.sigmoid(gs[0]) * ((outs[0] * jax.nn.silu(zs[0])) @ w_branch[0])
    for i in range(1, N_BRANCH):
        y = y + jax.nn.sigmoid(gs[i]) * ((outs[i] * jax.nn.silu(zs[i])) @ w_branch[i])
    return y @ w_out


def hybrid_layer(x, xc, c, c_ctx, w_mod, b_mod, norm_w, w_in, na_q_norm, na_k_norm, na_rpb,
                 gla_w_gate, gla_b_gate, gla_out_norm, ret_out_norm, w_branch, w_out, with_ctx):
    mod = jax.nn.silu(c) @ w_mod + b_mod
    mod_c = jax.nn.silu(c_ctx) @ w_mod + b_mod
    shift, scale, gate = jnp.split(mod[:, None, :], 3, axis=-1)
    shift_c, scale_c, gate_c = jnp.split(mod_c, 3, axis=-1)
    h = rms_norm(x, norm_w) * (1.0 + scale) + shift
    hc = rms_norm(xc, norm_w) * (1.0 + scale_c) + shift_c
    (na_q, na_k, na_v, gla_q, gla_k, gla_v, gla_g, ret_q, ret_k, ret_v, z, g) = split_cols(h @ w_in)
    (na_qc, na_kc, na_vc, gla_qc, gla_kc, gla_vc, gla_gc, ret_qc, ret_kc, ret_vc, zc, gc) = split_cols(hc @ w_in)
    na_l, na_c = na_branch((na_q, na_k, na_v), (na_qc, na_kc, na_vc), na_q_norm, na_k_norm, na_rpb, with_ctx)
    gla_l, gla_c = gla_branch((gla_q, gla_k, gla_v, gla_g), (gla_qc, gla_kc, gla_vc, gla_gc),
                              gla_w_gate, gla_b_gate, gla_out_norm, with_ctx)
    ret_l, ret_c = ret_branch((ret_q, ret_k, ret_v), (ret_qc, ret_kc, ret_vc), ret_out_norm, with_ctx)
    x = x + gate * merge_branches((na_l, gla_l, ret_l), z, g, w_branch, w_out)
    if with_ctx:
        xc = xc + gate_c * merge_branches((na_c, gla_c, ret_c), zc, gc, w_branch, w_out)
    return x, xc


def setup_inputs(seed: int = 0) -> dict:
    key = jax.random.key(seed)
    ks = jax.random.split(key, 18)
    L, D = DEPTH, D_MODEL
    nrm = lambda k, shape, s: jax.random.normal(k, shape, jnp.float32) * s
    return {
        'x': nrm(ks[0], (BATCH, SEQ, D), 1.0),
        'c': nrm(ks[1], (BATCH, D), 1.0),
        'ctx': nrm(ks[2], (BATCH, CTX_LEN, D), 1.0),
        'c_ctx': nrm(ks[3], (D,), 1.0),
        'w_mod': nrm(ks[4], (L, D, 3 * D), 0.5 * D ** -0.5),
        'b_mod': nrm(ks[5], (L, 3 * D), 0.02),
        'norm_w': 1.0 + nrm(ks[6], (L, D), 0.02),
        'w_in': nrm(ks[7], (L, D, N_IN), D ** -0.5),
        'na_q_norm': 1.0 + nrm(ks[8], (L, NA_HEAD_DIM), 0.02),
        'na_k_norm': 1.0 + nrm(ks[9], (L, NA_HEAD_DIM), 0.02),
        'na_rpb': nrm(ks[10], (L, NA_HEADS, 2 * NA_WIN_ROWS - 1, 2 * NA_WIN_COLS - 1), 0.1),
        'gla_w_gate': nrm(ks[11], (L, 2, GLA_GATE_RANK, GLA_HEADS * GLA_DK), GLA_GATE_RANK ** -0.5),
        'gla_b_gate': nrm(ks[12], (L, 2, GLA_HEADS * GLA_DK), 0.1),
        'gla_out_norm': 1.0 + nrm(ks[13], (L, GLA_HEADS * GLA_DV), 0.02),
        'ret_out_norm': 1.0 + nrm(ks[14], (L, RET_HEADS * RET_DV), 0.02),
        'w_branch': nrm(ks[15], (L, N_BRANCH, BRANCH_W, D), BRANCH_W ** -0.5),
        'w_out': nrm(ks[16], (L, D, D), D ** -0.5),
    }


def reference(x, c, ctx, c_ctx, w_mod, b_mod, norm_w, w_in, na_q_norm, na_k_norm, na_rpb,
              gla_w_gate, gla_b_gate, gla_out_norm, ret_out_norm, w_branch, w_out):
    xc = ctx
    for layer in range(DEPTH):
        x, xc = hybrid_layer(x, xc, c, c_ctx, w_mod[layer], b_mod[layer], norm_w[layer], w_in[layer],
                             na_q_norm[layer], na_k_norm[layer], na_rpb[layer], gla_w_gate[layer],
                             gla_b_gate[layer], gla_out_norm[layer], ret_out_norm[layer], w_branch[layer],
                             w_out[layer], layer < DEPTH - 1)
    return x
```

```python
import functools

import numpy as np
import jax
import jax.numpy as jnp
from jax import lax
from jax.experimental import pallas as pl
from jax.experimental.pallas import tpu as pltpu

D_MODEL = 1024
GRID_W = 64
NORM_EPS = 1e-6
N_BRANCH = 3
BRANCH_W = D_MODEL // 2
NA_HEAD_DIM = 64
NA_HEADS = BRANCH_W // NA_HEAD_DIM
NA_WIN_ROWS = 8
NA_WIN_COLS = 16
SCAN_HEADS = 4
SCAN_DV = BRANCH_W // SCAN_HEADS
SCAN_DK = SCAN_DV // 2
GLA_GATE_RANK = 16
GLA_GATE_TAU = 16.0
RET_DECAY_FWD = 5.0
RET_DECAY_BWD = 5.5
CHUNK = 64
ROPE_BASE = 10000.0

SEC = 512
SEC_NA_Q, SEC_NA_K, SEC_NA_V, SEC_GLA_QK, SEC_GLA_V, SEC_RET_QK, SEC_RET_V = range(7)
SEC_Z0 = 7
SEC_G0 = 10
N_SEC = 16
GATE_PAD = 128
TRI_ROWS = 256
NA_QROWS = 4
NA_KROWS = 12
NEG_BIG = -1e30

VMEM_LIMIT = 56 * 1024 * 1024

F32 = jnp.float32
BF16 = jnp.bfloat16


def _dot(a, b):
    return jnp.dot(a, b, preferred_element_type=F32)


def _dot_nt(a, b):
    return lax.dot_general(a, b, (((1,), (1,)), ((), ())), preferred_element_type=F32)


def _dot_tn(a, b):
    return lax.dot_general(a, b, (((0,), (0,)), ((), ())), preferred_element_type=F32)


def _split3(x):
    hi = x.astype(BF16)
    r = x - hi.astype(F32)
    mid = r.astype(BF16)
    lo = (r - mid.astype(F32)).astype(BF16)
    return hi, mid, lo


def _exact_left_mul(m_bf16, x):
    hi, mid, lo = _split3(x)
    return _dot(m_bf16, hi) + _dot(m_bf16, mid) + _dot(m_bf16, lo)


def _exact_right_mul(x, m_bf16):
    hi, mid, lo = _split3(x)
    return _dot(hi, m_bf16) + _dot(mid, m_bf16) + _dot(lo, m_bf16)


def _log_sigmoid(x):
    return jnp.minimum(x, 0.0) - jnp.log(1.0 + jnp.exp(-jnp.abs(x)))


def _sigmoid(x):
    return 1.0 / (1.0 + jnp.exp(-x))


def _silu(x):
    return x * _sigmoid(x)


def _params(*sem):
    return pltpu.CompilerParams(dimension_semantics=sem, vmem_limit_bytes=VMEM_LIMIT)


def _mod_kernel(c_ref, w_ref, b_ref, o_ref):
    a = _silu(c_ref[...])
    o_ref[...] = jnp.dot(a, w_ref[...], preferred_element_type=F32,
                         precision=lax.Precision.HIGHEST) + b_ref[...]


def _modulation(c_rows, w_mod, b_mod):
    depth, d, d3 = w_mod.shape
    rows = c_rows.shape[0]
    tn = 1024
    return pl.pallas_call(
        _mod_kernel,
        grid=(depth, d3 // tn),
        in_specs=[
            pl.BlockSpec((rows, d), lambda l, j: (0, 0)),
            pl.BlockSpec((None, d, tn), lambda l, j: (l, 0, j)),
            pl.BlockSpec((None, 1, tn), lambda l, j: (l, 0, j)),
        ],
        out_specs=pl.BlockSpec((None, rows, tn), lambda l, j: (l, 0, j)),
        out_shape=jax.ShapeDtypeStruct((depth, rows, d3), F32),
        compiler_params=_params("parallel", "parallel"),
        name="modulation",
    )(c_rows, w_mod, b_mod.reshape(depth, 1, d3))


def _in_proj_kernel(x_ref, nw_ref, sc_ref, sh_ref, w_ref, cs_ref, gm_ref, wg_ref, wgate_ref,
                    bgate_ref, ltri_ref, utri_ref, cos_ref, sa_ref, sb_ref,
                    p_ref, cum_ref, h_scr, *, rope):
    j = pl.program_id(1)
    tm = x_ref.shape[0]

    @pl.when(j == 0)
    def _():
        x = x_ref[...]
        ms = jnp.mean(x * x, axis=-1, keepdims=True)
        xh = x * lax.rsqrt(ms + NORM_EPS)
        h = xh * (nw_ref[...] * (1.0 + sc_ref[...])) + sh_ref[...]
        hb = h.astype(BF16)
        h_scr[...] = hb
        g = _dot(hb, wg_ref[...])
        xg = _dot(g.astype(BF16), wgate_ref[...]) + bgate_ref[...]
        la = _log_sigmoid(xg) * (1.0 / GLA_GATE_TAU)
        half = la.shape[1] // 2
        for r in range(tm // TRI_ROWS):
            rows = slice(r * TRI_ROWS, (r + 1) * TRI_ROWS)
            cum_ref[rows, :half] = _exact_left_mul(ltri_ref[...], la[rows, :half])
            cum_ref[rows, half:] = _exact_left_mul(utri_ref[...], la[rows, half:])

    raw = _dot(h_scr[...], w_ref[...])
    acc = raw * cs_ref[...]

    @pl.when(j <= SEC_NA_K)
    def _():
        ss = _exact_right_mul(raw * raw, gm_ref[...])
        p_ref[...] = (acc * lax.rsqrt(ss * (1.0 / NA_HEAD_DIM) + NORM_EPS)).astype(BF16)

    if rope:
        @pl.when(j == SEC_RET_QK)
        def _():
            reps = SEC // cos_ref.shape[1]
            cos = jnp.concatenate([cos_ref[...]] * reps, axis=1)
            sa = jnp.concatenate([sa_ref[...]] * reps, axis=1)
            sb = jnp.concatenate([sb_ref[...]] * reps, axis=1)
            quarter = SCAN_DK // 4
            up = pltpu.roll(acc, SEC - quarter, 1)
            dn = pltpu.roll(acc, quarter, 1)
            p_ref[...] = (acc * cos + up * sa + dn * sb).astype(BF16)

        plain = jnp.logical_and(j > SEC_NA_K, j != SEC_RET_QK)
    else:
        plain = j > SEC_NA_K

    @pl.when(plain)
    def _():
        p_ref[...] = acc.astype(BF16)


def _in_proj(x2d, nw, sc, sh, wts, tabs, *, tm, rows_per_batch, rope):
    t, d = x2d.shape
    blocks_per_batch = rows_per_batch // tm
    nb = sc.shape[0]

    def bidx(i):
        return (i // blocks_per_batch) if nb > 1 else 0

    const2 = lambda i, j: (0, 0)
    row_tab = pl.BlockSpec((tm, tabs["cos"].shape[1]), lambda i, j: (i % blocks_per_batch, 0))
    return pl.pallas_call(
        functools.partial(_in_proj_kernel, rope=rope),
        grid=(t // tm, N_SEC),
        in_specs=[
            pl.BlockSpec((tm, d), lambda i, j: (i, 0)),
            pl.BlockSpec((1, d), const2),
            pl.BlockSpec((None, 1, d), lambda i, j: (bidx(i), 0, 0)),
            pl.BlockSpec((None, 1, d), lambda i, j: (bidx(i), 0, 0)),
            pl.BlockSpec((d, SEC), lambda i, j: (0, j)),
            pl.BlockSpec((None, 1, SEC), lambda i, j: (j, 0, 0)),
            pl.BlockSpec((SEC, SEC), const2),
            pl.BlockSpec((d, GATE_PAD), const2),
            pl.BlockSpec((GATE_PAD, SEC), const2),
            pl.BlockSpec((1, SEC), const2),
            pl.BlockSpec((TRI_ROWS, TRI_ROWS), const2),
            pl.BlockSpec((TRI_ROWS, TRI_ROWS), const2),
            row_tab, row_tab, row_tab,
        ],
        out_specs=[
            pl.BlockSpec((tm, SEC), lambda i, j: (i, j)),
            pl.BlockSpec((tm, SEC), lambda i, j: (i, 0)),
        ],
        out_shape=[
            jax.ShapeDtypeStruct((t, N_SEC * SEC), BF16),
            jax.ShapeDtypeStruct((t, SEC), F32),
        ],
        scratch_shapes=[pltpu.VMEM((tm, d), BF16)],
        compiler_params=_params("parallel", "arbitrary"),
        name="in_proj_rope" if rope else "in_proj",
    )(x2d, nw, sc, sh, wts["w"], wts["colscale"], wts["gmat"], wts["wg"], wts["wgate"],
      wts["bgate"], tabs["ltri"], tabs["utri"], tabs["cos"], tabs["sa"], tabs["sb"])


def _head_masks(width, group, n):
    lane = lax.broadcasted_iota(jnp.int32, (1, width), 1)
    return [jnp.where(jnp.logical_and(lane >= h * group, lane < (h + 1) * group), 1.0, 0.0).astype(BF16)
            for h in range(n)]


def _na_kernel(q_ref, k_ref, v_ref, kc_ref, vc_ref, bias_ref, o_ref, *, n_qb):
    qb = pl.program_id(2)
    tq = q_ref.shape[0]
    tk = bias_ref.shape[-1]
    t0 = jnp.clip(qb - 1, 0, n_qb - tk // tq)
    start = pl.multiple_of(t0 * tq, tq)
    q = q_ref[...]
    k3 = k_ref[pl.ds(start, tk), :]
    v3 = v_ref[pl.ds(start, tk), :]
    kc = kc_ref[...]
    vc = vc_ref[...]
    heads = q.shape[1] // NA_HEAD_DIM
    masks = _head_masks(q.shape[1], NA_HEAD_DIM, heads)
    acc = jnp.zeros(q.shape, F32)
    for h in range(heads):
        qh = q * masks[h]
        s = _dot_nt(qh, k3) + bias_ref[0, h]
        sc = _dot_nt(qh, kc)
        m = jnp.maximum(jnp.max(s, axis=-1, keepdims=True), jnp.max(sc, axis=-1, keepdims=True))
        p = jnp.exp(s - m)
        pc = jnp.exp(sc - m)
        l = jnp.sum(p, axis=-1, keepdims=True) + jnp.sum(pc, axis=-1, keepdims=True)
        oh = _dot(p.astype(BF16), v3 * masks[h]) + _dot(pc.astype(BF16), vc * masks[h])
        acc = acc + oh * (1.0 / l)
    o_ref[...] = acc.astype(BF16)


def _na_attention(p_lat, p_ctx, bias, *, batch, seq, ctx_len):
    tq = NA_QROWS * GRID_W
    tk = NA_KROWS * GRID_W
    n_qb = seq // tq
    hw = 256
    n_hg = BRANCH_W // hw
    sec_blk = SEC // hw
    return pl.pallas_call(
        functools.partial(_na_kernel, n_qb=n_qb),
        grid=(batch, n_hg, n_qb),
        in_specs=[
            pl.BlockSpec((tq, hw), lambda b, g, i: (b * n_qb + i, SEC_NA_Q * sec_blk + g)),
            pl.BlockSpec((seq, hw), lambda b, g, i: (b, SEC_NA_K * sec_blk + g)),
            pl.BlockSpec((seq, hw), lambda b, g, i: (b, SEC_NA_V * sec_blk + g)),
            pl.BlockSpec((ctx_len, hw), lambda b, g, i: (b, SEC_NA_K * sec_blk + g)),
            pl.BlockSpec((ctx_len, hw), lambda b, g, i: (b, SEC_NA_V * sec_blk + g)),
            pl.BlockSpec((1, hw // NA_HEAD_DIM, tq, tk),
                         lambda b, g, i: (jnp.where(i == 0, 0, jnp.where(i == n_qb - 1, 2, 1)), g, 0, 0)),
        ],
        out_specs=pl.BlockSpec((tq, hw), lambda b, g, i: (b * n_qb + i, g)),
        out_shape=jax.ShapeDtypeStruct((batch * seq, BRANCH_W), BF16),
        compiler_params=_params("parallel", "parallel", "arbitrary"),
        name="na_attention",
    )(p_lat, p_lat, p_lat, p_ctx, p_ctx, bias)


def _ctx_attn_kernel(q_ref, k_ref, v_ref, o_ref):
    q = q_ref[...]
    k = k_ref[...]
    v = v_ref[...]
    heads = q.shape[1] // NA_HEAD_DIM
    masks = _head_masks(q.shape[1], NA_HEAD_DIM, heads)
    acc = jnp.zeros(q.shape, F32)
    for h in range(heads):
        s = _dot_nt(q * masks[h], k)
        m = jnp.max(s, axis=-1, keepdims=True)
        p = jnp.exp(s - m)
        l = jnp.sum(p, axis=-1, keepdims=True)
        acc = acc + _dot(p.astype(BF16), v * masks[h]) * (1.0 / l)
    o_ref[...] = acc.astype(BF16)


def _ctx_attention(p_ctx, *, batch, ctx_len):
    hw = 256
    n_hg = BRANCH_W // hw
    sec_blk = SEC // hw
    spec = lambda sec: pl.BlockSpec((ctx_len, hw), lambda b, g: (b, sec * sec_blk + g))
    return pl.pallas_call(
        _ctx_attn_kernel,
        grid=(batch, n_hg),
        in_specs=[spec(SEC_NA_Q), spec(SEC_NA_K), spec(SEC_NA_V)],
        out_specs=pl.BlockSpec((ctx_len, hw), lambda b, g: (b, g)),
        out_shape=jax.ShapeDtypeStruct((batch * ctx_len, BRANCH_W), BF16),
        compiler_params=_params("parallel", "parallel"),
        name="ctx_attention",
    )(p_ctx, p_ctx, p_ctx)


def _log2(n):
    assert n & (n - 1) == 0
    return n.bit_length() - 1


def _block_mask(shape, row_group, col_group):
    r = lax.shift_right_logical(lax.broadcasted_iota(jnp.int32, shape, 0), _log2(row_group))
    c = lax.shift_right_logical(lax.broadcasted_iota(jnp.int32, shape, 1), _log2(col_group))
    return r == c


def _decay_factors(cum, ref_row, total_row):
    cm = cum[ref_row:ref_row + 1, :]
    tot = cum[total_row:total_row + 1, :]
    eq = jnp.exp(cum - cm)
    ek = jnp.exp(cm - cum)
    return eq, ek, jnp.exp(cm), jnp.exp(tot - cm), jnp.exp(tot)


def _chunk_step(q, k, v, factors, tri_wide, st_ref, masks):
    eq, ek, e_m, e_tm, e_t = factors
    kbd_mask, vbd_mask, st_mask = masks
    qt = q * eq
    kt = k * ek
    qhat = (qt * e_m).astype(BF16)
    khat = (kt * e_tm).astype(BF16)
    ktb = kt.astype(BF16)
    kbd = jnp.where(kbd_mask, jnp.concatenate([ktb] * SCAN_HEADS, axis=0), jnp.zeros((), BF16))
    vbd = jnp.where(vbd_mask, jnp.concatenate([v] * SCAN_HEADS, axis=0), jnp.zeros((), BF16))
    att = _dot_nt(qt.astype(BF16), kbd)
    att = jnp.where(tri_wide, att, 0.0).astype(BF16)
    st = st_ref[...]
    o = _dot(att, vbd) + _dot_nt(qhat, st.astype(BF16))
    upd = _dot_tn(v, khat)
    st_ref[...] = st * e_t + jnp.where(st_mask, upd, 0.0)
    return o


def _scan_masks():
    hk = SCAN_HEADS * SCAN_DK
    hv = SCAN_HEADS * SCAN_DV
    hc = SCAN_HEADS * CHUNK
    kbd_mask = _block_mask((hc, hk), CHUNK, SCAN_DK)
    vbd_mask = _block_mask((hc, hv), CHUNK, SCAN_DV)
    st_mask = _block_mask((hv, hk), SCAN_DV, SCAN_DK)
    t = lax.broadcasted_iota(jnp.int32, (CHUNK, hc), 0)
    s = lax.broadcasted_iota(jnp.int32, (CHUNK, hc), 1) & (CHUNK - 1)
    return (kbd_mask, vbd_mask, st_mask), s <= t, s >= t


def _ret_cum(lg_row, forward):
    t = lax.broadcasted_iota(jnp.int32, (CHUNK, lg_row.shape[1]), 0).astype(F32)
    steps = (t + 1.0) if forward else (float(CHUNK) - t)
    return steps * lg_row


def _scan_kernel(*refs, gla, n_chunks, from_zero):
    if gla:
        (qkf_ref, vf_ref, cf_ref, qkb_ref, vb_ref, cb_ref, s0_ref,
         of_ref, ob_ref, sfin_ref, stf, stb) = refs
    else:
        (qkf_ref, vf_ref, qkb_ref, vb_ref, lg_ref, s0_ref,
         of_ref, ob_ref, sfin_ref, stf, stb) = refs
    step = pl.program_id(1)
    hk = SCAN_HEADS * SCAN_DK

    @pl.when(step == 0)
    def _():
        if from_zero:
            stf[...] = jnp.zeros_like(stf)
            stb[...] = jnp.zeros_like(stb)
        else:
            stf[...] = s0_ref[0]
            stb[...] = s0_ref[1]

    masks, tril, triu = _scan_masks()
    if not gla:
        fac_f = _decay_factors(_ret_cum(lg_ref[0:1, :], True), CHUNK // 2 - 1, CHUNK - 1)
        fac_b = _decay_factors(_ret_cum(lg_ref[1:2, :], False), CHUNK // 2, 0)

    for c in range(n_chunks):
        rf = slice(c * CHUNK, (c + 1) * CHUNK)
        cb = n_chunks - 1 - c
        rb = slice(cb * CHUNK, (cb + 1) * CHUNK)
        if gla:
            fac_f = _decay_factors(cf_ref[rf, :], CHUNK // 2 - 1, CHUNK - 1)
            fac_b = _decay_factors(cb_ref[rb, :], CHUNK // 2, 0)
        qkf = qkf_ref[rf, :].astype(F32)
        of_ref[rf, :] = _chunk_step(qkf[:, :hk], qkf[:, hk:], vf_ref[rf, :], fac_f, tril, stf,
                                    masks).astype(BF16)
        qkb = qkb_ref[rb, :].astype(F32)
        ob_ref[rb, :] = _chunk_step(qkb[:, :hk], qkb[:, hk:], vb_ref[rb, :], fac_b, triu, stb,
                                    masks).astype(BF16)

    @pl.when(step == pl.num_programs(1) - 1)
    def _():
        sfin_ref[0] = stf[...]
        sfin_ref[1] = stb[...]


def _scan(p, cum, lg, s0, *, gla, batch, rows_per_batch, tb):
    nb = rows_per_batch // tb
    hk = SCAN_HEADS * SCAN_DK
    hv = SCAN_HEADS * SCAN_DV
    sec_qk = SEC_GLA_QK if gla else SEC_RET_QK
    sec_v = SEC_GLA_V if gla else SEC_RET_V
    fwd = lambda b, s: b * nb + s
    bwd = lambda b, s: b * nb + nb - 1 - s
    from_zero = s0 is None
    if from_zero:
        s0 = jnp.zeros((batch, 2, hv, hk), F32)
    in_specs = [
        pl.BlockSpec((tb, SEC), lambda b, s: (fwd(b, s), sec_qk)),
        pl.BlockSpec((tb, SEC), lambda b, s: (fwd(b, s), sec_v)),
    ]
    args = [p, p]
    if gla:
        in_specs.append(pl.BlockSpec((tb, hk), lambda b, s: (fwd(b, s), 0)))
        args.append(cum)
    in_specs += [
        pl.BlockSpec((tb, SEC), lambda b, s: (bwd(b, s), sec_qk)),
        pl.BlockSpec((tb, SEC), lambda b, s: (bwd(b, s), sec_v)),
    ]
    args += [p, p]
    if gla:
        in_specs.append(pl.BlockSpec((tb, hk), lambda b, s: (bwd(b, s), 1)))
        args.append(cum)
    else:
        in_specs.append(pl.BlockSpec((2, hk), lambda b, s: (0, 0)))
        args.append(lg)
    in_specs.append(pl.BlockSpec((None, 2, hv, hk), lambda b, s: (b, 0, 0, 0)))
    args.append(s0)
    return pl.pallas_call(
        functools.partial(_scan_kernel, gla=gla, n_chunks=tb // CHUNK, from_zero=from_zero),
        grid=(batch, nb),
        in_specs=in_specs,
        out_specs=[
            pl.BlockSpec((tb, hv), lambda b, s: (fwd(b, s), 0)),
            pl.BlockSpec((tb, hv), lambda b, s: (bwd(b, s), 0)),
            pl.BlockSpec((None, 2, hv, hk), lambda b, s: (b, 0, 0, 0)),
        ],
        out_shape=[
            jax.ShapeDtypeStruct((batch * rows_per_batch, hv), BF16),
            jax.ShapeDtypeStruct((batch * rows_per_batch, hv), BF16),
            jax.ShapeDtypeStruct((batch, 2, hv, hk), F32),
        ],
        scratch_shapes=[pltpu.VMEM((hv, hk), F32), pltpu.VMEM((hv, hk), F32)],
        compiler_params=_params("parallel", "arbitrary"),
        name=("gla" if gla else "ret") + ("_ctx_scan" if from_zero else "_scan"),
    )(*args)


def _head_rms(o, gain):
    parts = []
    for h in range(SCAN_HEADS):
        sl = o[:, h * SCAN_DV:(h + 1) * SCAN_DV]
        ms = jnp.mean(sl * sl, axis=-1, keepdims=True)
        parts.append(sl * lax.rsqrt(ms + NORM_EPS))
    return jnp.concatenate(parts, axis=1) * gain


def _merge_kernel(x_ref, gate_ref, na_ref, gf_ref, gb_ref, rf_ref, rb_ref, z0_ref, z1_ref, z2_ref,
                  g0_ref, g1_ref, g2_ref, gg_ref, rg_ref, wb_ref, wo_ref, o_ref):
    outs = (
        na_ref[...].astype(F32),
        _head_rms(gf_ref[...].astype(F32) + gb_ref[...].astype(F32), gg_ref[...]),
        _head_rms(rf_ref[...].astype(F32) + rb_ref[...].astype(F32), rg_ref[...]),
    )
    zs = (z0_ref, z1_ref, z2_ref)
    gs = (g0_ref, g1_ref, g2_ref)
    y = None
    for i in range(N_BRANCH):
        u = (outs[i] * _silu(zs[i][...].astype(F32))).astype(BF16)
        t = _sigmoid(gs[i][...].astype(F32)) * _dot(u, wb_ref[i])
        y = t if y is None else y + t
    r = _dot(y.astype(BF16), wo_ref[...])
    o_ref[...] = x_ref[...] + gate_ref[...] * r


def _merge(x2d, gate, na_o, gla_f, gla_b, ret_f, ret_b, p, gla_gain, ret_gain, wb, wo, *, tm,
           rows_per_batch):
    t, d = x2d.shape
    blocks_per_batch = rows_per_batch // tm
    nb = gate.shape[0]
    bidx = (lambda i: i // blocks_per_batch) if nb > 1 else (lambda i: 0)
    row = lambda w: pl.BlockSpec((tm, w), lambda i: (i, 0))
    zspec = lambda n: pl.BlockSpec((tm, SEC), lambda i: (i, SEC_Z0 + n))
    gspec = lambda n: pl.BlockSpec((tm, d), lambda i: (i, SEC_G0 * SEC // d + n))
    return pl.pallas_call(
        _merge_kernel,
        grid=(t // tm,),
        in_specs=[
            row(d),
            pl.BlockSpec((None, 1, d), lambda i: (bidx(i), 0, 0)),
            row(BRANCH_W), row(BRANCH_W), row(BRANCH_W), row(BRANCH_W), row(BRANCH_W),
            zspec(0), zspec(1), zspec(2), gspec(0), gspec(1), gspec(2),
            pl.BlockSpec((1, BRANCH_W), lambda i: (0, 0)),
            pl.BlockSpec((1, BRANCH_W), lambda i: (0, 0)),
            pl.BlockSpec((N_BRANCH, BRANCH_W, d), lambda i: (0, 0, 0)),
            pl.BlockSpec((d, d), lambda i: (0, 0)),
        ],
        out_specs=row(d),
        out_shape=jax.ShapeDtypeStruct((t, d), F32),
        compiler_params=_params("parallel"),
        name="merge",
    )(x2d, gate, na_o, gla_f, gla_b, ret_f, ret_b, p, p, p, p, p, p, gla_gain, ret_gain, wb, wo)


def _in_splits():
    na = NA_HEADS * NA_HEAD_DIM
    qk = SCAN_HEADS * SCAN_DK
    vv = SCAN_HEADS * SCAN_DV
    sizes = (na, na, na, qk, qk, vv, 2 * GLA_GATE_RANK, qk, qk, vv, N_BRANCH * BRANCH_W,
             N_BRANCH * D_MODEL)
    return np.concatenate([[0], np.cumsum(sizes)])


def _prep_layer_weights(w_in, q_norm, k_norm, w_gate, b_gate):
    o = _in_splits()
    g0, g1 = int(o[6]), int(o[7])
    w = jnp.concatenate([w_in[:, :g0], w_in[:, g1:]], axis=1).astype(BF16)
    wg = jnp.pad(w_in[:, g0:g1], ((0, 0), (0, GATE_PAD - (g1 - g0)))).astype(BF16)
    hk = SCAN_HEADS * SCAN_DK
    wgate = jnp.zeros((GATE_PAD, 2 * hk), F32)
    wgate = wgate.at[:GLA_GATE_RANK, :hk].set(w_gate[0])
    wgate = wgate.at[GLA_GATE_RANK:2 * GLA_GATE_RANK, hk:].set(w_gate[1])
    bgate = jnp.concatenate([b_gate[0], b_gate[1]]).reshape(1, 2 * hk)
    colscale = np.ones((N_SEC, 1, SEC), np.float32)
    colscale[SEC_NA_Q] = NA_HEAD_DIM ** -0.5
    colscale[SEC_GLA_QK, 0, :hk] = SCAN_DK ** -0.5
    colscale[SEC_RET_QK, 0, :hk] = SCAN_DK ** -0.5
    colscale = jnp.asarray(colscale)
    reps = SEC // NA_HEAD_DIM
    colscale = colscale.at[SEC_NA_Q, 0].multiply(jnp.tile(q_norm, reps))
    colscale = colscale.at[SEC_NA_K, 0].multiply(jnp.tile(k_norm, reps))
    grp = np.arange(SEC) // NA_HEAD_DIM
    gmat = jnp.asarray((grp[:, None] == grp[None, :]).astype(np.float32)).astype(BF16)
    return dict(w=w, wg=wg, wgate=wgate.astype(BF16), bgate=bgate, colscale=colscale, gmat=gmat)


def _tables(seq, ctx_len):
    i = np.arange(TRI_ROWS)
    same = (i[:, None] // CHUNK) == (i[None, :] // CHUNK)
    ltri = jnp.asarray((same & (i[None, :] <= i[:, None])).astype(np.float32)).astype(BF16)
    utri = jnp.asarray((same & (i[None, :] >= i[:, None])).astype(np.float32)).astype(BF16)
    half = SCAN_DK // 2
    quarter = half // 2
    pos = jnp.arange(seq)
    row = (pos // GRID_W).astype(F32)
    col = (pos % GRID_W).astype(F32)
    inv = ROPE_BASE ** (-jnp.arange(quarter, dtype=F32) / quarter)
    ang_r = row[:, None] * inv[None, :]
    ang_c = col[:, None] * inv[None, :]
    zero = jnp.zeros_like(ang_r)
    cos = jnp.concatenate([jnp.cos(ang_r)] * 2 + [jnp.cos(ang_c)] * 2, axis=1)
    sa = jnp.concatenate([-jnp.sin(ang_r), zero, -jnp.sin(ang_c), zero], axis=1)
    sb = jnp.concatenate([zero, jnp.sin(ang_r), zero, jnp.sin(ang_c)], axis=1)
    two = lambda a: jnp.concatenate([a, a], axis=1)
    lat = dict(ltri=ltri, utri=utri, cos=two(cos), sa=two(sa), sb=two(sb))
    dummy = jnp.zeros((ctx_len, 2 * SCAN_DK), F32)
    ctx = dict(ltri=ltri, utri=utri, cos=dummy, sa=dummy, sb=dummy)
    return lat, ctx


def _na_bias_tables(rpb, rows):
    n_qb = rows // NA_QROWS
    kr = min(NA_WIN_ROWS, rows)
    c = np.arange(GRID_W)
    cs = np.clip(c - NA_WIN_COLS // 2, 0, GRID_W - NA_WIN_COLS)
    kc = np.arange(GRID_W)
    col_ok = (kc[None, :] >= cs[:, None]) & (kc[None, :] < cs[:, None] + NA_WIN_COLS)
    col_idx = np.clip(kc[None, :] - c[:, None] + NA_WIN_COLS - 1, 0, 2 * NA_WIN_COLS - 2)
    tabs = []
    for qb in (0, min(1, n_qb - 1), n_qb - 1):
        t0 = int(np.clip(qb - 1, 0, n_qb - NA_KROWS // NA_QROWS))
        r = qb * NA_QROWS + np.arange(NA_QROWS)
        krow = t0 * NA_QROWS + np.arange(NA_KROWS)
        rs = np.clip(r - kr // 2, 0, rows - kr)
        row_ok = (krow[None, :] >= rs[:, None]) & (krow[None, :] < rs[:, None] + kr)
        row_idx = np.clip(krow[None, :] - r[:, None] + NA_WIN_ROWS - 1, 0, 2 * NA_WIN_ROWS - 2)
        g = rpb[:, row_idx[:, :, None, None], col_idx[None, None, :, :]]
        ok = row_ok[:, :, None, None] & col_ok[None, None, :, :]
        g = jnp.where(jnp.asarray(ok)[None], g, NEG_BIG)
        g = g.transpose(0, 1, 3, 2, 4).reshape(rpb.shape[0], NA_QROWS * GRID_W, NA_KROWS * GRID_W)
        tabs.append(g)
    return jnp.stack(tabs)


def _log_gamma(offset):
    g = jnp.log1p(-jnp.exp2(-(offset + jnp.arange(SCAN_HEADS, dtype=F32))))
    return jnp.repeat(g, SCAN_DK)


def kernel(x, c, ctx, c_ctx, w_mod, b_mod, norm_w, w_in, na_q_norm, na_k_norm, na_rpb, gla_w_gate,
           gla_b_gate, gla_out_norm, ret_out_norm, w_branch, w_out):
    batch, seq, d = x.shape
    ctx_len = ctx.shape[1]
    depth = w_mod.shape[0]
    rows = seq // GRID_W

    c_rows = jnp.zeros((8, d), F32).at[:batch].set(c).at[batch].set(c_ctx)
    mod = _modulation(c_rows, w_mod, b_mod)
    lat_tabs, ctx_tabs = _tables(seq, ctx_len)
    lg = jnp.stack([_log_gamma(RET_DECAY_FWD), _log_gamma(RET_DECAY_BWD)])

    xl = x.reshape(batch * seq, d)
    xc = ctx.reshape(batch * ctx_len, d)
    for layer in range(depth):
        with_ctx = layer < depth - 1
        m = mod[layer]
        shift, scale, gate = (m[:batch, i * d:(i + 1) * d].reshape(batch, 1, d) for i in range(3))
        shift_c, scale_c, gate_c = (m[batch:batch + 1, i * d:(i + 1) * d].reshape(1, 1, d) for i in range(3))
        wts = _prep_layer_weights(w_in[layer], na_q_norm[layer], na_k_norm[layer],
                                  gla_w_gate[layer], gla_b_gate[layer])
        nw = norm_w[layer].reshape(1, d)

        p_lat, cum_lat = _in_proj(xl, nw, scale, shift, wts, lat_tabs, tm=1024,
                                  rows_per_batch=seq, rope=True)
        p_ctx, cum_ctx = _in_proj(xc, nw, scale_c, shift_c, wts, ctx_tabs, tm=ctx_len,
                                  rows_per_batch=ctx_len, rope=False)

        bias = _na_bias_tables(na_rpb[layer], rows)
        na_l = _na_attention(p_lat, p_ctx, bias, batch=batch, seq=seq, ctx_len=ctx_len)

        gla_cf, gla_cb, gla_s = _scan(p_ctx, cum_ctx, None, None, gla=True, batch=batch,
                                      rows_per_batch=ctx_len, tb=ctx_len)
        gla_f, gla_b, _ = _scan(p_lat, cum_lat, None, gla_s, gla=True, batch=batch,
                                rows_per_batch=seq, tb=256)
        ret_cf, ret_cb, ret_s = _scan(p_ctx, None, lg, None, gla=False, batch=batch,
                                      rows_per_batch=ctx_len, tb=ctx_len)
        ret_f, ret_b, _ = _scan(p_lat, None, lg, ret_s, gla=False, batch=batch,
                                rows_per_batch=seq, tb=256)

        wb = w_branch[layer].astype(BF16)
        wo = w_out[layer].astype(BF16)
        gg = gla_out_norm[layer].reshape(1, BRANCH_W)
        rg = ret_out_norm[layer].reshape(1, BRANCH_W)
        xl_new = _merge(xl, gate, na_l, gla_f, gla_b, ret_f, ret_b, p_lat, gg, rg, wb, wo, tm=512,
                        rows_per_batch=seq)
        if with_ctx:
            na_c = _ctx_attention(p_ctx, batch=batch, ctx_len=ctx_len)
            xc = _merge(xc, gate_c, na_c, gla_cf, gla_cb, ret_cf, ret_cb, p_ctx, gg, rg, wb, wo,
                        tm=ctx_len, rows_per_batch=ctx_len)
        xl = xl_new
    return xl.reshape(batch, seq, d)
```

```python
import functools

import numpy as np
import jax
import jax.numpy as jnp
from jax import lax
from jax.experimental import pallas as pl
from jax.experimental.pallas import tpu as pltpu

D_MODEL = 1024
GRID_W = 64
NORM_EPS = 1e-6
N_BRANCH = 3
BRANCH_W = D_MODEL // 2
NA_HEAD_DIM = 64
NA_HEADS = BRANCH_W // NA_HEAD_DIM
NA_WIN_ROWS = 8
NA_WIN_COLS = 16
SCAN_HEADS = 4
SCAN_DV = BRANCH_W // SCAN_HEADS
SCAN_DK = SCAN_DV // 2
GLA_GATE_RANK = 16
GLA_GATE_TAU = 16.0
RET_DECAY_FWD = 5.0
RET_DECAY_BWD = 5.5
CHUNK = 64
ROPE_BASE = 10000.0

SEC = 512
SEC_NA_Q, SEC_NA_K, SEC_NA_V, SEC_GLA_QK, SEC_GLA_V, SEC_RET_QK, SEC_RET_V = range(7)
SEC_Z0 = 7
SEC_G0 = 10
N_SEC = 16
GATE_PAD = 128
TRI_ROWS = 256
NA_QROWS = 4
NA_KROWS = 12
NEG_BIG = -1e30

VMEM_LIMIT = 56 * 1024 * 1024

F32 = jnp.float32
BF16 = jnp.bfloat16


def _dot(a, b):
    return jnp.dot(a, b, preferred_element_type=F32)


def _dot_nt(a, b):
    return lax.dot_general(a, b, (((1,), (1,)), ((), ())), preferred_element_type=F32)


def _dot_tn(a, b):
    return lax.dot_general(a, b, (((0,), (0,)), ((), ())), preferred_element_type=F32)


def _split3(x):
    hi = x.astype(BF16)
    r = x - hi.astype(F32)
    mid = r.astype(BF16)
    lo = (r - mid.astype(F32)).astype(BF16)
    return hi, mid, lo


def _exact_left_mul(m_bf16, x):
    hi, mid, lo = _split3(x)
    return _dot(m_bf16, hi) + _dot(m_bf16, mid) + _dot(m_bf16, lo)


def _exact_right_mul(x, m_bf16):
    hi, mid, lo = _split3(x)
    return _dot(hi, m_bf16) + _dot(mid, m_bf16) + _dot(lo, m_bf16)


def _log_sigmoid(x):
    return jnp.minimum(x, 0.0) - jnp.log(1.0 + jnp.exp(-jnp.abs(x)))


def _sigmoid(x):
    return 1.0 / (1.0 + jnp.exp(-x))


def _silu(x):
    return x * _sigmoid(x)


def _params(*sem):
    return pltpu.CompilerParams(dimension_semantics=sem, vmem_limit_bytes=VMEM_LIMIT)


def _mod_kernel(c_ref, w_ref, b_ref, o_ref):
    a = _silu(c_ref[...])
    o_ref[...] = jnp.dot(a, w_ref[...], preferred_element_type=F32,
                         precision=lax.Precision.HIGHEST) + b_ref[...]


def _modulation(c_rows, w_mod, b_mod):
    depth, d, d3 = w_mod.shape
    rows = c_rows.shape[0]
    tn = 1024
    return pl.pallas_call(
        _mod_kernel,
        grid=(depth, d3 // tn),
        in_specs=[
            pl.BlockSpec((rows, d), lambda l, j: (0, 0)),
            pl.BlockSpec((None, d, tn), lambda l, j: (l, 0, j)),
            pl.BlockSpec((None, 1, tn), lambda l, j: (l, 0, j)),
        ],
        out_specs=pl.BlockSpec((None, rows, tn), lambda l, j: (l, 0, j)),
        out_shape=jax.ShapeDtypeStruct((depth, rows, d3), F32),
        compiler_params=_params("parallel", "parallel"),
        name="modulation",
    )(c_rows, w_mod, b_mod.reshape(depth, 1, d3))


def _in_proj_kernel(x_ref, nw_ref, sc_ref, sh_ref, w_ref, cs_ref, gm_ref, wg_ref, wgate_ref,
                    bgate_ref, ltri_ref, utri_ref, cos_ref, sa_ref, sb_ref,
                    p_ref, cum_ref, h_scr, *, rope):
    j = pl.program_id(1)
    tm = x_ref.shape[0]

    @pl.when(j == 0)
    def _():
        x = x_ref[...]
        ms = jnp.mean(x * x, axis=-1, keepdims=True)
        xh = x * lax.rsqrt(ms + NORM_EPS)
        h = xh * (nw_ref[...] * (1.0 + sc_ref[...])) + sh_ref[...]
        hb = h.astype(BF16)
        h_scr[...] = hb
        g = _dot(hb, wg_ref[...])
        xg = _dot(g.astype(BF16), wgate_ref[...]) + bgate_ref[...]
        la = _log_sigmoid(xg) * (1.0 / GLA_GATE_TAU)
        half = la.shape[1] // 2
        for r in range(tm // TRI_ROWS):
            rows = slice(r * TRI_ROWS, (r + 1) * TRI_ROWS)
            cum_ref[rows, :half] = _exact_left_mul(ltri_ref[...], la[rows, :half])
            cum_ref[rows, half:] = _exact_left_mul(utri_ref[...], la[rows, half:])

    raw = _dot(h_scr[...], w_ref[...])
    acc = raw * cs_ref[...]

    @pl.when(j <= SEC_NA_K)
    def _():
        ss = _exact_right_mul(raw * raw, gm_ref[...])
        p_ref[...] = (acc * lax.rsqrt(ss * (1.0 / NA_HEAD_DIM) + NORM_EPS)).astype(BF16)

    if rope:
        @pl.when(j == SEC_RET_QK)
        def _():
            reps = SEC // cos_ref.shape[1]
            cos = jnp.concatenate([cos_ref[...]] * reps, axis=1)
            sa = jnp.concatenate([sa_ref[...]] * reps, axis=1)
            sb = jnp.concatenate([sb_ref[...]] * reps, axis=1)
            quarter = SCAN_DK // 4
            up = pltpu.roll(acc, SEC - quarter, 1)
            dn = pltpu.roll(acc, quarter, 1)
            p_ref[...] = (acc * cos + up * sa + dn * sb).astype(BF16)

        plain = jnp.logical_and(j > SEC_NA_K, j != SEC_RET_QK)
    else:
        plain = j > SEC_NA_K

    @pl.when(plain)
    def _():
        p_ref[...] = acc.astype(BF16)


def _in_proj(x2d, nw, sc, sh, wts, tabs, *, tm, rows_per_batch, rope):
    t, d = x2d.shape
    blocks_per_batch = rows_per_batch // tm
    nb = sc.shape[0]

    def bidx(i):
        return (i // blocks_per_batch) if nb > 1 else 0

    const2 = lambda i, j: (0, 0)
    row_tab = pl.BlockSpec((tm, tabs["cos"].shape[1]), lambda i, j: (i % blocks_per_batch, 0))
    return pl.pallas_call(
        functools.partial(_in_proj_kernel, rope=rope),
        grid=(t // tm, N_SEC),
        in_specs=[
            pl.BlockSpec((tm, d), lambda i, j: (i, 0)),
            pl.BlockSpec((1, d), const2),
            pl.BlockSpec((None, 1, d), lambda i, j: (bidx(i), 0, 0)),
            pl.BlockSpec((None, 1, d), lambda i, j: (bidx(i), 0, 0)),
            pl.BlockSpec((d, SEC), lambda i, j: (0, j)),
            pl.BlockSpec((None, 1, SEC), lambda i, j: (j, 0, 0)),
            pl.BlockSpec((SEC, SEC), const2),
            pl.BlockSpec((d, GATE_PAD), const2),
            pl.BlockSpec((GATE_PAD, SEC), const2),
            pl.BlockSpec((1, SEC), const2),
            pl.BlockSpec((TRI_ROWS, TRI_ROWS), const2),
            pl.BlockSpec((TRI_ROWS, TRI_ROWS), const2),
            row_tab, row_tab, row_tab,
        ],
        out_specs=[
            pl.BlockSpec((tm, SEC), lambda i, j: (i, j)),
            pl.BlockSpec((tm, SEC), lambda i, j: (i, 0)),
        ],
        out_shape=[
            jax.ShapeDtypeStruct((t, N_SEC * SEC), BF16),
            jax.ShapeDtypeStruct((t, SEC), F32),
        ],
        scratch_shapes=[pltpu.VMEM((tm, d), BF16)],
        compiler_params=_params("parallel", "arbitrary"),
        name="in_proj_rope" if rope else "in_proj",
    )(x2d, nw, sc, sh, wts["w"], wts["colscale"], wts["gmat"], wts["wg"], wts["wgate"],
      wts["bgate"], tabs["ltri"], tabs["utri"], tabs["cos"], tabs["sa"], tabs["sb"])


def _head_masks(width, group, n):
    lane = lax.broadcasted_iota(jnp.int32, (1, width), 1)
    return [jnp.where(jnp.logical_and(lane >= h * group, lane < (h + 1) * group), 1.0, 0.0).astype(BF16)
            for h in range(n)]


def _na_kernel(q_ref, k_ref, v_ref, kc_ref, vc_ref, bias_ref, o_ref, *, n_qb):
    qb = pl.program_id(2)
    tq = q_ref.shape[0]
    tk = bias_ref.shape[-1]
    t0 = jnp.clip(qb - 1, 0, n_qb - tk // tq)
    start = pl.multiple_of(t0 * tq, tq)
    q = q_ref[...]
    k3 = k_ref[pl.ds(start, tk), :]
    v3 = v_ref[pl.ds(start, tk), :]
    kc = kc_ref[...]
    vc = vc_ref[...]
    heads = q.shape[1] // NA_HEAD_DIM
    masks = _head_masks(q.shape[1], NA_HEAD_DIM, heads)
    acc = jnp.zeros(q.shape, F32)
    for h in range(heads):
        qh = q * masks[h]
        s = _dot_nt(qh, k3) + bias_ref[0, h]
        sc = _dot_nt(qh, kc)
        m = jnp.maximum(jnp.max(s, axis=-1, keepdims=True), jnp.max(sc, axis=-1, keepdims=True))
        p = jnp.exp(s - m)
        pc = jnp.exp(sc - m)
        l = jnp.sum(p, axis=-1, keepdims=True) + jnp.sum(pc, axis=-1, keepdims=True)
        oh = _dot(p.astype(BF16), v3 * masks[h]) + _dot(pc.astype(BF16), vc * masks[h])
        acc = acc + oh * (1.0 / l)
    o_ref[...] = acc.astype(BF16)


def _na_attention(p_lat, p_ctx, bias, *, batch, seq, ctx_len):
    tq = NA_QROWS * GRID_W
    tk = NA_KROWS * GRID_W
    n_qb = seq // tq
    hw = 256
    n_hg = BRANCH_W // hw
    sec_blk = SEC // hw
    return pl.pallas_call(
        functools.partial(_na_kernel, n_qb=n_qb),
        grid=(batch, n_hg, n_qb),
        in_specs=[
            pl.BlockSpec((tq, hw), lambda b, g, i: (b * n_qb + i, SEC_NA_Q * sec_blk + g)),
            pl.BlockSpec((seq, hw), lambda b, g, i: (b, SEC_NA_K * sec_blk + g)),
            pl.BlockSpec((seq, hw), lambda b, g, i: (b, SEC_NA_V * sec_blk + g)),
            pl.BlockSpec((ctx_len, hw), lambda b, g, i: (b, SEC_NA_K * sec_blk + g)),
            pl.BlockSpec((ctx_len, hw), lambda b, g, i: (b, SEC_NA_V * sec_blk + g)),
            pl.BlockSpec((1, hw // NA_HEAD_DIM, tq, tk),
                         lambda b, g, i: (jnp.where(i == 0, 0, jnp.where(i == n_qb - 1, 2, 1)), g, 0, 0)),
        ],
        out_specs=pl.BlockSpec((tq, hw), lambda b, g, i: (b * n_qb + i, g)),
        out_shape=jax.ShapeDtypeStruct((batch * seq, BRANCH_W), BF16),
        compiler_params=_params("parallel", "parallel", "arbitrary"),
        name="na_attention",
    )(p_lat, p_lat, p_lat, p_ctx, p_ctx, bias)


def _ctx_attn_kernel(q_ref, k_ref, v_ref, o_ref):
    q = q_ref[...]
    k = k_ref[...]
    v = v_ref[...]
    heads = q.shape[1] // NA_HEAD_DIM
    masks = _head_masks(q.shape[1], NA_HEAD_DIM, heads)
    acc = jnp.zeros(q.shape, F32)
    for h in range(heads):
        s = _dot_nt(q * masks[h], k)
        m = jnp.max(s, axis=-1, keepdims=True)
        p = jnp.exp(s - m)
        l = jnp.sum(p, axis=-1, keepdims=True)
        acc = acc + _dot(p.astype(BF16), v * masks[h]) * (1.0 / l)
    o_ref[...] = acc.astype(BF16)


def _ctx_attention(p_ctx, *, batch, ctx_len):
    hw = 256
    n_hg = BRANCH_W // hw
    sec_blk = SEC // hw
    spec = lambda sec: pl.BlockSpec((ctx_len, hw), lambda b, g: (b, sec * sec_blk + g))
    return pl.pallas_call(
        _ctx_attn_kernel,
        grid=(batch, n_hg),
        in_specs=[spec(SEC_NA_Q), spec(SEC_NA_K), spec(SEC_NA_V)],
        out_specs=pl.BlockSpec((ctx_len, hw), lambda b, g: (b, g)),
        out_shape=jax.ShapeDtypeStruct((batch * ctx_len, BRANCH_W), BF16),
        compiler_params=_params("parallel", "parallel"),
        name="ctx_attention",
    )(p_ctx, p_ctx, p_ctx)


def _log2(n):
    assert n & (n - 1) == 0
    return n.bit_length() - 1


def _block_mask(shape, row_group, col_group):
    r = lax.shift_right_logical(lax.broadcasted_iota(jnp.int32, shape, 0), _log2(row_group))
    c = lax.shift_right_logical(lax.broadcasted_iota(jnp.int32, shape, 1), _log2(col_group))
    return r == c


def _decay_factors(cum, ref_row, total_row):
    cm = cum[ref_row:ref_row + 1, :]
    tot = cum[total_row:total_row + 1, :]
    eq = jnp.exp(cum - cm)
    ek = jnp.exp(cm - cum)
    return eq, ek, jnp.exp(cm), jnp.exp(tot - cm), jnp.exp(tot)


def _chunk_step(q, k, v, factors, tri_wide, st_ref, masks):
    eq, ek, e_m, e_tm, e_t = factors
    kbd_mask, vbd_mask, st_mask = masks
    qt = q * eq
    kt = k * ek
    qhat = (qt * e_m).astype(BF16)
    khat = (kt * e_tm).astype(BF16)
    ktb = kt.astype(BF16)
    kbd = jnp.where(kbd_mask, jnp.concatenate([ktb] * SCAN_HEADS, axis=0), jnp.zeros((), BF16))
    vbd = jnp.where(vbd_mask, jnp.concatenate([v] * SCAN_HEADS, axis=0), jnp.zeros((), BF16))
    att = _dot_nt(qt.astype(BF16), kbd)
    att = jnp.where(tri_wide, att, 0.0).astype(BF16)
    st = st_ref[...]
    o = _dot(att, vbd) + _dot_nt(qhat, st.astype(BF16))
    upd = _dot_tn(v, khat)
    st_ref[...] = st * e_t + jnp.where(st_mask, upd, 0.0)
    return o


def _scan_masks():
    hk = SCAN_HEADS * SCAN_DK
    hv = SCAN_HEADS * SCAN_DV
    hc = SCAN_HEADS * CHUNK
    kbd_mask = _block_mask((hc, hk), CHUNK, SCAN_DK)
    vbd_mask = _block_mask((hc, hv), CHUNK, SCAN_DV)
    st_mask = _block_mask((hv, hk), SCAN_DV, SCAN_DK)
    t = lax.broadcasted_iota(jnp.int32, (CHUNK, hc), 0)
    s = lax.broadcasted_iota(jnp.int32, (CHUNK, hc), 1) & (CHUNK - 1)
    return (kbd_mask, vbd_mask, st_mask), s <= t, s >= t


def _ret_cum(lg_row, forward):
    t = lax.broadcasted_iota(jnp.int32, (CHUNK, lg_row.shape[1]), 0).astype(F32)
    steps = (t + 1.0) if forward else (float(CHUNK) - t)
    return steps * lg_row


def _scan_kernel(*refs, gla, n_chunks, from_zero):
    if gla:
        (qkf_ref, vf_ref, cf_ref, qkb_ref, vb_ref, cb_ref, s0_ref,
         of_ref, ob_ref, sfin_ref, stf, stb) = refs
    else:
        (qkf_ref, vf_ref, qkb_ref, vb_ref, lg_ref, s0_ref,
         of_ref, ob_ref, sfin_ref, stf, stb) = refs
    step = pl.program_id(1)
    hk = SCAN_HEADS * SCAN_DK

    @pl.when(step == 0)
    def _():
        if from_zero:
            stf[...] = jnp.zeros_like(stf)
            stb[...] = jnp.zeros_like(stb)
        else:
            stf[...] = s0_ref[0]
            stb[...] = s0_ref[1]

    masks, tril, triu = _scan_masks()
    if not gla:
        fac_f = _decay_factors(_ret_cum(lg_ref[0:1, :], True), CHUNK // 2 - 1, CHUNK - 1)
        fac_b = _decay_factors(_ret_cum(lg_ref[1:2, :], False), CHUNK // 2, 0)

    for c in range(n_chunks):
        rf = slice(c * CHUNK, (c + 1) * CHUNK)
        cb = n_chunks - 1 - c
        rb = slice(cb * CHUNK, (cb + 1) * CHUNK)
        if gla:
            fac_f = _decay_factors(cf_ref[rf, :], CHUNK // 2 - 1, CHUNK - 1)
            fac_b = _decay_factors(cb_ref[rb, :], CHUNK // 2, 0)
        qkf = qkf_ref[rf, :].astype(F32)
        of_ref[rf, :] = _chunk_step(qkf[:, :hk], qkf[:, hk:], vf_ref[rf, :], fac_f, tril, stf,
                                    masks).astype(BF16)
        qkb = qkb_ref[rb, :].astype(F32)
        ob_ref[rb, :] = _chunk_step(qkb[:, :hk], qkb[:, hk:], vb_ref[rb, :], fac_b, triu, stb,
                                    masks).astype(BF16)

    @pl.when(step == pl.num_programs(1) - 1)
    def _():
        sfin_ref[0] = stf[...]
        sfin_ref[1] = stb[...]


def _scan(p, cum, lg, s0, *, gla, batch, rows_per_batch, tb):
    nb = rows_per_batch // tb
    hk = SCAN_HEADS * SCAN_DK
    hv = SCAN_HEADS * SCAN_DV
    sec_qk = SEC_GLA_QK if gla else SEC_RET_QK
    sec_v = SEC_GLA_V if gla else SEC_RET_V
    fwd = lambda b, s: b * nb + s
    bwd = lambda b, s: b * nb + nb - 1 - s
    from_zero = s0 is None
    if from_zero:
        s0 = jnp.zeros((batch, 2, hv, hk), F32)
    in_specs = [
        pl.BlockSpec((tb, SEC), lambda b, s: (fwd(b, s), sec_qk)),
        pl.BlockSpec((tb, SEC), lambda b, s: (fwd(b, s), sec_v)),
    ]
    args = [p, p]
    if gla:
        in_specs.append(pl.BlockSpec((tb, hk), lambda b, s: (fwd(b, s), 0)))
        args.append(cum)
    in_specs += [
        pl.BlockSpec((tb, SEC), lambda b, s: (bwd(b, s), sec_qk)),
        pl.BlockSpec((tb, SEC), lambda b, s: (bwd(b, s), sec_v)),
    ]
    args += [p, p]
    if gla:
        in_specs.append(pl.BlockSpec((tb, hk), lambda b, s: (bwd(b, s), 1)))
        args.append(cum)
    else:
        in_specs.append(pl.BlockSpec((2, hk), lambda b, s: (0, 0)))
        args.append(lg)
    in_specs.append(pl.BlockSpec((None, 2, hv, hk), lambda b, s: (b, 0, 0, 0)))
    args.append(s0)
    return pl.pallas_call(
        functools.partial(_scan_kernel, gla=gla, n_chunks=tb // CHUNK, from_zero=from_zero),
        grid=(batch, nb),
        in_specs=in_specs,
        out_specs=[
            pl.BlockSpec((tb, hv), lambda b, s: (fwd(b, s), 0)),
            pl.BlockSpec((tb, hv), lambda b, s: (bwd(b, s), 0)),
            pl.BlockSpec((None, 2, hv, hk), lambda b, s: (b, 0, 0, 0)),
        ],
        out_shape=[
            jax.ShapeDtypeStruct((batch * rows_per_batch, hv), BF16),
            jax.ShapeDtypeStruct((batch * rows_per_batch, hv), BF16),
            jax.ShapeDtypeStruct((batch, 2, hv, hk), F32),
        ],
        scratch_shapes=[pltpu.VMEM((hv, hk), F32), pltpu.VMEM((hv, hk), F32)],
        compiler_params=_params("parallel", "arbitrary"),
        name=("gla" if gla else "ret") + ("_ctx_scan" if from_zero else "_scan"),
    )(*args)


def _head_rms(o, gain):
    parts = []
    for h in range(SCAN_HEADS):
        sl = o[:, h * SCAN_DV:(h + 1) * SCAN_DV]
        ms = jnp.mean(sl * sl, axis=-1, keepdims=True)
        parts.append(sl * lax.rsqrt(ms + NORM_EPS))
    return jnp.concatenate(parts, axis=1) * gain


def _merge_kernel(x_ref, gate_ref, na_ref, gf_ref, gb_ref, rf_ref, rb_ref, z0_ref, z1_ref, z2_ref,
                  g0_ref, g1_ref, g2_ref, gg_ref, rg_ref, wb_ref, wo_ref, o_ref):
    outs = (
        na_ref[...].astype(F32),
        _head_rms(gf_ref[...].astype(F32) + gb_ref[...].astype(F32), gg_ref[...]),
        _head_rms(rf_ref[...].astype(F32) + rb_ref[...].astype(F32), rg_ref[...]),
    )
    zs = (z0_ref, z1_ref, z2_ref)
    gs = (g0_ref, g1_ref, g2_ref)
    y = None
    for i in range(N_BRANCH):
        u = (outs[i] * _silu(zs[i][...].astype(F32))).astype(BF16)
        t = _sigmoid(gs[i][...].astype(F32)) * _dot(u, wb_ref[i])
        y = t if y is None else y + t
    r = _dot(y.astype(BF16), wo_ref[...])
    o_ref[...] = x_ref[...] + gate_ref[...] * r


def _merge(x2d, gate, na_o, gla_f, gla_b, ret_f, ret_b, p, gla_gain, ret_gain, wb, wo, *, tm,
           rows_per_batch):
    t, d = x2d.shape
    blocks_per_batch = rows_per_batch // tm
    nb = gate.shape[0]
    bidx = (lambda i: i // blocks_per_batch) if nb > 1 else (lambda i: 0)
    row = lambda w: pl.BlockSpec((tm, w), lambda i: (i, 0))
    zspec = lambda n: pl.BlockSpec((tm, SEC), lambda i: (i, SEC_Z0 + n))
    gspec = lambda n: pl.BlockSpec((tm, d), lambda i: (i, SEC_G0 * SEC // d + n))
    return pl.pallas_call(
        _merge_kernel,
        grid=(t // tm,),
        in_specs=[
            row(d),
            pl.BlockSpec((None, 1, d), lambda i: (bidx(i), 0, 0)),
            row(BRANCH_W), row(BRANCH_W), row(BRANCH_W), row(BRANCH_W), row(BRANCH_W),
            zspec(0), zspec(1), zspec(2), gspec(0), gspec(1), gspec(2),
            pl.BlockSpec((1, BRANCH_W), lambda i: (0, 0)),
            pl.BlockSpec((1, BRANCH_W), lambda i: (0, 0)),
            pl.BlockSpec((N_BRANCH, BRANCH_W, d), lambda i: (0, 0, 0)),
            pl.BlockSpec((d, d), lambda i: (0, 0)),
        ],
        out_specs=row(d),
        out_shape=jax.ShapeDtypeStruct((t, d), F32),
        compiler_params=_params("parallel"),
        name="merge",
    )(x2d, gate, na_o, gla_f, gla_b, ret_f, ret_b, p, p, p, p, p, p, gla_gain, ret_gain, wb, wo)


def _in_splits():
    na = NA_HEADS * NA_HEAD_DIM
    qk = SCAN_HEADS * SCAN_DK
    vv = SCAN_HEADS * SCAN_DV
    sizes = (na, na, na, qk, qk, vv, 2 * GLA_GATE_RANK, qk, qk, vv, N_BRANCH * BRANCH_W,
             N_BRANCH * D_MODEL)
    return np.concatenate([[0], np.cumsum(sizes)])


def _prep_layer_weights(w_in, q_norm, k_norm, w_gate, b_gate):
    o = _in_splits()
    g0, g1 = int(o[6]), int(o[7])
    w = jnp.concatenate([w_in[:, :g0], w_in[:, g1:]], axis=1).astype(BF16)
    wg = jnp.pad(w_in[:, g0:g1], ((0, 0), (0, GATE_PAD - (g1 - g0)))).astype(BF16)
    hk = SCAN_HEADS * SCAN_DK
    wgate = jnp.zeros((GATE_PAD, 2 * hk), F32)
    wgate = wgate.at[:GLA_GATE_RANK, :hk].set(w_gate[0])
    wgate = wgate.at[GLA_GATE_RANK:2 * GLA_GATE_RANK, hk:].set(w_gate[1])
    bgate = jnp.concatenate([b_gate[0], b_gate[1]]).reshape(1, 2 * hk)
    colscale = np.ones((N_SEC, 1, SEC), np.float32)
    colscale[SEC_NA_Q] = NA_HEAD_DIM ** -0.5
    colscale[SEC_GLA_QK, 0, :hk] = SCAN_DK ** -0.5
    colscale[SEC_RET_QK, 0, :hk] = SCAN_DK ** -0.5
    colscale = jnp.asarray(colscale)
    reps = SEC // NA_HEAD_DIM
    colscale = colscale.at[SEC_NA_Q, 0].multiply(jnp.tile(q_norm, reps))
    colscale = colscale.at[SEC_NA_K, 0].multiply(jnp.tile(k_norm, reps))
    grp = np.arange(SEC) // NA_HEAD_DIM
    gmat = jnp.asarray((grp[:, None] == grp[None, :]).astype(np.float32)).astype(BF16)
    return dict(w=w, wg=wg, wgate=wgate.astype(BF16), bgate=bgate, colscale=colscale, gmat=gmat)


def _tables(seq, ctx_len):
    i = np.arange(TRI_ROWS)
    same = (i[:, None] // CHUNK) == (i[None, :] // CHUNK)
    ltri = jnp.asarray((same & (i[None, :] <= i[:, None])).astype(np.float32)).astype(BF16)
    utri = jnp.asarray((same & (i[None, :] >= i[:, None])).astype(np.float32)).astype(BF16)
    half = SCAN_DK // 2
    quarter = half // 2
    pos = jnp.arange(seq)
    row = (pos // GRID_W).astype(F32)
    col = (pos % GRID_W).astype(F32)
    inv = ROPE_BASE ** (-jnp.arange(quarter, dtype=F32) / quarter)
    ang_r = row[:, None] * inv[None, :]
    ang_c = col[:, None] * inv[None, :]
    zero = jnp.zeros_like(ang_r)
    cos = jnp.concatenate([jnp.cos(ang_r)] * 2 + [jnp.cos(ang_c)] * 2, axis=1)
    sa = jnp.concatenate([-jnp.sin(ang_r), zero, -jnp.sin(ang_c), zero], axis=1)
    sb = jnp.concatenate([zero, jnp.sin(ang_r), zero, jnp.sin(ang_c)], axis=1)
    two = lambda a: jnp.concatenate([a, a], axis=1)
    lat = dict(ltri=ltri, utri=utri, cos=two(cos), sa=two(sa), sb=two(sb))
    dummy = jnp.zeros((ctx_len, 2 * SCAN_DK), F32)
    ctx = dict(ltri=ltri, utri=utri, cos=dummy, sa=dummy, sb=dummy)
    return lat, ctx


def _na_bias_tables(rpb, rows):
    n_qb = rows // NA_QROWS
    kr = min(NA_WIN_ROWS, rows)
    heads = rpb.shape[0]
    c = np.arange(GRID_W)
    cs = np.clip(c - NA_WIN_COLS // 2, 0, GRID_W - NA_WIN_COLS)
    kc = np.arange(GRID_W)
    col_ok = (kc[None, :] >= cs[:, None]) & (kc[None, :] < cs[:, None] + NA_WIN_COLS)
    rp = jnp.pad(rpb, ((0, 0), (0, 0), (GRID_W, GRID_W)))
    off = GRID_W + NA_WIN_COLS - 1
    toep = jnp.stack([rp[:, :, off - ci:off - ci + GRID_W] for ci in range(GRID_W)], axis=2)
    toep = jnp.where(jnp.asarray(col_ok)[None, None], toep, NEG_BIG)
    masked = jnp.full((heads, GRID_W, GRID_W), NEG_BIG, F32)
    tabs = []
    for qb in (0, min(1, n_qb - 1), n_qb - 1):
        t0 = int(np.clip(qb - 1, 0, n_qb - NA_KROWS // NA_QROWS))
        strips = []
        for i in range(NA_QROWS):
            r = qb * NA_QROWS + i
            rs = int(np.clip(r - kr // 2, 0, rows - kr))
            blocks = []
            for j in range(NA_KROWS):
                krow = t0 * NA_QROWS + j
                ok = rs <= krow < rs + kr
                blocks.append(toep[:, krow - r + NA_WIN_ROWS - 1] if ok else masked)
            strips.append(jnp.concatenate(blocks, axis=2))
        tabs.append(jnp.concatenate(strips, axis=1))
    return jnp.stack(tabs)


def _log_gamma(offset):
    g = jnp.log1p(-jnp.exp2(-(offset + jnp.arange(SCAN_HEADS, dtype=F32))))
    return jnp.repeat(g, SCAN_DK)


def kernel(x, c, ctx, c_ctx, w_mod, b_mod, norm_w, w_in, na_q_norm, na_k_norm, na_rpb, gla_w_gate,
           gla_b_gate, gla_out_norm, ret_out_norm, w_branch, w_out):
    batch, seq, d = x.shape
    ctx_len = ctx.shape[1]
    depth = w_mod.shape[0]
    rows = seq // GRID_W

    c_rows = jnp.zeros((8, d), F32).at[:batch].set(c).at[batch].set(c_ctx)
    mod = _modulation(c_rows, w_mod, b_mod)
    lat_tabs, ctx_tabs = _tables(seq, ctx_len)
    lg = jnp.stack([_log_gamma(RET_DECAY_FWD), _log_gamma(RET_DECAY_BWD)])

    xl = x.reshape(batch * seq, d)
    xc = ctx.reshape(batch * ctx_len, d)
    for layer in range(depth):
        with_ctx = layer < depth - 1
        m = mod[layer]
        shift, scale, gate = (m[:batch, i * d:(i + 1) * d].reshape(batch, 1, d) for i in range(3))
        shift_c, scale_c, gate_c = (m[batch:batch + 1, i * d:(i + 1) * d].reshape(1, 1, d) for i in range(3))
        wts = _prep_layer_weights(w_in[layer], na_q_norm[layer], na_k_norm[layer],
                                  gla_w_gate[layer], gla_b_gate[layer])
        nw = norm_w[layer].reshape(1, d)

        p_lat, cum_lat = _in_proj(xl, nw, scale, shift, wts, lat_tabs, tm=1024,
                                  rows_per_batch=seq, rope=True)
        p_ctx, cum_ctx = _in_proj(xc, nw, scale_c, shift_c, wts, ctx_tabs, tm=ctx_len,
                                  rows_per_batch=ctx_len, rope=False)

        bias = _na_bias_tables(na_rpb[layer], rows)
        na_l = _na_attention(p_lat, p_ctx, bias, batch=batch, seq=seq, ctx_len=ctx_len)

        gla_cf, gla_cb, gla_s = _scan(p_ctx, cum_ctx, None, None, gla=True, batch=batch,
                                      rows_per_batch=ctx_len, tb=ctx_len)
        gla_f, gla_b, _ = _scan(p_lat, cum_lat, None, gla_s, gla=True, batch=batch,
                                rows_per_batch=seq, tb=256)
        ret_cf, ret_cb, ret_s = _scan(p_ctx, None, lg, None, gla=False, batch=batch,
                                      rows_per_batch=ctx_len, tb=ctx_len)
        ret_f, ret_b, _ = _scan(p_lat, None, lg, ret_s, gla=False, batch=batch,
                                rows_per_batch=seq, tb=256)

        wb = w_branch[layer].astype(BF16)
        wo = w_out[layer].astype(BF16)
        gg = gla_out_norm[layer].reshape(1, BRANCH_W)
        rg = ret_out_norm[layer].reshape(1, BRANCH_W)
        xl_new = _merge(xl, gate, na_l, gla_f, gla_b, ret_f, ret_b, p_lat, gg, rg, wb, wo, tm=512,
                        rows_per_batch=seq)
        if with_ctx:
            na_c = _ctx_attention(p_ctx, batch=batch, ctx_len=ctx_len)
            xc = _merge(xc, gate_c, na_c, gla_cf, gla_cb, ret_cf, ret_cb, p_ctx, gg, rg, wb, wo,
                        tm=ctx_len, rows_per_batch=ctx_len)
        xl = xl_new
    return xl.reshape(batch, seq, d)
```

```python
import functools

import numpy as np
import jax
import jax.numpy as jnp
from jax import lax
from jax.experimental import pallas as pl
from jax.experimental.pallas import tpu as pltpu

D_MODEL = 1024
GRID_W = 64
NORM_EPS = 1e-6
N_BRANCH = 3
BRANCH_W = D_MODEL // 2
NA_HEAD_DIM = 64
NA_HEADS = BRANCH_W // NA_HEAD_DIM
NA_WIN_ROWS = 8
NA_WIN_COLS = 16
SCAN_HEADS = 4
SCAN_DV = BRANCH_W // SCAN_HEADS
SCAN_DK = SCAN_DV // 2
GLA_GATE_RANK = 16
GLA_GATE_TAU = 16.0
RET_DECAY_FWD = 5.0
RET_DECAY_BWD = 5.5
CHUNK = 64
ROPE_BASE = 10000.0

LANES = 128
MXU_DIM = 256

SEC = 512
SEC_NA_Q, SEC_NA_K, SEC_NA_V, SEC_GLA_QK, SEC_GLA_V, SEC_RET_QK, SEC_RET_V = range(7)
N_SEC = 7
GATE_PAD = LANES
TRI_ROWS = MXU_DIM
NA_QROWS = 4
NA_KROWS = 12
NEG_BIG = -1e30

VMEM_LIMIT = 56 * 1024 * 1024

F32 = jnp.float32
BF16 = jnp.bfloat16


def _dot(a, b):
    return jnp.dot(a, b, preferred_element_type=F32)


def _dot_nt(a, b):
    return lax.dot_general(a, b, (((1,), (1,)), ((), ())), preferred_element_type=F32)


def _dot_tn(a, b):
    return lax.dot_general(a, b, (((0,), (0,)), ((), ())), preferred_element_type=F32)


def _split2(x):
    hi = x.astype(BF16)
    lo = (x - hi.astype(F32)).astype(BF16)
    return hi, lo


def _log_sigmoid(x):
    return jnp.minimum(x, 0.0) - jnp.log(1.0 + jnp.exp(-jnp.abs(x)))


def _sigmoid(x):
    return 1.0 / (1.0 + jnp.exp(-x))


def _silu(x):
    return x * _sigmoid(x)


def _params(*sem):
    return pltpu.CompilerParams(dimension_semantics=sem, vmem_limit_bytes=VMEM_LIMIT)


def _resident(block_shape, index_map):
    return pl.BlockSpec(block_shape, index_map, pipeline_mode=pl.Buffered(1))


def _ada_norm(x, nw, sc, sh):
    ms = jnp.mean(x * x, axis=-1, keepdims=True)
    return (x * lax.rsqrt(ms + NORM_EPS)) * (nw * (1.0 + sc)) + sh


def _mod_kernel(c_ref, w_ref, b_ref, o_ref):
    a = _silu(c_ref[...])
    o_ref[...] = jnp.dot(a, w_ref[...], preferred_element_type=F32,
                         precision=lax.Precision.HIGHEST) + b_ref[...]


def _modulation(c_rows, w_mod, b_mod):
    depth, d, d3 = w_mod.shape
    rows = c_rows.shape[0]
    tn = 1024
    return pl.pallas_call(
        _mod_kernel,
        grid=(depth, d3 // tn),
        in_specs=[
            pl.BlockSpec((rows, d), lambda l, j: (0, 0)),
            pl.BlockSpec((None, d, tn), lambda l, j: (l, 0, j)),
            pl.BlockSpec((None, 1, tn), lambda l, j: (l, 0, j)),
        ],
        out_specs=pl.BlockSpec((None, rows, tn), lambda l, j: (l, 0, j)),
        out_shape=jax.ShapeDtypeStruct((depth, rows, d3), F32),
        compiler_params=_params("parallel", "parallel"),
        name="modulation",
    )(c_rows, w_mod, b_mod.reshape(depth, 1, d3))


def _mod_specs(mod, layer, row_of, grid_rank):
    d = mod.shape[-1] // 3

    def spec(third):
        if grid_rank == 2:
            return pl.BlockSpec((None, None, 1, d), lambda i, j: (layer, row_of(i), 0, third))
        return pl.BlockSpec((None, None, 1, d), lambda i: (layer, row_of(i), 0, third))

    return spec(0), spec(1), spec(2)


def _in_proj_kernel(x_ref, nw_ref, sh_ref, sc_ref, w_ref, cs_ref, gm_ref, wg_ref, wgate_ref,
                    bgate_ref, ltri_ref, utri_ref, cos_ref, sa_ref, sb_ref,
                    p_ref, cum_ref, h_scr, *, rope):
    j = pl.program_id(1)
    tm = x_ref.shape[0]

    @pl.when(j == 0)
    def _():
        hb = _ada_norm(x_ref[...], nw_ref[...], sc_ref[...], sh_ref[...]).astype(BF16)
        h_scr[...] = hb
        g = _dot(hb, wg_ref[...])
        xg = _dot(g.astype(BF16), wgate_ref[...]) + bgate_ref[...]
        la = _log_sigmoid(xg) * (1.0 / GLA_GATE_TAU)
        half = la.shape[1] // 2
        for r in range(tm // TRI_ROWS):
            rows = slice(r * TRI_ROWS, (r + 1) * TRI_ROWS)
            fh, fl = _split2(la[rows, :half])
            bh, bl = _split2(la[rows, half:])
            cum_ref[rows, :half] = _dot(ltri_ref[...], fh) + _dot(ltri_ref[...], fl)
            cum_ref[rows, half:] = _dot(utri_ref[...], bh) + _dot(utri_ref[...], bl)

    raw = _dot(h_scr[...], w_ref[...])
    acc = raw * cs_ref[...]

    @pl.when(j <= SEC_NA_K)
    def _():
        parts = []
        for c in range(SEC // MXU_DIM):
            cols = slice(c * MXU_DIM, (c + 1) * MXU_DIM)
            hi, lo = _split2(raw[:, cols] * raw[:, cols])
            ss = _dot(hi, gm_ref[...]) + _dot(lo, gm_ref[...])
            parts.append(acc[:, cols] * lax.rsqrt(ss * (1.0 / NA_HEAD_DIM) + NORM_EPS))
        p_ref[...] = jnp.concatenate(parts, axis=1).astype(BF16)

    if rope:
        @pl.when(j == SEC_RET_QK)
        def _():
            reps = SEC // cos_ref.shape[1]
            cos = jnp.concatenate([cos_ref[...]] * reps, axis=1)
            sa = jnp.concatenate([sa_ref[...]] * reps, axis=1)
            sb = jnp.concatenate([sb_ref[...]] * reps, axis=1)
            quarter = SCAN_DK // 4
            up = pltpu.roll(acc, SEC - quarter, 1)
            dn = pltpu.roll(acc, quarter, 1)
            p_ref[...] = (acc * cos + up * sa + dn * sb).astype(BF16)

        plain = jnp.logical_and(j > SEC_NA_K, j != SEC_RET_QK)
    else:
        plain = j > SEC_NA_K

    @pl.when(plain)
    def _():
        p_ref[...] = acc.astype(BF16)


def _in_proj(x2d, mod, prm, tabs, *, layer, tm, row_of, blocks_per_batch, rope):
    t, d = x2d.shape
    const2 = lambda i, j: (0, 0)
    lconst = lambda i, j: (layer, 0, 0)
    row_tab = pl.BlockSpec((tm, tabs["cos"].shape[1]), lambda i, j: (i % blocks_per_batch, 0))
    sh_spec, sc_spec, _ = _mod_specs(mod, layer, row_of, 2)
    return pl.pallas_call(
        functools.partial(_in_proj_kernel, rope=rope),
        grid=(t // tm, N_SEC),
        in_specs=[
            pl.BlockSpec((tm, d), lambda i, j: (i, 0)),
            pl.BlockSpec((None, 1, d), lconst),
            sh_spec, sc_spec,
            pl.BlockSpec((None, d, SEC), lambda i, j: (layer, 0, j)),
            pl.BlockSpec((None, None, 1, SEC), lambda i, j: (layer, j, 0, 0)),
            _resident((MXU_DIM, MXU_DIM), const2),
            _resident((None, d, GATE_PAD), lconst),
            _resident((None, GATE_PAD, SEC), lconst),
            _resident((None, 1, SEC), lconst),
            _resident((TRI_ROWS, TRI_ROWS), const2),
            _resident((TRI_ROWS, TRI_ROWS), const2),
            row_tab, row_tab, row_tab,
        ],
        out_specs=[
            pl.BlockSpec((tm, SEC), lambda i, j: (i, j)),
            pl.BlockSpec((tm, SEC), lambda i, j: (i, 0)),
        ],
        out_shape=[
            jax.ShapeDtypeStruct((t, N_SEC * SEC), BF16),
            jax.ShapeDtypeStruct((t, SEC), F32),
        ],
        scratch_shapes=[pltpu.VMEM((tm, d), BF16)],
        compiler_params=_params("parallel", "arbitrary"),
        name="in_proj_rope" if rope else "in_proj",
    )(x2d, prm["nw"], mod, mod, prm["wproj"], prm["colscale"], tabs["gmat"], prm["wg"],
      prm["wgate"], prm["bgate"], tabs["ltri"], tabs["utri"], tabs["cos"], tabs["sa"], tabs["sb"])


def _na_block_plan(rows):
    n_qb = rows // NA_QROWS
    kr = min(NA_WIN_ROWS, rows)
    plans = []
    for qb in (0, min(1, n_qb - 1), n_qb - 1):
        t0 = int(np.clip(qb - 1, 0, n_qb - NA_KROWS // NA_QROWS))
        plan = []
        for i in range(NA_QROWS):
            r = qb * NA_QROWS + i
            rs = int(np.clip(r - kr // 2, 0, rows - kr))
            plan.append([(t0 * NA_QROWS + j) - r + NA_WIN_ROWS - 1
                         if rs <= t0 * NA_QROWS + j < rs + kr else None
                         for j in range(NA_KROWS)])
        plans.append(plan)
    return plans


def _bias_kernel(rpb_ref, o_ref, *, plans):
    kind = pl.program_id(1)
    shape = (GRID_W, LANES)
    lane = lax.broadcasted_iota(jnp.int32, shape, 1)
    c = lax.broadcasted_iota(jnp.int32, shape, 0)
    kc = lane & (GRID_W - 1)
    cs = jnp.clip(c - NA_WIN_COLS // 2, 0, GRID_W - NA_WIN_COLS)
    col_ok = jnp.logical_and(kc >= cs, kc < cs + NA_WIN_COLS)
    low = lane < GRID_W
    neg = jnp.full(shape, NEG_BIG, F32)
    toep = []
    for a in range(2 * NA_WIN_ROWS - 1):
        x = jnp.broadcast_to(rpb_ref[a:a + 1, :], shape)
        lo = pltpu.roll(x, LANES - (NA_WIN_COLS - 1), 1, stride=1, stride_axis=0)
        hi = pltpu.roll(x, GRID_W - (NA_WIN_COLS - 1), 1, stride=1, stride_axis=0)
        toep.append(jnp.where(col_ok, jnp.where(low, lo, hi), neg))
    for k, plan in enumerate(plans):
        @pl.when(kind == k)
        def _(plan=plan):
            for i in range(NA_QROWS):
                for jp in range(NA_KROWS // 2):
                    a0, a1 = plan[i][2 * jp], plan[i][2 * jp + 1]
                    b0 = neg if a0 is None else toep[a0]
                    b1 = neg if a1 is None else toep[a1]
                    o_ref[i * GRID_W:(i + 1) * GRID_W, jp * LANES:(jp + 1) * LANES] = (
                        jnp.where(low, b0, b1))


def _na_bias_tables(rpb, rows):
    depth, heads, nr, nc = rpb.shape
    rp = jnp.pad(rpb, ((0, 0), (0, 0), (0, 16 - nr), (0, LANES - nc)))
    tq, tk = NA_QROWS * GRID_W, NA_KROWS * GRID_W
    return pl.pallas_call(
        functools.partial(_bias_kernel, plans=_na_block_plan(rows)),
        grid=(depth, 3, heads),
        in_specs=[pl.BlockSpec((None, None, 16, LANES), lambda l, k, h: (l, h, 0, 0))],
        out_specs=pl.BlockSpec((None, None, None, tq, tk), lambda l, k, h: (l, k, h, 0, 0)),
        out_shape=jax.ShapeDtypeStruct((depth, 3, heads, tq, tk), F32),
        compiler_params=_params("parallel", "parallel", "parallel"),
        name="na_bias",
    )(rp)


def _lane_mask(width, lo, hi):
    lane = lax.broadcasted_iota(jnp.int32, (1, width), 1)
    return jnp.logical_and(lane >= lo, lane < hi)


def _na_kernel(q_ref, k_ref, v_ref, kc_ref, vc_ref, bias_ref, o_ref, *, n_qb):
    qb = pl.program_id(2)
    tq = q_ref.shape[0]
    tk = bias_ref.shape[-1]
    t0 = jnp.clip(qb - 1, 0, n_qb - tk // tq)
    start = pl.multiple_of(t0 * tq, tq)
    q = q_ref[...]
    k3 = k_ref[pl.ds(start, tk), :]
    v3 = v_ref[pl.ds(start, tk), :]
    kc = kc_ref[...]
    vc = vc_ref[...]
    heads = q.shape[1] // NA_HEAD_DIM
    acc = jnp.zeros(q.shape, F32)
    for h in range(heads):
        mask = _lane_mask(q.shape[1], h * NA_HEAD_DIM, (h + 1) * NA_HEAD_DIM).astype(BF16)
        qh = q * mask
        s = _dot_nt(qh, k3) + bias_ref[h]
        sc = _dot_nt(qh, kc)
        m = jnp.maximum(jnp.max(s, axis=-1, keepdims=True), jnp.max(sc, axis=-1, keepdims=True))
        p = jnp.exp(s - m)
        pc = jnp.exp(sc - m)
        l = jnp.sum(p, axis=-1, keepdims=True) + jnp.sum(pc, axis=-1, keepdims=True)
        oh = _dot(p.astype(BF16), v3 * mask) + _dot(pc.astype(BF16), vc * mask)
        acc = acc + oh * (1.0 / l)
    o_ref[...] = acc.astype(BF16)


def _na_attention(p_lat, p_ctx, bias, *, layer, batch, seq, ctx_len):
    tq = NA_QROWS * GRID_W
    tk = NA_KROWS * GRID_W
    n_qb = seq // tq
    hw = 2 * LANES
    n_hg = BRANCH_W // hw
    sec_blk = SEC // hw
    heads = hw // NA_HEAD_DIM
    kind = lambda i: jnp.where(i == 0, 0, jnp.where(i == n_qb - 1, 2, 1))
    return pl.pallas_call(
        functools.partial(_na_kernel, n_qb=n_qb),
        grid=(batch, n_hg, n_qb),
        in_specs=[
            pl.BlockSpec((tq, hw), lambda b, g, i: (b * n_qb + i, SEC_NA_Q * sec_blk + g)),
            pl.BlockSpec((seq, hw), lambda b, g, i: (b, SEC_NA_K * sec_blk + g)),
            pl.BlockSpec((seq, hw), lambda b, g, i: (b, SEC_NA_V * sec_blk + g)),
            pl.BlockSpec((ctx_len, hw), lambda b, g, i: (b, SEC_NA_K * sec_blk + g)),
            pl.BlockSpec((ctx_len, hw), lambda b, g, i: (b, SEC_NA_V * sec_blk + g)),
            pl.BlockSpec((None, None, heads, tq, tk), lambda b, g, i: (layer, kind(i), g, 0, 0)),
        ],
        out_specs=pl.BlockSpec((tq, hw), lambda b, g, i: (b * n_qb + i, g)),
        out_shape=jax.ShapeDtypeStruct((batch * seq, BRANCH_W), BF16),
        compiler_params=_params("parallel", "parallel", "arbitrary"),
        name="na_attention",
    )(p_lat, p_lat, p_lat, p_ctx, p_ctx, bias)


def _ctx_attn_kernel(q_ref, k_ref, v_ref, o_ref):
    first = _lane_mask(LANES, 0, NA_HEAD_DIM)
    sel = (first.astype(BF16), jnp.logical_not(first).astype(BF16))
    outs = []
    for pair in range(q_ref.shape[1] // LANES):
        lanes = slice(pair * LANES, (pair + 1) * LANES)
        q2, k2, v2 = q_ref[:, lanes], k_ref[:, lanes], v_ref[:, lanes]
        o_pair = []
        for hh in range(2):
            s = _dot_nt(q2 * sel[hh], k2)
            m = jnp.max(s, axis=-1, keepdims=True)
            p = jnp.exp(s - m)
            l = jnp.sum(p, axis=-1, keepdims=True)
            o_pair.append(_dot(p.astype(BF16), v2) * (1.0 / l))
        outs.append(jnp.where(first, o_pair[0], o_pair[1]))
    o_ref[...] = jnp.concatenate(outs, axis=1).astype(BF16)


def _ctx_attention(p_ctx, *, batch, ctx_len):
    hw = 2 * LANES
    n_hg = BRANCH_W // hw
    sec_blk = SEC // hw
    spec = lambda sec: pl.BlockSpec((ctx_len, hw), lambda b, g: (b, sec * sec_blk + g))
    return pl.pallas_call(
        _ctx_attn_kernel,
        grid=(batch, n_hg),
        in_specs=[spec(SEC_NA_Q), spec(SEC_NA_K), spec(SEC_NA_V)],
        out_specs=pl.BlockSpec((ctx_len, hw), lambda b, g: (b, g)),
        out_shape=jax.ShapeDtypeStruct((batch * ctx_len, BRANCH_W), BF16),
        compiler_params=_params("parallel", "parallel"),
        name="ctx_attention",
    )(p_ctx, p_ctx, p_ctx)


def _log2(n):
    assert n & (n - 1) == 0
    return n.bit_length() - 1


def _block_mask(shape, row_group, col_group):
    r = lax.shift_right_logical(lax.broadcasted_iota(jnp.int32, shape, 0), _log2(row_group))
    c = lax.shift_right_logical(lax.broadcasted_iota(jnp.int32, shape, 1), _log2(col_group))
    return r == c


def _decay_factors(cum, ref_row, total_row):
    cm = cum[ref_row:ref_row + 1, :]
    tot = cum[total_row:total_row + 1, :]
    eq = jnp.exp(cum - cm)
    ek = jnp.exp(cm - cum)
    return eq, ek, jnp.exp(cm), jnp.exp(tot - cm), jnp.exp(tot)


def _chunk_step(q, k, v, factors, tri_wide, st_ref, masks):
    eq, ek, e_m, e_tm, e_t = factors
    kbd_mask, vbd_mask, st_mask = masks
    qt = q * eq
    kt = k * ek
    qhat = (qt * e_m).astype(BF16)
    khat = (kt * e_tm).astype(BF16)
    ktb = kt.astype(BF16)
    kbd = jnp.where(kbd_mask, jnp.concatenate([ktb] * SCAN_HEADS, axis=0), jnp.zeros((), BF16))
    vbd = jnp.where(vbd_mask, jnp.concatenate([v] * SCAN_HEADS, axis=0), jnp.zeros((), BF16))
    att = _dot_nt(qt.astype(BF16), kbd)
    att = jnp.where(tri_wide, att, 0.0).astype(BF16)
    st = st_ref[...]
    o = _dot(att, vbd) + _dot_nt(qhat, st.astype(BF16))
    upd = _dot_tn(v, khat)
    st_ref[...] = st * e_t + jnp.where(st_mask, upd, 0.0)
    return o


def _scan_masks():
    hk = SCAN_HEADS * SCAN_DK
    hv = SCAN_HEADS * SCAN_DV
    hc = SCAN_HEADS * CHUNK
    kbd_mask = _block_mask((hc, hk), CHUNK, SCAN_DK)
    vbd_mask = _block_mask((hc, hv), CHUNK, SCAN_DV)
    st_mask = _block_mask((hv, hk), SCAN_DV, SCAN_DK)
    t = lax.broadcasted_iota(jnp.int32, (CHUNK, hc), 0)
    s = lax.broadcasted_iota(jnp.int32, (CHUNK, hc), 1) & (CHUNK - 1)
    return (kbd_mask, vbd_mask, st_mask), s <= t, s >= t


def _ret_cum(lg_row, forward):
    t = lax.broadcasted_iota(jnp.int32, (CHUNK, lg_row.shape[1]), 0).astype(F32)
    steps = (t + 1.0) if forward else (float(CHUNK) - t)
    return steps * lg_row


def _scan_kernel(*refs, gla, n_chunks, from_zero):
    if gla:
        (qkf_ref, vf_ref, cf_ref, qkb_ref, vb_ref, cb_ref, s0_ref,
         of_ref, ob_ref, sfin_ref, stf, stb) = refs
    else:
        (qkf_ref, vf_ref, qkb_ref, vb_ref, lg_ref, s0_ref,
         of_ref, ob_ref, sfin_ref, stf, stb) = refs
    step = pl.program_id(1)
    hk = SCAN_HEADS * SCAN_DK

    @pl.when(step == 0)
    def _():
        if from_zero:
            stf[...] = jnp.zeros_like(stf)
            stb[...] = jnp.zeros_like(stb)
        else:
            stf[...] = s0_ref[0]
            stb[...] = s0_ref[1]

    masks, tril, triu = _scan_masks()
    if not gla:
        fac_f = _decay_factors(_ret_cum(lg_ref[0:1, :], True), CHUNK // 2 - 1, CHUNK - 1)
        fac_b = _decay_factors(_ret_cum(lg_ref[1:2, :], False), CHUNK // 2, 0)

    for c in range(n_chunks):
        rf = slice(c * CHUNK, (c + 1) * CHUNK)
        cb = n_chunks - 1 - c
        rb = slice(cb * CHUNK, (cb + 1) * CHUNK)
        if gla:
            fac_f = _decay_factors(cf_ref[rf, :], CHUNK // 2 - 1, CHUNK - 1)
            fac_b = _decay_factors(cb_ref[rb, :], CHUNK // 2, 0)
        qkf = qkf_ref[rf, :].astype(F32)
        of_ref[rf, :] = _chunk_step(qkf[:, :hk], qkf[:, hk:], vf_ref[rf, :], fac_f, tril, stf,
                                    masks).astype(BF16)
        qkb = qkb_ref[rb, :].astype(F32)
        ob_ref[rb, :] = _chunk_step(qkb[:, :hk], qkb[:, hk:], vb_ref[rb, :], fac_b, triu, stb,
                                    masks).astype(BF16)

    @pl.when(step == pl.num_programs(1) - 1)
    def _():
        sfin_ref[0] = stf[...]
        sfin_ref[1] = stb[...]


def _scan(p, cum, lg, s0, *, gla, batch, rows_per_batch, tb):
    nb = rows_per_batch // tb
    hk = SCAN_HEADS * SCAN_DK
    hv = SCAN_HEADS * SCAN_DV
    sec_qk = SEC_GLA_QK if gla else SEC_RET_QK
    sec_v = SEC_GLA_V if gla else SEC_RET_V
    fwd = lambda b, s: b * nb + s
    bwd = lambda b, s: b * nb + nb - 1 - s
    from_zero = s0 is None
    if from_zero:
        s0 = jnp.zeros((batch, 2, hv, hk), F32)
    in_specs = [
        pl.BlockSpec((tb, SEC), lambda b, s: (fwd(b, s), sec_qk)),
        pl.BlockSpec((tb, SEC), lambda b, s: (fwd(b, s), sec_v)),
    ]
    args = [p, p]
    if gla:
        in_specs.append(pl.BlockSpec((tb, hk), lambda b, s: (fwd(b, s), 0)))
        args.append(cum)
    in_specs += [
        pl.BlockSpec((tb, SEC), lambda b, s: (bwd(b, s), sec_qk)),
        pl.BlockSpec((tb, SEC), lambda b, s: (bwd(b, s), sec_v)),
    ]
    args += [p, p]
    if gla:
        in_specs.append(pl.BlockSpec((tb, hk), lambda b, s: (bwd(b, s), 1)))
        args.append(cum)
    else:
        in_specs.append(pl.BlockSpec((2, hk), lambda b, s: (0, 0)))
        args.append(lg)
    in_specs.append(pl.BlockSpec((None, 2, hv, hk), lambda b, s: (b, 0, 0, 0)))
    args.append(s0)
    return pl.pallas_call(
        functools.partial(_scan_kernel, gla=gla, n_chunks=tb // CHUNK, from_zero=from_zero),
        grid=(batch, nb),
        in_specs=in_specs,
        out_specs=[
            pl.BlockSpec((tb, hv), lambda b, s: (fwd(b, s), 0)),
            pl.BlockSpec((tb, hv), lambda b, s: (bwd(b, s), 0)),
            pl.BlockSpec((None, 2, hv, hk), lambda b, s: (b, 0, 0, 0)),
        ],
        out_shape=[
            jax.ShapeDtypeStruct((batch * rows_per_batch, hv), BF16),
            jax.ShapeDtypeStruct((batch * rows_per_batch, hv), BF16),
            jax.ShapeDtypeStruct((batch, 2, hv, hk), F32),
        ],
        scratch_shapes=[pltpu.VMEM((hv, hk), F32), pltpu.VMEM((hv, hk), F32)],
        compiler_params=_params("parallel", "arbitrary"),
        name=("gla" if gla else "ret") + ("_ctx_scan" if from_zero else "_scan"),
    )(*args)


def _head_rms(o, gain):
    parts = []
    for h in range(SCAN_HEADS):
        sl = o[:, h * SCAN_DV:(h + 1) * SCAN_DV]
        ms = jnp.mean(sl * sl, axis=-1, keepdims=True)
        parts.append(sl * lax.rsqrt(ms + NORM_EPS))
    return jnp.concatenate(parts, axis=1) * gain


def _merge_kernel(x_ref, nw_ref, sh_ref, sc_ref, gate_ref, na_ref, gf_ref, gb_ref, rf_ref, rb_ref,
                  gg_ref, rg_ref, wz_ref, wgt_ref, wb_ref, wo_ref, o_ref):
    x = x_ref[...]
    d = x.shape[1]
    hb = _ada_norm(x, nw_ref[...], sc_ref[...], sh_ref[...]).astype(BF16)
    outs = (
        na_ref[...].astype(F32),
        _head_rms(gf_ref[...].astype(F32) + gb_ref[...].astype(F32), gg_ref[...]),
        _head_rms(rf_ref[...].astype(F32) + rb_ref[...].astype(F32), rg_ref[...]),
    )
    y = None
    for i in range(N_BRANCH):
        z = _dot(hb, wz_ref[:, i * BRANCH_W:(i + 1) * BRANCH_W])
        u = (outs[i] * _silu(z)).astype(BF16)
        g = _dot(hb, wgt_ref[:, i * d:(i + 1) * d])
        t = _sigmoid(g) * _dot(u, wb_ref[i])
        y = t if y is None else y + t
    r = _dot(y.astype(BF16), wo_ref[...])
    o_ref[...] = x + gate_ref[...] * r


def _merge(x2d, mod, prm, na_o, gla_f, gla_b, ret_f, ret_b, *, layer, tm, row_of):
    t, d = x2d.shape
    row = lambda w: pl.BlockSpec((tm, w), lambda i: (i, 0))
    sh_spec, sc_spec, gate_spec = _mod_specs(mod, layer, row_of, 1)
    lconst3 = lambda i: (layer, 0, 0)
    return pl.pallas_call(
        _merge_kernel,
        grid=(t // tm,),
        in_specs=[
            row(d),
            pl.BlockSpec((None, 1, d), lconst3),
            sh_spec, sc_spec, gate_spec,
            row(BRANCH_W), row(BRANCH_W), row(BRANCH_W), row(BRANCH_W), row(BRANCH_W),
            pl.BlockSpec((None, 1, BRANCH_W), lconst3),
            pl.BlockSpec((None, 1, BRANCH_W), lconst3),
            _resident((None, d, N_BRANCH * BRANCH_W), lconst3),
            _resident((None, d, N_BRANCH * d), lconst3),
            _resident((None, N_BRANCH, BRANCH_W, d), lambda i: (layer, 0, 0, 0)),
            _resident((None, d, d), lconst3),
        ],
        out_specs=row(d),
        out_shape=jax.ShapeDtypeStruct((t, d), F32),
        compiler_params=_params("parallel"),
        name="merge",
    )(x2d, prm["nw"], mod, mod, mod, na_o, gla_f, gla_b, ret_f, ret_b, prm["gla_gain"],
      prm["ret_gain"], prm["wz"], prm["wgt"], prm["wb"], prm["wo"])


def _in_splits():
    na = NA_HEADS * NA_HEAD_DIM
    qk = SCAN_HEADS * SCAN_DK
    vv = SCAN_HEADS * SCAN_DV
    sizes = (na, na, na, qk, qk, vv, 2 * GLA_GATE_RANK, qk, qk, vv, N_BRANCH * BRANCH_W,
             N_BRANCH * D_MODEL)
    return [int(v) for v in np.concatenate([[0], np.cumsum(sizes)])]


def _prep_params(norm_w, w_in, q_norm, k_norm, w_gate, b_gate, gla_out_norm, ret_out_norm,
                 w_branch, w_out):
    depth, d, _ = w_in.shape
    o = _in_splits()
    g0, g1, z0, gt0 = o[6], o[7], o[10], o[11]
    hk = SCAN_HEADS * SCAN_DK
    wproj = jnp.concatenate([w_in[:, :, :g0], w_in[:, :, g1:z0]], axis=2).astype(BF16)
    wg = jnp.pad(w_in[:, :, g0:g1], ((0, 0), (0, 0), (0, GATE_PAD - (g1 - g0)))).astype(BF16)
    wgate = jnp.zeros((depth, GATE_PAD, 2 * hk), F32)
    wgate = wgate.at[:, :GLA_GATE_RANK, :hk].set(w_gate[:, 0])
    wgate = wgate.at[:, GLA_GATE_RANK:2 * GLA_GATE_RANK, hk:].set(w_gate[:, 1])
    bgate = b_gate.reshape(depth, 1, 2 * hk)
    base = np.ones((N_SEC, 1, SEC), np.float32)
    base[SEC_NA_Q] = NA_HEAD_DIM ** -0.5
    base[SEC_GLA_QK, 0, :hk] = SCAN_DK ** -0.5
    base[SEC_RET_QK, 0, :hk] = SCAN_DK ** -0.5
    reps = SEC // NA_HEAD_DIM
    ones = jnp.ones((depth, N_SEC - 2, 1, SEC), F32)
    gains = jnp.concatenate([jnp.tile(q_norm, (1, reps))[:, None, None, :],
                             jnp.tile(k_norm, (1, reps))[:, None, None, :], ones], axis=1)
    return dict(
        nw=norm_w.reshape(depth, 1, d),
        wproj=wproj, wg=wg, wgate=wgate.astype(BF16), bgate=bgate,
        colscale=gains * jnp.asarray(base)[None],
        wz=w_in[:, :, z0:gt0].astype(BF16), wgt=w_in[:, :, gt0:].astype(BF16),
        wb=w_branch.astype(BF16), wo=w_out.astype(BF16),
        gla_gain=gla_out_norm.reshape(depth, 1, BRANCH_W),
        ret_gain=ret_out_norm.reshape(depth, 1, BRANCH_W),
    )


def _tables(seq, ctx_len):
    i = np.arange(TRI_ROWS)
    same = (i[:, None] // CHUNK) == (i[None, :] // CHUNK)
    ltri = jnp.asarray((same & (i[None, :] <= i[:, None])).astype(np.float32)).astype(BF16)
    utri = jnp.asarray((same & (i[None, :] >= i[:, None])).astype(np.float32)).astype(BF16)
    grp = np.arange(MXU_DIM) // NA_HEAD_DIM
    gmat = jnp.asarray((grp[:, None] == grp[None, :]).astype(np.float32)).astype(BF16)
    quarter = SCAN_DK // 4
    n_rows = seq // GRID_W
    inv = ROPE_BASE ** (-jnp.arange(quarter, dtype=F32) / quarter)
    ang_r = jnp.arange(n_rows, dtype=F32)[:, None] * inv[None, :]
    ang_c = jnp.arange(GRID_W, dtype=F32)[:, None] * inv[None, :]
    by_row = lambda a: jnp.broadcast_to(a[:, None, :], (n_rows, GRID_W, quarter)).reshape(seq, quarter)
    by_col = lambda a: jnp.broadcast_to(a[None, :, :], (n_rows, GRID_W, quarter)).reshape(seq, quarter)
    cr, sr = by_row(jnp.cos(ang_r)), by_row(jnp.sin(ang_r))
    cc, sc = by_col(jnp.cos(ang_c)), by_col(jnp.sin(ang_c))
    zero = jnp.zeros_like(cr)
    cos = jnp.concatenate([cr, cr, cc, cc] * 2, axis=1)
    sa = jnp.concatenate([-sr, zero, -sc, zero] * 2, axis=1)
    sb = jnp.concatenate([zero, sr, zero, sc] * 2, axis=1)
    lat = dict(ltri=ltri, utri=utri, gmat=gmat, cos=cos, sa=sa, sb=sb)
    dummy = jnp.zeros((ctx_len, 2 * SCAN_DK), F32)
    ctx = dict(ltri=ltri, utri=utri, gmat=gmat, cos=dummy, sa=dummy, sb=dummy)
    return lat, ctx


def _log_gamma(offset):
    g = jnp.log1p(-jnp.exp2(-(offset + jnp.arange(SCAN_HEADS, dtype=F32))))
    return jnp.repeat(g, SCAN_DK)


def kernel(x, c, ctx, c_ctx, w_mod, b_mod, norm_w, w_in, na_q_norm, na_k_norm, na_rpb, gla_w_gate,
           gla_b_gate, gla_out_norm, ret_out_norm, w_branch, w_out):
    batch, seq, d = x.shape
    ctx_len = ctx.shape[1]
    depth = w_mod.shape[0]
    rows = seq // GRID_W
    tm_proj, tm_merge, tb_scan = 1024, 512, 256

    c_rows = jnp.zeros((8, d), F32).at[:batch].set(c).at[batch].set(c_ctx)
    mod = _modulation(c_rows, w_mod, b_mod).reshape(depth, 8, 1, 3 * d)
    prm = _prep_params(norm_w, w_in, na_q_norm, na_k_norm, gla_w_gate, gla_b_gate, gla_out_norm,
                       ret_out_norm, w_branch, w_out)
    lat_tabs, ctx_tabs = _tables(seq, ctx_len)
    bias = _na_bias_tables(na_rpb, rows)
    lg = jnp.stack([_log_gamma(RET_DECAY_FWD), _log_gamma(RET_DECAY_BWD)])
    ctx_row = lambda i: batch

    xl = x.reshape(batch * seq, d)
    xc = ctx.reshape(batch * ctx_len, d)
    for layer in range(depth):
        with_ctx = layer < depth - 1
        p_lat, cum_lat = _in_proj(xl, mod, prm, lat_tabs, layer=layer, tm=tm_proj,
                                  row_of=lambda i: i // (seq // tm_proj),
                                  blocks_per_batch=seq // tm_proj, rope=True)
        p_ctx, cum_ctx = _in_proj(xc, mod, prm, ctx_tabs, layer=layer, tm=ctx_len, row_of=ctx_row,
                                  blocks_per_batch=1, rope=False)

        na_l = _na_attention(p_lat, p_ctx, bias, layer=layer, batch=batch, seq=seq, ctx_len=ctx_len)

        gla_cf, gla_cb, gla_s = _scan(p_ctx, cum_ctx, None, None, gla=True, batch=batch,
                                      rows_per_batch=ctx_len, tb=ctx_len)
        gla_f, gla_b, _ = _scan(p_lat, cum_lat, None, gla_s, gla=True, batch=batch,
                                rows_per_batch=seq, tb=tb_scan)
        ret_cf, ret_cb, ret_s = _scan(p_ctx, None, lg, None, gla=False, batch=batch,
                                      rows_per_batch=ctx_len, tb=ctx_len)
        ret_f, ret_b, _ = _scan(p_lat, None, lg, ret_s, gla=False, batch=batch,
                                rows_per_batch=seq, tb=tb_scan)

        xl_new = _merge(xl, mod, prm, na_l, gla_f, gla_b, ret_f, ret_b, layer=layer, tm=tm_merge,
                        row_of=lambda i: i // (seq // tm_merge))
        if with_ctx:
            na_c = _ctx_attention(p_ctx, batch=batch, ctx_len=ctx_len)
            xc = _merge(xc, mod, prm, na_c, gla_cf, gla_cb, ret_cf, ret_cb, layer=layer, tm=ctx_len,
                        row_of=ctx_row)
        xl = xl_new
    return xl.reshape(batch, seq, d)
```

```python
import functools

import numpy as np
import jax
import jax.numpy as jnp
from jax import lax
from jax.experimental import pallas as pl
from jax.experimental.pallas import tpu as pltpu

D_MODEL = 1024
GRID_W = 64
NORM_EPS = 1e-6
N_BRANCH = 3
BRANCH_W = D_MODEL // 2
NA_HEAD_DIM = 64
NA_HEADS = BRANCH_W // NA_HEAD_DIM
NA_WIN_ROWS = 8
NA_WIN_COLS = 16
SCAN_HEADS = 4
SCAN_DV = BRANCH_W // SCAN_HEADS
SCAN_DK = SCAN_DV // 2
GLA_GATE_RANK = 16
GLA_GATE_TAU = 16.0
RET_DECAY_FWD = 5.0
RET_DECAY_BWD = 5.5
CHUNK = 64
ROPE_BASE = 10000.0

LANES = 128
MXU_DIM = 256

SEC = 512
SEC_NA_Q, SEC_NA_K, SEC_NA_V, SEC_GLA_QK, SEC_GLA_V, SEC_RET_QK, SEC_RET_V = range(7)
N_SEC = 7
GATE_PAD = LANES
TRI_ROWS = MXU_DIM
NA_QROWS = 4
NA_KROWS = 12
NEG_BIG = -1e30

VMEM_LIMIT = 56 * 1024 * 1024

F32 = jnp.float32
BF16 = jnp.bfloat16


def _dot(a, b):
    return jnp.dot(a, b, preferred_element_type=F32)


def _dot_nt(a, b):
    return lax.dot_general(a, b, (((1,), (1,)), ((), ())), preferred_element_type=F32)


def _dot_tn(a, b):
    return lax.dot_general(a, b, (((0,), (0,)), ((), ())), preferred_element_type=F32)


def _split2(x):
    hi = x.astype(BF16)
    lo = (x - hi.astype(F32)).astype(BF16)
    return hi, lo


def _log_sigmoid(x):
    return jnp.minimum(x, 0.0) - jnp.log(1.0 + jnp.exp(-jnp.abs(x)))


def _sigmoid(x):
    return 1.0 / (1.0 + jnp.exp(-x))


def _silu(x):
    return x * _sigmoid(x)


def _params(*sem):
    return pltpu.CompilerParams(dimension_semantics=sem, vmem_limit_bytes=VMEM_LIMIT)


def _resident(block_shape, index_map):
    return pl.BlockSpec(block_shape, index_map, pipeline_mode=pl.Buffered(1))


def _ada_norm(x, nw, sc, sh):
    ms = jnp.mean(x * x, axis=-1, keepdims=True)
    return (x * lax.rsqrt(ms + NORM_EPS)) * (nw * (1.0 + sc)) + sh


def _mod_kernel(c_ref, w_ref, b_ref, o_ref):
    a = _silu(c_ref[...])
    o_ref[...] = jnp.dot(a, w_ref[...], preferred_element_type=F32,
                         precision=lax.Precision.HIGHEST) + b_ref[...]


def _modulation(c_rows, w_mod, b_mod):
    depth, d, d3 = w_mod.shape
    rows = c_rows.shape[0]
    tn = 1024
    return pl.pallas_call(
        _mod_kernel,
        grid=(depth, d3 // tn),
        in_specs=[
            pl.BlockSpec((rows, d), lambda l, j: (0, 0)),
            pl.BlockSpec((None, d, tn), lambda l, j: (l, 0, j)),
            pl.BlockSpec((None, 1, tn), lambda l, j: (l, 0, j)),
        ],
        out_specs=pl.BlockSpec((None, rows, tn), lambda l, j: (l, 0, j)),
        out_shape=jax.ShapeDtypeStruct((depth, rows, d3), F32),
        compiler_params=_params("parallel", "parallel"),
        name="modulation",
    )(c_rows, w_mod, b_mod.reshape(depth, 1, d3))


def _mod_specs(mod, layer, row_of, grid_rank):
    d = mod.shape[-1] // 3

    def spec(third):
        if grid_rank == 2:
            return pl.BlockSpec((None, None, 1, d), lambda i, j: (layer, row_of(i), 0, third))
        return pl.BlockSpec((None, None, 1, d), lambda i: (layer, row_of(i), 0, third))

    return spec(0), spec(1), spec(2)


def _in_proj_kernel(x_ref, nw_ref, sh_ref, sc_ref, w_ref, cs_ref, gm_ref, wg_ref, wgate_ref,
                    bgate_ref, ltri_ref, utri_ref, cos_ref, sin_ref,
                    p_ref, cum_ref, h_scr, *, rope):
    j = pl.program_id(1)
    tm = x_ref.shape[0]

    @pl.when(j == 0)
    def _():
        hb = _ada_norm(x_ref[...], nw_ref[...], sc_ref[...], sh_ref[...]).astype(BF16)
        h_scr[...] = hb
        g = _dot(hb, wg_ref[...])
        xg = _dot(g.astype(BF16), wgate_ref[...]) + bgate_ref[...]
        la = _log_sigmoid(xg) * (1.0 / GLA_GATE_TAU)
        half = la.shape[1] // 2
        for r in range(tm // TRI_ROWS):
            rows = slice(r * TRI_ROWS, (r + 1) * TRI_ROWS)
            fh, fl = _split2(la[rows, :half])
            bh, bl = _split2(la[rows, half:])
            cum_ref[rows, :half] = _dot(ltri_ref[...], fh) + _dot(ltri_ref[...], fl)
            cum_ref[rows, half:] = _dot(utri_ref[...], bh) + _dot(utri_ref[...], bl)

    raw = _dot(h_scr[...], w_ref[...])
    acc = raw * cs_ref[...]

    @pl.when(j <= SEC_NA_K)
    def _():
        for c in range(SEC // MXU_DIM):
            cols = slice(c * MXU_DIM, (c + 1) * MXU_DIM)
            ss = _dot((raw[:, cols] * raw[:, cols]).astype(BF16), gm_ref[...])
            p_ref[:, cols] = (acc[:, cols] * lax.rsqrt(ss * (1.0 / NA_HEAD_DIM) + NORM_EPS)).astype(BF16)

    if rope:
        @pl.when(j == SEC_RET_QK)
        def _():
            cos = cos_ref[...]
            sin = sin_ref[...]
            for c in range(0, SEC // LANES, 2):
                u1 = acc[:, c * LANES:(c + 1) * LANES]
                u2 = acc[:, (c + 1) * LANES:(c + 2) * LANES]
                p_ref[:, c * LANES:(c + 1) * LANES] = (u1 * cos - u2 * sin).astype(BF16)
                p_ref[:, (c + 1) * LANES:(c + 2) * LANES] = (u1 * sin + u2 * cos).astype(BF16)

        plain = jnp.logical_and(j > SEC_NA_K, j != SEC_RET_QK)
    else:
        plain = j > SEC_NA_K

    @pl.when(plain)
    def _():
        p_ref[...] = acc.astype(BF16)


def _in_proj(x2d, mod, prm, tabs, *, layer, tm, row_of, blocks_per_batch, rope):
    t, d = x2d.shape
    const2 = lambda i, j: (0, 0)
    lconst = lambda i, j: (layer, 0, 0)
    row_tab = pl.BlockSpec((tm, tabs["cos"].shape[1]), lambda i, j: (i % blocks_per_batch, 0))
    sh_spec, sc_spec, _ = _mod_specs(mod, layer, row_of, 2)
    return pl.pallas_call(
        functools.partial(_in_proj_kernel, rope=rope),
        grid=(t // tm, N_SEC),
        in_specs=[
            pl.BlockSpec((tm, d), lambda i, j: (i, 0)),
            pl.BlockSpec((None, 1, d), lconst),
            sh_spec, sc_spec,
            pl.BlockSpec((None, d, SEC), lambda i, j: (layer, 0, j)),
            pl.BlockSpec((None, None, 1, SEC), lambda i, j: (layer, j, 0, 0)),
            _resident((MXU_DIM, MXU_DIM), const2),
            _resident((None, d, GATE_PAD), lconst),
            _resident((None, GATE_PAD, SEC), lconst),
            _resident((None, 1, SEC), lconst),
            _resident((TRI_ROWS, TRI_ROWS), const2),
            _resident((TRI_ROWS, TRI_ROWS), const2),
            row_tab, row_tab,
        ],
        out_specs=[
            pl.BlockSpec((tm, SEC), lambda i, j: (i, j)),
            pl.BlockSpec((tm, SEC), lambda i, j: (i, 0)),
        ],
        out_shape=[
            jax.ShapeDtypeStruct((t, N_SEC * SEC), BF16),
            jax.ShapeDtypeStruct((t, SEC), F32),
        ],
        scratch_shapes=[pltpu.VMEM((tm, d), BF16)],
        compiler_params=_params("parallel", "arbitrary"),
        name="in_proj_rope" if rope else "in_proj",
    )(x2d, prm["nw"], mod, mod, prm["wproj"], prm["colscale"], tabs["gmat"], prm["wg"],
      prm["wgate"], prm["bgate"], tabs["ltri"], tabs["utri"], tabs["cos"], tabs["sin"])


def _na_block_plan(rows):
    n_qb = rows // NA_QROWS
    kr = min(NA_WIN_ROWS, rows)
    plans = []
    for qb in (0, min(1, n_qb - 1), n_qb - 1):
        t0 = int(np.clip(qb - 1, 0, n_qb - NA_KROWS // NA_QROWS))
        plan = []
        for i in range(NA_QROWS):
            r = qb * NA_QROWS + i
            rs = int(np.clip(r - kr // 2, 0, rows - kr))
            plan.append([(t0 * NA_QROWS + j) - r + NA_WIN_ROWS - 1
                         if rs <= t0 * NA_QROWS + j < rs + kr else None
                         for j in range(NA_KROWS)])
        plans.append(plan)
    return plans


def _bias_kernel(rpb_ref, o_ref, *, plans):
    kind = pl.program_id(1)
    shape = (GRID_W, LANES)
    lane = lax.broadcasted_iota(jnp.int32, shape, 1)
    c = lax.broadcasted_iota(jnp.int32, shape, 0)
    kc = lane & (GRID_W - 1)
    cs = jnp.clip(c - NA_WIN_COLS // 2, 0, GRID_W - NA_WIN_COLS)
    col_ok = jnp.logical_and(kc >= cs, kc < cs + NA_WIN_COLS)
    low = lane < GRID_W
    neg = jnp.full(shape, NEG_BIG, F32)
    toep = []
    for a in range(2 * NA_WIN_ROWS - 1):
        x = jnp.broadcast_to(rpb_ref[a:a + 1, :], shape)
        lo = pltpu.roll(x, LANES - (NA_WIN_COLS - 1), 1, stride=1, stride_axis=0)
        hi = pltpu.roll(x, GRID_W - (NA_WIN_COLS - 1), 1, stride=1, stride_axis=0)
        toep.append(jnp.where(col_ok, jnp.where(low, lo, hi), neg))
    for k, plan in enumerate(plans):
        @pl.when(kind == k)
        def _(plan=plan):
            for i in range(NA_QROWS):
                for jp in range(NA_KROWS // 2):
                    a0, a1 = plan[i][2 * jp], plan[i][2 * jp + 1]
                    b0 = neg if a0 is None else toep[a0]
                    b1 = neg if a1 is None else toep[a1]
                    o_ref[i * GRID_W:(i + 1) * GRID_W, jp * LANES:(jp + 1) * LANES] = (
                        jnp.where(low, b0, b1))


def _na_bias_tables(rpb, rows):
    depth, heads, nr, nc = rpb.shape
    rp = jnp.pad(rpb, ((0, 0), (0, 0), (0, 16 - nr), (0, LANES - nc)))
    tq, tk = NA_QROWS * GRID_W, NA_KROWS * GRID_W
    return pl.pallas_call(
        functools.partial(_bias_kernel, plans=_na_block_plan(rows)),
        grid=(depth, 3, heads),
        in_specs=[pl.BlockSpec((None, None, 16, LANES), lambda l, k, h: (l, h, 0, 0))],
        out_specs=pl.BlockSpec((None, None, None, tq, tk), lambda l, k, h: (l, k, h, 0, 0)),
        out_shape=jax.ShapeDtypeStruct((depth, 3, heads, tq, tk), F32),
        compiler_params=_params("parallel", "parallel", "parallel"),
        name="na_bias",
    )(rp)


def _lane_mask(width, lo, hi):
    lane = lax.broadcasted_iota(jnp.int32, (1, width), 1)
    return jnp.logical_and(lane >= lo, lane < hi)


def _na_kernel(q_ref, k_ref, v_ref, kc_ref, vc_ref, bias_ref, o_ref, *, n_qb):
    qb = pl.program_id(2)
    tq = q_ref.shape[0]
    tk = bias_ref.shape[-1]
    t0 = jnp.clip(qb - 1, 0, n_qb - tk // tq)
    start = pl.multiple_of(t0 * tq, tq)
    q = q_ref[...]
    k3 = k_ref[pl.ds(start, tk), :]
    v3 = v_ref[pl.ds(start, tk), :]
    kc = kc_ref[...]
    vc = vc_ref[...]
    heads = q.shape[1] // NA_HEAD_DIM
    acc = jnp.zeros(q.shape, F32)
    for h in range(heads):
        mask = _lane_mask(q.shape[1], h * NA_HEAD_DIM, (h + 1) * NA_HEAD_DIM).astype(BF16)
        qh = q * mask
        s = _dot_nt(qh, k3) + bias_ref[h]
        sc = _dot_nt(qh, kc)
        m = jnp.maximum(jnp.max(s, axis=-1, keepdims=True), jnp.max(sc, axis=-1, keepdims=True))
        p = jnp.exp(s - m)
        pc = jnp.exp(sc - m)
        l = jnp.sum(p, axis=-1, keepdims=True) + jnp.sum(pc, axis=-1, keepdims=True)
        oh = _dot(p.astype(BF16), v3 * mask) + _dot(pc.astype(BF16), vc * mask)
        acc = acc + oh * (1.0 / l)
    o_ref[...] = acc.astype(BF16)


def _na_attention(p_lat, p_ctx, bias, *, layer, batch, seq, ctx_len):
    tq = NA_QROWS * GRID_W
    tk = NA_KROWS * GRID_W
    n_qb = seq // tq
    hw = 2 * LANES
    n_hg = BRANCH_W // hw
    sec_blk = SEC // hw
    heads = hw // NA_HEAD_DIM
    kind = lambda i: jnp.where(i == 0, 0, jnp.where(i == n_qb - 1, 2, 1))
    return pl.pallas_call(
        functools.partial(_na_kernel, n_qb=n_qb),
        grid=(batch, n_hg, n_qb),
        in_specs=[
            pl.BlockSpec((tq, hw), lambda b, g, i: (b * n_qb + i, SEC_NA_Q * sec_blk + g)),
            pl.BlockSpec((seq, hw), lambda b, g, i: (b, SEC_NA_K * sec_blk + g)),
            pl.BlockSpec((seq, hw), lambda b, g, i: (b, SEC_NA_V * sec_blk + g)),
            pl.BlockSpec((ctx_len, hw), lambda b, g, i: (b, SEC_NA_K * sec_blk + g)),
            pl.BlockSpec((ctx_len, hw), lambda b, g, i: (b, SEC_NA_V * sec_blk + g)),
            pl.BlockSpec((None, None, heads, tq, tk), lambda b, g, i: (layer, kind(i), g, 0, 0)),
        ],
        out_specs=pl.BlockSpec((tq, hw), lambda b, g, i: (b * n_qb + i, g)),
        out_shape=jax.ShapeDtypeStruct((batch * seq, BRANCH_W), BF16),
        compiler_params=_params("parallel", "parallel", "arbitrary"),
        name="na_attention",
    )(p_lat, p_lat, p_lat, p_ctx, p_ctx, bias)


def _ctx_attn_kernel(q_ref, k_ref, v_ref, o_ref):
    first = _lane_mask(LANES, 0, NA_HEAD_DIM)
    sel = (first.astype(BF16), jnp.logical_not(first).astype(BF16))
    outs = []
    for pair in range(q_ref.shape[1] // LANES):
        lanes = slice(pair * LANES, (pair + 1) * LANES)
        q2, k2, v2 = q_ref[:, lanes], k_ref[:, lanes], v_ref[:, lanes]
        o_pair = []
        for hh in range(2):
            s = _dot_nt(q2 * sel[hh], k2)
            m = jnp.max(s, axis=-1, keepdims=True)
            p = jnp.exp(s - m)
            l = jnp.sum(p, axis=-1, keepdims=True)
            o_pair.append(_dot(p.astype(BF16), v2) * (1.0 / l))
        outs.append(jnp.where(first, o_pair[0], o_pair[1]))
    o_ref[...] = jnp.concatenate(outs, axis=1).astype(BF16)


def _ctx_attention(p_ctx, *, batch, ctx_len):
    hw = 2 * LANES
    n_hg = BRANCH_W // hw
    sec_blk = SEC // hw
    spec = lambda sec: pl.BlockSpec((ctx_len, hw), lambda b, g: (b, sec * sec_blk + g))
    return pl.pallas_call(
        _ctx_attn_kernel,
        grid=(batch, n_hg),
        in_specs=[spec(SEC_NA_Q), spec(SEC_NA_K), spec(SEC_NA_V)],
        out_specs=pl.BlockSpec((ctx_len, hw), lambda b, g: (b, g)),
        out_shape=jax.ShapeDtypeStruct((batch * ctx_len, BRANCH_W), BF16),
        compiler_params=_params("parallel", "parallel"),
        name="ctx_attention",
    )(p_ctx, p_ctx, p_ctx)


def _scan_head_of_lane(gla):
    lane = lax.broadcasted_iota(jnp.int32, (1, SCAN_HEADS * SCAN_DK), 1)
    if gla:
        return lax.shift_right_logical(lane, SCAN_DK.bit_length() - 1)
    return lax.shift_right_logical(lane & (LANES - 1), (SCAN_DK // 2).bit_length() - 1)


def _decay_factors(cum, ref_row, total_row):
    cm = cum[ref_row:ref_row + 1, :]
    tot = cum[total_row:total_row + 1, :]
    eq = jnp.exp(cum - cm)
    ek = jnp.exp(cm - cum)
    return eq, ek, jnp.exp(cm), jnp.exp(tot - cm), jnp.exp(tot)


def _chunk_step(q, k, v, factors, tri_stack, st, head_sel):
    eq, ek, e_m, e_tm, e_t = factors
    qt = q * eq
    kt = k * ek
    qtb = qt.astype(BF16)
    qhb = (qt * e_m).astype(BF16)
    khb = (kt * e_tm).astype(BF16)
    hmul = [sel.astype(BF16) for sel in head_sel]
    q_stack = jnp.concatenate([qtb * m for m in hmul], axis=0)
    qh_stack = jnp.concatenate([qhb * m for m in hmul], axis=0)
    att = jnp.where(tri_stack, _dot_nt(q_stack, kt.astype(BF16)), 0.0).astype(BF16)
    inter = _dot_nt(qh_stack, st.astype(BF16))
    outs = []
    for h in range(SCAN_HEADS):
        rows = slice(h * CHUNK, (h + 1) * CHUNK)
        outs.append(_dot(att[rows], v[:, h * SCAN_DV:(h + 1) * SCAN_DV]) + inter[rows])
    v_rows = jnp.concatenate([v[:, h * SCAN_DV:(h + 1) * SCAN_DV] for h in range(SCAN_HEADS)], axis=0)
    k_rows = jnp.concatenate([khb * m for m in hmul], axis=0)
    return jnp.concatenate(outs, axis=1), st * e_t + _dot_tn(v_rows, k_rows)


def _scan_masks(gla):
    head = _scan_head_of_lane(gla)
    head_sel = [head == h for h in range(SCAN_HEADS)]
    t = lax.broadcasted_iota(jnp.int32, (SCAN_HEADS * CHUNK, CHUNK), 0) & (CHUNK - 1)
    s = lax.broadcasted_iota(jnp.int32, (SCAN_HEADS * CHUNK, CHUNK), 1)
    return head_sel, s <= t, s >= t


def _ret_cum(lg_row, forward):
    t = lax.broadcasted_iota(jnp.int32, (CHUNK, lg_row.shape[1]), 0).astype(F32)
    steps = (t + 1.0) if forward else (float(CHUNK) - t)
    return steps * lg_row


def _scan_kernel(*refs, gla, n_chunks, from_zero):
    if gla:
        (qkf_ref, vf_ref, cf_ref, qkb_ref, vb_ref, cb_ref, s0_ref,
         of_ref, ob_ref, sfin_ref, stf, stb) = refs
    else:
        (qkf_ref, vf_ref, qkb_ref, vb_ref, lg_ref, s0_ref,
         of_ref, ob_ref, sfin_ref, stf, stb) = refs
    step = pl.program_id(1)
    hk = SCAN_HEADS * SCAN_DK

    @pl.when(step == 0)
    def _():
        if from_zero:
            stf[...] = jnp.zeros_like(stf)
            stb[...] = jnp.zeros_like(stb)
        else:
            stf[...] = s0_ref[0]
            stb[...] = s0_ref[1]

    head_sel, tril, triu = _scan_masks(gla)
    if not gla:
        fac_f = _decay_factors(_ret_cum(lg_ref[0:1, :], True), CHUNK // 2 - 1, CHUNK - 1)
        fac_b = _decay_factors(_ret_cum(lg_ref[1:2, :], False), CHUNK // 2, 0)

    sf = stf[...]
    sb = stb[...]
    for c in range(n_chunks):
        rf = slice(c * CHUNK, (c + 1) * CHUNK)
        cb = n_chunks - 1 - c
        rb = slice(cb * CHUNK, (cb + 1) * CHUNK)
        if gla:
            fac_f = _decay_factors(cf_ref[rf, :], CHUNK // 2 - 1, CHUNK - 1)
            fac_b = _decay_factors(cb_ref[rb, :], CHUNK // 2, 0)
        qkf = qkf_ref[rf, :].astype(F32)
        o, sf = _chunk_step(qkf[:, :hk], qkf[:, hk:], vf_ref[rf, :], fac_f, tril, sf, head_sel)
        of_ref[rf, :] = o.astype(BF16)
        qkb = qkb_ref[rb, :].astype(F32)
        o, sb = _chunk_step(qkb[:, :hk], qkb[:, hk:], vb_ref[rb, :], fac_b, triu, sb, head_sel)
        ob_ref[rb, :] = o.astype(BF16)
    stf[...] = sf
    stb[...] = sb

    @pl.when(step == pl.num_programs(1) - 1)
    def _():
        sfin_ref[0] = sf
        sfin_ref[1] = sb


def _scan(p, cum, lg, s0, *, gla, batch, rows_per_batch, tb):
    nb = rows_per_batch // tb
    hk = SCAN_HEADS * SCAN_DK
    hv = SCAN_HEADS * SCAN_DV
    sec_qk = SEC_GLA_QK if gla else SEC_RET_QK
    sec_v = SEC_GLA_V if gla else SEC_RET_V
    fwd = lambda b, s: b * nb + s
    bwd = lambda b, s: b * nb + nb - 1 - s
    from_zero = s0 is None
    if from_zero:
        s0 = jnp.zeros((batch, 2, SCAN_DV, hk), F32)
    in_specs = [
        pl.BlockSpec((tb, SEC), lambda b, s: (fwd(b, s), sec_qk)),
        pl.BlockSpec((tb, SEC), lambda b, s: (fwd(b, s), sec_v)),
    ]
    args = [p, p]
    if gla:
        in_specs.append(pl.BlockSpec((tb, hk), lambda b, s: (fwd(b, s), 0)))
        args.append(cum)
    in_specs += [
        pl.BlockSpec((tb, SEC), lambda b, s: (bwd(b, s), sec_qk)),
        pl.BlockSpec((tb, SEC), lambda b, s: (bwd(b, s), sec_v)),
    ]
    args += [p, p]
    if gla:
        in_specs.append(pl.BlockSpec((tb, hk), lambda b, s: (bwd(b, s), 1)))
        args.append(cum)
    else:
        in_specs.append(pl.BlockSpec((2, hk), lambda b, s: (0, 0)))
        args.append(lg)
    in_specs.append(pl.BlockSpec((None, 2, SCAN_DV, hk), lambda b, s: (b, 0, 0, 0)))
    args.append(s0)
    return pl.pallas_call(
        functools.partial(_scan_kernel, gla=gla, n_chunks=tb // CHUNK, from_zero=from_zero),
        grid=(batch, nb),
        in_specs=in_specs,
        out_specs=[
            pl.BlockSpec((tb, hv), lambda b, s: (fwd(b, s), 0)),
            pl.BlockSpec((tb, hv), lambda b, s: (bwd(b, s), 0)),
            pl.BlockSpec((None, 2, SCAN_DV, hk), lambda b, s: (b, 0, 0, 0)),
        ],
        out_shape=[
            jax.ShapeDtypeStruct((batch * rows_per_batch, hv), BF16),
            jax.ShapeDtypeStruct((batch * rows_per_batch, hv), BF16),
            jax.ShapeDtypeStruct((batch, 2, SCAN_DV, hk), F32),
        ],
        scratch_shapes=[pltpu.VMEM((SCAN_DV, hk), F32), pltpu.VMEM((SCAN_DV, hk), F32)],
        compiler_params=_params("parallel", "arbitrary"),
        name=("gla" if gla else "ret") + ("_ctx_scan" if from_zero else "_scan"),
    )(*args)


def _head_rms(o, gain):
    parts = []
    for h in range(SCAN_HEADS):
        sl = o[:, h * SCAN_DV:(h + 1) * SCAN_DV]
        ms = jnp.mean(sl * sl, axis=-1, keepdims=True)
        parts.append(sl * lax.rsqrt(ms + NORM_EPS))
    return jnp.concatenate(parts, axis=1) * gain


def _merge_kernel(x_ref, nw_ref, sh_ref, sc_ref, gate_ref, na_ref, gf_ref, gb_ref, rf_ref, rb_ref,
                  gg_ref, rg_ref, wz_ref, wgt_ref, wb_ref, wo_ref, o_ref):
    x = x_ref[...]
    d = x.shape[1]
    hb = _ada_norm(x, nw_ref[...], sc_ref[...], sh_ref[...]).astype(BF16)
    outs = (
        na_ref[...].astype(F32),
        _head_rms(gf_ref[...].astype(F32) + gb_ref[...].astype(F32), gg_ref[...]),
        _head_rms(rf_ref[...].astype(F32) + rb_ref[...].astype(F32), rg_ref[...]),
    )
    y = None
    for i in range(N_BRANCH):
        z = _dot(hb, wz_ref[:, i * BRANCH_W:(i + 1) * BRANCH_W])
        u = (outs[i] * _silu(z)).astype(BF16)
        g = _dot(hb, wgt_ref[:, i * d:(i + 1) * d])
        t = _sigmoid(g) * _dot(u, wb_ref[i])
        y = t if y is None else y + t
    r = _dot(y.astype(BF16), wo_ref[...])
    o_ref[...] = x + gate_ref[...] * r


def _merge(x2d, mod, prm, na_o, gla_f, gla_b, ret_f, ret_b, *, layer, tm, row_of):
    t, d = x2d.shape
    row = lambda w: pl.BlockSpec((tm, w), lambda i: (i, 0))
    sh_spec, sc_spec, gate_spec = _mod_specs(mod, layer, row_of, 1)
    lconst3 = lambda i: (layer, 0, 0)
    return pl.pallas_call(
        _merge_kernel,
        grid=(t // tm,),
        in_specs=[
            row(d),
            pl.BlockSpec((None, 1, d), lconst3),
            sh_spec, sc_spec, gate_spec,
            row(BRANCH_W), row(BRANCH_W), row(BRANCH_W), row(BRANCH_W), row(BRANCH_W),
            pl.BlockSpec((None, 1, BRANCH_W), lconst3),
            pl.BlockSpec((None, 1, BRANCH_W), lconst3),
            _resident((None, d, N_BRANCH * BRANCH_W), lconst3),
            _resident((None, d, N_BRANCH * d), lconst3),
            _resident((None, N_BRANCH, BRANCH_W, d), lambda i: (layer, 0, 0, 0)),
            _resident((None, d, d), lconst3),
        ],
        out_specs=row(d),
        out_shape=jax.ShapeDtypeStruct((t, d), F32),
        compiler_params=_params("parallel"),
        name="merge",
    )(x2d, prm["nw"], mod, mod, mod, na_o, gla_f, gla_b, ret_f, ret_b, prm["gla_gain"],
      prm["ret_gain"], prm["wz"], prm["wgt"], prm["wb"], prm["wo"])


def _in_splits():
    na = NA_HEADS * NA_HEAD_DIM
    qk = SCAN_HEADS * SCAN_DK
    vv = SCAN_HEADS * SCAN_DV
    sizes = (na, na, na, qk, qk, vv, 2 * GLA_GATE_RANK, qk, qk, vv, N_BRANCH * BRANCH_W,
             N_BRANCH * D_MODEL)
    return [int(v) for v in np.concatenate([[0], np.cumsum(sizes)])]


def _prep_params(norm_w, w_in, q_norm, k_norm, w_gate, b_gate, gla_out_norm, ret_out_norm,
                 w_branch, w_out):
    depth, d, _ = w_in.shape
    o = _in_splits()
    g0, g1, z0, gt0 = o[6], o[7], o[10], o[11]
    hk = SCAN_HEADS * SCAN_DK
    quarter = SCAN_DK // 4

    def rotary_pack(w):
        w = w.reshape(depth, d, SCAN_HEADS, 2, 2, quarter)
        return w.transpose(0, 1, 4, 2, 3, 5).reshape(depth, d, hk)

    wproj = jnp.concatenate([w_in[:, :, :g0], rotary_pack(w_in[:, :, o[7]:o[8]]),
                             rotary_pack(w_in[:, :, o[8]:o[9]]), w_in[:, :, o[9]:z0]],
                            axis=2).astype(BF16)
    wg = jnp.pad(w_in[:, :, g0:g1], ((0, 0), (0, 0), (0, GATE_PAD - (g1 - g0)))).astype(BF16)
    wgate = jnp.zeros((depth, GATE_PAD, 2 * hk), F32)
    wgate = wgate.at[:, :GLA_GATE_RANK, :hk].set(w_gate[:, 0])
    wgate = wgate.at[:, GLA_GATE_RANK:2 * GLA_GATE_RANK, hk:].set(w_gate[:, 1])
    bgate = b_gate.reshape(depth, 1, 2 * hk)
    base = np.ones((N_SEC, 1, SEC), np.float32)
    base[SEC_NA_Q] = NA_HEAD_DIM ** -0.5
    base[SEC_GLA_QK, 0, :hk] = SCAN_DK ** -0.5
    base[SEC_RET_QK, 0, :hk] = SCAN_DK ** -0.5
    reps = SEC // NA_HEAD_DIM
    ones = jnp.ones((depth, N_SEC - 2, 1, SEC), F32)
    gains = jnp.concatenate([jnp.tile(q_norm, (1, reps))[:, None, None, :],
                             jnp.tile(k_norm, (1, reps))[:, None, None, :], ones], axis=1)
    return dict(
        nw=norm_w.reshape(depth, 1, d),
        wproj=wproj, wg=wg, wgate=wgate.astype(BF16), bgate=bgate,
        colscale=gains * jnp.asarray(base)[None],
        wz=w_in[:, :, z0:gt0].astype(BF16), wgt=w_in[:, :, gt0:].astype(BF16),
        wb=w_branch.astype(BF16), wo=w_out.astype(BF16),
        gla_gain=gla_out_norm.reshape(depth, 1, BRANCH_W),
        ret_gain=ret_out_norm.reshape(depth, 1, BRANCH_W),
    )


def _tables(seq, ctx_len):
    i = np.arange(TRI_ROWS)
    same = (i[:, None] // CHUNK) == (i[None, :] // CHUNK)
    ltri = jnp.asarray((same & (i[None, :] <= i[:, None])).astype(np.float32)).astype(BF16)
    utri = jnp.asarray((same & (i[None, :] >= i[:, None])).astype(np.float32)).astype(BF16)
    grp = np.arange(MXU_DIM) // NA_HEAD_DIM
    gmat = jnp.asarray((grp[:, None] == grp[None, :]).astype(np.float32)).astype(BF16)
    quarter = SCAN_DK // 4
    n_rows = seq // GRID_W
    inv = ROPE_BASE ** (-jnp.arange(quarter, dtype=F32) / quarter)
    ang_r = jnp.arange(n_rows, dtype=F32)[:, None] * inv[None, :]
    ang_c = jnp.arange(GRID_W, dtype=F32)[:, None] * inv[None, :]
    by_row = lambda a: jnp.broadcast_to(a[:, None, :], (n_rows, GRID_W, quarter)).reshape(seq, quarter)
    by_col = lambda a: jnp.broadcast_to(a[None, :, :], (n_rows, GRID_W, quarter)).reshape(seq, quarter)
    cr, sr = by_row(jnp.cos(ang_r)), by_row(jnp.sin(ang_r))
    cc, sc = by_col(jnp.cos(ang_c)), by_col(jnp.sin(ang_c))
    cos = jnp.concatenate([cr, cc] * SCAN_HEADS, axis=1)
    sin = jnp.concatenate([sr, sc] * SCAN_HEADS, axis=1)
    lat = dict(ltri=ltri, utri=utri, gmat=gmat, cos=cos, sin=sin)
    dummy = jnp.zeros((ctx_len, LANES), F32)
    ctx = dict(ltri=ltri, utri=utri, gmat=gmat, cos=dummy, sin=dummy)
    return lat, ctx


def _log_gamma(offset):
    g = jnp.log1p(-jnp.exp2(-(offset + jnp.arange(SCAN_HEADS, dtype=F32))))
    return jnp.tile(jnp.repeat(g, SCAN_DK // 2), 2)


def kernel(x, c, ctx, c_ctx, w_mod, b_mod, norm_w, w_in, na_q_norm, na_k_norm, na_rpb, gla_w_gate,
           gla_b_gate, gla_out_norm, ret_out_norm, w_branch, w_out):
    batch, seq, d = x.shape
    ctx_len = ctx.shape[1]
    depth = w_mod.shape[0]
    rows = seq // GRID_W
    tm_proj, tm_merge, tb_scan = 1024, 512, 512

    c_rows = jnp.zeros((8, d), F32).at[:batch].set(c).at[batch].set(c_ctx)
    mod = _modulation(c_rows, w_mod, b_mod).reshape(depth, 8, 1, 3 * d)
    prm = _prep_params(norm_w, w_in, na_q_norm, na_k_norm, gla_w_gate, gla_b_gate, gla_out_norm,
                       ret_out_norm, w_branch, w_out)
    lat_tabs, ctx_tabs = _tables(seq, ctx_len)
    bias = _na_bias_tables(na_rpb, rows)
    lg = jnp.stack([_log_gamma(RET_DECAY_FWD), _log_gamma(RET_DECAY_BWD)])
    ctx_row = lambda i: batch

    xl = x.reshape(batch * seq, d)
    xc = ctx.reshape(batch * ctx_len, d)
    for layer in range(depth):
        with_ctx = layer < depth - 1
        p_lat, cum_lat = _in_proj(xl, mod, prm, lat_tabs, layer=layer, tm=tm_proj,
                                  row_of=lambda i: i // (seq // tm_proj),
                                  blocks_per_batch=seq // tm_proj, rope=True)
        p_ctx, cum_ctx = _in_proj(xc, mod, prm, ctx_tabs, layer=layer, tm=ctx_len, row_of=ctx_row,
                                  blocks_per_batch=1, rope=False)

        na_l = _na_attention(p_lat, p_ctx, bias, layer=layer, batch=batch, seq=seq, ctx_len=ctx_len)

        gla_cf, gla_cb, gla_s = _scan(p_ctx, cum_ctx, None, None, gla=True, batch=batch,
                                      rows_per_batch=ctx_len, tb=ctx_len)
        gla_f, gla_b, _ = _scan(p_lat, cum_lat, None, gla_s, gla=True, batch=batch,
                                rows_per_batch=seq, tb=tb_scan)
        ret_cf, ret_cb, ret_s = _scan(p_ctx, None, lg, None, gla=False, batch=batch,
                                      rows_per_batch=ctx_len, tb=ctx_len)
        ret_f, ret_b, _ = _scan(p_lat, None, lg, ret_s, gla=False, batch=batch,
                                rows_per_batch=seq, tb=tb_scan)

        xl_new = _merge(xl, mod, prm, na_l, gla_f, gla_b, ret_f, ret_b, layer=layer, tm=tm_merge,
                        row_of=lambda i: i // (seq // tm_merge))
        if with_ctx:
            na_c = _ctx_attention(p_ctx, batch=batch, ctx_len=ctx_len)
            xc = _merge(xc, mod, prm, na_c, gla_cf, gla_cb, ret_cf, ret_cb, layer=layer, tm=ctx_len,
                        row_of=ctx_row)
        xl = xl_new
    return xl.reshape(batch, seq, d)
```

```python
import functools

import numpy as np
import jax
import jax.numpy as jnp
from jax import lax
from jax.experimental import pallas as pl
from jax.experimental.pallas import tpu as pltpu

D_MODEL = 1024
GRID_W = 64
NORM_EPS = 1e-6
N_BRANCH = 3
BRANCH_W = D_MODEL // 2
NA_HEAD_DIM = 64
NA_HEADS = BRANCH_W // NA_HEAD_DIM
NA_WIN_ROWS = 8
NA_WIN_COLS = 16
SCAN_HEADS = 4
SCAN_DV = BRANCH_W // SCAN_HEADS
SCAN_DK = SCAN_DV // 2
GLA_GATE_RANK = 16
GLA_GATE_TAU = 16.0
RET_DECAY_FWD = 5.0
RET_DECAY_BWD = 5.5
CHUNK = 64
ROPE_BASE = 10000.0

LANES = 128
MXU_DIM = 256

SEC = 512
SEC_NA_Q, SEC_NA_K, SEC_NA_V, SEC_GLA_QK, SEC_GLA_V, SEC_RET_QK, SEC_RET_V = range(7)
N_SEC = 7
GATE_PAD = LANES
TRI_ROWS = MXU_DIM
NA_QROWS = 4
NA_KROWS = 12
NEG_BIG = -1e30

VMEM_LIMIT = 56 * 1024 * 1024

F32 = jnp.float32
BF16 = jnp.bfloat16


def _dot(a, b):
    return jnp.dot(a, b, preferred_element_type=F32)


def _dot_nt(a, b):
    return lax.dot_general(a, b, (((1,), (1,)), ((), ())), preferred_element_type=F32)


def _dot_tn(a, b):
    return lax.dot_general(a, b, (((0,), (0,)), ((), ())), preferred_element_type=F32)


def _split2(x):
    hi = x.astype(BF16)
    lo = (x - hi.astype(F32)).astype(BF16)
    return hi, lo


def _log_sigmoid(x):
    return jnp.minimum(x, 0.0) - jnp.log(1.0 + jnp.exp(-jnp.abs(x)))


def _sigmoid(x):
    return 1.0 / (1.0 + jnp.exp(-x))


def _silu(x):
    return x * _sigmoid(x)


def _params(*sem):
    return pltpu.CompilerParams(dimension_semantics=sem, vmem_limit_bytes=VMEM_LIMIT)


def _resident(block_shape, index_map):
    return pl.BlockSpec(block_shape, index_map, pipeline_mode=pl.Buffered(1))


def _ada_norm(x, nw, sc, sh):
    ms = jnp.mean(x * x, axis=-1, keepdims=True)
    return (x * lax.rsqrt(ms + NORM_EPS)) * (nw * (1.0 + sc)) + sh


def _mod_kernel(c_ref, w_ref, b_ref, o_ref):
    a = _silu(c_ref[...])
    o_ref[...] = jnp.dot(a, w_ref[...], preferred_element_type=F32,
                         precision=lax.Precision.HIGHEST) + b_ref[...]


def _modulation(c_rows, w_mod, b_mod):
    depth, d, d3 = w_mod.shape
    rows = c_rows.shape[0]
    tn = 1024
    return pl.pallas_call(
        _mod_kernel,
        grid=(depth, d3 // tn),
        in_specs=[
            pl.BlockSpec((rows, d), lambda l, j: (0, 0)),
            pl.BlockSpec((None, d, tn), lambda l, j: (l, 0, j)),
            pl.BlockSpec((None, 1, tn), lambda l, j: (l, 0, j)),
        ],
        out_specs=pl.BlockSpec((None, rows, tn), lambda l, j: (l, 0, j)),
        out_shape=jax.ShapeDtypeStruct((depth, rows, d3), F32),
        compiler_params=_params("parallel", "parallel"),
        name="modulation",
    )(c_rows, w_mod, b_mod.reshape(depth, 1, d3))


def _mod_specs(mod, layer, row_of):
    d = mod.shape[-1] // 3
    spec = lambda third: pl.BlockSpec((None, None, 1, d), lambda i: (layer, row_of(i), 0, third))
    return spec(0), spec(1), spec(2)


def _in_proj_kernel(x_ref, nw_ref, sh_ref, sc_ref, w_ref, cs_ref, gm_ref, wg_ref, wgate_ref,
                    bgate_ref, ltri_ref, utri_ref, cos_ref, sin_ref, p_ref, cum_ref, *, rope):
    tm = x_ref.shape[0]
    hb = _ada_norm(x_ref[...], nw_ref[...], sc_ref[...], sh_ref[...]).astype(BF16)

    g = _dot(hb, wg_ref[...])
    xg = _dot(g.astype(BF16), wgate_ref[...]) + bgate_ref[...]
    la = _log_sigmoid(xg) * (1.0 / GLA_GATE_TAU)
    half = la.shape[1] // 2
    for r in range(tm // TRI_ROWS):
        rows = slice(r * TRI_ROWS, (r + 1) * TRI_ROWS)
        fh, fl = _split2(la[rows, :half])
        bh, bl = _split2(la[rows, half:])
        cum_ref[rows, :half] = _dot(ltri_ref[...], fh) + _dot(ltri_ref[...], fl)
        cum_ref[rows, half:] = _dot(utri_ref[...], bh) + _dot(utri_ref[...], bl)

    for sec in range(N_SEC):
        base = sec * SEC
        raw = _dot(hb, w_ref[:, base:base + SEC])
        acc = raw * cs_ref[sec]
        if sec <= SEC_NA_K:
            for c in range(0, SEC, MXU_DIM):
                sq = raw[:, c:c + MXU_DIM] * raw[:, c:c + MXU_DIM]
                ss = _dot(sq.astype(BF16), gm_ref[...])
                p_ref[:, base + c:base + c + MXU_DIM] = (
                    acc[:, c:c + MXU_DIM] * lax.rsqrt(ss * (1.0 / NA_HEAD_DIM) + NORM_EPS)).astype(BF16)
        elif rope and sec == SEC_RET_QK:
            cos = cos_ref[...]
            sin = sin_ref[...]
            for c in range(0, SEC, 2 * LANES):
                u1 = acc[:, c:c + LANES]
                u2 = acc[:, c + LANES:c + 2 * LANES]
                p_ref[:, base + c:base + c + LANES] = (u1 * cos - u2 * sin).astype(BF16)
                p_ref[:, base + c + LANES:base + c + 2 * LANES] = (u1 * sin + u2 * cos).astype(BF16)
        else:
            p_ref[:, base:base + SEC] = acc.astype(BF16)


def _in_proj(x2d, mod, prm, tabs, *, layer, tm, row_of, blocks_per_batch, rope):
    t, d = x2d.shape
    const2 = lambda i: (0, 0)
    lconst = lambda i: (layer, 0, 0)
    row_tab = pl.BlockSpec((tm, tabs["cos"].shape[1]), lambda i: (i % blocks_per_batch, 0))
    sh_spec, sc_spec, _ = _mod_specs(mod, layer, row_of)
    return pl.pallas_call(
        functools.partial(_in_proj_kernel, rope=rope),
        grid=(t // tm,),
        in_specs=[
            pl.BlockSpec((tm, d), lambda i: (i, 0)),
            pl.BlockSpec((None, 1, d), lconst),
            sh_spec, sc_spec,
            _resident((None, d, N_SEC * SEC), lconst),
            _resident((None, N_SEC, 1, SEC), lambda i: (layer, 0, 0, 0)),
            _resident((MXU_DIM, MXU_DIM), const2),
            _resident((None, d, GATE_PAD), lconst),
            _resident((None, GATE_PAD, SEC), lconst),
            _resident((None, 1, SEC), lconst),
            _resident((TRI_ROWS, TRI_ROWS), const2),
            _resident((TRI_ROWS, TRI_ROWS), const2),
            row_tab, row_tab,
        ],
        out_specs=[
            pl.BlockSpec((tm, N_SEC * SEC), lambda i: (i, 0)),
            pl.BlockSpec((tm, SEC), lambda i: (i, 0)),
        ],
        out_shape=[
            jax.ShapeDtypeStruct((t, N_SEC * SEC), BF16),
            jax.ShapeDtypeStruct((t, SEC), F32),
        ],
        compiler_params=_params("parallel"),
        name="in_proj_rope" if rope else "in_proj",
    )(x2d, prm["nw"], mod, mod, prm["wproj"], prm["colscale"], tabs["gmat"], prm["wg"],
      prm["wgate"], prm["bgate"], tabs["ltri"], tabs["utri"], tabs["cos"], tabs["sin"])


def _na_block_plan(rows):
    n_qb = rows // NA_QROWS
    kr = min(NA_WIN_ROWS, rows)
    plans = []
    for qb in (0, min(1, n_qb - 1), n_qb - 1):
        t0 = int(np.clip(qb - 1, 0, n_qb - NA_KROWS // NA_QROWS))
        plan = []
        for i in range(NA_QROWS):
            r = qb * NA_QROWS + i
            rs = int(np.clip(r - kr // 2, 0, rows - kr))
            plan.append([(t0 * NA_QROWS + j) - r + NA_WIN_ROWS - 1
                         if rs <= t0 * NA_QROWS + j < rs + kr else None
                         for j in range(NA_KROWS)])
        plans.append(plan)
    return plans


def _bias_kernel(rpb_ref, o_ref, *, plans):
    shape = (GRID_W, LANES)
    lane = lax.broadcasted_iota(jnp.int32, shape, 1)
    c = lax.broadcasted_iota(jnp.int32, shape, 0)
    kc = lane & (GRID_W - 1)
    cs = jnp.clip(c - NA_WIN_COLS // 2, 0, GRID_W - NA_WIN_COLS)
    col_ok = jnp.logical_and(kc >= cs, kc < cs + NA_WIN_COLS)
    low = lane < GRID_W
    neg = jnp.full(shape, NEG_BIG, F32)
    toep = []
    for a in range(2 * NA_WIN_ROWS - 1):
        x = jnp.broadcast_to(rpb_ref[a:a + 1, :], shape)
        lo = pltpu.roll(x, LANES - (NA_WIN_COLS - 1), 1, stride=1, stride_axis=0)
        hi = pltpu.roll(x, GRID_W - (NA_WIN_COLS - 1), 1, stride=1, stride_axis=0)
        toep.append(jnp.where(col_ok, jnp.where(low, lo, hi), neg))
    for k, plan in enumerate(plans):
        for i in range(NA_QROWS):
            for jp in range(NA_KROWS // 2):
                a0, a1 = plan[i][2 * jp], plan[i][2 * jp + 1]
                b0 = neg if a0 is None else toep[a0]
                b1 = neg if a1 is None else toep[a1]
                o_ref[k, i * GRID_W:(i + 1) * GRID_W, jp * LANES:(jp + 1) * LANES] = (
                    jnp.where(low, b0, b1))


def _na_bias_tables(rpb, rows):
    depth, heads, nr, nc = rpb.shape
    rp = jnp.pad(rpb, ((0, 0), (0, 0), (0, 16 - nr), (0, LANES - nc)))
    tq, tk = NA_QROWS * GRID_W, NA_KROWS * GRID_W
    return pl.pallas_call(
        functools.partial(_bias_kernel, plans=_na_block_plan(rows)),
        grid=(depth, heads),
        in_specs=[pl.BlockSpec((None, None, 16, LANES), lambda l, h: (l, h, 0, 0))],
        out_specs=pl.BlockSpec((None, 3, None, tq, tk), lambda l, h: (l, 0, h, 0, 0)),
        out_shape=jax.ShapeDtypeStruct((depth, 3, heads, tq, tk), F32),
        compiler_params=_params("parallel", "parallel"),
        name="na_bias",
    )(rp)


def _lane_mask(width, lo, hi):
    lane = lax.broadcasted_iota(jnp.int32, (1, width), 1)
    return jnp.logical_and(lane >= lo, lane < hi)


def _na_kernel(q_ref, k_ref, v_ref, kc_ref, vc_ref, bias_ref, o_ref, *, n_qb):
    qb = pl.program_id(2)
    tq = q_ref.shape[0]
    tk = bias_ref.shape[-1]
    t0 = jnp.clip(qb - 1, 0, n_qb - tk // tq)
    start = pl.multiple_of(t0 * tq, tq)
    q = q_ref[...]
    k3 = k_ref[pl.ds(start, tk), :]
    v3 = v_ref[pl.ds(start, tk), :]
    kc = kc_ref[...]
    vc = vc_ref[...]
    heads = q.shape[1] // NA_HEAD_DIM
    acc = jnp.zeros(q.shape, F32)
    for h in range(heads):
        mask = _lane_mask(q.shape[1], h * NA_HEAD_DIM, (h + 1) * NA_HEAD_DIM).astype(BF16)
        qh = q * mask
        s = _dot_nt(qh, k3) + bias_ref[h]
        sc = _dot_nt(qh, kc)
        m = jnp.maximum(jnp.max(s, axis=-1, keepdims=True), jnp.max(sc, axis=-1, keepdims=True))
        p = jnp.exp(s - m)
        pc = jnp.exp(sc - m)
        l = jnp.sum(p, axis=-1, keepdims=True) + jnp.sum(pc, axis=-1, keepdims=True)
        oh = _dot(p.astype(BF16), v3 * mask) + _dot(pc.astype(BF16), vc * mask)
        acc = acc + oh * (1.0 / l)
    o_ref[...] = acc.astype(BF16)


def _na_attention(p_lat, p_ctx, bias, *, layer, batch, seq, ctx_len):
    tq = NA_QROWS * GRID_W
    tk = NA_KROWS * GRID_W
    n_qb = seq // tq
    hw = 2 * LANES
    n_hg = BRANCH_W // hw
    sec_blk = SEC // hw
    heads = hw // NA_HEAD_DIM
    kind = lambda i: jnp.where(i == 0, 0, jnp.where(i == n_qb - 1, 2, 1))
    return pl.pallas_call(
        functools.partial(_na_kernel, n_qb=n_qb),
        grid=(batch, n_hg, n_qb),
        in_specs=[
            pl.BlockSpec((tq, hw), lambda b, g, i: (b * n_qb + i, SEC_NA_Q * sec_blk + g)),
            pl.BlockSpec((seq, hw), lambda b, g, i: (b, SEC_NA_K * sec_blk + g)),
            pl.BlockSpec((seq, hw), lambda b, g, i: (b, SEC_NA_V * sec_blk + g)),
            pl.BlockSpec((ctx_len, hw), lambda b, g, i: (b, SEC_NA_K * sec_blk + g)),
            pl.BlockSpec((ctx_len, hw), lambda b, g, i: (b, SEC_NA_V * sec_blk + g)),
            pl.BlockSpec((None, None, heads, tq, tk), lambda b, g, i: (layer, kind(i), g, 0, 0)),
        ],
        out_specs=pl.BlockSpec((tq, hw), lambda b, g, i: (b * n_qb + i, g)),
        out_shape=jax.ShapeDtypeStruct((batch * seq, BRANCH_W), BF16),
        compiler_params=_params("parallel", "parallel", "arbitrary"),
        name="na_attention",
    )(p_lat, p_lat, p_lat, p_ctx, p_ctx, bias)


def _ctx_attn_kernel(q_ref, k_ref, v_ref, o_ref):
    first = _lane_mask(LANES, 0, NA_HEAD_DIM)
    sel = (first.astype(BF16), jnp.logical_not(first).astype(BF16))
    outs = []
    for pair in range(q_ref.shape[1] // LANES):
        lanes = slice(pair * LANES, (pair + 1) * LANES)
        q2, k2, v2 = q_ref[:, lanes], k_ref[:, lanes], v_ref[:, lanes]
        o_pair = []
        for hh in range(2):
            s = _dot_nt(q2 * sel[hh], k2)
            m = jnp.max(s, axis=-1, keepdims=True)
            p = jnp.exp(s - m)
            l = jnp.sum(p, axis=-1, keepdims=True)
            o_pair.append(_dot(p.astype(BF16), v2) * (1.0 / l))
        outs.append(jnp.where(first, o_pair[0], o_pair[1]))
    o_ref[...] = jnp.concatenate(outs, axis=1).astype(BF16)


def _ctx_attention(p_ctx, *, batch, ctx_len):
    hw = 2 * LANES
    n_hg = BRANCH_W // hw
    sec_blk = SEC // hw
    spec = lambda sec: pl.BlockSpec((ctx_len, hw), lambda b, g: (b, sec * sec_blk + g))
    return pl.pallas_call(
        _ctx_attn_kernel,
        grid=(batch, n_hg),
        in_specs=[spec(SEC_NA_Q), spec(SEC_NA_K), spec(SEC_NA_V)],
        out_specs=pl.BlockSpec((ctx_len, hw), lambda b, g: (b, g)),
        out_shape=jax.ShapeDtypeStruct((batch * ctx_len, BRANCH_W), BF16),
        compiler_params=_params("parallel", "parallel"),
        name="ctx_attention",
    )(p_ctx, p_ctx, p_ctx)


def _scan_head_of_lane(gla):
    lane = lax.broadcasted_iota(jnp.int32, (1, SCAN_HEADS * SCAN_DK), 1)
    if gla:
        return lax.shift_right_logical(lane, SCAN_DK.bit_length() - 1)
    return lax.shift_right_logical(lane & (LANES - 1), (SCAN_DK // 2).bit_length() - 1)


def _decay_factors(cum, ref_row, total_row):
    cm = cum[ref_row:ref_row + 1, :]
    tot = cum[total_row:total_row + 1, :]
    eq = jnp.exp(cum - cm)
    ek = jnp.exp(cm - cum)
    return eq, ek, jnp.exp(cm), jnp.exp(tot - cm), jnp.exp(tot)


def _chunk_step(q, k, v, factors, tri_stack, st, head_sel):
    eq, ek, e_m, e_tm, e_t = factors
    qt = q * eq
    kt = k * ek
    qtb = qt.astype(BF16)
    qhb = (qt * e_m).astype(BF16)
    khb = (kt * e_tm).astype(BF16)
    hmul = [sel.astype(BF16) for sel in head_sel]
    q_stack = jnp.concatenate([qtb * m for m in hmul], axis=0)
    qh_stack = jnp.concatenate([qhb * m for m in hmul], axis=0)
    att = jnp.where(tri_stack, _dot_nt(q_stack, kt.astype(BF16)), 0.0).astype(BF16)
    inter = _dot_nt(qh_stack, st.astype(BF16))
    outs = []
    for h in range(SCAN_HEADS):
        rows = slice(h * CHUNK, (h + 1) * CHUNK)
        outs.append(_dot(att[rows], v[:, h * SCAN_DV:(h + 1) * SCAN_DV]) + inter[rows])
    v_rows = jnp.concatenate([v[:, h * SCAN_DV:(h + 1) * SCAN_DV] for h in range(SCAN_HEADS)], axis=0)
    k_rows = jnp.concatenate([khb * m for m in hmul], axis=0)
    return jnp.concatenate(outs, axis=1), st * e_t + _dot_tn(v_rows, k_rows)


def _scan_masks(gla):
    head = _scan_head_of_lane(gla)
    head_sel = [head == h for h in range(SCAN_HEADS)]
    t = lax.broadcasted_iota(jnp.int32, (SCAN_HEADS * CHUNK, CHUNK), 0) & (CHUNK - 1)
    s = lax.broadcasted_iota(jnp.int32, (SCAN_HEADS * CHUNK, CHUNK), 1)
    return head_sel, s <= t, s >= t


def _ret_cum(lg_row, forward):
    t = lax.broadcasted_iota(jnp.int32, (CHUNK, lg_row.shape[1]), 0).astype(F32)
    steps = (t + 1.0) if forward else (float(CHUNK) - t)
    return steps * lg_row


def _scan_kernel(*refs, gla, n_chunks, from_zero):
    if gla:
        (qkf_ref, vf_ref, cf_ref, qkb_ref, vb_ref, cb_ref, s0_ref,
         of_ref, ob_ref, sfin_ref, stf, stb) = refs
    else:
        (qkf_ref, vf_ref, qkb_ref, vb_ref, lg_ref, s0_ref,
         of_ref, ob_ref, sfin_ref, stf, stb) = refs
    step = pl.program_id(1)
    hk = SCAN_HEADS * SCAN_DK

    @pl.when(step == 0)
    def _():
        if from_zero:
            stf[...] = jnp.zeros_like(stf)
            stb[...] = jnp.zeros_like(stb)
        else:
            stf[...] = s0_ref[0]
            stb[...] = s0_ref[1]

    head_sel, tril, triu = _scan_masks(gla)
    if not gla:
        fac_f = _decay_factors(_ret_cum(lg_ref[0:1, :], True), CHUNK // 2 - 1, CHUNK - 1)
        fac_b = _decay_factors(_ret_cum(lg_ref[1:2, :], False), CHUNK // 2, 0)

    sf = stf[...]
    sb = stb[...]
    for c in range(n_chunks):
        rf = slice(c * CHUNK, (c + 1) * CHUNK)
        cb = n_chunks - 1 - c
        rb = slice(cb * CHUNK, (cb + 1) * CHUNK)
        if gla:
            fac_f = _decay_factors(cf_ref[rf, :], CHUNK // 2 - 1, CHUNK - 1)
            fac_b = _decay_factors(cb_ref[rb, :], CHUNK // 2, 0)
        qkf = qkf_ref[rf, :].astype(F32)
        o, sf = _chunk_step(qkf[:, :hk], qkf[:, hk:], vf_ref[rf, :], fac_f, tril, sf, head_sel)
        of_ref[rf, :] = o.astype(BF16)
        qkb = qkb_ref[rb, :].astype(F32)
        o, sb = _chunk_step(qkb[:, :hk], qkb[:, hk:], vb_ref[rb, :], fac_b, triu, sb, head_sel)
        ob_ref[rb, :] = o.astype(BF16)
    stf[...] = sf
    stb[...] = sb

    @pl.when(step == pl.num_programs(1) - 1)
    def _():
        sfin_ref[0] = sf
        sfin_ref[1] = sb


def _scan(p, cum, lg, s0, *, gla, batch, rows_per_batch, tb):
    nb = rows_per_batch // tb
    hk = SCAN_HEADS * SCAN_DK
    hv = SCAN_HEADS * SCAN_DV
    sec_qk = SEC_GLA_QK if gla else SEC_RET_QK
    sec_v = SEC_GLA_V if gla else SEC_RET_V
    fwd = lambda b, s: b * nb + s
    bwd = lambda b, s: b * nb + nb - 1 - s
    from_zero = s0 is None
    if from_zero:
        s0 = jnp.zeros((batch, 2, SCAN_DV, hk), F32)
    in_specs = [
        pl.BlockSpec((tb, SEC), lambda b, s: (fwd(b, s), sec_qk)),
        pl.BlockSpec((tb, SEC), lambda b, s: (fwd(b, s), sec_v)),
    ]
    args = [p, p]
    if gla:
        in_specs.append(pl.BlockSpec((tb, hk), lambda b, s: (fwd(b, s), 0)))
        args.append(cum)
    in_specs += [
        pl.BlockSpec((tb, SEC), lambda b, s: (bwd(b, s), sec_qk)),
        pl.BlockSpec((tb, SEC), lambda b, s: (bwd(b, s), sec_v)),
    ]
    args += [p, p]
    if gla:
        in_specs.append(pl.BlockSpec((tb, hk), lambda b, s: (bwd(b, s), 1)))
        args.append(cum)
    else:
        in_specs.append(pl.BlockSpec((2, hk), lambda b, s: (0, 0)))
        args.append(lg)
    in_specs.append(pl.BlockSpec((None, 2, SCAN_DV, hk), lambda b, s: (b, 0, 0, 0)))
    args.append(s0)
    return pl.pallas_call(
        functools.partial(_scan_kernel, gla=gla, n_chunks=tb // CHUNK, from_zero=from_zero),
        grid=(batch, nb),
        in_specs=in_specs,
        out_specs=[
            pl.BlockSpec((tb, hv), lambda b, s: (fwd(b, s), 0)),
            pl.BlockSpec((tb, hv), lambda b, s: (bwd(b, s), 0)),
            pl.BlockSpec((None, 2, SCAN_DV, hk), lambda b, s: (b, 0, 0, 0)),
        ],
        out_shape=[
            jax.ShapeDtypeStruct((batch * rows_per_batch, hv), BF16),
            jax.ShapeDtypeStruct((batch * rows_per_batch, hv), BF16),
            jax.ShapeDtypeStruct((batch, 2, SCAN_DV, hk), F32),
        ],
        scratch_shapes=[pltpu.VMEM((SCAN_DV, hk), F32), pltpu.VMEM((SCAN_DV, hk), F32)],
        compiler_params=_params("parallel", "arbitrary"),
        name=("gla" if gla else "ret") + ("_ctx_scan" if from_zero else "_scan"),
    )(*args)


def _head_rms(o, gain):
    parts = []
    for h in range(SCAN_HEADS):
        sl = o[:, h * SCAN_DV:(h + 1) * SCAN_DV]
        ms = jnp.mean(sl * sl, axis=-1, keepdims=True)
        parts.append(sl * lax.rsqrt(ms + NORM_EPS))
    return jnp.concatenate(parts, axis=1) * gain


def _merge_kernel(x_ref, nw_ref, sh_ref, sc_ref, gate_ref, na_ref, gf_ref, gb_ref, rf_ref, rb_ref,
                  gg_ref, rg_ref, wz_ref, wgt_ref, wb_ref, wo_ref, o_ref):
    x = x_ref[...]
    d = x.shape[1]
    hb = _ada_norm(x, nw_ref[...], sc_ref[...], sh_ref[...]).astype(BF16)
    outs = (
        na_ref[...].astype(F32),
        _head_rms(gf_ref[...].astype(F32) + gb_ref[...].astype(F32), gg_ref[...]),
        _head_rms(rf_ref[...].astype(F32) + rb_ref[...].astype(F32), rg_ref[...]),
    )
    y = None
    for i in range(N_BRANCH):
        z = _dot(hb, wz_ref[:, i * BRANCH_W:(i + 1) * BRANCH_W])
        u = (outs[i] * _silu(z)).astype(BF16)
        g = _dot(hb, wgt_ref[:, i * d:(i + 1) * d])
        t = _sigmoid(g) * _dot(u, wb_ref[i])
        y = t if y is None else y + t
    r = _dot(y.astype(BF16), wo_ref[...])
    o_ref[...] = x + gate_ref[...] * r


def _merge(x2d, mod, prm, na_o, gla_f, gla_b, ret_f, ret_b, *, layer, tm, row_of):
    t, d = x2d.shape
    row = lambda w: pl.BlockSpec((tm, w), lambda i: (i, 0))
    sh_spec, sc_spec, gate_spec = _mod_specs(mod, layer, row_of)
    lconst3 = lambda i: (layer, 0, 0)
    return pl.pallas_call(
        _merge_kernel,
        grid=(t // tm,),
        in_specs=[
            row(d),
            pl.BlockSpec((None, 1, d), lconst3),
            sh_spec, sc_spec, gate_spec,
            row(BRANCH_W), row(BRANCH_W), row(BRANCH_W), row(BRANCH_W), row(BRANCH_W),
            pl.BlockSpec((None, 1, BRANCH_W), lconst3),
            pl.BlockSpec((None, 1, BRANCH_W), lconst3),
            _resident((None, d, N_BRANCH * BRANCH_W), lconst3),
            _resident((None, d, N_BRANCH * d), lconst3),
            _resident((None, N_BRANCH, BRANCH_W, d), lambda i: (layer, 0, 0, 0)),
            _resident((None, d, d), lconst3),
        ],
        out_specs=row(d),
        out_shape=jax.ShapeDtypeStruct((t, d), F32),
        compiler_params=_params("parallel"),
        name="merge",
    )(x2d, prm["nw"], mod, mod, mod, na_o, gla_f, gla_b, ret_f, ret_b, prm["gla_gain"],
      prm["ret_gain"], prm["wz"], prm["wgt"], prm["wb"], prm["wo"])


def _in_splits():
    na = NA_HEADS * NA_HEAD_DIM
    qk = SCAN_HEADS * SCAN_DK
    vv = SCAN_HEADS * SCAN_DV
    sizes = (na, na, na, qk, qk, vv, 2 * GLA_GATE_RANK, qk, qk, vv, N_BRANCH * BRANCH_W,
             N_BRANCH * D_MODEL)
    return [int(v) for v in np.concatenate([[0], np.cumsum(sizes)])]


def _prep_params(norm_w, w_in, q_norm, k_norm, w_gate, b_gate, gla_out_norm, ret_out_norm,
                 w_branch, w_out):
    depth, d, _ = w_in.shape
    o = _in_splits()
    g0, g1, z0, gt0 = o[6], o[7], o[10], o[11]
    hk = SCAN_HEADS * SCAN_DK
    quarter = SCAN_DK // 4

    def rotary_pack(w):
        w = w.reshape(depth, d, SCAN_HEADS, 2, 2, quarter)
        return w.transpose(0, 1, 4, 2, 3, 5).reshape(depth, d, hk)

    wproj = jnp.concatenate([w_in[:, :, :g0], rotary_pack(w_in[:, :, o[7]:o[8]]),
                             rotary_pack(w_in[:, :, o[8]:o[9]]), w_in[:, :, o[9]:z0]],
                            axis=2).astype(BF16)
    wg = jnp.pad(w_in[:, :, g0:g1], ((0, 0), (0, 0), (0, GATE_PAD - (g1 - g0)))).astype(BF16)
    wgate = jnp.zeros((depth, GATE_PAD, 2 * hk), F32)
    wgate = wgate.at[:, :GLA_GATE_RANK, :hk].set(w_gate[:, 0])
    wgate = wgate.at[:, GLA_GATE_RANK:2 * GLA_GATE_RANK, hk:].set(w_gate[:, 1])
    bgate = b_gate.reshape(depth, 1, 2 * hk)
    base = np.ones((N_SEC, 1, SEC), np.float32)
    base[SEC_NA_Q] = NA_HEAD_DIM ** -0.5
    base[SEC_GLA_QK, 0, :hk] = SCAN_DK ** -0.5
    base[SEC_RET_QK, 0, :hk] = SCAN_DK ** -0.5
    reps = SEC // NA_HEAD_DIM
    ones = jnp.ones((depth, N_SEC - 2, 1, SEC), F32)
    gains = jnp.concatenate([jnp.tile(q_norm, (1, reps))[:, None, None, :],
                             jnp.tile(k_norm, (1, reps))[:, None, None, :], ones], axis=1)
    return dict(
        nw=norm_w.reshape(depth, 1, d),
        wproj=wproj, wg=wg, wgate=wgate.astype(BF16), bgate=bgate,
        colscale=gains * jnp.asarray(base)[None],
        wz=w_in[:, :, z0:gt0].astype(BF16), wgt=w_in[:, :, gt0:].astype(BF16),
        wb=w_branch.astype(BF16), wo=w_out.astype(BF16),
        gla_gain=gla_out_norm.reshape(depth, 1, BRANCH_W),
        ret_gain=ret_out_norm.reshape(depth, 1, BRANCH_W),
    )


def _tables(seq, ctx_len):
    i = np.arange(TRI_ROWS)
    same = (i[:, None] // CHUNK) == (i[None, :] // CHUNK)
    ltri = jnp.asarray((same & (i[None, :] <= i[:, None])).astype(np.float32)).astype(BF16)
    utri = jnp.asarray((same & (i[None, :] >= i[:, None])).astype(np.float32)).astype(BF16)
    grp = np.arange(MXU_DIM) // NA_HEAD_DIM
    gmat = jnp.asarray((grp[:, None] == grp[None, :]).astype(np.float32)).astype(BF16)
    quarter = SCAN_DK // 4
    n_rows = seq // GRID_W
    inv = ROPE_BASE ** (-jnp.arange(quarter, dtype=F32) / quarter)
    ang_r = jnp.arange(n_rows, dtype=F32)[:, None] * inv[None, :]
    ang_c = jnp.arange(GRID_W, dtype=F32)[:, None] * inv[None, :]
    by_row = lambda a: jnp.broadcast_to(a[:, None, :], (n_rows, GRID_W, quarter)).reshape(seq, quarter)
    by_col = lambda a: jnp.broadcast_to(a[None, :, :], (n_rows, GRID_W, quarter)).reshape(seq, quarter)
    cr, sr = by_row(jnp.cos(ang_r)), by_row(jnp.sin(ang_r))
    cc, sc = by_col(jnp.cos(ang_c)), by_col(jnp.sin(ang_c))
    cos = jnp.concatenate([cr, cc] * SCAN_HEADS, axis=1)
    sin = jnp.concatenate([sr, sc] * SCAN_HEADS, axis=1)
    lat = dict(ltri=ltri, utri=utri, gmat=gmat, cos=cos, sin=sin)
    dummy = jnp.zeros((ctx_len, LANES), F32)
    ctx = dict(ltri=ltri, utri=utri, gmat=gmat, cos=dummy, sin=dummy)
    return lat, ctx


def _log_gamma(offset):
    g = jnp.log1p(-jnp.exp2(-(offset + jnp.arange(SCAN_HEADS, dtype=F32))))
    return jnp.tile(jnp.repeat(g, SCAN_DK // 2), 2)


def kernel(x, c, ctx, c_ctx, w_mod, b_mod, norm_w, w_in, na_q_norm, na_k_norm, na_rpb, gla_w_gate,
           gla_b_gate, gla_out_norm, ret_out_norm, w_branch, w_out):
    batch, seq, d = x.shape
    ctx_len = ctx.shape[1]
    depth = w_mod.shape[0]
    rows = seq // GRID_W
    tm_proj, tm_merge, tb_scan = 512, 512, 512

    c_rows = jnp.zeros((8, d), F32).at[:batch].set(c).at[batch].set(c_ctx)
    mod = _modulation(c_rows, w_mod, b_mod).reshape(depth, 8, 1, 3 * d)
    prm = _prep_params(norm_w, w_in, na_q_norm, na_k_norm, gla_w_gate, gla_b_gate, gla_out_norm,
                       ret_out_norm, w_branch, w_out)
    lat_tabs, ctx_tabs = _tables(seq, ctx_len)
    bias = _na_bias_tables(na_rpb, rows)
    lg = jnp.stack([_log_gamma(RET_DECAY_FWD), _log_gamma(RET_DECAY_BWD)])
    ctx_row = lambda i: batch

    xl = x.reshape(batch * seq, d)
    xc = ctx.reshape(batch * ctx_len, d)
    for layer in range(depth):
        with_ctx = layer < depth - 1
        p_lat, cum_lat = _in_proj(xl, mod, prm, lat_tabs, layer=layer, tm=tm_proj,
                                  row_of=lambda i: i // (seq // tm_proj),
                                  blocks_per_batch=seq // tm_proj, rope=True)
        p_ctx, cum_ctx = _in_proj(xc, mod, prm, ctx_tabs, layer=layer, tm=ctx_len, row_of=ctx_row,
                                  blocks_per_batch=1, rope=False)

        na_l = _na_attention(p_lat, p_ctx, bias, layer=layer, batch=batch, seq=seq, ctx_len=ctx_len)

        gla_cf, gla_cb, gla_s = _scan(p_ctx, cum_ctx, None, None, gla=True, batch=batch,
                                      rows_per_batch=ctx_len, tb=ctx_len)
        gla_f, gla_b, _ = _scan(p_lat, cum_lat, None, gla_s, gla=True, batch=batch,
                                rows_per_batch=seq, tb=tb_scan)
        ret_cf, ret_cb, ret_s = _scan(p_ctx, None, lg, None, gla=False, batch=batch,
                                      rows_per_batch=ctx_len, tb=ctx_len)
        ret_f, ret_b, _ = _scan(p_lat, None, lg, ret_s, gla=False, batch=batch,
                                rows_per_batch=seq, tb=tb_scan)

        xl_new = _merge(xl, mod, prm, na_l, gla_f, gla_b, ret_f, ret_b, layer=layer, tm=tm_merge,
                        row_of=lambda i: i // (seq // tm_merge))
        if with_ctx:
            na_c = _ctx_attention(p_ctx, batch=batch, ctx_len=ctx_len)
            xc = _merge(xc, mod, prm, na_c, gla_cf, gla_cb, ret_cf, ret_cb, layer=layer, tm=ctx_len,
                        row_of=ctx_row)
        xl = xl_new
    return xl.reshape(batch, seq, d)
```

```python
import functools

import numpy as np
import jax
import jax.numpy as jnp
from jax import lax
from jax.experimental import pallas as pl
from jax.experimental.pallas import tpu as pltpu

D_MODEL = 1024
GRID_W = 64
NORM_EPS = 1e-6
N_BRANCH = 3
BRANCH_W = D_MODEL // 2
NA_HEAD_DIM = 64
NA_HEADS = BRANCH_W // NA_HEAD_DIM
NA_WIN_ROWS = 8
NA_WIN_COLS = 16
SCAN_HEADS = 4
SCAN_DV = BRANCH_W // SCAN_HEADS
SCAN_DK = SCAN_DV // 2
GLA_GATE_RANK = 16
GLA_GATE_TAU = 16.0
RET_DECAY_FWD = 5.0
RET_DECAY_BWD = 5.5
CHUNK = 64
ROPE_BASE = 10000.0

LANES = 128
MXU_DIM = 256

SEC = 512
SEC_NA_Q, SEC_NA_K, SEC_NA_V, SEC_GLA_QK, SEC_GLA_V, SEC_RET_QK, SEC_RET_V = range(7)
N_SEC = 7
GATE_PAD = LANES
TRI_ROWS = MXU_DIM
NA_QROWS = 4
NA_KROWS = 12
NEG_BIG = -1e30

VMEM_LIMIT = 56 * 1024 * 1024

F32 = jnp.float32
BF16 = jnp.bfloat16


def _dot(a, b):
    return jnp.dot(a, b, preferred_element_type=F32)


def _dot_nt(a, b):
    return lax.dot_general(a, b, (((1,), (1,)), ((), ())), preferred_element_type=F32)


def _dot_tn(a, b):
    return lax.dot_general(a, b, (((0,), (0,)), ((), ())), preferred_element_type=F32)


def _split2(x):
    hi = x.astype(BF16)
    lo = (x - hi.astype(F32)).astype(BF16)
    return hi, lo


def _log_sigmoid(x):
    return jnp.minimum(x, 0.0) - jnp.log(1.0 + jnp.exp(-jnp.abs(x)))


def _sigmoid(x):
    return 1.0 / (1.0 + jnp.exp(-x))


def _silu(x):
    return x * _sigmoid(x)


def _params(*sem):
    return pltpu.CompilerParams(dimension_semantics=sem, vmem_limit_bytes=VMEM_LIMIT)


def _resident(block_shape, index_map):
    return pl.BlockSpec(block_shape, index_map, pipeline_mode=pl.Buffered(1))


def _ada_norm(x, nw, sc, sh):
    ms = jnp.mean(x * x, axis=-1, keepdims=True)
    return (x * lax.rsqrt(ms + NORM_EPS)) * (nw * (1.0 + sc)) + sh


def _mod_kernel(c_ref, w_ref, b_ref, o_ref):
    a = _silu(c_ref[...])
    o_ref[...] = jnp.dot(a, w_ref[...], preferred_element_type=F32,
                         precision=lax.Precision.HIGHEST) + b_ref[...]


def _modulation(c_rows, w_mod, b_mod):
    depth, d, d3 = w_mod.shape
    rows = c_rows.shape[0]
    tn = 1024
    return pl.pallas_call(
        _mod_kernel,
        grid=(depth, d3 // tn),
        in_specs=[
            pl.BlockSpec((rows, d), lambda l, j: (0, 0)),
            pl.BlockSpec((None, d, tn), lambda l, j: (l, 0, j)),
            pl.BlockSpec((None, 1, tn), lambda l, j: (l, 0, j)),
        ],
        out_specs=pl.BlockSpec((None, rows, tn), lambda l, j: (l, 0, j)),
        out_shape=jax.ShapeDtypeStruct((depth, rows, d3), F32),
        compiler_params=_params("parallel", "parallel"),
        name="modulation",
    )(c_rows, w_mod, b_mod.reshape(depth, 1, d3))


def _mod_specs(mod, layer, row_of):
    d = mod.shape[-1] // 3
    spec = lambda third: pl.BlockSpec((None, None, 1, d), lambda i: (layer, row_of(i), 0, third))
    return spec(0), spec(1), spec(2)


def _in_proj_kernel(x_ref, nw_ref, sh_ref, sc_ref, w_ref, cs_ref, gm_ref, wg_ref, wgate_ref,
                    bgate_ref, ltri_ref, utri_ref, cos_ref, sin_ref, p_ref, cum_ref, *, rope):
    tm = x_ref.shape[0]
    hb = _ada_norm(x_ref[...], nw_ref[...], sc_ref[...], sh_ref[...]).astype(BF16)

    g = _dot(hb, wg_ref[...])
    xg = _dot(g.astype(BF16), wgate_ref[...]) + bgate_ref[...]
    la = _log_sigmoid(xg) * (1.0 / GLA_GATE_TAU)
    half = la.shape[1] // 2
    for r in range(tm // TRI_ROWS):
        rows = slice(r * TRI_ROWS, (r + 1) * TRI_ROWS)
        fh, fl = _split2(la[rows, :half])
        bh, bl = _split2(la[rows, half:])
        cum_ref[rows, :half] = _dot(ltri_ref[...], fh) + _dot(ltri_ref[...], fl)
        cum_ref[rows, half:] = _dot(utri_ref[...], bh) + _dot(utri_ref[...], bl)

    for sec in range(N_SEC):
        base = sec * SEC
        raw = _dot(hb, w_ref[:, base:base + SEC])
        acc = raw * cs_ref[sec]
        if sec <= SEC_NA_K:
            for c in range(0, SEC, MXU_DIM):
                sq = raw[:, c:c + MXU_DIM] * raw[:, c:c + MXU_DIM]
                ss = _dot(sq.astype(BF16), gm_ref[...])
                p_ref[:, base + c:base + c + MXU_DIM] = (
                    acc[:, c:c + MXU_DIM] * lax.rsqrt(ss * (1.0 / NA_HEAD_DIM) + NORM_EPS)).astype(BF16)
        elif rope and sec == SEC_RET_QK:
            cos = cos_ref[...]
            sin = sin_ref[...]
            for c in range(0, SEC, 2 * LANES):
                u1 = acc[:, c:c + LANES]
                u2 = acc[:, c + LANES:c + 2 * LANES]
                p_ref[:, base + c:base + c + LANES] = (u1 * cos - u2 * sin).astype(BF16)
                p_ref[:, base + c + LANES:base + c + 2 * LANES] = (u1 * sin + u2 * cos).astype(BF16)
        else:
            p_ref[:, base:base + SEC] = acc.astype(BF16)


def _in_proj(x2d, mod, prm, tabs, *, layer, tm, row_of, blocks_per_batch, rope):
    t, d = x2d.shape
    const2 = lambda i: (0, 0)
    lconst = lambda i: (layer, 0, 0)
    row_tab = pl.BlockSpec((tm, tabs["cos"].shape[1]), lambda i: (i % blocks_per_batch, 0))
    sh_spec, sc_spec, _ = _mod_specs(mod, layer, row_of)
    return pl.pallas_call(
        functools.partial(_in_proj_kernel, rope=rope),
        grid=(t // tm,),
        in_specs=[
            pl.BlockSpec((tm, d), lambda i: (i, 0)),
            pl.BlockSpec((None, 1, d), lconst),
            sh_spec, sc_spec,
            _resident((None, d, N_SEC * SEC), lconst),
            _resident((None, N_SEC, 1, SEC), lambda i: (layer, 0, 0, 0)),
            _resident((MXU_DIM, MXU_DIM), const2),
            _resident((None, d, GATE_PAD), lconst),
            _resident((None, GATE_PAD, SEC), lconst),
            _resident((None, 1, SEC), lconst),
            _resident((TRI_ROWS, TRI_ROWS), const2),
            _resident((TRI_ROWS, TRI_ROWS), const2),
            row_tab, row_tab,
        ],
        out_specs=[
            pl.BlockSpec((tm, N_SEC * SEC), lambda i: (i, 0)),
            pl.BlockSpec((tm, SEC), lambda i: (i, 0)),
        ],
        out_shape=[
            jax.ShapeDtypeStruct((t, N_SEC * SEC), BF16),
            jax.ShapeDtypeStruct((t, SEC), F32),
        ],
        compiler_params=_params("parallel"),
        name="in_proj_rope" if rope else "in_proj",
    )(x2d, prm["nw"], mod, mod, prm["wpack"], prm["colscale"], tabs["gmat"], prm["wg"],
      prm["wgate"], prm["bgate"], tabs["ltri"], tabs["utri"], tabs["cos"], tabs["sin"])


def _na_block_plan(rows):
    n_qb = rows // NA_QROWS
    kr = min(NA_WIN_ROWS, rows)
    plans = []
    for qb in (0, min(1, n_qb - 1), n_qb - 1):
        t0 = int(np.clip(qb - 1, 0, n_qb - NA_KROWS // NA_QROWS))
        plan = []
        for i in range(NA_QROWS):
            r = qb * NA_QROWS + i
            rs = int(np.clip(r - kr // 2, 0, rows - kr))
            plan.append([(t0 * NA_QROWS + j) - r + NA_WIN_ROWS - 1
                         if rs <= t0 * NA_QROWS + j < rs + kr else None
                         for j in range(NA_KROWS)])
        plans.append(plan)
    return plans


def _bias_kernel(rpb_ref, o_ref, *, plans):
    shape = (GRID_W, LANES)
    lane = lax.broadcasted_iota(jnp.int32, shape, 1)
    c = lax.broadcasted_iota(jnp.int32, shape, 0)
    kc = lane & (GRID_W - 1)
    cs = jnp.clip(c - NA_WIN_COLS // 2, 0, GRID_W - NA_WIN_COLS)
    col_ok = jnp.logical_and(kc >= cs, kc < cs + NA_WIN_COLS)
    low = lane < GRID_W
    neg = jnp.full(shape, NEG_BIG, F32)
    toep = []
    for a in range(2 * NA_WIN_ROWS - 1):
        x = jnp.broadcast_to(rpb_ref[a:a + 1, :], shape)
        lo = pltpu.roll(x, LANES - (NA_WIN_COLS - 1), 1, stride=1, stride_axis=0)
        hi = pltpu.roll(x, GRID_W - (NA_WIN_COLS - 1), 1, stride=1, stride_axis=0)
        toep.append(jnp.where(col_ok, jnp.where(low, lo, hi), neg))
    for k, plan in enumerate(plans):
        for i in range(NA_QROWS):
            for jp in range(NA_KROWS // 2):
                a0, a1 = plan[i][2 * jp], plan[i][2 * jp + 1]
                b0 = neg if a0 is None else toep[a0]
                b1 = neg if a1 is None else toep[a1]
                o_ref[k, i * GRID_W:(i + 1) * GRID_W, jp * LANES:(jp + 1) * LANES] = (
                    jnp.where(low, b0, b1))


def _na_bias_tables(rpb, rows):
    depth, heads, nr, nc = rpb.shape
    rp = jnp.pad(rpb, ((0, 0), (0, 0), (0, 16 - nr), (0, LANES - nc)))
    tq, tk = NA_QROWS * GRID_W, NA_KROWS * GRID_W
    return pl.pallas_call(
        functools.partial(_bias_kernel, plans=_na_block_plan(rows)),
        grid=(depth, heads),
        in_specs=[pl.BlockSpec((None, None, 16, LANES), lambda l, h: (l, h, 0, 0))],
        out_specs=pl.BlockSpec((None, 3, None, tq, tk), lambda l, h: (l, 0, h, 0, 0)),
        out_shape=jax.ShapeDtypeStruct((depth, 3, heads, tq, tk), F32),
        compiler_params=_params("parallel", "parallel"),
        name="na_bias",
    )(rp)


def _lane_mask(width, lo, hi):
    lane = lax.broadcasted_iota(jnp.int32, (1, width), 1)
    return jnp.logical_and(lane >= lo, lane < hi)


def _na_kernel(q_ref, k_ref, v_ref, kc_ref, vc_ref, bias_ref, o_ref, *, n_qb, n_sub):
    tq = q_ref.shape[0] // n_sub
    tk = bias_ref.shape[-1]
    width = q_ref.shape[1]
    heads = width // NA_HEAD_DIM
    kc = kc_ref[...]
    vc = vc_ref[...]
    for sub in range(n_sub):
        qb = pl.program_id(2) * n_sub + sub
        kind = jnp.where(qb == 0, 0, jnp.where(qb == n_qb - 1, 2, 1))
        t0 = jnp.clip(qb - 1, 0, n_qb - tk // tq)
        start = pl.multiple_of(t0 * tq, tq)
        q = q_ref[sub * tq:(sub + 1) * tq, :]
        k3 = k_ref[pl.ds(start, tk), :]
        v3 = v_ref[pl.ds(start, tk), :]
        acc = jnp.zeros(q.shape, F32)
        for h in range(heads):
            inside = _lane_mask(width, h * NA_HEAD_DIM, (h + 1) * NA_HEAD_DIM)
            sum_lane = ((h + 1) % heads) * NA_HEAD_DIM
            ones_row = _lane_mask(width, sum_lane, sum_lane + 1)
            keep = inside.astype(BF16)
            fill = ones_row.astype(BF16)
            qh = q * keep
            s = _dot_nt(qh, k3) + bias_ref[kind, h]
            sc = _dot_nt(qh, kc)
            m = jnp.maximum(jnp.max(s, axis=-1, keepdims=True), jnp.max(sc, axis=-1, keepdims=True))
            p = jnp.exp((s - m).astype(BF16))
            pc = jnp.exp((sc - m).astype(BF16))
            oh = _dot(p, v3 * keep + fill) + _dot(pc, vc * keep + fill)
            l = jnp.sum(jnp.where(ones_row, oh, 0.0), axis=-1, keepdims=True)
            acc = jnp.where(inside, oh * (1.0 / l), acc)
        o_ref[sub * tq:(sub + 1) * tq, :] = acc.astype(BF16)


def _na_attention(p_lat, p_ctx, bias, *, layer, batch, seq, ctx_len, n_sub):
    tq = NA_QROWS * GRID_W
    tk = NA_KROWS * GRID_W
    n_qb = seq // tq
    n_steps = n_qb // n_sub
    hw = 2 * LANES
    n_hg = BRANCH_W // hw
    sec_blk = SEC // hw
    heads = hw // NA_HEAD_DIM
    return pl.pallas_call(
        functools.partial(_na_kernel, n_qb=n_qb, n_sub=n_sub),
        grid=(batch, n_hg, n_steps),
        in_specs=[
            pl.BlockSpec((n_sub * tq, hw), lambda b, g, i: (b * n_steps + i, SEC_NA_Q * sec_blk + g)),
            pl.BlockSpec((seq, hw), lambda b, g, i: (b, SEC_NA_K * sec_blk + g)),
            pl.BlockSpec((seq, hw), lambda b, g, i: (b, SEC_NA_V * sec_blk + g)),
            pl.BlockSpec((ctx_len, hw), lambda b, g, i: (b, SEC_NA_K * sec_blk + g)),
            pl.BlockSpec((ctx_len, hw), lambda b, g, i: (b, SEC_NA_V * sec_blk + g)),
            pl.BlockSpec((None, bias.shape[1], heads, tq, tk), lambda b, g, i: (layer, 0, g, 0, 0)),
        ],
        out_specs=pl.BlockSpec((n_sub * tq, hw), lambda b, g, i: (b * n_steps + i, g)),
        out_shape=jax.ShapeDtypeStruct((batch * seq, BRANCH_W), BF16),
        compiler_params=_params("parallel", "parallel", "arbitrary"),
        name="na_attention",
    )(p_lat, p_lat, p_lat, p_ctx, p_ctx, bias)


def _ctx_attn_kernel(q_ref, k_ref, v_ref, o_ref):
    first = _lane_mask(LANES, 0, NA_HEAD_DIM)
    sel = (first.astype(BF16), jnp.logical_not(first).astype(BF16))
    outs = []
    for pair in range(q_ref.shape[1] // LANES):
        lanes = slice(pair * LANES, (pair + 1) * LANES)
        q2, k2, v2 = q_ref[:, lanes], k_ref[:, lanes], v_ref[:, lanes]
        o_pair = []
        for hh in range(2):
            s = _dot_nt(q2 * sel[hh], k2)
            m = jnp.max(s, axis=-1, keepdims=True)
            p = jnp.exp(s - m)
            l = jnp.sum(p, axis=-1, keepdims=True)
            o_pair.append(_dot(p.astype(BF16), v2) * (1.0 / l))
        outs.append(jnp.where(first, o_pair[0], o_pair[1]))
    o_ref[...] = jnp.concatenate(outs, axis=1).astype(BF16)


def _ctx_attention(p_ctx, *, batch, ctx_len):
    hw = 2 * LANES
    n_hg = BRANCH_W // hw
    sec_blk = SEC // hw
    spec = lambda sec: pl.BlockSpec((ctx_len, hw), lambda b, g: (b, sec * sec_blk + g))
    return pl.pallas_call(
        _ctx_attn_kernel,
        grid=(batch, n_hg),
        in_specs=[spec(SEC_NA_Q), spec(SEC_NA_K), spec(SEC_NA_V)],
        out_specs=pl.BlockSpec((ctx_len, hw), lambda b, g: (b, g)),
        out_shape=jax.ShapeDtypeStruct((batch * ctx_len, BRANCH_W), BF16),
        compiler_params=_params("parallel", "parallel"),
        name="ctx_attention",
    )(p_ctx, p_ctx, p_ctx)


def _scan_head_of_lane(gla):
    lane = lax.broadcasted_iota(jnp.int32, (1, SCAN_HEADS * SCAN_DK), 1)
    if gla:
        return lax.shift_right_logical(lane, SCAN_DK.bit_length() - 1)
    return lax.shift_right_logical(lane & (LANES - 1), (SCAN_DK // 2).bit_length() - 1)


def _decay_factors(cum, ref_row, total_row):
    cm = cum[ref_row:ref_row + 1, :]
    tot = cum[total_row:total_row + 1, :]
    eq = jnp.exp(cum - cm)
    ek = jnp.exp(cm - cum)
    return eq, ek, jnp.exp(cm), jnp.exp(tot - cm), jnp.exp(tot)


def _chunk_step(q, k, v, factors, tri_stack, st, head_sel):
    eq, ek, e_m, e_tm, e_t = factors
    qt = q * eq
    kt = k * ek
    qtb = qt.astype(BF16)
    qhb = (qt * e_m).astype(BF16)
    khb = (kt * e_tm).astype(BF16)
    hmul = [sel.astype(BF16) for sel in head_sel]
    q_stack = jnp.concatenate([qtb * m for m in hmul], axis=0)
    qh_stack = jnp.concatenate([qhb * m for m in hmul], axis=0)
    att = jnp.where(tri_stack, _dot_nt(q_stack, kt.astype(BF16)), 0.0).astype(BF16)
    inter = _dot_nt(qh_stack, st.astype(BF16))
    outs = []
    for h in range(SCAN_HEADS):
        rows = slice(h * CHUNK, (h + 1) * CHUNK)
        outs.append(_dot(att[rows], v[:, h * SCAN_DV:(h + 1) * SCAN_DV]) + inter[rows])
    v_rows = jnp.concatenate([v[:, h * SCAN_DV:(h + 1) * SCAN_DV] for h in range(SCAN_HEADS)], axis=0)
    k_rows = jnp.concatenate([khb * m for m in hmul], axis=0)
    return jnp.concatenate(outs, axis=1), st * e_t + _dot_tn(v_rows, k_rows)


def _scan_masks(gla):
    head = _scan_head_of_lane(gla)
    head_sel = [head == h for h in range(SCAN_HEADS)]
    t = lax.broadcasted_iota(jnp.int32, (SCAN_HEADS * CHUNK, CHUNK), 0) & (CHUNK - 1)
    s = lax.broadcasted_iota(jnp.int32, (SCAN_HEADS * CHUNK, CHUNK), 1)
    return head_sel, s <= t, s >= t


def _ret_cum(lg_row, forward):
    t = lax.broadcasted_iota(jnp.int32, (CHUNK, lg_row.shape[1]), 0).astype(F32)
    steps = (t + 1.0) if forward else (float(CHUNK) - t)
    return steps * lg_row


def _scan_kernel(*refs, gla, n_chunks, from_zero):
    if gla:
        (qkf_ref, vf_ref, cf_ref, qkb_ref, vb_ref, cb_ref, s0_ref,
         of_ref, ob_ref, sfin_ref, stf, stb) = refs
    else:
        (qkf_ref, vf_ref, qkb_ref, vb_ref, lg_ref, s0_ref,
         of_ref, ob_ref, sfin_ref, stf, stb) = refs
    step = pl.program_id(1)
    hk = SCAN_HEADS * SCAN_DK

    @pl.when(step == 0)
    def _():
        if from_zero:
            stf[...] = jnp.zeros_like(stf)
            stb[...] = jnp.zeros_like(stb)
        else:
            stf[...] = s0_ref[0]
            stb[...] = s0_ref[1]

    head_sel, tril, triu = _scan_masks(gla)
    if not gla:
        fac_f = _decay_factors(_ret_cum(lg_ref[0:1, :], True), CHUNK // 2 - 1, CHUNK - 1)
        fac_b = _decay_factors(_ret_cum(lg_ref[1:2, :], False), CHUNK // 2, 0)

    sf = stf[...]
    sb = stb[...]
    for c in range(n_chunks):
        rf = slice(c * CHUNK, (c + 1) * CHUNK)
        cb = n_chunks - 1 - c
        rb = slice(cb * CHUNK, (cb + 1) * CHUNK)
        if gla:
            fac_f = _decay_factors(cf_ref[rf, :], CHUNK // 2 - 1, CHUNK - 1)
            fac_b = _decay_factors(cb_ref[rb, :], CHUNK // 2, 0)
        qkf = qkf_ref[rf, :].astype(F32)
        o, sf = _chunk_step(qkf[:, :hk], qkf[:, hk:], vf_ref[rf, :], fac_f, tril, sf, head_sel)
        of_ref[rf, :] = o.astype(BF16)
        qkb = qkb_ref[rb, :].astype(F32)
        o, sb = _chunk_step(qkb[:, :hk], qkb[:, hk:], vb_ref[rb, :], fac_b, triu, sb, head_sel)
        ob_ref[rb, :] = o.astype(BF16)
    stf[...] = sf
    stb[...] = sb

    @pl.when(step == pl.num_programs(1) - 1)
    def _():
        sfin_ref[0] = sf
        sfin_ref[1] = sb


def _scan(p, cum, lg, s0, *, gla, batch, rows_per_batch, tb):
    nb = rows_per_batch // tb
    hk = SCAN_HEADS * SCAN_DK
    hv = SCAN_HEADS * SCAN_DV
    sec_qk = SEC_GLA_QK if gla else SEC_RET_QK
    sec_v = SEC_GLA_V if gla else SEC_RET_V
    fwd = lambda b, s: b * nb + s
    bwd = lambda b, s: b * nb + nb - 1 - s
    from_zero = s0 is None
    if from_zero:
        s0 = jnp.zeros((batch, 2, SCAN_DV, hk), F32)
    in_specs = [
        pl.BlockSpec((tb, SEC), lambda b, s: (fwd(b, s), sec_qk)),
        pl.BlockSpec((tb, SEC), lambda b, s: (fwd(b, s), sec_v)),
    ]
    args = [p, p]
    if gla:
        in_specs.append(pl.BlockSpec((tb, hk), lambda b, s: (fwd(b, s), 0)))
        args.append(cum)
    in_specs += [
        pl.BlockSpec((tb, SEC), lambda b, s: (bwd(b, s), sec_qk)),
        pl.BlockSpec((tb, SEC), lambda b, s: (bwd(b, s), sec_v)),
    ]
    args += [p, p]
    if gla:
        in_specs.append(pl.BlockSpec((tb, hk), lambda b, s: (bwd(b, s), 1)))
        args.append(cum)
    else:
        in_specs.append(pl.BlockSpec((2, hk), lambda b, s: (0, 0)))
        args.append(lg)
    in_specs.append(pl.BlockSpec((None, 2, SCAN_DV, hk), lambda b, s: (b, 0, 0, 0)))
    args.append(s0)
    return pl.pallas_call(
        functools.partial(_scan_kernel, gla=gla, n_chunks=tb // CHUNK, from_zero=from_zero),
        grid=(batch, nb),
        in_specs=in_specs,
        out_specs=[
            pl.BlockSpec((tb, hv), lambda b, s: (fwd(b, s), 0)),
            pl.BlockSpec((tb, hv), lambda b, s: (bwd(b, s), 0)),
            pl.BlockSpec((None, 2, SCAN_DV, hk), lambda b, s: (b, 0, 0, 0)),
        ],
        out_shape=[
            jax.ShapeDtypeStruct((batch * rows_per_batch, hv), BF16),
            jax.ShapeDtypeStruct((batch * rows_per_batch, hv), BF16),
            jax.ShapeDtypeStruct((batch, 2, SCAN_DV, hk), F32),
        ],
        scratch_shapes=[pltpu.VMEM((SCAN_DV, hk), F32), pltpu.VMEM((SCAN_DV, hk), F32)],
        compiler_params=_params("parallel", "arbitrary"),
        name=("gla" if gla else "ret") + ("_ctx_scan" if from_zero else "_scan"),
    )(*args)


def _head_rms(o, gain):
    parts = []
    for h in range(SCAN_HEADS):
        sl = o[:, h * SCAN_DV:(h + 1) * SCAN_DV]
        ms = jnp.mean(sl * sl, axis=-1, keepdims=True)
        parts.append(sl * lax.rsqrt(ms + NORM_EPS))
    return jnp.concatenate(parts, axis=1) * gain


def _merge_kernel(x_ref, nw_ref, sh_ref, sc_ref, gate_ref, na_ref, gf_ref, gb_ref, rf_ref, rb_ref,
                  gg_ref, rg_ref, wz0_ref, wz1_ref, wz2_ref, wg0_ref, wg1_ref, wg2_ref, wb_ref, wo_ref,
                  o_ref):
    x = x_ref[...]
    wz_refs = (wz0_ref, wz1_ref, wz2_ref)
    wg_refs = (wg0_ref, wg1_ref, wg2_ref)
    hb = _ada_norm(x, nw_ref[...], sc_ref[...], sh_ref[...]).astype(BF16)
    outs = (
        na_ref[...].astype(F32),
        _head_rms(gf_ref[...].astype(F32) + gb_ref[...].astype(F32), gg_ref[...]),
        _head_rms(rf_ref[...].astype(F32) + rb_ref[...].astype(F32), rg_ref[...]),
    )
    y = None
    for i in range(N_BRANCH):
        z = _dot(hb, wz_refs[i][...])
        u = (outs[i] * _silu(z)).astype(BF16)
        g = _dot(hb, wg_refs[i][...])
        t = _sigmoid(g) * _dot(u, wb_ref[i])
        y = t if y is None else y + t
    r = _dot(y.astype(BF16), wo_ref[...])
    o_ref[...] = x + gate_ref[...] * r


def _merge(x2d, mod, prm, na_o, gla_f, gla_b, ret_f, ret_b, *, layer, tm, row_of):
    t, d = x2d.shape
    row = lambda w: pl.BlockSpec((tm, w), lambda i: (i, 0))
    sh_spec, sc_spec, gate_spec = _mod_specs(mod, layer, row_of)
    lconst3 = lambda i: (layer, 0, 0)
    return pl.pallas_call(
        _merge_kernel,
        grid=(t // tm,),
        in_specs=[
            row(d),
            pl.BlockSpec((None, 1, d), lconst3),
            sh_spec, sc_spec, gate_spec,
            row(BRANCH_W), row(BRANCH_W), row(BRANCH_W), row(BRANCH_W), row(BRANCH_W),
            pl.BlockSpec((None, 1, BRANCH_W), lconst3),
            pl.BlockSpec((None, 1, BRANCH_W), lconst3),
            *[_resident((None, d, BRANCH_W), lambda i, n=n: (layer, 0, N_SEC * SEC // BRANCH_W + n))
              for n in range(N_BRANCH)],
            *[_resident((None, d, d), lambda i, n=n: (layer, 0, (N_SEC * SEC + N_BRANCH * BRANCH_W) // d + n))
              for n in range(N_BRANCH)],
            _resident((None, N_BRANCH, BRANCH_W, d), lambda i: (layer, 0, 0, 0)),
            _resident((None, d, d), lconst3),
        ],
        out_specs=row(d),
        out_shape=jax.ShapeDtypeStruct((t, d), F32),
        compiler_params=_params("parallel"),
        name="merge",
    )(x2d, prm["nw"], mod, mod, mod, na_o, gla_f, gla_b, ret_f, ret_b, prm["gla_gain"],
      prm["ret_gain"], *([prm["wpack"]] * (2 * N_BRANCH)), prm["wb"], prm["wo"])


def _in_splits():
    na = NA_HEADS * NA_HEAD_DIM
    qk = SCAN_HEADS * SCAN_DK
    vv = SCAN_HEADS * SCAN_DV
    sizes = (na, na, na, qk, qk, vv, 2 * GLA_GATE_RANK, qk, qk, vv, N_BRANCH * BRANCH_W,
             N_BRANCH * D_MODEL)
    return [int(v) for v in np.concatenate([[0], np.cumsum(sizes)])]


def _rotary_pack_matrix():
    hk = SCAN_HEADS * SCAN_DK
    quarter = SCAN_DK // 4
    p = np.zeros((SEC, SEC), np.float32)
    for blk in range(SEC // hk):
        for h in range(SCAN_HEADS):
            for half in range(2):
                for partner in range(2):
                    for i in range(quarter):
                        old = h * SCAN_DK + half * 2 * quarter + partner * quarter + i
                        new = partner * (hk // 2) + h * 2 * quarter + half * quarter + i
                        p[blk * hk + old, blk * hk + new] = 1.0
    return jnp.asarray(p).astype(BF16)


def _pack_kernel(a_ref, b_ref, perm_ref, o_ref, *, first_shifted, perm_block, shift):
    j = pl.program_id(1)
    sec = a_ref.shape[1]

    def shifted():
        x = jnp.concatenate([a_ref[...], b_ref[...]], axis=1)
        return pltpu.roll(x, x.shape[1] - shift, 1)[:, :sec].astype(BF16)

    @pl.when(j < first_shifted)
    def _():
        o_ref[...] = a_ref[...].astype(BF16)

    @pl.when(j == perm_block)
    def _():
        o_ref[...] = _dot(shifted(), perm_ref[...]).astype(BF16)

    @pl.when(jnp.logical_and(j >= first_shifted, j != perm_block))
    def _():
        o_ref[...] = shifted()


def _pack_w_in(w_in):
    depth, d, n_in = w_in.shape
    o = _in_splits()
    g0, g1 = o[6], o[7]
    assert g0 % SEC == 0 and (n_in - (g1 - g0)) % SEC == 0
    n_blocks = (n_in - (g1 - g0)) // SEC
    tail = SEC // LANES
    return pl.pallas_call(
        functools.partial(_pack_kernel, first_shifted=g0 // SEC, perm_block=SEC_RET_QK,
                          shift=g1 - g0),
        grid=(depth, n_blocks),
        in_specs=[
            pl.BlockSpec((None, d, SEC), lambda l, j: (l, 0, j)),
            pl.BlockSpec((None, d, LANES), lambda l, j: (l, 0, (j + 1) * tail)),
            _resident((SEC, SEC), lambda l, j: (0, 0)),
        ],
        out_specs=pl.BlockSpec((None, d, SEC), lambda l, j: (l, 0, j)),
        out_shape=jax.ShapeDtypeStruct((depth, d, n_blocks * SEC), BF16),
        compiler_params=_params("parallel", "parallel"),
        name="pack_w_in",
    )(w_in, w_in, _rotary_pack_matrix())


def _prep_params(norm_w, w_in, q_norm, k_norm, w_gate, b_gate, gla_out_norm, ret_out_norm,
                 w_branch, w_out):
    depth, d, _ = w_in.shape
    o = _in_splits()
    g0, g1 = o[6], o[7]
    hk = SCAN_HEADS * SCAN_DK
    wg = jnp.pad(w_in[:, :, g0:g1], ((0, 0), (0, 0), (0, GATE_PAD - (g1 - g0)))).astype(BF16)
    wgate = jnp.zeros((depth, GATE_PAD, 2 * hk), F32)
    wgate = wgate.at[:, :GLA_GATE_RANK, :hk].set(w_gate[:, 0])
    wgate = wgate.at[:, GLA_GATE_RANK:2 * GLA_GATE_RANK, hk:].set(w_gate[:, 1])
    bgate = b_gate.reshape(depth, 1, 2 * hk)
    base = np.ones((N_SEC, 1, SEC), np.float32)
    base[SEC_NA_Q] = NA_HEAD_DIM ** -0.5
    base[SEC_GLA_QK, 0, :hk] = SCAN_DK ** -0.5
    base[SEC_RET_QK, 0, :hk] = SCAN_DK ** -0.5
    reps = SEC // NA_HEAD_DIM
    ones = jnp.ones((depth, N_SEC - 2, 1, SEC), F32)
    gains = jnp.concatenate([jnp.tile(q_norm, (1, reps))[:, None, None, :],
                             jnp.tile(k_norm, (1, reps))[:, None, None, :], ones], axis=1)
    return dict(
        nw=norm_w.reshape(depth, 1, d),
        wpack=_pack_w_in(w_in), wg=wg, wgate=wgate.astype(BF16), bgate=bgate,
        colscale=gains * jnp.asarray(base)[None],
        wb=w_branch.astype(BF16), wo=w_out.astype(BF16),
        gla_gain=gla_out_norm.reshape(depth, 1, BRANCH_W),
        ret_gain=ret_out_norm.reshape(depth, 1, BRANCH_W),
    )


def _tables(seq, ctx_len):
    i = np.arange(TRI_ROWS)
    same = (i[:, None] // CHUNK) == (i[None, :] // CHUNK)
    ltri = jnp.asarray((same & (i[None, :] <= i[:, None])).astype(np.float32)).astype(BF16)
    utri = jnp.asarray((same & (i[None, :] >= i[:, None])).astype(np.float32)).astype(BF16)
    grp = np.arange(MXU_DIM) // NA_HEAD_DIM
    gmat = jnp.asarray((grp[:, None] == grp[None, :]).astype(np.float32)).astype(BF16)
    quarter = SCAN_DK // 4
    n_rows = seq // GRID_W
    inv = ROPE_BASE ** (-jnp.arange(quarter, dtype=F32) / quarter)
    ang_r = jnp.arange(n_rows, dtype=F32)[:, None] * inv[None, :]
    ang_c = jnp.arange(GRID_W, dtype=F32)[:, None] * inv[None, :]
    by_row = lambda a: jnp.broadcast_to(a[:, None, :], (n_rows, GRID_W, quarter)).reshape(seq, quarter)
    by_col = lambda a: jnp.broadcast_to(a[None, :, :], (n_rows, GRID_W, quarter)).reshape(seq, quarter)
    cr, sr = by_row(jnp.cos(ang_r)), by_row(jnp.sin(ang_r))
    cc, sc = by_col(jnp.cos(ang_c)), by_col(jnp.sin(ang_c))
    cos = jnp.concatenate([cr, cc] * SCAN_HEADS, axis=1)
    sin = jnp.concatenate([sr, sc] * SCAN_HEADS, axis=1)
    lat = dict(ltri=ltri, utri=utri, gmat=gmat, cos=cos, sin=sin)
    dummy = jnp.zeros((ctx_len, LANES), F32)
    ctx = dict(ltri=ltri, utri=utri, gmat=gmat, cos=dummy, sin=dummy)
    return lat, ctx


def _log_gamma(offset):
    g = jnp.log1p(-jnp.exp2(-(offset + jnp.arange(SCAN_HEADS, dtype=F32))))
    return jnp.tile(jnp.repeat(g, SCAN_DK // 2), 2)


def kernel(x, c, ctx, c_ctx, w_mod, b_mod, norm_w, w_in, na_q_norm, na_k_norm, na_rpb, gla_w_gate,
           gla_b_gate, gla_out_norm, ret_out_norm, w_branch, w_out):
    batch, seq, d = x.shape
    ctx_len = ctx.shape[1]
    depth = w_mod.shape[0]
    rows = seq // GRID_W
    tm_proj, tm_merge, tb_scan = 512, 512, 512
    na_blocks_per_step = 4

    c_rows = jnp.zeros((8, d), F32).at[:batch].set(c).at[batch].set(c_ctx)
    mod = _modulation(c_rows, w_mod, b_mod).reshape(depth, 8, 1, 3 * d)
    prm = _prep_params(norm_w, w_in, na_q_norm, na_k_norm, gla_w_gate, gla_b_gate, gla_out_norm,
                       ret_out_norm, w_branch, w_out)
    lat_tabs, ctx_tabs = _tables(seq, ctx_len)
    bias = _na_bias_tables(na_rpb, rows)
    lg = jnp.stack([_log_gamma(RET_DECAY_FWD), _log_gamma(RET_DECAY_BWD)])
    ctx_row = lambda i: batch

    xl = x.reshape(batch * seq, d)
    xc = ctx.reshape(batch * ctx_len, d)
    for layer in range(depth):
        with_ctx = layer < depth - 1
        p_lat, cum_lat = _in_proj(xl, mod, prm, lat_tabs, layer=layer, tm=tm_proj,
                                  row_of=lambda i: i // (seq // tm_proj),
                                  blocks_per_batch=seq // tm_proj, rope=True)
        p_ctx, cum_ctx = _in_proj(xc, mod, prm, ctx_tabs, layer=layer, tm=ctx_len, row_of=ctx_row,
                                  blocks_per_batch=1, rope=False)

        na_l = _na_attention(p_lat, p_ctx, bias, layer=layer, batch=batch, seq=seq, ctx_len=ctx_len,
                             n_sub=na_blocks_per_step)

        gla_cf, gla_cb, gla_s = _scan(p_ctx, cum_ctx, None, None, gla=True, batch=batch,
                                      rows_per_batch=ctx_len, tb=ctx_len)
        gla_f, gla_b, _ = _scan(p_lat, cum_lat, None, gla_s, gla=True, batch=batch,
                                rows_per_batch=seq, tb=tb_scan)
        ret_cf, ret_cb, ret_s = _scan(p_ctx, None, lg, None, gla=False, batch=batch,
                                      rows_per_batch=ctx_len, tb=ctx_len)
        ret_f, ret_b, _ = _scan(p_lat, None, lg, ret_s, gla=False, batch=batch,
                                rows_per_batch=seq, tb=tb_scan)

        xl_new = _merge(xl, mod, prm, na_l, gla_f, gla_b, ret_f, ret_b, layer=layer, tm=tm_merge,
                        row_of=lambda i: i // (seq // tm_merge))
        if with_ctx:
            na_c = _ctx_attention(p_ctx, batch=batch, ctx_len=ctx_len)
            xc = _merge(xc, mod, prm, na_c, gla_cf, gla_cb, ret_cf, ret_cb, layer=layer, tm=ctx_len,
                        row_of=ctx_row)
        xl = xl_new
    return xl.reshape(batch, seq, d)
```

```python
import functools

import numpy as np
import jax
import jax.numpy as jnp
from jax import lax
from jax.experimental import pallas as pl
from jax.experimental.pallas import tpu as pltpu

D_MODEL = 1024
GRID_W = 64
NORM_EPS = 1e-6
N_BRANCH = 3
BRANCH_W = D_MODEL // 2
NA_HEAD_DIM = 64
NA_HEADS = BRANCH_W // NA_HEAD_DIM
NA_WIN_ROWS = 8
NA_WIN_COLS = 16
SCAN_HEADS = 4
SCAN_DV = BRANCH_W // SCAN_HEADS
SCAN_DK = SCAN_DV // 2
GLA_GATE_RANK = 16
GLA_GATE_TAU = 16.0
RET_DECAY_FWD = 5.0
RET_DECAY_BWD = 5.5
CHUNK = 64
ROPE_BASE = 10000.0

LANES = 128
MXU_DIM = 256

SEC = 512
SEC_NA_Q, SEC_NA_K, SEC_NA_V, SEC_GLA_QK, SEC_GLA_V, SEC_RET_QK, SEC_RET_V = range(7)
N_SEC = 7
GATE_PAD = LANES
TRI_ROWS = MXU_DIM
NA_QROWS = 4
NA_KROWS = 12
NEG_BIG = -1e30

VMEM_LIMIT = 56 * 1024 * 1024

F32 = jnp.float32
BF16 = jnp.bfloat16


def _dot(a, b):
    return jnp.dot(a, b, preferred_element_type=F32)


def _dot_nt(a, b):
    return lax.dot_general(a, b, (((1,), (1,)), ((), ())), preferred_element_type=F32)


def _dot_tn(a, b):
    return lax.dot_general(a, b, (((0,), (0,)), ((), ())), preferred_element_type=F32)


def _split2(x):
    hi = x.astype(BF16)
    lo = (x - hi.astype(F32)).astype(BF16)
    return hi, lo


def _log_sigmoid(x):
    return jnp.minimum(x, 0.0) - jnp.log(1.0 + jnp.exp(-jnp.abs(x)))


def _sigmoid(x):
    return 1.0 / (1.0 + jnp.exp(-x))


def _silu(x):
    return x * _sigmoid(x)


def _params(*sem):
    return pltpu.CompilerParams(dimension_semantics=sem, vmem_limit_bytes=VMEM_LIMIT)


def _resident(block_shape, index_map):
    return pl.BlockSpec(block_shape, index_map, pipeline_mode=pl.Buffered(1))


def _ada_norm(x, nw, sc, sh):
    ms = jnp.mean(x * x, axis=-1, keepdims=True)
    return (x * lax.rsqrt(ms + NORM_EPS)) * (nw * (1.0 + sc)) + sh


def _mod_kernel(c_ref, w_ref, b_ref, o_ref):
    a = _silu(c_ref[...])
    o_ref[...] = jnp.dot(a, w_ref[...], preferred_element_type=F32,
                         precision=lax.Precision.HIGHEST) + b_ref[...]


def _modulation(c_rows, w_mod, b_mod):
    depth, d, d3 = w_mod.shape
    rows = c_rows.shape[0]
    tn = 1024
    return pl.pallas_call(
        _mod_kernel,
        grid=(depth, d3 // tn),
        in_specs=[
            pl.BlockSpec((rows, d), lambda l, j: (0, 0)),
            pl.BlockSpec((None, d, tn), lambda l, j: (l, 0, j)),
            pl.BlockSpec((None, 1, tn), lambda l, j: (l, 0, j)),
        ],
        out_specs=pl.BlockSpec((None, rows, tn), lambda l, j: (l, 0, j)),
        out_shape=jax.ShapeDtypeStruct((depth, rows, d3), F32),
        compiler_params=_params("parallel", "parallel"),
        name="modulation",
    )(c_rows, w_mod, b_mod.reshape(depth, 1, d3))


def _mod_specs(mod, layer, row_of):
    d = mod.shape[-1] // 3
    spec = lambda third: pl.BlockSpec((None, None, 1, d), lambda i: (layer, row_of(i), 0, third))
    return spec(0), spec(1), spec(2)


def _in_proj_kernel(x_ref, nw_ref, sh_ref, sc_ref, w_ref, cs_ref, gm_ref, wg_ref, wgate_ref,
                    bgate_ref, ltri_ref, utri_ref, rt_ref, ct_ref, p_ref, cum_ref, *, rope):
    tm = x_ref.shape[0]
    hb = _ada_norm(x_ref[...], nw_ref[...], sc_ref[...], sh_ref[...]).astype(BF16)

    g = _dot(hb, wg_ref[...])
    xg = _dot(g.astype(BF16), wgate_ref[...]) + bgate_ref[...]
    la = _log_sigmoid(xg) * (1.0 / GLA_GATE_TAU)
    half = la.shape[1] // 2
    for r in range(tm // TRI_ROWS):
        rows = slice(r * TRI_ROWS, (r + 1) * TRI_ROWS)
        fh, fl = _split2(la[rows, :half])
        bh, bl = _split2(la[rows, half:])
        cum_ref[rows, :half] = _dot(ltri_ref[...], fh) + _dot(ltri_ref[...], fl)
        cum_ref[rows, half:] = _dot(utri_ref[...], bh) + _dot(utri_ref[...], bl)

    for sec in range(N_SEC):
        base = sec * SEC
        raw = _dot(hb, w_ref[:, base:base + SEC])
        acc = raw * cs_ref[sec]
        if sec <= SEC_NA_K:
            for c in range(0, SEC, MXU_DIM):
                sq = raw[:, c:c + MXU_DIM] * raw[:, c:c + MXU_DIM]
                ss = _dot(sq.astype(BF16), gm_ref[...])
                p_ref[:, base + c:base + c + MXU_DIM] = (
                    acc[:, c:c + MXU_DIM] * lax.rsqrt(ss * (1.0 / NA_HEAD_DIM) + NORM_EPS)).astype(BF16)
        elif rope and sec == SEC_RET_QK:
            n_grid_rows = tm // GRID_W

            def table(t):
                by_row = jnp.concatenate(
                    [jnp.broadcast_to(rt_ref[t, g:g + 1, :], (GRID_W, LANES)) for g in range(n_grid_rows)],
                    axis=0)
                return by_row + jnp.concatenate([ct_ref[t]] * n_grid_rows, axis=0)

            cos = table(0)
            sin = table(1)
            for c in range(0, SEC, 2 * LANES):
                u1 = acc[:, c:c + LANES]
                u2 = acc[:, c + LANES:c + 2 * LANES]
                p_ref[:, base + c:base + c + LANES] = (u1 * cos - u2 * sin).astype(BF16)
                p_ref[:, base + c + LANES:base + c + 2 * LANES] = (u1 * sin + u2 * cos).astype(BF16)
        else:
            p_ref[:, base:base + SEC] = acc.astype(BF16)


def _in_proj(x2d, mod, prm, tabs, *, layer, tm, row_of, blocks_per_batch, rope):
    t, d = x2d.shape
    const2 = lambda i: (0, 0)
    lconst = lambda i: (layer, 0, 0)
    row_tab = pl.BlockSpec((2, tm // GRID_W, LANES), lambda i: (0, i % blocks_per_batch, 0))
    col_tab = _resident((2, GRID_W, LANES), lambda i: (0, 0, 0))
    sh_spec, sc_spec, _ = _mod_specs(mod, layer, row_of)
    return pl.pallas_call(
        functools.partial(_in_proj_kernel, rope=rope),
        grid=(t // tm,),
        in_specs=[
            pl.BlockSpec((tm, d), lambda i: (i, 0)),
            pl.BlockSpec((None, 1, d), lconst),
            sh_spec, sc_spec,
            _resident((None, d, N_SEC * SEC), lconst),
            _resident((None, N_SEC, 1, SEC), lambda i: (layer, 0, 0, 0)),
            _resident((MXU_DIM, MXU_DIM), const2),
            _resident((None, d, GATE_PAD), lconst),
            _resident((None, GATE_PAD, SEC), lconst),
            _resident((None, 1, SEC), lconst),
            _resident((TRI_ROWS, TRI_ROWS), const2),
            _resident((TRI_ROWS, TRI_ROWS), const2),
            row_tab, col_tab,
        ],
        out_specs=[
            pl.BlockSpec((tm, N_SEC * SEC), lambda i: (i, 0)),
            pl.BlockSpec((tm, SEC), lambda i: (i, 0)),
        ],
        out_shape=[
            jax.ShapeDtypeStruct((t, N_SEC * SEC), BF16),
            jax.ShapeDtypeStruct((t, SEC), F32),
        ],
        compiler_params=_params("parallel"),
        name="in_proj_rope" if rope else "in_proj",
    )(x2d, prm["nw"], mod, mod, prm["wpack"], prm["colscale"], tabs["gmat"], prm["wg"],
      prm["wgate"], prm["bgate"], tabs["ltri"], tabs["utri"], tabs["rot_rows"], tabs["rot_cols"])


def _na_block_plan(rows):
    n_qb = rows // NA_QROWS
    kr = min(NA_WIN_ROWS, rows)
    plans = []
    for qb in (0, min(1, n_qb - 1), n_qb - 1):
        t0 = int(np.clip(qb - 1, 0, n_qb - NA_KROWS // NA_QROWS))
        plan = []
        for i in range(NA_QROWS):
            r = qb * NA_QROWS + i
            rs = int(np.clip(r - kr // 2, 0, rows - kr))
            plan.append([(t0 * NA_QROWS + j) - r + NA_WIN_ROWS - 1
                         if rs <= t0 * NA_QROWS + j < rs + kr else None
                         for j in range(NA_KROWS)])
        plans.append(plan)
    return plans


def _bias_kernel(rpb_ref, o_ref, *, plans):
    shape = (GRID_W, LANES)
    lane = lax.broadcasted_iota(jnp.int32, shape, 1)
    c = lax.broadcasted_iota(jnp.int32, shape, 0)
    kc = lane & (GRID_W - 1)
    cs = jnp.clip(c - NA_WIN_COLS // 2, 0, GRID_W - NA_WIN_COLS)
    col_ok = jnp.logical_and(kc >= cs, kc < cs + NA_WIN_COLS)
    low = lane < GRID_W
    neg = jnp.full(shape, NEG_BIG, F32)
    toep = []
    for a in range(2 * NA_WIN_ROWS - 1):
        x = jnp.broadcast_to(rpb_ref[a:a + 1, :], shape)
        lo = pltpu.roll(x, LANES - (NA_WIN_COLS - 1), 1, stride=1, stride_axis=0)
        hi = pltpu.roll(x, GRID_W - (NA_WIN_COLS - 1), 1, stride=1, stride_axis=0)
        toep.append(jnp.where(col_ok, jnp.where(low, lo, hi), neg))
    for k, plan in enumerate(plans):
        for i in range(NA_QROWS):
            for jp in range(NA_KROWS // 2):
                a0, a1 = plan[i][2 * jp], plan[i][2 * jp + 1]
                b0 = neg if a0 is None else toep[a0]
                b1 = neg if a1 is None else toep[a1]
                o_ref[k, i * GRID_W:(i + 1) * GRID_W, jp * LANES:(jp + 1) * LANES] = (
                    jnp.where(low, b0, b1))


def _na_bias_tables(rpb, rows):
    depth, heads, nr, nc = rpb.shape
    rp = jnp.pad(rpb, ((0, 0), (0, 0), (0, 16 - nr), (0, LANES - nc)))
    tq, tk = NA_QROWS * GRID_W, NA_KROWS * GRID_W
    return pl.pallas_call(
        functools.partial(_bias_kernel, plans=_na_block_plan(rows)),
        grid=(depth, heads),
        in_specs=[pl.BlockSpec((None, None, 16, LANES), lambda l, h: (l, h, 0, 0))],
        out_specs=pl.BlockSpec((None, 3, None, tq, tk), lambda l, h: (l, 0, h, 0, 0)),
        out_shape=jax.ShapeDtypeStruct((depth, 3, heads, tq, tk), F32),
        compiler_params=_params("parallel", "parallel"),
        name="na_bias",
    )(rp)


def _lane_mask(width, lo, hi):
    lane = lax.broadcasted_iota(jnp.int32, (1, width), 1)
    return jnp.logical_and(lane >= lo, lane < hi)


def _na_kernel(q_ref, k_ref, v_ref, kc_ref, vc_ref, bias_ref, o_ref, *, n_qb, n_sub):
    tq = q_ref.shape[0] // n_sub
    tk = bias_ref.shape[-1]
    width = q_ref.shape[1]
    heads = width // NA_HEAD_DIM
    kc = kc_ref[...]
    vc = vc_ref[...]
    for sub in range(n_sub):
        qb = pl.program_id(2) * n_sub + sub
        kind = jnp.where(qb == 0, 0, jnp.where(qb == n_qb - 1, 2, 1))
        t0 = jnp.clip(qb - 1, 0, n_qb - tk // tq)
        start = pl.multiple_of(t0 * tq, tq)
        q = q_ref[sub * tq:(sub + 1) * tq, :]
        k3 = k_ref[pl.ds(start, tk), :]
        v3 = v_ref[pl.ds(start, tk), :]
        acc = jnp.zeros(q.shape, F32)
        for h in range(heads):
            inside = _lane_mask(width, h * NA_HEAD_DIM, (h + 1) * NA_HEAD_DIM)
            sum_lane = ((h + 1) % heads) * NA_HEAD_DIM
            ones_row = _lane_mask(width, sum_lane, sum_lane + 1)
            keep = inside.astype(BF16)
            fill = ones_row.astype(BF16)
            qh = q * keep
            s = _dot_nt(qh, k3) + bias_ref[kind, h]
            sc = _dot_nt(qh, kc)
            m = jnp.maximum(jnp.max(s, axis=-1, keepdims=True), jnp.max(sc, axis=-1, keepdims=True))
            p = jnp.exp((s - m).astype(BF16))
            pc = jnp.exp((sc - m).astype(BF16))
            oh = _dot(p, v3 * keep + fill) + _dot(pc, vc * keep + fill)
            l = jnp.sum(jnp.where(ones_row, oh, 0.0), axis=-1, keepdims=True)
            acc = jnp.where(inside, oh * (1.0 / l), acc)
        o_ref[sub * tq:(sub + 1) * tq, :] = acc.astype(BF16)


def _na_attention(p_lat, p_ctx, bias, *, layer, batch, seq, ctx_len, n_sub):
    tq = NA_QROWS * GRID_W
    tk = NA_KROWS * GRID_W
    n_qb = seq // tq
    n_steps = n_qb // n_sub
    hw = 2 * LANES
    n_hg = BRANCH_W // hw
    sec_blk = SEC // hw
    heads = hw // NA_HEAD_DIM
    return pl.pallas_call(
        functools.partial(_na_kernel, n_qb=n_qb, n_sub=n_sub),
        grid=(batch, n_hg, n_steps),
        in_specs=[
            pl.BlockSpec((n_sub * tq, hw), lambda b, g, i: (b * n_steps + i, SEC_NA_Q * sec_blk + g)),
            pl.BlockSpec((seq, hw), lambda b, g, i: (b, SEC_NA_K * sec_blk + g)),
            pl.BlockSpec((seq, hw), lambda b, g, i: (b, SEC_NA_V * sec_blk + g)),
            pl.BlockSpec((ctx_len, hw), lambda b, g, i: (b, SEC_NA_K * sec_blk + g)),
            pl.BlockSpec((ctx_len, hw), lambda b, g, i: (b, SEC_NA_V * sec_blk + g)),
            pl.BlockSpec((None, bias.shape[1], heads, tq, tk), lambda b, g, i: (layer, 0, g, 0, 0)),
        ],
        out_specs=pl.BlockSpec((n_sub * tq, hw), lambda b, g, i: (b * n_steps + i, g)),
        out_shape=jax.ShapeDtypeStruct((batch * seq, BRANCH_W), BF16),
        compiler_params=_params("parallel", "parallel", "arbitrary"),
        name="na_attention",
    )(p_lat, p_lat, p_lat, p_ctx, p_ctx, bias)


def _ctx_attn_kernel(q_ref, k_ref, v_ref, o_ref):
    first = _lane_mask(LANES, 0, NA_HEAD_DIM)
    sel = (first.astype(BF16), jnp.logical_not(first).astype(BF16))
    outs = []
    for pair in range(q_ref.shape[1] // LANES):
        lanes = slice(pair * LANES, (pair + 1) * LANES)
        q2, k2, v2 = q_ref[:, lanes], k_ref[:, lanes], v_ref[:, lanes]
        o_pair = []
        for hh in range(2):
            s = _dot_nt(q2 * sel[hh], k2)
            m = jnp.max(s, axis=-1, keepdims=True)
            p = jnp.exp(s - m)
            l = jnp.sum(p, axis=-1, keepdims=True)
            o_pair.append(_dot(p.astype(BF16), v2) * (1.0 / l))
        outs.append(jnp.where(first, o_pair[0], o_pair[1]))
    o_ref[...] = jnp.concatenate(outs, axis=1).astype(BF16)


def _ctx_attention(p_ctx, *, batch, ctx_len):
    hw = 2 * LANES
    n_hg = BRANCH_W // hw
    sec_blk = SEC // hw
    spec = lambda sec: pl.BlockSpec((ctx_len, hw), lambda b, g: (b, sec * sec_blk + g))
    return pl.pallas_call(
        _ctx_attn_kernel,
        grid=(batch, n_hg),
        in_specs=[spec(SEC_NA_Q), spec(SEC_NA_K), spec(SEC_NA_V)],
        out_specs=pl.BlockSpec((ctx_len, hw), lambda b, g: (b, g)),
        out_shape=jax.ShapeDtypeStruct((batch * ctx_len, BRANCH_W), BF16),
        compiler_params=_params("parallel", "parallel"),
        name="ctx_attention",
    )(p_ctx, p_ctx, p_ctx)


def _scan_head_of_lane(gla):
    lane = lax.broadcasted_iota(jnp.int32, (1, SCAN_HEADS * SCAN_DK), 1)
    if gla:
        return lax.shift_right_logical(lane, SCAN_DK.bit_length() - 1)
    return lax.shift_right_logical(lane & (LANES - 1), (SCAN_DK // 2).bit_length() - 1)


def _decay_factors(cum, ref_row, total_row):
    cm = cum[ref_row:ref_row + 1, :]
    tot = cum[total_row:total_row + 1, :]
    eq = jnp.exp(cum - cm)
    ek = jnp.exp(cm - cum)
    return eq, ek, jnp.exp(cm), jnp.exp(tot - cm), jnp.exp(tot)


def _chunk_step(q, k, v, factors, tri_stack, st, head_sel):
    eq, ek, e_m, e_tm, e_t = factors
    qt = q * eq
    kt = k * ek
    qtb = qt.astype(BF16)
    qhb = (qt * e_m).astype(BF16)
    khb = (kt * e_tm).astype(BF16)
    hmul = [sel.astype(BF16) for sel in head_sel]
    q_stack = jnp.concatenate([qtb * m for m in hmul], axis=0)
    qh_stack = jnp.concatenate([qhb * m for m in hmul], axis=0)
    att = jnp.where(tri_stack, _dot_nt(q_stack, kt.astype(BF16)), 0.0).astype(BF16)
    inter = _dot_nt(qh_stack, st.astype(BF16))
    outs = []
    for h in range(SCAN_HEADS):
        rows = slice(h * CHUNK, (h + 1) * CHUNK)
        outs.append(_dot(att[rows], v[:, h * SCAN_DV:(h + 1) * SCAN_DV]) + inter[rows])
    v_rows = jnp.concatenate([v[:, h * SCAN_DV:(h + 1) * SCAN_DV] for h in range(SCAN_HEADS)], axis=0)
    k_rows = jnp.concatenate([khb * m for m in hmul], axis=0)
    return jnp.concatenate(outs, axis=1), st * e_t + _dot_tn(v_rows, k_rows)


def _scan_masks(gla):
    head = _scan_head_of_lane(gla)
    head_sel = [head == h for h in range(SCAN_HEADS)]
    t = lax.broadcasted_iota(jnp.int32, (SCAN_HEADS * CHUNK, CHUNK), 0) & (CHUNK - 1)
    s = lax.broadcasted_iota(jnp.int32, (SCAN_HEADS * CHUNK, CHUNK), 1)
    return head_sel, s <= t, s >= t


def _ret_cum(lg_row, forward):
    t = lax.broadcasted_iota(jnp.int32, (CHUNK, lg_row.shape[1]), 0).astype(F32)
    steps = (t + 1.0) if forward else (float(CHUNK) - t)
    return steps * lg_row


def _scan_kernel(*refs, gla, n_chunks, from_zero):
    if gla:
        (qkf_ref, vf_ref, cf_ref, qkb_ref, vb_ref, cb_ref, s0_ref,
         of_ref, ob_ref, sfin_ref, stf, stb) = refs
    else:
        (qkf_ref, vf_ref, qkb_ref, vb_ref, lg_ref, s0_ref,
         of_ref, ob_ref, sfin_ref, stf, stb) = refs
    step = pl.program_id(1)
    hk = SCAN_HEADS * SCAN_DK

    @pl.when(step == 0)
    def _():
        if from_zero:
            stf[...] = jnp.zeros_like(stf)
            stb[...] = jnp.zeros_like(stb)
        else:
            stf[...] = s0_ref[0]
            stb[...] = s0_ref[1]

    head_sel, tril, triu = _scan_masks(gla)
    if not gla:
        fac_f = _decay_factors(_ret_cum(lg_ref[0:1, :], True), CHUNK // 2 - 1, CHUNK - 1)
        fac_b = _decay_factors(_ret_cum(lg_ref[1:2, :], False), CHUNK // 2, 0)

    sf = stf[...]
    sb = stb[...]
    for c in range(n_chunks):
        rf = slice(c * CHUNK, (c + 1) * CHUNK)
        cb = n_chunks - 1 - c
        rb = slice(cb * CHUNK, (cb + 1) * CHUNK)
        if gla:
            fac_f = _decay_factors(cf_ref[rf, :], CHUNK // 2 - 1, CHUNK - 1)
            fac_b = _decay_factors(cb_ref[rb, :], CHUNK // 2, 0)
        qkf = qkf_ref[rf, :].astype(F32)
        o, sf = _chunk_step(qkf[:, :hk], qkf[:, hk:], vf_ref[rf, :], fac_f, tril, sf, head_sel)
        of_ref[rf, :] = o.astype(BF16)
        qkb = qkb_ref[rb, :].astype(F32)
        o, sb = _chunk_step(qkb[:, :hk], qkb[:, hk:], vb_ref[rb, :], fac_b, triu, sb, head_sel)
        ob_ref[rb, :] = o.astype(BF16)
    stf[...] = sf
    stb[...] = sb

    @pl.when(step == pl.num_programs(1) - 1)
    def _():
        sfin_ref[0] = sf
        sfin_ref[1] = sb


def _scan(p, cum, lg, s0, *, gla, batch, rows_per_batch, tb):
    nb = rows_per_batch // tb
    hk = SCAN_HEADS * SCAN_DK
    hv = SCAN_HEADS * SCAN_DV
    sec_qk = SEC_GLA_QK if gla else SEC_RET_QK
    sec_v = SEC_GLA_V if gla else SEC_RET_V
    fwd = lambda b, s: b * nb + s
    bwd = lambda b, s: b * nb + nb - 1 - s
    from_zero = s0 is None
    if from_zero:
        s0 = jnp.zeros((batch, 2, SCAN_DV, hk), F32)
    in_specs = [
        pl.BlockSpec((tb, SEC), lambda b, s: (fwd(b, s), sec_qk)),
        pl.BlockSpec((tb, SEC), lambda b, s: (fwd(b, s), sec_v)),
    ]
    args = [p, p]
    if gla:
        in_specs.append(pl.BlockSpec((tb, hk), lambda b, s: (fwd(b, s), 0)))
        args.append(cum)
    in_specs += [
        pl.BlockSpec((tb, SEC), lambda b, s: (bwd(b, s), sec_qk)),
        pl.BlockSpec((tb, SEC), lambda b, s: (bwd(b, s), sec_v)),
    ]
    args += [p, p]
    if gla:
        in_specs.append(pl.BlockSpec((tb, hk), lambda b, s: (bwd(b, s), 1)))
        args.append(cum)
    else:
        in_specs.append(pl.BlockSpec((2, hk), lambda b, s: (0, 0)))
        args.append(lg)
    in_specs.append(pl.BlockSpec((None, 2, SCAN_DV, hk), lambda b, s: (b, 0, 0, 0)))
    args.append(s0)
    return pl.pallas_call(
        functools.partial(_scan_kernel, gla=gla, n_chunks=tb // CHUNK, from_zero=from_zero),
        grid=(batch, nb),
        in_specs=in_specs,
        out_specs=[
            pl.BlockSpec((tb, hv), lambda b, s: (fwd(b, s), 0)),
            pl.BlockSpec((tb, hv), lambda b, s: (bwd(b, s), 0)),
            pl.BlockSpec((None, 2, SCAN_DV, hk), lambda b, s: (b, 0, 0, 0)),
        ],
        out_shape=[
            jax.ShapeDtypeStruct((batch * rows_per_batch, hv), BF16),
            jax.ShapeDtypeStruct((batch * rows_per_batch, hv), BF16),
            jax.ShapeDtypeStruct((batch, 2, SCAN_DV, hk), F32),
        ],
        scratch_shapes=[pltpu.VMEM((SCAN_DV, hk), F32), pltpu.VMEM((SCAN_DV, hk), F32)],
        compiler_params=_params("parallel", "arbitrary"),
        name=("gla" if gla else "ret") + ("_ctx_scan" if from_zero else "_scan"),
    )(*args)


def _head_rms(o, gain):
    parts = []
    for h in range(SCAN_HEADS):
        sl = o[:, h * SCAN_DV:(h + 1) * SCAN_DV]
        ms = jnp.mean(sl * sl, axis=-1, keepdims=True)
        parts.append(sl * lax.rsqrt(ms + NORM_EPS))
    return jnp.concatenate(parts, axis=1) * gain


def _merge_kernel(x_ref, nw_ref, sh_ref, sc_ref, gate_ref, na_ref, gf_ref, gb_ref, rf_ref, rb_ref,
                  gg_ref, rg_ref, wz0_ref, wz1_ref, wz2_ref, wg0_ref, wg1_ref, wg2_ref, wb_ref, wo_ref,
                  o_ref):
    x = x_ref[...]
    wz_refs = (wz0_ref, wz1_ref, wz2_ref)
    wg_refs = (wg0_ref, wg1_ref, wg2_ref)
    hb = _ada_norm(x, nw_ref[...], sc_ref[...], sh_ref[...]).astype(BF16)
    outs = (
        na_ref[...].astype(F32),
        _head_rms(gf_ref[...].astype(F32) + gb_ref[...].astype(F32), gg_ref[...]),
        _head_rms(rf_ref[...].astype(F32) + rb_ref[...].astype(F32), rg_ref[...]),
    )
    y = None
    for i in range(N_BRANCH):
        z = _dot(hb, wz_refs[i][...])
        u = (outs[i] * _silu(z)).astype(BF16)
        g = _dot(hb, wg_refs[i][...])
        t = _sigmoid(g) * _dot(u, wb_ref[i])
        y = t if y is None else y + t
    r = _dot(y.astype(BF16), wo_ref[...])
    o_ref[...] = x + gate_ref[...] * r


def _merge(x2d, mod, prm, na_o, gla_f, gla_b, ret_f, ret_b, *, layer, tm, row_of):
    t, d = x2d.shape
    row = lambda w: pl.BlockSpec((tm, w), lambda i: (i, 0))
    sh_spec, sc_spec, gate_spec = _mod_specs(mod, layer, row_of)
    lconst3 = lambda i: (layer, 0, 0)
    return pl.pallas_call(
        _merge_kernel,
        grid=(t // tm,),
        in_specs=[
            row(d),
            pl.BlockSpec((None, 1, d), lconst3),
            sh_spec, sc_spec, gate_spec,
            row(BRANCH_W), row(BRANCH_W), row(BRANCH_W), row(BRANCH_W), row(BRANCH_W),
            pl.BlockSpec((None, 1, BRANCH_W), lconst3),
            pl.BlockSpec((None, 1, BRANCH_W), lconst3),
            *[_resident((None, d, BRANCH_W), lambda i, n=n: (layer, 0, N_SEC * SEC // BRANCH_W + n))
              for n in range(N_BRANCH)],
            *[_resident((None, d, d), lambda i, n=n: (layer, 0, (N_SEC * SEC + N_BRANCH * BRANCH_W) // d + n))
              for n in range(N_BRANCH)],
            _resident((None, N_BRANCH, BRANCH_W, d), lambda i: (layer, 0, 0, 0)),
            _resident((None, d, d), lconst3),
        ],
        out_specs=row(d),
        out_shape=jax.ShapeDtypeStruct((t, d), F32),
        compiler_params=_params("parallel"),
        name="merge",
    )(x2d, prm["nw"], mod, mod, mod, na_o, gla_f, gla_b, ret_f, ret_b, prm["gla_gain"],
      prm["ret_gain"], *([prm["wpack"]] * (2 * N_BRANCH)), prm["wb"], prm["wo"])


def _in_splits():
    na = NA_HEADS * NA_HEAD_DIM
    qk = SCAN_HEADS * SCAN_DK
    vv = SCAN_HEADS * SCAN_DV
    sizes = (na, na, na, qk, qk, vv, 2 * GLA_GATE_RANK, qk, qk, vv, N_BRANCH * BRANCH_W,
             N_BRANCH * D_MODEL)
    return [int(v) for v in np.concatenate([[0], np.cumsum(sizes)])]


def _rotary_pack_matrix():
    hk = SCAN_HEADS * SCAN_DK
    quarter = SCAN_DK // 4
    p = np.zeros((SEC, SEC), np.float32)
    for blk in range(SEC // hk):
        for h in range(SCAN_HEADS):
            for half in range(2):
                for partner in range(2):
                    for i in range(quarter):
                        old = h * SCAN_DK + half * 2 * quarter + partner * quarter + i
                        new = partner * (hk // 2) + h * 2 * quarter + half * quarter + i
                        p[blk * hk + old, blk * hk + new] = 1.0
    return jnp.asarray(p).astype(BF16)


def _pack_kernel(a_ref, b_ref, perm_ref, o_ref, *, first_shifted, perm_block, shift):
    j = pl.program_id(1)
    sec = a_ref.shape[0]

    def shifted():
        x = jnp.concatenate([a_ref[...], b_ref[...]], axis=0)
        return x[shift:shift + sec]

    @pl.when(j < first_shifted)
    def _():
        o_ref[...] = a_ref[...].T.astype(BF16)

    @pl.when(j == perm_block)
    def _():
        o_ref[...] = _dot(shifted().T.astype(BF16), perm_ref[...]).astype(BF16)

    @pl.when(jnp.logical_and(j >= first_shifted, j != perm_block))
    def _():
        o_ref[...] = shifted().T.astype(BF16)


def _pack_w_in(w_in):
    depth, d, n_in = w_in.shape
    o = _in_splits()
    g0, g1 = o[6], o[7]
    assert g0 % SEC == 0 and (n_in - (g1 - g0)) % SEC == 0
    n_blocks = (n_in - (g1 - g0)) // SEC
    tail_rows = LANES
    w_t = jnp.swapaxes(w_in, 1, 2)
    return pl.pallas_call(
        functools.partial(_pack_kernel, first_shifted=g0 // SEC, perm_block=SEC_RET_QK,
                          shift=g1 - g0),
        grid=(depth, n_blocks),
        in_specs=[
            pl.BlockSpec((None, SEC, d), lambda l, j: (l, j, 0)),
            pl.BlockSpec((None, tail_rows, d), lambda l, j: (l, (j + 1) * (SEC // tail_rows), 0)),
            _resident((SEC, SEC), lambda l, j: (0, 0)),
        ],
        out_specs=pl.BlockSpec((None, d, SEC), lambda l, j: (l, 0, j)),
        out_shape=jax.ShapeDtypeStruct((depth, d, n_blocks * SEC), BF16),
        compiler_params=_params("parallel", "parallel"),
        name="pack_w_in",
    )(w_t, w_t, _rotary_pack_matrix())


def _prep_params(norm_w, w_in, q_norm, k_norm, w_gate, b_gate, gla_out_norm, ret_out_norm,
                 w_branch, w_out):
    depth, d, _ = w_in.shape
    o = _in_splits()
    g0, g1 = o[6], o[7]
    hk = SCAN_HEADS * SCAN_DK
    wg = jnp.pad(w_in[:, :, g0:g1], ((0, 0), (0, 0), (0, GATE_PAD - (g1 - g0)))).astype(BF16)
    wgate = jnp.zeros((depth, GATE_PAD, 2 * hk), F32)
    wgate = wgate.at[:, :GLA_GATE_RANK, :hk].set(w_gate[:, 0])
    wgate = wgate.at[:, GLA_GATE_RANK:2 * GLA_GATE_RANK, hk:].set(w_gate[:, 1])
    bgate = b_gate.reshape(depth, 1, 2 * hk)
    base = np.ones((N_SEC, 1, SEC), np.float32)
    base[SEC_NA_Q] = NA_HEAD_DIM ** -0.5
    base[SEC_GLA_QK, 0, :hk] = SCAN_DK ** -0.5
    base[SEC_RET_QK, 0, :hk] = SCAN_DK ** -0.5
    reps = SEC // NA_HEAD_DIM
    ones = jnp.ones((depth, N_SEC - 2, 1, SEC), F32)
    gains = jnp.concatenate([jnp.tile(q_norm, (1, reps))[:, None, None, :],
                             jnp.tile(k_norm, (1, reps))[:, None, None, :], ones], axis=1)
    return dict(
        nw=norm_w.reshape(depth, 1, d),
        wpack=_pack_w_in(w_in), wg=wg, wgate=wgate.astype(BF16), bgate=bgate,
        colscale=gains * jnp.asarray(base)[None],
        wb=w_branch.astype(BF16), wo=w_out.astype(BF16),
        gla_gain=gla_out_norm.reshape(depth, 1, BRANCH_W),
        ret_gain=ret_out_norm.reshape(depth, 1, BRANCH_W),
    )


def _tables(seq, ctx_len):
    i = np.arange(TRI_ROWS)
    same = (i[:, None] // CHUNK) == (i[None, :] // CHUNK)
    ltri = jnp.asarray((same & (i[None, :] <= i[:, None])).astype(np.float32)).astype(BF16)
    utri = jnp.asarray((same & (i[None, :] >= i[:, None])).astype(np.float32)).astype(BF16)
    grp = np.arange(MXU_DIM) // NA_HEAD_DIM
    gmat = jnp.asarray((grp[:, None] == grp[None, :]).astype(np.float32)).astype(BF16)
    quarter = SCAN_DK // 4
    n_rows = seq // GRID_W
    inv = ROPE_BASE ** (-jnp.arange(quarter, dtype=F32) / quarter)
    ang_r = jnp.arange(n_rows, dtype=F32)[:, None] * inv[None, :]
    ang_c = jnp.arange(GRID_W, dtype=F32)[:, None] * inv[None, :]
    zr, zc = jnp.zeros_like(ang_r), jnp.zeros_like(ang_c)
    lanes = lambda a, b: jnp.concatenate([a, b] * SCAN_HEADS, axis=1)
    rot_rows = jnp.stack([lanes(jnp.cos(ang_r), zr), lanes(jnp.sin(ang_r), zr)])
    rot_cols = jnp.stack([lanes(zc, jnp.cos(ang_c)), lanes(zc, jnp.sin(ang_c))])
    lat = dict(ltri=ltri, utri=utri, gmat=gmat, rot_rows=rot_rows, rot_cols=rot_cols)
    ctx = dict(ltri=ltri, utri=utri, gmat=gmat, rot_cols=jnp.zeros_like(rot_cols),
               rot_rows=jnp.zeros((2, ctx_len // GRID_W, LANES), F32))
    return lat, ctx


def _log_gamma(offset):
    g = jnp.log1p(-jnp.exp2(-(offset + jnp.arange(SCAN_HEADS, dtype=F32))))
    return jnp.tile(jnp.repeat(g, SCAN_DK // 2), 2)


def kernel(x, c, ctx, c_ctx, w_mod, b_mod, norm_w, w_in, na_q_norm, na_k_norm, na_rpb, gla_w_gate,
           gla_b_gate, gla_out_norm, ret_out_norm, w_branch, w_out):
    batch, seq, d = x.shape
    ctx_len = ctx.shape[1]
    depth = w_mod.shape[0]
    rows = seq // GRID_W
    tm_proj, tm_merge, tb_scan = 512, 512, 512
    na_blocks_per_step = 4

    c_rows = jnp.zeros((8, d), F32).at[:batch].set(c).at[batch].set(c_ctx)
    mod = _modulation(c_rows, w_mod, b_mod).reshape(depth, 8, 1, 3 * d)
    prm = _prep_params(norm_w, w_in, na_q_norm, na_k_norm, gla_w_gate, gla_b_gate, gla_out_norm,
                       ret_out_norm, w_branch, w_out)
    lat_tabs, ctx_tabs = _tables(seq, ctx_len)
    bias = _na_bias_tables(na_rpb, rows)
    lg = jnp.stack([_log_gamma(RET_DECAY_FWD), _log_gamma(RET_DECAY_BWD)])
    ctx_row = lambda i: batch

    xl = x.reshape(batch * seq, d)
    xc = ctx.reshape(batch * ctx_len, d)
    for layer in range(depth):
        with_ctx = layer < depth - 1
        p_lat, cum_lat = _in_proj(xl, mod, prm, lat_tabs, layer=layer, tm=tm_proj,
                                  row_of=lambda i: i // (seq // tm_proj),
                                  blocks_per_batch=seq // tm_proj, rope=True)
        p_ctx, cum_ctx = _in_proj(xc, mod, prm, ctx_tabs, layer=layer, tm=ctx_len, row_of=ctx_row,
                                  blocks_per_batch=1, rope=False)

        na_l = _na_attention(p_lat, p_ctx, bias, layer=layer, batch=batch, seq=seq, ctx_len=ctx_len,
                             n_sub=na_blocks_per_step)

        gla_cf, gla_cb, gla_s = _scan(p_ctx, cum_ctx, None, None, gla=True, batch=batch,
                                      rows_per_batch=ctx_len, tb=ctx_len)
        gla_f, gla_b, _ = _scan(p_lat, cum_lat, None, gla_s, gla=True, batch=batch,
                                rows_per_batch=seq, tb=tb_scan)
        ret_cf, ret_cb, ret_s = _scan(p_ctx, None, lg, None, gla=False, batch=batch,
                                      rows_per_batch=ctx_len, tb=ctx_len)
        ret_f, ret_b, _ = _scan(p_lat, None, lg, ret_s, gla=False, batch=batch,
                                rows_per_batch=seq, tb=tb_scan)

        xl_new = _merge(xl, mod, prm, na_l, gla_f, gla_b, ret_f, ret_b, layer=layer, tm=tm_merge,
                        row_of=lambda i: i // (seq // tm_merge))
        if with_ctx:
            na_c = _ctx_attention(p_ctx, batch=batch, ctx_len=ctx_len)
            xc = _merge(xc, mod, prm, na_c, gla_cf, gla_cb, ret_cf, ret_cb, layer=layer, tm=ctx_len,
                        row_of=ctx_row)
        xl = xl_new
    return xl.reshape(batch, seq, d)
```

```python
import functools

import numpy as np
import jax
import jax.numpy as jnp
from jax import lax
from jax.experimental import pallas as pl
from jax.experimental.pallas import tpu as pltpu

D_MODEL = 1024
GRID_W = 64
NORM_EPS = 1e-6
N_BRANCH = 3
BRANCH_W = D_MODEL // 2
NA_HEAD_DIM = 64
NA_HEADS = BRANCH_W // NA_HEAD_DIM
NA_WIN_ROWS = 8
NA_WIN_COLS = 16
SCAN_HEADS = 4
SCAN_DV = BRANCH_W // SCAN_HEADS
SCAN_DK = SCAN_DV // 2
GLA_GATE_RANK = 16
GLA_GATE_TAU = 16.0
RET_DECAY_FWD = 5.0
RET_DECAY_BWD = 5.5
CHUNK = 64
ROPE_BASE = 10000.0

LANES = 128
MXU_DIM = 256

SEC = 512
SEC_NA_Q, SEC_NA_K, SEC_NA_V, SEC_GLA_QK, SEC_GLA_V, SEC_RET_QK, SEC_RET_V = range(7)
N_SEC = 7
GATE_PAD = LANES
TRI_ROWS = MXU_DIM
NA_QROWS = 4
NA_KROWS = 12
NEG_BIG = -1e30

VMEM_LIMIT = 56 * 1024 * 1024

F32 = jnp.float32
BF16 = jnp.bfloat16


def _dot(a, b):
    return jnp.dot(a, b, preferred_element_type=F32)


def _dot_nt(a, b):
    return lax.dot_general(a, b, (((1,), (1,)), ((), ())), preferred_element_type=F32)


def _dot_tn(a, b):
    return lax.dot_general(a, b, (((0,), (0,)), ((), ())), preferred_element_type=F32)


def _split2(x):
    hi = x.astype(BF16)
    lo = (x - hi.astype(F32)).astype(BF16)
    return hi, lo


def _log_sigmoid(x):
    return jnp.minimum(x, 0.0) - jnp.log(1.0 + jnp.exp(-jnp.abs(x)))


def _sigmoid(x):
    return 1.0 / (1.0 + jnp.exp(-x))


def _silu(x):
    return x * _sigmoid(x)


def _params(*sem):
    return pltpu.CompilerParams(dimension_semantics=sem, vmem_limit_bytes=VMEM_LIMIT)


def _resident(block_shape, index_map):
    return pl.BlockSpec(block_shape, index_map, pipeline_mode=pl.Buffered(1))


def _ada_norm(x, nw, sc, sh):
    ms = jnp.mean(x * x, axis=-1, keepdims=True)
    return (x * lax.rsqrt(ms + NORM_EPS)) * (nw * (1.0 + sc)) + sh


def _mod_kernel(c_ref, w_ref, b_ref, o_ref):
    a = _silu(c_ref[...])
    o_ref[...] = jnp.dot(a, w_ref[...], preferred_element_type=F32,
                         precision=lax.Precision.HIGHEST) + b_ref[...]


def _modulation(c_rows, w_mod, b_mod):
    depth, d, d3 = w_mod.shape
    rows = c_rows.shape[0]
    tn = 1024
    return pl.pallas_call(
        _mod_kernel,
        grid=(depth, d3 // tn),
        in_specs=[
            pl.BlockSpec((rows, d), lambda l, j: (0, 0)),
            pl.BlockSpec((None, d, tn), lambda l, j: (l, 0, j)),
            pl.BlockSpec((None, 1, tn), lambda l, j: (l, 0, j)),
        ],
        out_specs=pl.BlockSpec((None, rows, tn), lambda l, j: (l, 0, j)),
        out_shape=jax.ShapeDtypeStruct((depth, rows, d3), F32),
        compiler_params=_params("parallel", "parallel"),
        name="modulation",
    )(c_rows, w_mod, b_mod.reshape(depth, 1, d3))


def _mod_specs(mod, layer, row_of):
    d = mod.shape[-1] // 3
    spec = lambda third: pl.BlockSpec((None, None, 1, d), lambda i: (layer, row_of(i), 0, third))
    return spec(0), spec(1), spec(2)


def _in_proj_kernel(x_ref, nw_ref, sh_ref, sc_ref, w_ref, cs_ref, gm_ref, wg_ref, wgate_ref,
                    bgate_ref, ltri_ref, utri_ref, rt_ref, ct_ref, p_ref, cum_ref, *, rope):
    tm = x_ref.shape[0]
    hb = _ada_norm(x_ref[...], nw_ref[...], sc_ref[...], sh_ref[...]).astype(BF16)

    g = _dot(hb, wg_ref[...])
    xg = _dot(g.astype(BF16), wgate_ref[...]) + bgate_ref[...]
    la = _log_sigmoid(xg) * (1.0 / GLA_GATE_TAU)
    half = la.shape[1] // 2
    for r in range(tm // TRI_ROWS):
        rows = slice(r * TRI_ROWS, (r + 1) * TRI_ROWS)
        fh, fl = _split2(la[rows, :half])
        bh, bl = _split2(la[rows, half:])
        cum_ref[rows, :half] = _dot(ltri_ref[...], fh) + _dot(ltri_ref[...], fl)
        cum_ref[rows, half:] = _dot(utri_ref[...], bh) + _dot(utri_ref[...], bl)

    for sec in range(N_SEC):
        base = sec * SEC
        raw = _dot(hb, w_ref[:, base:base + SEC])
        acc = raw * cs_ref[sec]
        if sec <= SEC_NA_K:
            for c in range(0, SEC, MXU_DIM):
                sq = raw[:, c:c + MXU_DIM] * raw[:, c:c + MXU_DIM]
                ss = _dot(sq.astype(BF16), gm_ref[...])
                p_ref[:, base + c:base + c + MXU_DIM] = (
                    acc[:, c:c + MXU_DIM] * lax.rsqrt(ss * (1.0 / NA_HEAD_DIM) + NORM_EPS)).astype(BF16)
        elif rope and sec == SEC_RET_QK:
            n_grid_rows = tm // GRID_W

            def table(t):
                by_row = jnp.concatenate(
                    [jnp.broadcast_to(rt_ref[t, g:g + 1, :], (GRID_W, LANES)) for g in range(n_grid_rows)],
                    axis=0)
                return by_row + jnp.concatenate([ct_ref[t]] * n_grid_rows, axis=0)

            cos = table(0)
            sin = table(1)
            for c in range(0, SEC, 2 * LANES):
                u1 = acc[:, c:c + LANES]
                u2 = acc[:, c + LANES:c + 2 * LANES]
                p_ref[:, base + c:base + c + LANES] = (u1 * cos - u2 * sin).astype(BF16)
                p_ref[:, base + c + LANES:base + c + 2 * LANES] = (u1 * sin + u2 * cos).astype(BF16)
        else:
            p_ref[:, base:base + SEC] = acc.astype(BF16)


def _in_proj(x2d, mod, prm, tabs, *, layer, tm, row_of, blocks_per_batch, rope):
    t, d = x2d.shape
    const2 = lambda i: (0, 0)
    lconst = lambda i: (layer, 0, 0)
    row_tab = pl.BlockSpec((2, tm // GRID_W, LANES), lambda i: (0, i % blocks_per_batch, 0))
    col_tab = _resident((2, GRID_W, LANES), lambda i: (0, 0, 0))
    sh_spec, sc_spec, _ = _mod_specs(mod, layer, row_of)
    return pl.pallas_call(
        functools.partial(_in_proj_kernel, rope=rope),
        grid=(t // tm,),
        in_specs=[
            pl.BlockSpec((tm, d), lambda i: (i, 0)),
            pl.BlockSpec((None, 1, d), lconst),
            sh_spec, sc_spec,
            _resident((None, d, N_SEC * SEC), lconst),
            _resident((None, N_SEC, 1, SEC), lambda i: (layer, 0, 0, 0)),
            _resident((MXU_DIM, MXU_DIM), const2),
            _resident((None, d, GATE_PAD), lconst),
            _resident((None, GATE_PAD, SEC), lconst),
            _resident((None, 1, SEC), lconst),
            _resident((TRI_ROWS, TRI_ROWS), const2),
            _resident((TRI_ROWS, TRI_ROWS), const2),
            row_tab, col_tab,
        ],
        out_specs=[
            pl.BlockSpec((tm, N_SEC * SEC), lambda i: (i, 0)),
            pl.BlockSpec((tm, SEC), lambda i: (i, 0)),
        ],
        out_shape=[
            jax.ShapeDtypeStruct((t, N_SEC * SEC), BF16),
            jax.ShapeDtypeStruct((t, SEC), F32),
        ],
        compiler_params=_params("parallel"),
        name="in_proj_rope" if rope else "in_proj",
    )(x2d, prm["nw"], mod, mod, prm["wpack"], prm["colscale"], tabs["gmat"], prm["wg"],
      prm["wgate"], prm["bgate"], tabs["ltri"], tabs["utri"], tabs["rot_rows"], tabs["rot_cols"])


def _na_block_plan(rows):
    n_qb = rows // NA_QROWS
    kr = min(NA_WIN_ROWS, rows)
    plans = []
    for qb in (0, min(1, n_qb - 1), n_qb - 1):
        t0 = int(np.clip(qb - 1, 0, n_qb - NA_KROWS // NA_QROWS))
        plan = []
        for i in range(NA_QROWS):
            r = qb * NA_QROWS + i
            rs = int(np.clip(r - kr // 2, 0, rows - kr))
            plan.append([(t0 * NA_QROWS + j) - r + NA_WIN_ROWS - 1
                         if rs <= t0 * NA_QROWS + j < rs + kr else None
                         for j in range(NA_KROWS)])
        plans.append(plan)
    return plans


def _bias_kernel(rpb_ref, o_ref, *, plans):
    shape = (GRID_W, LANES)
    lane = lax.broadcasted_iota(jnp.int32, shape, 1)
    c = lax.broadcasted_iota(jnp.int32, shape, 0)
    kc = lane & (GRID_W - 1)
    cs = jnp.clip(c - NA_WIN_COLS // 2, 0, GRID_W - NA_WIN_COLS)
    col_ok = jnp.logical_and(kc >= cs, kc < cs + NA_WIN_COLS)
    low = lane < GRID_W
    neg = jnp.full(shape, NEG_BIG, F32)
    toep = []
    for a in range(2 * NA_WIN_ROWS - 1):
        x = jnp.broadcast_to(rpb_ref[a:a + 1, :], shape)
        lo = pltpu.roll(x, LANES - (NA_WIN_COLS - 1), 1, stride=1, stride_axis=0)
        hi = pltpu.roll(x, GRID_W - (NA_WIN_COLS - 1), 1, stride=1, stride_axis=0)
        toep.append(jnp.where(col_ok, jnp.where(low, lo, hi), neg))
    for k, plan in enumerate(plans):
        for i in range(NA_QROWS):
            for jp in range(NA_KROWS // 2):
                a0, a1 = plan[i][2 * jp], plan[i][2 * jp + 1]
                b0 = neg if a0 is None else toep[a0]
                b1 = neg if a1 is None else toep[a1]
                o_ref[k, i * GRID_W:(i + 1) * GRID_W, jp * LANES:(jp + 1) * LANES] = (
                    jnp.where(low, b0, b1))


def _na_bias_tables(rpb, rows):
    depth, heads, nr, nc = rpb.shape
    rp = jnp.pad(rpb, ((0, 0), (0, 0), (0, 16 - nr), (0, LANES - nc)))
    tq, tk = NA_QROWS * GRID_W, NA_KROWS * GRID_W
    return pl.pallas_call(
        functools.partial(_bias_kernel, plans=_na_block_plan(rows)),
        grid=(depth, heads),
        in_specs=[pl.BlockSpec((None, None, 16, LANES), lambda l, h: (l, h, 0, 0))],
        out_specs=pl.BlockSpec((None, 3, None, tq, tk), lambda l, h: (l, 0, h, 0, 0)),
        out_shape=jax.ShapeDtypeStruct((depth, 3, heads, tq, tk), F32),
        compiler_params=_params("parallel", "parallel"),
        name="na_bias",
    )(rp)


def _lane_mask(width, lo, hi):
    lane = lax.broadcasted_iota(jnp.int32, (1, width), 1)
    return jnp.logical_and(lane >= lo, lane < hi)


def _na_kernel(q_ref, k_ref, v_ref, kc_ref, vc_ref, bias_ref, o_ref, *, n_qb, n_sub):
    tq = q_ref.shape[0] // n_sub
    tk = bias_ref.shape[-1]
    width = q_ref.shape[1]
    heads = width // NA_HEAD_DIM
    kc = kc_ref[...]
    vc = vc_ref[...]
    inside = [_lane_mask(width, h * NA_HEAD_DIM, (h + 1) * NA_HEAD_DIM) for h in range(heads)]
    ones_row = [_lane_mask(width, ((h + 1) % heads) * NA_HEAD_DIM, ((h + 1) % heads) * NA_HEAD_DIM + 1)
                for h in range(heads)]
    starts = []
    for sub in range(n_sub):
        qb = pl.program_id(2) * n_sub + sub
        t0 = jnp.clip(qb - 1, 0, n_qb - tk // tq)
        starts.append((qb, pl.multiple_of(t0 * tq, tq)))
    scores = []
    for sub, (qb, start) in enumerate(starts):
        kind = jnp.where(qb == 0, 0, jnp.where(qb == n_qb - 1, 2, 1))
        q = q_ref[sub * tq:(sub + 1) * tq, :]
        k3 = k_ref[pl.ds(start, tk), :]
        for h in range(heads):
            qh = q * inside[h].astype(BF16)
            scores.append((_dot_nt(qh, k3) + bias_ref[kind, h], _dot_nt(qh, kc)))
    probs = []
    for s, sc in scores:
        m = jnp.maximum(jnp.max(s, axis=-1, keepdims=True), jnp.max(sc, axis=-1, keepdims=True))
        probs.append((jnp.exp((s - m).astype(BF16)), jnp.exp((sc - m).astype(BF16))))
    for sub, (_, start) in enumerate(starts):
        v3 = v_ref[pl.ds(start, tk), :]
        acc = jnp.zeros((tq, width), F32)
        for h in range(heads):
            p, pc = probs[sub * heads + h]
            keep = inside[h].astype(BF16)
            fill = ones_row[h].astype(BF16)
            oh = _dot(p, v3 * keep + fill) + _dot(pc, vc * keep + fill)
            l = jnp.sum(jnp.where(ones_row[h], oh, 0.0), axis=-1, keepdims=True)
            acc = jnp.where(inside[h], oh * (1.0 / l), acc)
        o_ref[sub * tq:(sub + 1) * tq, :] = acc.astype(BF16)


def _na_attention(p_lat, p_ctx, bias, *, layer, batch, seq, ctx_len, n_sub):
    tq = NA_QROWS * GRID_W
    tk = NA_KROWS * GRID_W
    n_qb = seq // tq
    n_steps = n_qb // n_sub
    hw = 2 * LANES
    n_hg = BRANCH_W // hw
    sec_blk = SEC // hw
    heads = hw // NA_HEAD_DIM
    return pl.pallas_call(
        functools.partial(_na_kernel, n_qb=n_qb, n_sub=n_sub),
        grid=(batch, n_hg, n_steps),
        in_specs=[
            pl.BlockSpec((n_sub * tq, hw), lambda b, g, i: (b * n_steps + i, SEC_NA_Q * sec_blk + g)),
            pl.BlockSpec((seq, hw), lambda b, g, i: (b, SEC_NA_K * sec_blk + g)),
            pl.BlockSpec((seq, hw), lambda b, g, i: (b, SEC_NA_V * sec_blk + g)),
            pl.BlockSpec((ctx_len, hw), lambda b, g, i: (b, SEC_NA_K * sec_blk + g)),
            pl.BlockSpec((ctx_len, hw), lambda b, g, i: (b, SEC_NA_V * sec_blk + g)),
            pl.BlockSpec((None, bias.shape[1], heads, tq, tk), lambda b, g, i: (layer, 0, g, 0, 0)),
        ],
        out_specs=pl.BlockSpec((n_sub * tq, hw), lambda b, g, i: (b * n_steps + i, g)),
        out_shape=jax.ShapeDtypeStruct((batch * seq, BRANCH_W), BF16),
        compiler_params=_params("parallel", "parallel", "arbitrary"),
        name="na_attention",
    )(p_lat, p_lat, p_lat, p_ctx, p_ctx, bias)


def _ctx_attn_kernel(q_ref, k_ref, v_ref, o_ref):
    first = _lane_mask(LANES, 0, NA_HEAD_DIM)
    sel = (first.astype(BF16), jnp.logical_not(first).astype(BF16))
    outs = []
    for pair in range(q_ref.shape[1] // LANES):
        lanes = slice(pair * LANES, (pair + 1) * LANES)
        q2, k2, v2 = q_ref[:, lanes], k_ref[:, lanes], v_ref[:, lanes]
        o_pair = []
        for hh in range(2):
            s = _dot_nt(q2 * sel[hh], k2)
            m = jnp.max(s, axis=-1, keepdims=True)
            p = jnp.exp(s - m)
            l = jnp.sum(p, axis=-1, keepdims=True)
            o_pair.append(_dot(p.astype(BF16), v2) * (1.0 / l))
        outs.append(jnp.where(first, o_pair[0], o_pair[1]))
    o_ref[...] = jnp.concatenate(outs, axis=1).astype(BF16)


def _ctx_attention(p_ctx, *, batch, ctx_len):
    hw = 2 * LANES
    n_hg = BRANCH_W // hw
    sec_blk = SEC // hw
    spec = lambda sec: pl.BlockSpec((ctx_len, hw), lambda b, g: (b, sec * sec_blk + g))
    return pl.pallas_call(
        _ctx_attn_kernel,
        grid=(batch, n_hg),
        in_specs=[spec(SEC_NA_Q), spec(SEC_NA_K), spec(SEC_NA_V)],
        out_specs=pl.BlockSpec((ctx_len, hw), lambda b, g: (b, g)),
        out_shape=jax.ShapeDtypeStruct((batch * ctx_len, BRANCH_W), BF16),
        compiler_params=_params("parallel", "parallel"),
        name="ctx_attention",
    )(p_ctx, p_ctx, p_ctx)


def _scan_head_of_lane(gla):
    lane = lax.broadcasted_iota(jnp.int32, (1, SCAN_HEADS * SCAN_DK), 1)
    if gla:
        return lax.shift_right_logical(lane, SCAN_DK.bit_length() - 1)
    return lax.shift_right_logical(lane & (LANES - 1), (SCAN_DK // 2).bit_length() - 1)


def _decay_factors(cum, ref_row, total_row):
    cm = cum[ref_row:ref_row + 1, :]
    tot = cum[total_row:total_row + 1, :]
    eq = jnp.exp(cum - cm)
    ek = jnp.exp(cm - cum)
    return eq, ek, jnp.exp(cm), jnp.exp(tot - cm), jnp.exp(tot)


def _chunk_prepare(q, k, v, factors, head_sel):
    eq, ek, _, e_tm, _ = factors
    qtb = (q * eq).astype(BF16)
    kt = k * ek
    khb = (kt * e_tm).astype(BF16)
    hmul = [sel.astype(BF16) for sel in head_sel]
    q_stack = jnp.concatenate([qtb * m for m in hmul], axis=0)
    v_rows = jnp.concatenate([v[:, h * SCAN_DV:(h + 1) * SCAN_DV] for h in range(SCAN_HEADS)], axis=0)
    k_rows = jnp.concatenate([khb * m for m in hmul], axis=0)
    return q_stack, kt.astype(BF16), _dot_tn(v_rows, k_rows)


def _chunk_scores(q_stack, ktb, st, e_m, tri_stack):
    rhs = jnp.concatenate([(st * e_m).astype(BF16), ktb], axis=0)
    both = _dot_nt(q_stack, rhs)
    return both[:, :SCAN_DV], jnp.where(tri_stack, both[:, SCAN_DV:], 0.0).astype(BF16)


def _chunk_output(inter, att, v):
    outs = []
    for h in range(SCAN_HEADS):
        rows = slice(h * CHUNK, (h + 1) * CHUNK)
        outs.append(_dot(att[rows], v[:, h * SCAN_DV:(h + 1) * SCAN_DV]) + inter[rows])
    return jnp.concatenate(outs, axis=1)


def _scan_masks(gla):
    head = _scan_head_of_lane(gla)
    head_sel = [head == h for h in range(SCAN_HEADS)]
    t = lax.broadcasted_iota(jnp.int32, (SCAN_HEADS * CHUNK, CHUNK), 0) & (CHUNK - 1)
    s = lax.broadcasted_iota(jnp.int32, (SCAN_HEADS * CHUNK, CHUNK), 1)
    return head_sel, s <= t, s >= t


def _ret_cum(lg_row, forward):
    t = lax.broadcasted_iota(jnp.int32, (CHUNK, lg_row.shape[1]), 0).astype(F32)
    steps = (t + 1.0) if forward else (float(CHUNK) - t)
    return steps * lg_row


def _scan_kernel(*refs, gla, n_chunks, from_zero):
    if gla:
        (qkf_ref, vf_ref, cf_ref, qkb_ref, vb_ref, cb_ref, s0_ref,
         of_ref, ob_ref, sfin_ref, stf, stb) = refs
    else:
        (qkf_ref, vf_ref, qkb_ref, vb_ref, lg_ref, s0_ref,
         of_ref, ob_ref, sfin_ref, stf, stb) = refs
    step = pl.program_id(1)
    hk = SCAN_HEADS * SCAN_DK

    @pl.when(step == 0)
    def _():
        if from_zero:
            stf[...] = jnp.zeros_like(stf)
            stb[...] = jnp.zeros_like(stb)
        else:
            stf[...] = s0_ref[0]
            stb[...] = s0_ref[1]

    head_sel, tril, triu = _scan_masks(gla)
    if not gla:
        fac_f = _decay_factors(_ret_cum(lg_ref[0:1, :], True), CHUNK // 2 - 1, CHUNK - 1)
        fac_b = _decay_factors(_ret_cum(lg_ref[1:2, :], False), CHUNK // 2, 0)

    chains = (
        (qkf_ref, vf_ref, cf_ref if gla else None, of_ref, stf, tril, (CHUNK // 2 - 1, CHUNK - 1),
         [slice(c * CHUNK, (c + 1) * CHUNK) for c in range(n_chunks)]),
        (qkb_ref, vb_ref, cb_ref if gla else None, ob_ref, stb, triu, (CHUNK // 2, 0),
         [slice(c * CHUNK, (c + 1) * CHUNK) for c in reversed(range(n_chunks))]),
    )
    prepared = []
    for qk_ref, v_ref, cum_ref, _, _, _, (ref_row, total_row), order in chains:
        items = []
        for rows in order:
            if gla:
                fac = _decay_factors(cum_ref[rows, :], ref_row, total_row)
            else:
                fac = fac_f if ref_row < total_row else fac_b
            qk = qk_ref[rows, :].astype(F32)
            items.append((fac, _chunk_prepare(qk[:, :hk], qk[:, hk:], v_ref[rows, :], fac, head_sel)))
        prepared.append(items)
    finals = []
    scored = []
    for (_, _, _, _, st_ref, tri, _, _), items in zip(chains, prepared):
        st = st_ref[...]
        states = []
        for fac, (_, _, inc) in items:
            states.append(st)
            st = st * fac[4] + inc
        st_ref[...] = st
        finals.append(st)
        scored.append([_chunk_scores(q_stack, ktb, st_c, fac[2], tri)
                       for st_c, (fac, (q_stack, ktb, _)) in zip(states, items)])
    for (_, v_ref, _, o_ref, _, _, _, order), results in zip(chains, scored):
        for rows, (inter, att) in zip(order, results):
            o_ref[rows, :] = _chunk_output(inter, att, v_ref[rows, :]).astype(BF16)
    sf, sb = finals

    @pl.when(step == pl.num_programs(1) - 1)
    def _():
        sfin_ref[0] = sf
        sfin_ref[1] = sb


def _scan(p, cum, lg, s0, *, gla, batch, rows_per_batch, tb):
    nb = rows_per_batch // tb
    hk = SCAN_HEADS * SCAN_DK
    hv = SCAN_HEADS * SCAN_DV
    sec_qk = SEC_GLA_QK if gla else SEC_RET_QK
    sec_v = SEC_GLA_V if gla else SEC_RET_V
    fwd = lambda b, s: b * nb + s
    bwd = lambda b, s: b * nb + nb - 1 - s
    from_zero = s0 is None
    if from_zero:
        s0 = jnp.zeros((batch, 2, SCAN_DV, hk), F32)
    in_specs = [
        pl.BlockSpec((tb, SEC), lambda b, s: (fwd(b, s), sec_qk)),
        pl.BlockSpec((tb, SEC), lambda b, s: (fwd(b, s), sec_v)),
    ]
    args = [p, p]
    if gla:
        in_specs.append(pl.BlockSpec((tb, hk), lambda b, s: (fwd(b, s), 0)))
        args.append(cum)
    in_specs += [
        pl.BlockSpec((tb, SEC), lambda b, s: (bwd(b, s), sec_qk)),
        pl.BlockSpec((tb, SEC), lambda b, s: (bwd(b, s), sec_v)),
    ]
    args += [p, p]
    if gla:
        in_specs.append(pl.BlockSpec((tb, hk), lambda b, s: (bwd(b, s), 1)))
        args.append(cum)
    else:
        in_specs.append(pl.BlockSpec((2, hk), lambda b, s: (0, 0)))
        args.append(lg)
    in_specs.append(pl.BlockSpec((None, 2, SCAN_DV, hk), lambda b, s: (b, 0, 0, 0)))
    args.append(s0)
    return pl.pallas_call(
        functools.partial(_scan_kernel, gla=gla, n_chunks=tb // CHUNK, from_zero=from_zero),
        grid=(batch, nb),
        in_specs=in_specs,
        out_specs=[
            pl.BlockSpec((tb, hv), lambda b, s: (fwd(b, s), 0)),
            pl.BlockSpec((tb, hv), lambda b, s: (bwd(b, s), 0)),
            pl.BlockSpec((None, 2, SCAN_DV, hk), lambda b, s: (b, 0, 0, 0)),
        ],
        out_shape=[
            jax.ShapeDtypeStruct((batch * rows_per_batch, hv), BF16),
            jax.ShapeDtypeStruct((batch * rows_per_batch, hv), BF16),
            jax.ShapeDtypeStruct((batch, 2, SCAN_DV, hk), F32),
        ],
        scratch_shapes=[pltpu.VMEM((SCAN_DV, hk), F32), pltpu.VMEM((SCAN_DV, hk), F32)],
        compiler_params=_params("parallel", "arbitrary"),
        name=("gla" if gla else "ret") + ("_ctx_scan" if from_zero else "_scan"),
    )(*args)


def _head_rms(o, gain):
    parts = []
    for h in range(SCAN_HEADS):
        sl = o[:, h * SCAN_DV:(h + 1) * SCAN_DV]
        ms = jnp.mean(sl * sl, axis=-1, keepdims=True)
        parts.append(sl * lax.rsqrt(ms + NORM_EPS))
    return jnp.concatenate(parts, axis=1) * gain


def _merge_kernel(x_ref, nw_ref, sh_ref, sc_ref, gate_ref, na_ref, gf_ref, gb_ref, rf_ref, rb_ref,
                  gg_ref, rg_ref, wz0_ref, wz1_ref, wz2_ref, wg0_ref, wg1_ref, wg2_ref, wb_ref, wo_ref,
                  o_ref):
    x = x_ref[...]
    wz_refs = (wz0_ref, wz1_ref, wz2_ref)
    wg_refs = (wg0_ref, wg1_ref, wg2_ref)
    hb = _ada_norm(x, nw_ref[...], sc_ref[...], sh_ref[...]).astype(BF16)
    outs = (
        na_ref[...].astype(F32),
        _head_rms(gf_ref[...].astype(F32) + gb_ref[...].astype(F32), gg_ref[...]),
        _head_rms(rf_ref[...].astype(F32) + rb_ref[...].astype(F32), rg_ref[...]),
    )
    y = None
    for i in range(N_BRANCH):
        z = _dot(hb, wz_refs[i][...])
        u = (outs[i] * _silu(z)).astype(BF16)
        g = _dot(hb, wg_refs[i][...])
        t = _sigmoid(g) * _dot(u, wb_ref[i])
        y = t if y is None else y + t
    r = _dot(y.astype(BF16), wo_ref[...])
    o_ref[...] = x + gate_ref[...] * r


def _merge(x2d, mod, prm, na_o, gla_f, gla_b, ret_f, ret_b, *, layer, tm, row_of):
    t, d = x2d.shape
    row = lambda w: pl.BlockSpec((tm, w), lambda i: (i, 0))
    sh_spec, sc_spec, gate_spec = _mod_specs(mod, layer, row_of)
    lconst3 = lambda i: (layer, 0, 0)
    return pl.pallas_call(
        _merge_kernel,
        grid=(t // tm,),
        in_specs=[
            row(d),
            pl.BlockSpec((None, 1, d), lconst3),
            sh_spec, sc_spec, gate_spec,
            row(BRANCH_W), row(BRANCH_W), row(BRANCH_W), row(BRANCH_W), row(BRANCH_W),
            pl.BlockSpec((None, 1, BRANCH_W), lconst3),
            pl.BlockSpec((None, 1, BRANCH_W), lconst3),
            *[_resident((None, d, BRANCH_W), lambda i, n=n: (layer, 0, N_SEC * SEC // BRANCH_W + n))
              for n in range(N_BRANCH)],
            *[_resident((None, d, d), lambda i, n=n: (layer, 0, (N_SEC * SEC + N_BRANCH * BRANCH_W) // d + n))
              for n in range(N_BRANCH)],
            _resident((None, N_BRANCH, BRANCH_W, d), lambda i: (layer, 0, 0, 0)),
            _resident((None, d, d), lconst3),
        ],
        out_specs=row(d),
        out_shape=jax.ShapeDtypeStruct((t, d), F32),
        compiler_params=_params("parallel"),
        name="merge",
    )(x2d, prm["nw"], mod, mod, mod, na_o, gla_f, gla_b, ret_f, ret_b, prm["gla_gain"],
      prm["ret_gain"], *([prm["wpack"]] * (2 * N_BRANCH)), prm["wb"], prm["wo"])


def _in_splits():
    na = NA_HEADS * NA_HEAD_DIM
    qk = SCAN_HEADS * SCAN_DK
    vv = SCAN_HEADS * SCAN_DV
    sizes = (na, na, na, qk, qk, vv, 2 * GLA_GATE_RANK, qk, qk, vv, N_BRANCH * BRANCH_W,
             N_BRANCH * D_MODEL)
    return [int(v) for v in np.concatenate([[0], np.cumsum(sizes)])]


def _rotary_pack_matrix():
    hk = SCAN_HEADS * SCAN_DK
    quarter = SCAN_DK // 4
    p = np.zeros((SEC, SEC), np.float32)
    for blk in range(SEC // hk):
        for h in range(SCAN_HEADS):
            for half in range(2):
                for partner in range(2):
                    for i in range(quarter):
                        old = h * SCAN_DK + half * 2 * quarter + partner * quarter + i
                        new = partner * (hk // 2) + h * 2 * quarter + half * quarter + i
                        p[blk * hk + old, blk * hk + new] = 1.0
    return jnp.asarray(p).astype(BF16)


def _pack_kernel(a_ref, b_ref, perm_ref, o_ref, *, first_shifted, perm_block, shift):
    j = pl.program_id(1)
    sec = a_ref.shape[0]

    def shifted():
        x = jnp.concatenate([a_ref[...], b_ref[...]], axis=0)
        return x[shift:shift + sec]

    @pl.when(j < first_shifted)
    def _():
        o_ref[...] = a_ref[...].T.astype(BF16)

    @pl.when(j == perm_block)
    def _():
        o_ref[...] = _dot(shifted().T.astype(BF16), perm_ref[...]).astype(BF16)

    @pl.when(jnp.logical_and(j >= first_shifted, j != perm_block))
    def _():
        o_ref[...] = shifted().T.astype(BF16)


def _pack_w_in(w_in):
    depth, d, n_in = w_in.shape
    o = _in_splits()
    g0, g1 = o[6], o[7]
    assert g0 % SEC == 0 and (n_in - (g1 - g0)) % SEC == 0
    n_blocks = (n_in - (g1 - g0)) // SEC
    tail_rows = LANES
    w_t = jnp.swapaxes(w_in, 1, 2)
    return pl.pallas_call(
        functools.partial(_pack_kernel, first_shifted=g0 // SEC, perm_block=SEC_RET_QK,
                          shift=g1 - g0),
        grid=(depth, n_blocks),
        in_specs=[
            pl.BlockSpec((None, SEC, d), lambda l, j: (l, j, 0)),
            pl.BlockSpec((None, tail_rows, d), lambda l, j: (l, (j + 1) * (SEC // tail_rows), 0)),
            _resident((SEC, SEC), lambda l, j: (0, 0)),
        ],
        out_specs=pl.BlockSpec((None, d, SEC), lambda l, j: (l, 0, j)),
        out_shape=jax.ShapeDtypeStruct((depth, d, n_blocks * SEC), BF16),
        compiler_params=_params("parallel", "parallel"),
        name="pack_w_in",
    )(w_t, w_t, _rotary_pack_matrix())


def _prep_params(norm_w, w_in, q_norm, k_norm, w_gate, b_gate, gla_out_norm, ret_out_norm,
                 w_branch, w_out):
    depth, d, _ = w_in.shape
    o = _in_splits()
    g0, g1 = o[6], o[7]
    hk = SCAN_HEADS * SCAN_DK
    wg = jnp.pad(w_in[:, :, g0:g1], ((0, 0), (0, 0), (0, GATE_PAD - (g1 - g0)))).astype(BF16)
    wgate = jnp.zeros((depth, GATE_PAD, 2 * hk), F32)
    wgate = wgate.at[:, :GLA_GATE_RANK, :hk].set(w_gate[:, 0])
    wgate = wgate.at[:, GLA_GATE_RANK:2 * GLA_GATE_RANK, hk:].set(w_gate[:, 1])
    bgate = b_gate.reshape(depth, 1, 2 * hk)
    base = np.ones((N_SEC, 1, SEC), np.float32)
    base[SEC_NA_Q] = NA_HEAD_DIM ** -0.5
    base[SEC_GLA_QK, 0, :hk] = SCAN_DK ** -0.5
    base[SEC_RET_QK, 0, :hk] = SCAN_DK ** -0.5
    reps = SEC // NA_HEAD_DIM
    ones = jnp.ones((depth, N_SEC - 2, 1, SEC), F32)
    gains = jnp.concatenate([jnp.tile(q_norm, (1, reps))[:, None, None, :],
                             jnp.tile(k_norm, (1, reps))[:, None, None, :], ones], axis=1)
    return dict(
        nw=norm_w.reshape(depth, 1, d),
        wpack=_pack_w_in(w_in), wg=wg, wgate=wgate.astype(BF16), bgate=bgate,
        colscale=gains * jnp.asarray(base)[None],
        wb=w_branch.astype(BF16), wo=w_out.astype(BF16),
        gla_gain=gla_out_norm.reshape(depth, 1, BRANCH_W),
        ret_gain=ret_out_norm.reshape(depth, 1, BRANCH_W),
    )


def _tables(seq, ctx_len):
    i = np.arange(TRI_ROWS)
    same = (i[:, None] // CHUNK) == (i[None, :] // CHUNK)
    ltri = jnp.asarray((same & (i[None, :] <= i[:, None])).astype(np.float32)).astype(BF16)
    utri = jnp.asarray((same & (i[None, :] >= i[:, None])).astype(np.float32)).astype(BF16)
    grp = np.arange(MXU_DIM) // NA_HEAD_DIM
    gmat = jnp.asarray((grp[:, None] == grp[None, :]).astype(np.float32)).astype(BF16)
    quarter = SCAN_DK // 4
    n_rows = seq // GRID_W
    inv = ROPE_BASE ** (-jnp.arange(quarter, dtype=F32) / quarter)
    ang_r = jnp.arange(n_rows, dtype=F32)[:, None] * inv[None, :]
    ang_c = jnp.arange(GRID_W, dtype=F32)[:, None] * inv[None, :]
    zr, zc = jnp.zeros_like(ang_r), jnp.zeros_like(ang_c)
    lanes = lambda a, b: jnp.concatenate([a, b] * SCAN_HEADS, axis=1)
    rot_rows = jnp.stack([lanes(jnp.cos(ang_r), zr), lanes(jnp.sin(ang_r), zr)])
    rot_cols = jnp.stack([lanes(zc, jnp.cos(ang_c)), lanes(zc, jnp.sin(ang_c))])
    lat = dict(ltri=ltri, utri=utri, gmat=gmat, rot_rows=rot_rows, rot_cols=rot_cols)
    ctx = dict(ltri=ltri, utri=utri, gmat=gmat, rot_cols=jnp.zeros_like(rot_cols),
               rot_rows=jnp.zeros((2, ctx_len // GRID_W, LANES), F32))
    return lat, ctx


def _log_gamma(offset):
    g = jnp.log1p(-jnp.exp2(-(offset + jnp.arange(SCAN_HEADS, dtype=F32))))
    return jnp.tile(jnp.repeat(g, SCAN_DK // 2), 2)


def kernel(x, c, ctx, c_ctx, w_mod, b_mod, norm_w, w_in, na_q_norm, na_k_norm, na_rpb, gla_w_gate,
           gla_b_gate, gla_out_norm, ret_out_norm, w_branch, w_out):
    batch, seq, d = x.shape
    ctx_len = ctx.shape[1]
    depth = w_mod.shape[0]
    rows = seq // GRID_W
    tm_proj, tm_merge, tb_scan = 512, 512, 512
    na_blocks_per_step = 4

    c_rows = jnp.zeros((8, d), F32).at[:batch].set(c).at[batch].set(c_ctx)
    mod = _modulation(c_rows, w_mod, b_mod).reshape(depth, 8, 1, 3 * d)
    prm = _prep_params(norm_w, w_in, na_q_norm, na_k_norm, gla_w_gate, gla_b_gate, gla_out_norm,
                       ret_out_norm, w_branch, w_out)
    lat_tabs, ctx_tabs = _tables(seq, ctx_len)
    bias = _na_bias_tables(na_rpb, rows)
    lg = jnp.stack([_log_gamma(RET_DECAY_FWD), _log_gamma(RET_DECAY_BWD)])
    ctx_row = lambda i: batch

    xl = x.reshape(batch * seq, d)
    xc = ctx.reshape(batch * ctx_len, d)
    for layer in range(depth):
        with_ctx = layer < depth - 1
        p_lat, cum_lat = _in_proj(xl, mod, prm, lat_tabs, layer=layer, tm=tm_proj,
                                  row_of=lambda i: i // (seq // tm_proj),
                                  blocks_per_batch=seq // tm_proj, rope=True)
        p_ctx, cum_ctx = _in_proj(xc, mod, prm, ctx_tabs, layer=layer, tm=ctx_len, row_of=ctx_row,
                                  blocks_per_batch=1, rope=False)

        na_l = _na_attention(p_lat, p_ctx, bias, layer=layer, batch=batch, seq=seq, ctx_len=ctx_len,
                             n_sub=na_blocks_per_step)

        gla_cf, gla_cb, gla_s = _scan(p_ctx, cum_ctx, None, None, gla=True, batch=batch,
                                      rows_per_batch=ctx_len, tb=ctx_len)
        gla_f, gla_b, _ = _scan(p_lat, cum_lat, None, gla_s, gla=True, batch=batch,
                                rows_per_batch=seq, tb=tb_scan)
        ret_cf, ret_cb, ret_s = _scan(p_ctx, None, lg, None, gla=False, batch=batch,
                                      rows_per_batch=ctx_len, tb=ctx_len)
        ret_f, ret_b, _ = _scan(p_lat, None, lg, ret_s, gla=False, batch=batch,
                                rows_per_batch=seq, tb=tb_scan)

        xl_new = _merge(xl, mod, prm, na_l, gla_f, gla_b, ret_f, ret_b, layer=layer, tm=tm_merge,
                        row_of=lambda i: i // (seq // tm_merge))
        if with_ctx:
            na_c = _ctx_attention(p_ctx, batch=batch, ctx_len=ctx_len)
            xc = _merge(xc, mod, prm, na_c, gla_cf, gla_cb, ret_cf, ret_cb, layer=layer, tm=ctx_len,
                        row_of=ctx_row)
        xl = xl_new
    return xl.reshape(batch, seq, d)
```

```python
import functools

import numpy as np
import jax
import jax.numpy as jnp
from jax import lax
from jax.experimental import pallas as pl
from jax.experimental.pallas import tpu as pltpu

D_MODEL = 1024
GRID_W = 64
NORM_EPS = 1e-6
N_BRANCH = 3
BRANCH_W = D_MODEL // 2
NA_HEAD_DIM = 64
NA_HEADS = BRANCH_W // NA_HEAD_DIM
NA_WIN_ROWS = 8
NA_WIN_COLS = 16
SCAN_HEADS = 4
SCAN_DV = BRANCH_W // SCAN_HEADS
SCAN_DK = SCAN_DV // 2
GLA_GATE_RANK = 16
GLA_GATE_TAU = 16.0
RET_DECAY_FWD = 5.0
RET_DECAY_BWD = 5.5
CHUNK = 64
ROPE_BASE = 10000.0

LANES = 128
MXU_DIM = 256

SEC = 512
SEC_NA_Q, SEC_NA_K, SEC_NA_V, SEC_GLA_QK, SEC_GLA_V, SEC_RET_QK, SEC_RET_V = range(7)
N_SEC = 7
GATE_PAD = LANES
TRI_ROWS = MXU_DIM
NA_QROWS = 4
NA_KROWS = 12
NEG_BIG = -1e30

VMEM_LIMIT = 56 * 1024 * 1024

F32 = jnp.float32
BF16 = jnp.bfloat16


def _dot(a, b):
    return jnp.dot(a, b, preferred_element_type=F32)


def _dot_nt(a, b):
    return lax.dot_general(a, b, (((1,), (1,)), ((), ())), preferred_element_type=F32)


def _dot_tn(a, b):
    return lax.dot_general(a, b, (((0,), (0,)), ((), ())), preferred_element_type=F32)


def _split2(x):
    hi = x.astype(BF16)
    lo = (x - hi.astype(F32)).astype(BF16)
    return hi, lo


def _log_sigmoid(x):
    return jnp.minimum(x, 0.0) - jnp.log(1.0 + jnp.exp(-jnp.abs(x)))


def _sigmoid(x):
    return 1.0 / (1.0 + jnp.exp(-x))


def _silu(x):
    return x * _sigmoid(x)


def _params(*sem):
    return pltpu.CompilerParams(dimension_semantics=sem, vmem_limit_bytes=VMEM_LIMIT)


def _resident(block_shape, index_map):
    return pl.BlockSpec(block_shape, index_map, pipeline_mode=pl.Buffered(1))


def _ada_norm(x, nw, sc, sh):
    ms = jnp.mean(x * x, axis=-1, keepdims=True)
    return (x * lax.rsqrt(ms + NORM_EPS)) * (nw * (1.0 + sc)) + sh


def _mod_kernel(c_ref, w_ref, b_ref, o_ref):
    a = _silu(c_ref[...])
    o_ref[...] = jnp.dot(a, w_ref[...], preferred_element_type=F32,
                         precision=lax.Precision.HIGHEST) + b_ref[...]


def _modulation(c_rows, w_mod, b_mod):
    depth, d, d3 = w_mod.shape
    rows = c_rows.shape[0]
    tn = 1024
    return pl.pallas_call(
        _mod_kernel,
        grid=(depth, d3 // tn),
        in_specs=[
            pl.BlockSpec((rows, d), lambda l, j: (0, 0)),
            pl.BlockSpec((None, d, tn), lambda l, j: (l, 0, j)),
            pl.BlockSpec((None, 1, tn), lambda l, j: (l, 0, j)),
        ],
        out_specs=pl.BlockSpec((None, rows, tn), lambda l, j: (l, 0, j)),
        out_shape=jax.ShapeDtypeStruct((depth, rows, d3), F32),
        compiler_params=_params("parallel", "parallel"),
        name="modulation",
    )(c_rows, w_mod, b_mod.reshape(depth, 1, d3))


def _mod_specs(mod, layer, row_of):
    d = mod.shape[-1] // 3
    spec = lambda third: pl.BlockSpec((None, None, 1, d), lambda i: (layer, row_of(i), 0, third))
    return spec(0), spec(1), spec(2)


def _in_proj_kernel(x_ref, nw_ref, sh_ref, sc_ref, w_ref, cs_ref, gm_ref, wg_ref, wgate_ref,
                    bgate_ref, ltri_ref, utri_ref, rt_ref, ct_ref, p_ref, cum_ref, *, rope):
    tm = x_ref.shape[0]
    hb = _ada_norm(x_ref[...], nw_ref[...], sc_ref[...], sh_ref[...]).astype(BF16)

    g = _dot(hb, wg_ref[...])
    normed = {}
    for sec in range(N_SEC):
        base = sec * SEC
        raw = _dot(hb, w_ref[:, base:base + SEC])
        acc = raw * cs_ref[sec]
        if sec <= SEC_NA_K:
            normed[sec] = (raw, acc)
        elif rope and sec == SEC_RET_QK:
            n_grid_rows = tm // GRID_W

            def table(t):
                by_row = jnp.concatenate(
                    [jnp.broadcast_to(rt_ref[t, g:g + 1, :], (GRID_W, LANES)) for g in range(n_grid_rows)],
                    axis=0)
                return by_row + jnp.concatenate([ct_ref[t]] * n_grid_rows, axis=0)

            cos = table(0)
            sin = table(1)
            for c in range(0, SEC, 2 * LANES):
                u1 = acc[:, c:c + LANES]
                u2 = acc[:, c + LANES:c + 2 * LANES]
                p_ref[:, base + c:base + c + LANES] = (u1 * cos - u2 * sin).astype(BF16)
                p_ref[:, base + c + LANES:base + c + 2 * LANES] = (u1 * sin + u2 * cos).astype(BF16)
        else:
            p_ref[:, base:base + SEC] = acc.astype(BF16)
        if sec == SEC_NA_V:
            xg = _dot(g.astype(BF16), wgate_ref[...]) + bgate_ref[...]
            la = _log_sigmoid(xg) * (1.0 / GLA_GATE_TAU)

    for sec, (raw, acc) in normed.items():
        base = sec * SEC
        for c in range(0, SEC, MXU_DIM):
            sq = raw[:, c:c + MXU_DIM] * raw[:, c:c + MXU_DIM]
            ss = _dot(sq.astype(BF16), gm_ref[...])
            p_ref[:, base + c:base + c + MXU_DIM] = (
                acc[:, c:c + MXU_DIM] * lax.rsqrt(ss * (1.0 / NA_HEAD_DIM) + NORM_EPS)).astype(BF16)

    half = la.shape[1] // 2
    for r in range(tm // TRI_ROWS):
        rows = slice(r * TRI_ROWS, (r + 1) * TRI_ROWS)
        fh, fl = _split2(la[rows, :half])
        bh, bl = _split2(la[rows, half:])
        cum_ref[rows, :half] = _dot(ltri_ref[...], fh) + _dot(ltri_ref[...], fl)
        cum_ref[rows, half:] = _dot(utri_ref[...], bh) + _dot(utri_ref[...], bl)


def _in_proj(x2d, mod, prm, tabs, *, layer, tm, row_of, blocks_per_batch, rope):
    t, d = x2d.shape
    const2 = lambda i: (0, 0)
    lconst = lambda i: (layer, 0, 0)
    row_tab = pl.BlockSpec((2, tm // GRID_W, LANES), lambda i: (0, i % blocks_per_batch, 0))
    col_tab = _resident((2, GRID_W, LANES), lambda i: (0, 0, 0))
    sh_spec, sc_spec, _ = _mod_specs(mod, layer, row_of)
    return pl.pallas_call(
        functools.partial(_in_proj_kernel, rope=rope),
        grid=(t // tm,),
        in_specs=[
            pl.BlockSpec((tm, d), lambda i: (i, 0)),
            pl.BlockSpec((None, 1, d), lconst),
            sh_spec, sc_spec,
            _resident((None, d, N_SEC * SEC), lconst),
            _resident((None, N_SEC, 1, SEC), lambda i: (layer, 0, 0, 0)),
            _resident((MXU_DIM, MXU_DIM), const2),
            _resident((None, d, GATE_PAD), lconst),
            _resident((None, GATE_PAD, SEC), lconst),
            _resident((None, 1, SEC), lconst),
            _resident((TRI_ROWS, TRI_ROWS), const2),
            _resident((TRI_ROWS, TRI_ROWS), const2),
            row_tab, col_tab,
        ],
        out_specs=[
            pl.BlockSpec((tm, N_SEC * SEC), lambda i: (i, 0)),
            pl.BlockSpec((tm, SEC), lambda i: (i, 0)),
        ],
        out_shape=[
            jax.ShapeDtypeStruct((t, N_SEC * SEC), BF16),
            jax.ShapeDtypeStruct((t, SEC), F32),
        ],
        compiler_params=_params("parallel"),
        name="in_proj_rope" if rope else "in_proj",
    )(x2d, prm["nw"], mod, mod, prm["wpack"], prm["colscale"], tabs["gmat"], prm["wg"],
      prm["wgate"], prm["bgate"], tabs["ltri"], tabs["utri"], tabs["rot_rows"], tabs["rot_cols"])


def _na_block_plan(rows):
    n_qb = rows // NA_QROWS
    kr = min(NA_WIN_ROWS, rows)
    plans = []
    for qb in (0, min(1, n_qb - 1), n_qb - 1):
        t0 = int(np.clip(qb - 1, 0, n_qb - NA_KROWS // NA_QROWS))
        plan = []
        for i in range(NA_QROWS):
            r = qb * NA_QROWS + i
            rs = int(np.clip(r - kr // 2, 0, rows - kr))
            plan.append([(t0 * NA_QROWS + j) - r + NA_WIN_ROWS - 1
                         if rs <= t0 * NA_QROWS + j < rs + kr else None
                         for j in range(NA_KROWS)])
        plans.append(plan)
    return plans


def _bias_kernel(rpb_ref, o_ref, *, plans):
    shape = (GRID_W, LANES)
    lane = lax.broadcasted_iota(jnp.int32, shape, 1)
    c = lax.broadcasted_iota(jnp.int32, shape, 0)
    kc = lane & (GRID_W - 1)
    cs = jnp.clip(c - NA_WIN_COLS // 2, 0, GRID_W - NA_WIN_COLS)
    col_ok = jnp.logical_and(kc >= cs, kc < cs + NA_WIN_COLS)
    low = lane < GRID_W
    neg = jnp.full(shape, NEG_BIG, F32)
    toep = []
    for a in range(2 * NA_WIN_ROWS - 1):
        x = jnp.broadcast_to(rpb_ref[a:a + 1, :], shape)
        lo = pltpu.roll(x, LANES - (NA_WIN_COLS - 1), 1, stride=1, stride_axis=0)
        hi = pltpu.roll(x, GRID_W - (NA_WIN_COLS - 1), 1, stride=1, stride_axis=0)
        toep.append(jnp.where(col_ok, jnp.where(low, lo, hi), neg))
    for k, plan in enumerate(plans):
        for i in range(NA_QROWS):
            for jp in range(NA_KROWS // 2):
                a0, a1 = plan[i][2 * jp], plan[i][2 * jp + 1]
                b0 = neg if a0 is None else toep[a0]
                b1 = neg if a1 is None else toep[a1]
                o_ref[k, i * GRID_W:(i + 1) * GRID_W, jp * LANES:(jp + 1) * LANES] = (
                    jnp.where(low, b0, b1))


def _na_bias_tables(rpb, rows):
    depth, heads, nr, nc = rpb.shape
    rp = jnp.pad(rpb, ((0, 0), (0, 0), (0, 16 - nr), (0, LANES - nc)))
    tq, tk = NA_QROWS * GRID_W, NA_KROWS * GRID_W
    return pl.pallas_call(
        functools.partial(_bias_kernel, plans=_na_block_plan(rows)),
        grid=(depth, heads),
        in_specs=[pl.BlockSpec((None, None, 16, LANES), lambda l, h: (l, h, 0, 0))],
        out_specs=pl.BlockSpec((None, 3, None, tq, tk), lambda l, h: (l, 0, h, 0, 0)),
        out_shape=jax.ShapeDtypeStruct((depth, 3, heads, tq, tk), F32),
        compiler_params=_params("parallel", "parallel"),
        name="na_bias",
    )(rp)


def _lane_mask(width, lo, hi):
    lane = lax.broadcasted_iota(jnp.int32, (1, width), 1)
    return jnp.logical_and(lane >= lo, lane < hi)


def _na_kernel(q_ref, k_ref, v_ref, kc_ref, vc_ref, bias_ref, o_ref, *, n_qb, n_sub):
    tq = q_ref.shape[0] // n_sub
    tk = bias_ref.shape[-1]
    width = q_ref.shape[1]
    heads = width // NA_HEAD_DIM
    kc = kc_ref[...]
    vc = vc_ref[...]
    inside = [_lane_mask(width, h * NA_HEAD_DIM, (h + 1) * NA_HEAD_DIM) for h in range(heads)]
    ones_row = [_lane_mask(width, ((h + 1) % heads) * NA_HEAD_DIM, ((h + 1) % heads) * NA_HEAD_DIM + 1)
                for h in range(heads)]
    starts = []
    for sub in range(n_sub):
        qb = pl.program_id(2) * n_sub + sub
        t0 = jnp.clip(qb - 1, 0, n_qb - tk // tq)
        starts.append((qb, pl.multiple_of(t0 * tq, tq)))
    units = [(sub, h) for sub in range(n_sub) for h in range(heads)]

    def scores(sub, h):
        qb, start = starts[sub]
        kind = jnp.where(qb == 0, 0, jnp.where(qb == n_qb - 1, 2, 1))
        qh = q_ref[sub * tq:(sub + 1) * tq, :] * inside[h].astype(BF16)
        return _dot_nt(qh, k_ref[pl.ds(start, tk), :]) + bias_ref[kind, h], _dot_nt(qh, kc)

    pending = scores(*units[0])
    acc = None
    for idx, (sub, h) in enumerate(units):
        s, sc = pending
        if idx + 1 < len(units):
            pending = scores(*units[idx + 1])
        keep = inside[h].astype(BF16)
        fill = ones_row[h].astype(BF16)
        m = jnp.maximum(jnp.max(s, axis=-1, keepdims=True), jnp.max(sc, axis=-1, keepdims=True))
        p = jnp.exp((s - m).astype(BF16))
        pc = jnp.exp((sc - m).astype(BF16))
        v3 = v_ref[pl.ds(starts[sub][1], tk), :]
        oh = _dot(p, v3 * keep + fill) + _dot(pc, vc * keep + fill)
        l = jnp.sum(jnp.where(ones_row[h], oh, 0.0), axis=-1, keepdims=True)
        acc = oh * (1.0 / l) if h == 0 else jnp.where(inside[h], oh * (1.0 / l), acc)
        if h == heads - 1:
            o_ref[sub * tq:(sub + 1) * tq, :] = acc.astype(BF16)


def _na_attention(p_lat, p_ctx, bias, *, layer, batch, seq, ctx_len, n_sub):
    tq = NA_QROWS * GRID_W
    tk = NA_KROWS * GRID_W
    n_qb = seq // tq
    n_steps = n_qb // n_sub
    hw = 2 * LANES
    n_hg = BRANCH_W // hw
    sec_blk = SEC // hw
    heads = hw // NA_HEAD_DIM
    return pl.pallas_call(
        functools.partial(_na_kernel, n_qb=n_qb, n_sub=n_sub),
        grid=(batch, n_hg, n_steps),
        in_specs=[
            pl.BlockSpec((n_sub * tq, hw), lambda b, g, i: (b * n_steps + i, SEC_NA_Q * sec_blk + g)),
            pl.BlockSpec((seq, hw), lambda b, g, i: (b, SEC_NA_K * sec_blk + g)),
            pl.BlockSpec((seq, hw), lambda b, g, i: (b, SEC_NA_V * sec_blk + g)),
            pl.BlockSpec((ctx_len, hw), lambda b, g, i: (b, SEC_NA_K * sec_blk + g)),
            pl.BlockSpec((ctx_len, hw), lambda b, g, i: (b, SEC_NA_V * sec_blk + g)),
            pl.BlockSpec((None, bias.shape[1], heads, tq, tk), lambda b, g, i: (layer, 0, g, 0, 0)),
        ],
        out_specs=pl.BlockSpec((n_sub * tq, hw), lambda b, g, i: (b * n_steps + i, g)),
        out_shape=jax.ShapeDtypeStruct((batch * seq, BRANCH_W), BF16),
        compiler_params=_params("parallel", "parallel", "arbitrary"),
        name="na_attention",
    )(p_lat, p_lat, p_lat, p_ctx, p_ctx, bias)


def _ctx_attn_kernel(q_ref, k_ref, v_ref, o_ref):
    first = _lane_mask(LANES, 0, NA_HEAD_DIM)
    sel = (first.astype(BF16), jnp.logical_not(first).astype(BF16))
    outs = []
    for pair in range(q_ref.shape[1] // LANES):
        lanes = slice(pair * LANES, (pair + 1) * LANES)
        q2, k2, v2 = q_ref[:, lanes], k_ref[:, lanes], v_ref[:, lanes]
        o_pair = []
        for hh in range(2):
            s = _dot_nt(q2 * sel[hh], k2)
            m = jnp.max(s, axis=-1, keepdims=True)
            p = jnp.exp(s - m)
            l = jnp.sum(p, axis=-1, keepdims=True)
            o_pair.append(_dot(p.astype(BF16), v2) * (1.0 / l))
        outs.append(jnp.where(first, o_pair[0], o_pair[1]))
    o_ref[...] = jnp.concatenate(outs, axis=1).astype(BF16)


def _ctx_attention(p_ctx, *, batch, ctx_len):
    hw = 2 * LANES
    n_hg = BRANCH_W // hw
    sec_blk = SEC // hw
    spec = lambda sec: pl.BlockSpec((ctx_len, hw), lambda b, g: (b, sec * sec_blk + g))
    return pl.pallas_call(
        _ctx_attn_kernel,
        grid=(batch, n_hg),
        in_specs=[spec(SEC_NA_Q), spec(SEC_NA_K), spec(SEC_NA_V)],
        out_specs=pl.BlockSpec((ctx_len, hw), lambda b, g: (b, g)),
        out_shape=jax.ShapeDtypeStruct((batch * ctx_len, BRANCH_W), BF16),
        compiler_params=_params("parallel", "parallel"),
        name="ctx_attention",
    )(p_ctx, p_ctx, p_ctx)


def _scan_head_of_lane(gla):
    lane = lax.broadcasted_iota(jnp.int32, (1, SCAN_HEADS * SCAN_DK), 1)
    if gla:
        return lax.shift_right_logical(lane, SCAN_DK.bit_length() - 1)
    return lax.shift_right_logical(lane & (LANES - 1), (SCAN_DK // 2).bit_length() - 1)


def _decay_factors(cum, ref_row, total_row):
    cm = cum[ref_row:ref_row + 1, :]
    tot = cum[total_row:total_row + 1, :]
    eq = jnp.exp(cum - cm)
    ek = jnp.exp(cm - cum)
    return eq, ek, jnp.exp(cm), jnp.exp(tot - cm), jnp.exp(tot)


def _chunk_prepare(q, k, v, factors, head_sel):
    eq, ek, _, e_tm, _ = factors
    qtb = (q * eq).astype(BF16)
    kt = k * ek
    khb = (kt * e_tm).astype(BF16)
    hmul = [sel.astype(BF16) for sel in head_sel]
    q_stack = jnp.concatenate([qtb * m for m in hmul], axis=0)
    v_rows = jnp.concatenate([v[:, h * SCAN_DV:(h + 1) * SCAN_DV] for h in range(SCAN_HEADS)], axis=0)
    k_rows = jnp.concatenate([khb * m for m in hmul], axis=0)
    return q_stack, kt.astype(BF16), _dot_tn(v_rows, k_rows)


def _chunk_scores(q_stack, ktb, st, e_m, tri_stack):
    rhs = jnp.concatenate([(st * e_m).astype(BF16), ktb], axis=0)
    both = _dot_nt(q_stack, rhs)
    return both[:, :SCAN_DV], jnp.where(tri_stack, both[:, SCAN_DV:], 0.0).astype(BF16)


def _chunk_output(inter, att, v):
    outs = []
    for h in range(SCAN_HEADS):
        rows = slice(h * CHUNK, (h + 1) * CHUNK)
        outs.append(_dot(att[rows], v[:, h * SCAN_DV:(h + 1) * SCAN_DV]) + inter[rows])
    return jnp.concatenate(outs, axis=1)


def _scan_masks(gla):
    head = _scan_head_of_lane(gla)
    head_sel = [head == h for h in range(SCAN_HEADS)]
    t = lax.broadcasted_iota(jnp.int32, (SCAN_HEADS * CHUNK, CHUNK), 0) & (CHUNK - 1)
    s = lax.broadcasted_iota(jnp.int32, (SCAN_HEADS * CHUNK, CHUNK), 1)
    return head_sel, s <= t, s >= t


def _ret_cum(lg_row, forward):
    t = lax.broadcasted_iota(jnp.int32, (CHUNK, lg_row.shape[1]), 0).astype(F32)
    steps = (t + 1.0) if forward else (float(CHUNK) - t)
    return steps * lg_row


def _scan_kernel(*refs, gla, n_chunks, from_zero):
    if gla:
        (qkf_ref, vf_ref, cf_ref, qkb_ref, vb_ref, cb_ref, s0_ref,
         of_ref, ob_ref, sfin_ref, stf, stb) = refs
    else:
        (qkf_ref, vf_ref, qkb_ref, vb_ref, lg_ref, s0_ref,
         of_ref, ob_ref, sfin_ref, stf, stb) = refs
    step = pl.program_id(1)
    hk = SCAN_HEADS * SCAN_DK

    @pl.when(step == 0)
    def _():
        if from_zero:
            stf[...] = jnp.zeros_like(stf)
            stb[...] = jnp.zeros_like(stb)
        else:
            stf[...] = s0_ref[0]
            stb[...] = s0_ref[1]

    head_sel, tril, triu = _scan_masks(gla)
    if not gla:
        fac_f = _decay_factors(_ret_cum(lg_ref[0:1, :], True), CHUNK // 2 - 1, CHUNK - 1)
        fac_b = _decay_factors(_ret_cum(lg_ref[1:2, :], False), CHUNK // 2, 0)

    chains = (
        (qkf_ref, vf_ref, cf_ref if gla else None, of_ref, stf, tril, (CHUNK // 2 - 1, CHUNK - 1),
         [slice(c * CHUNK, (c + 1) * CHUNK) for c in range(n_chunks)]),
        (qkb_ref, vb_ref, cb_ref if gla else None, ob_ref, stb, triu, (CHUNK // 2, 0),
         [slice(c * CHUNK, (c + 1) * CHUNK) for c in reversed(range(n_chunks))]),
    )
    prepared = []
    for qk_ref, v_ref, cum_ref, _, _, _, (ref_row, total_row), order in chains:
        items = []
        for rows in order:
            if gla:
                fac = _decay_factors(cum_ref[rows, :], ref_row, total_row)
            else:
                fac = fac_f if ref_row < total_row else fac_b
            qk = qk_ref[rows, :].astype(F32)
            items.append((fac, _chunk_prepare(qk[:, :hk], qk[:, hk:], v_ref[rows, :], fac, head_sel)))
        prepared.append(items)
    finals = []
    scored = []
    for (_, _, _, _, st_ref, tri, _, _), items in zip(chains, prepared):
        st = st_ref[...]
        states = []
        for fac, (_, _, inc) in items:
            states.append(st)
            st = st * fac[4] + inc
        st_ref[...] = st
        finals.append(st)
        scored.append([_chunk_scores(q_stack, ktb, st_c, fac[2], tri)
                       for st_c, (fac, (q_stack, ktb, _)) in zip(states, items)])
    for (_, v_ref, _, o_ref, _, _, _, order), results in zip(chains, scored):
        for rows, (inter, att) in zip(order, results):
            o_ref[rows, :] = _chunk_output(inter, att, v_ref[rows, :]).astype(BF16)
    sf, sb = finals

    @pl.when(step == pl.num_programs(1) - 1)
    def _():
        sfin_ref[0] = sf
        sfin_ref[1] = sb


def _scan(p, cum, lg, s0, *, gla, batch, rows_per_batch, tb):
    nb = rows_per_batch // tb
    hk = SCAN_HEADS * SCAN_DK
    hv = SCAN_HEADS * SCAN_DV
    sec_qk = SEC_GLA_QK if gla else SEC_RET_QK
    sec_v = SEC_GLA_V if gla else SEC_RET_V
    fwd = lambda b, s: b * nb + s
    bwd = lambda b, s: b * nb + nb - 1 - s
    from_zero = s0 is None
    if from_zero:
        s0 = jnp.zeros((batch, 2, SCAN_DV, hk), F32)
    in_specs = [
        pl.BlockSpec((tb, SEC), lambda b, s: (fwd(b, s), sec_qk)),
        pl.BlockSpec((tb, SEC), lambda b, s: (fwd(b, s), sec_v)),
    ]
    args = [p, p]
    if gla:
        in_specs.append(pl.BlockSpec((tb, hk), lambda b, s: (fwd(b, s), 0)))
        args.append(cum)
    in_specs += [
        pl.BlockSpec((tb, SEC), lambda b, s: (bwd(b, s), sec_qk)),
        pl.BlockSpec((tb, SEC), lambda b, s: (bwd(b, s), sec_v)),
    ]
    args += [p, p]
    if gla:
        in_specs.append(pl.BlockSpec((tb, hk), lambda b, s: (bwd(b, s), 1)))
        args.append(cum)
    else:
        in_specs.append(pl.BlockSpec((2, hk), lambda b, s: (0, 0)))
        args.append(lg)
    in_specs.append(pl.BlockSpec((None, 2, SCAN_DV, hk), lambda b, s: (b, 0, 0, 0)))
    args.append(s0)
    return pl.pallas_call(
        functools.partial(_scan_kernel, gla=gla, n_chunks=tb // CHUNK, from_zero=from_zero),
        grid=(batch, nb),
        in_specs=in_specs,
        out_specs=[
            pl.BlockSpec((tb, hv), lambda b, s: (fwd(b, s), 0)),
            pl.BlockSpec((tb, hv), lambda b, s: (bwd(b, s), 0)),
            pl.BlockSpec((None, 2, SCAN_DV, hk), lambda b, s: (b, 0, 0, 0)),
        ],
        out_shape=[
            jax.ShapeDtypeStruct((batch * rows_per_batch, hv), BF16),
            jax.ShapeDtypeStruct((batch * rows_per_batch, hv), BF16),
            jax.ShapeDtypeStruct((batch, 2, SCAN_DV, hk), F32),
        ],
        scratch_shapes=[pltpu.VMEM((SCAN_DV, hk), F32), pltpu.VMEM((SCAN_DV, hk), F32)],
        compiler_params=_params("parallel", "arbitrary"),
        name=("gla" if gla else "ret") + ("_ctx_scan" if from_zero else "_scan"),
    )(*args)


def _head_rms(o, gain):
    parts = []
    for h in range(SCAN_HEADS):
        sl = o[:, h * SCAN_DV:(h + 1) * SCAN_DV]
        ms = jnp.mean(sl * sl, axis=-1, keepdims=True)
        parts.append(sl * lax.rsqrt(ms + NORM_EPS))
    return jnp.concatenate(parts, axis=1) * gain


def _merge_kernel(x_ref, nw_ref, sh_ref, sc_ref, gate_ref, na_ref, gf_ref, gb_ref, rf_ref, rb_ref,
                  gg_ref, rg_ref, wz0_ref, wz1_ref, wz2_ref, wg0_ref, wg1_ref, wg2_ref, wb_ref, wo_ref,
                  o_ref):
    x = x_ref[...]
    wz_refs = (wz0_ref, wz1_ref, wz2_ref)
    wg_refs = (wg0_ref, wg1_ref, wg2_ref)
    hb = _ada_norm(x, nw_ref[...], sc_ref[...], sh_ref[...]).astype(BF16)
    outs = (
        na_ref[...].astype(F32),
        _head_rms(gf_ref[...].astype(F32) + gb_ref[...].astype(F32), gg_ref[...]),
        _head_rms(rf_ref[...].astype(F32) + rb_ref[...].astype(F32), rg_ref[...]),
    )
    zs = [_dot(hb, wz_refs[i][...]) for i in range(N_BRANCH)]
    us = [(outs[i] * _silu(zs[i])).astype(BF16) for i in range(N_BRANCH)]
    gs = [_sigmoid(_dot(hb, wg_refs[i][...])) for i in range(N_BRANCH)]
    y = None
    for i in range(N_BRANCH):
        t = gs[i] * _dot(us[i], wb_ref[i])
        y = t if y is None else y + t
    r = _dot(y.astype(BF16), wo_ref[...])
    o_ref[...] = x + gate_ref[...] * r


def _merge(x2d, mod, prm, na_o, gla_f, gla_b, ret_f, ret_b, *, layer, tm, row_of):
    t, d = x2d.shape
    row = lambda w: pl.BlockSpec((tm, w), lambda i: (i, 0))
    sh_spec, sc_spec, gate_spec = _mod_specs(mod, layer, row_of)
    lconst3 = lambda i: (layer, 0, 0)
    return pl.pallas_call(
        _merge_kernel,
        grid=(t // tm,),
        in_specs=[
            row(d),
            pl.BlockSpec((None, 1, d), lconst3),
            sh_spec, sc_spec, gate_spec,
            row(BRANCH_W), row(BRANCH_W), row(BRANCH_W), row(BRANCH_W), row(BRANCH_W),
            pl.BlockSpec((None, 1, BRANCH_W), lconst3),
            pl.BlockSpec((None, 1, BRANCH_W), lconst3),
            *[_resident((None, d, BRANCH_W), lambda i, n=n: (layer, 0, N_SEC * SEC // BRANCH_W + n))
              for n in range(N_BRANCH)],
            *[_resident((None, d, d), lambda i, n=n: (layer, 0, (N_SEC * SEC + N_BRANCH * BRANCH_W) // d + n))
              for n in range(N_BRANCH)],
            _resident((None, N_BRANCH, BRANCH_W, d), lambda i: (layer, 0, 0, 0)),
            _resident((None, d, d), lconst3),
        ],
        out_specs=row(d),
        out_shape=jax.ShapeDtypeStruct((t, d), F32),
        compiler_params=_params("parallel"),
        name="merge",
    )(x2d, prm["nw"], mod, mod, mod, na_o, gla_f, gla_b, ret_f, ret_b, prm["gla_gain"],
      prm["ret_gain"], *([prm["wpack"]] * (2 * N_BRANCH)), prm["wb"], prm["wo"])


def _in_splits():
    na = NA_HEADS * NA_HEAD_DIM
    qk = SCAN_HEADS * SCAN_DK
    vv = SCAN_HEADS * SCAN_DV
    sizes = (na, na, na, qk, qk, vv, 2 * GLA_GATE_RANK, qk, qk, vv, N_BRANCH * BRANCH_W,
             N_BRANCH * D_MODEL)
    return [int(v) for v in np.concatenate([[0], np.cumsum(sizes)])]


def _rotary_pack_matrix():
    hk = SCAN_HEADS * SCAN_DK
    quarter = SCAN_DK // 4
    p = np.zeros((SEC, SEC), np.float32)
    for blk in range(SEC // hk):
        for h in range(SCAN_HEADS):
            for half in range(2):
                for partner in range(2):
                    for i in range(quarter):
                        old = h * SCAN_DK + half * 2 * quarter + partner * quarter + i
                        new = partner * (hk // 2) + h * 2 * quarter + half * quarter + i
                        p[blk * hk + old, blk * hk + new] = 1.0
    return jnp.asarray(p).astype(BF16)


def _pack_kernel(a_ref, b_ref, perm_ref, o_ref, *, first_shifted, perm_block, shift):
    j = pl.program_id(1)
    sec = a_ref.shape[0]

    def shifted():
        x = jnp.concatenate([a_ref[...], b_ref[...]], axis=0)
        return x[shift:shift + sec]

    @pl.when(j < first_shifted)
    def _():
        o_ref[...] = a_ref[...].T.astype(BF16)

    @pl.when(j == perm_block)
    def _():
        o_ref[...] = _dot(shifted().T.astype(BF16), perm_ref[...]).astype(BF16)

    @pl.when(jnp.logical_and(j >= first_shifted, j != perm_block))
    def _():
        o_ref[...] = shifted().T.astype(BF16)


def _pack_w_in(w_in):
    depth, d, n_in = w_in.shape
    o = _in_splits()
    g0, g1 = o[6], o[7]
    assert g0 % SEC == 0 and (n_in - (g1 - g0)) % SEC == 0
    n_blocks = (n_in - (g1 - g0)) // SEC
    tail_rows = LANES
    w_t = jnp.swapaxes(w_in, 1, 2)
    return pl.pallas_call(
        functools.partial(_pack_kernel, first_shifted=g0 // SEC, perm_block=SEC_RET_QK,
                          shift=g1 - g0),
        grid=(depth, n_blocks),
        in_specs=[
            pl.BlockSpec((None, SEC, d), lambda l, j: (l, j, 0)),
            pl.BlockSpec((None, tail_rows, d), lambda l, j: (l, (j + 1) * (SEC // tail_rows), 0)),
            _resident((SEC, SEC), lambda l, j: (0, 0)),
        ],
        out_specs=pl.BlockSpec((None, d, SEC), lambda l, j: (l, 0, j)),
        out_shape=jax.ShapeDtypeStruct((depth, d, n_blocks * SEC), BF16),
        compiler_params=_params("parallel", "parallel"),
        name="pack_w_in",
    )(w_t, w_t, _rotary_pack_matrix())


def _prep_params(norm_w, w_in, q_norm, k_norm, w_gate, b_gate, gla_out_norm, ret_out_norm,
                 w_branch, w_out):
    depth, d, _ = w_in.shape
    o = _in_splits()
    g0, g1 = o[6], o[7]
    hk = SCAN_HEADS * SCAN_DK
    wg = jnp.pad(w_in[:, :, g0:g1], ((0, 0), (0, 0), (0, GATE_PAD - (g1 - g0)))).astype(BF16)
    wgate = jnp.zeros((depth, GATE_PAD, 2 * hk), F32)
    wgate = wgate.at[:, :GLA_GATE_RANK, :hk].set(w_gate[:, 0])
    wgate = wgate.at[:, GLA_GATE_RANK:2 * GLA_GATE_RANK, hk:].set(w_gate[:, 1])
    bgate = b_gate.reshape(depth, 1, 2 * hk)
    base = np.ones((N_SEC, 1, SEC), np.float32)
    base[SEC_NA_Q] = NA_HEAD_DIM ** -0.5
    base[SEC_GLA_QK, 0, :hk] = SCAN_DK ** -0.5
    base[SEC_RET_QK, 0, :hk] = SCAN_DK ** -0.5
    reps = SEC // NA_HEAD_DIM
    ones = jnp.ones((depth, N_SEC - 2, 1, SEC), F32)
    gains = jnp.concatenate([jnp.tile(q_norm, (1, reps))[:, None, None, :],
                             jnp.tile(k_norm, (1, reps))[:, None, None, :], ones], axis=1)
    return dict(
        nw=norm_w.reshape(depth, 1, d),
        wpack=_pack_w_in(w_in), wg=wg, wgate=wgate.astype(BF16), bgate=bgate,
        colscale=gains * jnp.asarray(base)[None],
        wb=w_branch.astype(BF16), wo=w_out.astype(BF16),
        gla_gain=gla_out_norm.reshape(depth, 1, BRANCH_W),
        ret_gain=ret_out_norm.reshape(depth, 1, BRANCH_W),
    )


def _tables(seq, ctx_len):
    i = np.arange(TRI_ROWS)
    same = (i[:, None] // CHUNK) == (i[None, :] // CHUNK)
    ltri = jnp.asarray((same & (i[None, :] <= i[:, None])).astype(np.float32)).astype(BF16)
    utri = jnp.asarray((same & (i[None, :] >= i[:, None])).astype(np.float32)).astype(BF16)
    grp = np.arange(MXU_DIM) // NA_HEAD_DIM
    gmat = jnp.asarray((grp[:, None] == grp[None, :]).astype(np.float32)).astype(BF16)
    quarter = SCAN_DK // 4
    n_rows = seq // GRID_W
    inv = ROPE_BASE ** (-jnp.arange(quarter, dtype=F32) / quarter)
    ang_r = jnp.arange(n_rows, dtype=F32)[:, None] * inv[None, :]
    ang_c = jnp.arange(GRID_W, dtype=F32)[:, None] * inv[None, :]
    zr, zc = jnp.zeros_like(ang_r), jnp.zeros_like(ang_c)
    lanes = lambda a, b: jnp.concatenate([a, b] * SCAN_HEADS, axis=1)
    rot_rows = jnp.stack([lanes(jnp.cos(ang_r), zr), lanes(jnp.sin(ang_r), zr)])
    rot_cols = jnp.stack([lanes(zc, jnp.cos(ang_c)), lanes(zc, jnp.sin(ang_c))])
    lat = dict(ltri=ltri, utri=utri, gmat=gmat, rot_rows=rot_rows, rot_cols=rot_cols)
    ctx = dict(ltri=ltri, utri=utri, gmat=gmat, rot_cols=jnp.zeros_like(rot_cols),
               rot_rows=jnp.zeros((2, ctx_len // GRID_W, LANES), F32))
    return lat, ctx


def _log_gamma(offset):
    g = jnp.log1p(-jnp.exp2(-(offset + jnp.arange(SCAN_HEADS, dtype=F32))))
    return jnp.tile(jnp.repeat(g, SCAN_DK // 2), 2)


def kernel(x, c, ctx, c_ctx, w_mod, b_mod, norm_w, w_in, na_q_norm, na_k_norm, na_rpb, gla_w_gate,
           gla_b_gate, gla_out_norm, ret_out_norm, w_branch, w_out):
    batch, seq, d = x.shape
    ctx_len = ctx.shape[1]
    depth = w_mod.shape[0]
    rows = seq // GRID_W
    tm_proj, tm_merge, tb_scan = 512, 512, 512
    na_blocks_per_step = 4

    c_rows = jnp.zeros((8, d), F32).at[:batch].set(c).at[batch].set(c_ctx)
    mod = _modulation(c_rows, w_mod, b_mod).reshape(depth, 8, 1, 3 * d)
    prm = _prep_params(norm_w, w_in, na_q_norm, na_k_norm, gla_w_gate, gla_b_gate, gla_out_norm,
                       ret_out_norm, w_branch, w_out)
    lat_tabs, ctx_tabs = _tables(seq, ctx_len)
    bias = _na_bias_tables(na_rpb, rows)
    lg = jnp.stack([_log_gamma(RET_DECAY_FWD), _log_gamma(RET_DECAY_BWD)])
    ctx_row = lambda i: batch

    xl = x.reshape(batch * seq, d)
    xc = ctx.reshape(batch * ctx_len, d)
    for layer in range(depth):
        with_ctx = layer < depth - 1
        p_lat, cum_lat = _in_proj(xl, mod, prm, lat_tabs, layer=layer, tm=tm_proj,
                                  row_of=lambda i: i // (seq // tm_proj),
                                  blocks_per_batch=seq // tm_proj, rope=True)
        p_ctx, cum_ctx = _in_proj(xc, mod, prm, ctx_tabs, layer=layer, tm=ctx_len, row_of=ctx_row,
                                  blocks_per_batch=1, rope=False)

        na_l = _na_attention(p_lat, p_ctx, bias, layer=layer, batch=batch, seq=seq, ctx_len=ctx_len,
                             n_sub=na_blocks_per_step)

        gla_cf, gla_cb, gla_s = _scan(p_ctx, cum_ctx, None, None, gla=True, batch=batch,
                                      rows_per_batch=ctx_len, tb=ctx_len)
        gla_f, gla_b, _ = _scan(p_lat, cum_lat, None, gla_s, gla=True, batch=batch,
                                rows_per_batch=seq, tb=tb_scan)
        ret_cf, ret_cb, ret_s = _scan(p_ctx, None, lg, None, gla=False, batch=batch,
                                      rows_per_batch=ctx_len, tb=ctx_len)
        ret_f, ret_b, _ = _scan(p_lat, None, lg, ret_s, gla=False, batch=batch,
                                rows_per_batch=seq, tb=tb_scan)

        xl_new = _merge(xl, mod, prm, na_l, gla_f, gla_b, ret_f, ret_b, layer=layer, tm=tm_merge,
                        row_of=lambda i: i // (seq // tm_merge))
        if with_ctx:
            na_c = _ctx_attention(p_ctx, batch=batch, ctx_len=ctx_len)
            xc = _merge(xc, mod, prm, na_c, gla_cf, gla_cb, ret_cf, ret_cb, layer=layer, tm=ctx_len,
                        row_of=ctx_row)
        xl = xl_new
    return xl.reshape(batch, seq, d)
```

```python
import functools

import numpy as np
import jax
import jax.numpy as jnp
from jax import lax
from jax.experimental import pallas as pl
from jax.experimental.pallas import tpu as pltpu

D_MODEL = 1024
GRID_W = 64
NORM_EPS = 1e-6
N_BRANCH = 3
BRANCH_W = D_MODEL // 2
NA_HEAD_DIM = 64
NA_HEADS = BRANCH_W // NA_HEAD_DIM
NA_WIN_ROWS = 8
NA_WIN_COLS = 16
SCAN_HEADS = 4
SCAN_DV = BRANCH_W // SCAN_HEADS
SCAN_DK = SCAN_DV // 2
GLA_GATE_RANK = 16
GLA_GATE_TAU = 16.0
RET_DECAY_FWD = 5.0
RET_DECAY_BWD = 5.5
CHUNK = 64
ROPE_BASE = 10000.0

LANES = 128
MXU_DIM = 256

SEC = 512
SEC_NA_Q, SEC_NA_K, SEC_NA_V, SEC_GLA_QK, SEC_GLA_V, SEC_RET_QK, SEC_RET_V = range(7)
N_SEC = 7
GATE_PAD = LANES
TRI_ROWS = MXU_DIM
NA_QROWS = 4
NA_KROWS = 12
NEG_BIG = -1e30

VMEM_LIMIT = 56 * 1024 * 1024

F32 = jnp.float32
BF16 = jnp.bfloat16


def _dot(a, b):
    return jnp.dot(a, b, preferred_element_type=F32)


def _dot_nt(a, b):
    return lax.dot_general(a, b, (((1,), (1,)), ((), ())), preferred_element_type=F32)


def _dot_tn(a, b):
    return lax.dot_general(a, b, (((0,), (0,)), ((), ())), preferred_element_type=F32)


def _split2(x):
    hi = x.astype(BF16)
    lo = (x - hi.astype(F32)).astype(BF16)
    return hi, lo


def _log_sigmoid(x):
    return jnp.minimum(x, 0.0) - jnp.log(1.0 + jnp.exp(-jnp.abs(x)))


def _sigmoid(x):
    return 1.0 / (1.0 + jnp.exp(-x))


def _silu(x):
    return x * _sigmoid(x)


def _params(*sem):
    return pltpu.CompilerParams(dimension_semantics=sem, vmem_limit_bytes=VMEM_LIMIT)


def _resident(block_shape, index_map):
    return pl.BlockSpec(block_shape, index_map, pipeline_mode=pl.Buffered(1))


def _ada_norm(x, nw, sc, sh):
    ms = jnp.mean(x * x, axis=-1, keepdims=True)
    return (x * lax.rsqrt(ms + NORM_EPS)) * (nw * (1.0 + sc)) + sh


def _mod_kernel(c_ref, w_ref, b_ref, o_ref):
    a = _silu(c_ref[...])
    o_ref[...] = jnp.dot(a, w_ref[...], preferred_element_type=F32,
                         precision=lax.Precision.HIGHEST) + b_ref[...]


def _modulation(c_rows, w_mod, b_mod):
    depth, d, d3 = w_mod.shape
    rows = c_rows.shape[0]
    tn = d3
    return pl.pallas_call(
        _mod_kernel,
        grid=(depth, d3 // tn),
        in_specs=[
            pl.BlockSpec((rows, d), lambda l, j: (0, 0)),
            pl.BlockSpec((None, d, tn), lambda l, j: (l, 0, j)),
            pl.BlockSpec((None, 1, tn), lambda l, j: (l, 0, j)),
        ],
        out_specs=pl.BlockSpec((None, rows, tn), lambda l, j: (l, 0, j)),
        out_shape=jax.ShapeDtypeStruct((depth, rows, d3), F32),
        compiler_params=_params("parallel", "parallel"),
        name="modulation",
    )(c_rows, w_mod, b_mod.reshape(depth, 1, d3))


def _mod_specs(mod, layer, row_of):
    d = mod.shape[-1] // 3
    spec = lambda third: pl.BlockSpec((None, None, 1, d), lambda i: (layer, row_of(i), 0, third))
    return spec(0), spec(1), spec(2)


def _in_proj_kernel(x_ref, nw_ref, sh_ref, sc_ref, w_ref, cs_ref, gm_ref, wg_ref, wgate_ref,
                    bgate_ref, ltri_ref, utri_ref, rt_ref, ct_ref, p_ref, cum_ref, *, rope):
    tm = x_ref.shape[0]
    hb = _ada_norm(x_ref[...], nw_ref[...], sc_ref[...], sh_ref[...]).astype(BF16)

    g = _dot(hb, wg_ref[...])
    normed = {}
    for sec in range(N_SEC):
        base = sec * SEC
        raw = _dot(hb, w_ref[:, base:base + SEC])
        acc = raw * cs_ref[sec]
        if sec <= SEC_NA_K:
            normed[sec] = (raw, acc)
        elif rope and sec == SEC_RET_QK:
            n_grid_rows = tm // GRID_W

            def table(t):
                by_row = jnp.concatenate(
                    [jnp.broadcast_to(rt_ref[t, g:g + 1, :], (GRID_W, LANES)) for g in range(n_grid_rows)],
                    axis=0)
                return by_row + jnp.concatenate([ct_ref[t]] * n_grid_rows, axis=0)

            cos = table(0)
            sin = table(1)
            for c in range(0, SEC, 2 * LANES):
                u1 = acc[:, c:c + LANES]
                u2 = acc[:, c + LANES:c + 2 * LANES]
                p_ref[:, base + c:base + c + LANES] = (u1 * cos - u2 * sin).astype(BF16)
                p_ref[:, base + c + LANES:base + c + 2 * LANES] = (u1 * sin + u2 * cos).astype(BF16)
        else:
            p_ref[:, base:base + SEC] = acc.astype(BF16)
        if sec == SEC_NA_V:
            xg = _dot(g.astype(BF16), wgate_ref[...]) + bgate_ref[...]
            la = _log_sigmoid(xg) * (1.0 / GLA_GATE_TAU)

    for sec, (raw, acc) in normed.items():
        base = sec * SEC
        for c in range(0, SEC, MXU_DIM):
            sq = raw[:, c:c + MXU_DIM] * raw[:, c:c + MXU_DIM]
            ss = _dot(sq.astype(BF16), gm_ref[...])
            p_ref[:, base + c:base + c + MXU_DIM] = (
                acc[:, c:c + MXU_DIM] * lax.rsqrt(ss * (1.0 / NA_HEAD_DIM) + NORM_EPS)).astype(BF16)

    half = la.shape[1] // 2
    for r in range(tm // TRI_ROWS):
        rows = slice(r * TRI_ROWS, (r + 1) * TRI_ROWS)
        fh, fl = _split2(la[rows, :half])
        bh, bl = _split2(la[rows, half:])
        cum_ref[rows, :half] = _dot(ltri_ref[...], fh) + _dot(ltri_ref[...], fl)
        cum_ref[rows, half:] = _dot(utri_ref[...], bh) + _dot(utri_ref[...], bl)


def _in_proj(x2d, mod, prm, tabs, *, layer, tm, row_of, blocks_per_batch, rope):
    t, d = x2d.shape
    const2 = lambda i: (0, 0)
    lconst = lambda i: (layer, 0, 0)
    row_tab = pl.BlockSpec((2, tm // GRID_W, LANES), lambda i: (0, i % blocks_per_batch, 0))
    col_tab = _resident((2, GRID_W, LANES), lambda i: (0, 0, 0))
    sh_spec, sc_spec, _ = _mod_specs(mod, layer, row_of)
    return pl.pallas_call(
        functools.partial(_in_proj_kernel, rope=rope),
        grid=(t // tm,),
        in_specs=[
            pl.BlockSpec((tm, d), lambda i: (i, 0)),
            pl.BlockSpec((None, 1, d), lconst),
            sh_spec, sc_spec,
            _resident((None, d, N_SEC * SEC), lconst),
            _resident((None, N_SEC, 1, SEC), lambda i: (layer, 0, 0, 0)),
            _resident((MXU_DIM, MXU_DIM), const2),
            _resident((None, d, GATE_PAD), lconst),
            _resident((None, GATE_PAD, SEC), lconst),
            _resident((None, 1, SEC), lconst),
            _resident((TRI_ROWS, TRI_ROWS), const2),
            _resident((TRI_ROWS, TRI_ROWS), const2),
            row_tab, col_tab,
        ],
        out_specs=[
            pl.BlockSpec((tm, N_SEC * SEC), lambda i: (i, 0)),
            pl.BlockSpec((tm, SEC), lambda i: (i, 0)),
        ],
        out_shape=[
            jax.ShapeDtypeStruct((t, N_SEC * SEC), BF16),
            jax.ShapeDtypeStruct((t, SEC), F32),
        ],
        compiler_params=_params("parallel"),
        name="in_proj_rope" if rope else "in_proj",
    )(x2d, prm["nw"], mod, mod, prm["wpack"], prm["colscale"], tabs["gmat"], prm["wg"],
      prm["wgate"], prm["bgate"], tabs["ltri"], tabs["utri"], tabs["rot_rows"], tabs["rot_cols"])


def _na_block_plan(rows):
    n_qb = rows // NA_QROWS
    kr = min(NA_WIN_ROWS, rows)
    plans = []
    for qb in (0, min(1, n_qb - 1), n_qb - 1):
        t0 = int(np.clip(qb - 1, 0, n_qb - NA_KROWS // NA_QROWS))
        plan = []
        for i in range(NA_QROWS):
            r = qb * NA_QROWS + i
            rs = int(np.clip(r - kr // 2, 0, rows - kr))
            plan.append([(t0 * NA_QROWS + j) - r + NA_WIN_ROWS - 1
                         if rs <= t0 * NA_QROWS + j < rs + kr else None
                         for j in range(NA_KROWS)])
        plans.append(plan)
    return plans


def _bias_kernel(rpb_ref, o_ref, *, plans):
    shape = (GRID_W, LANES)
    lane = lax.broadcasted_iota(jnp.int32, shape, 1)
    c = lax.broadcasted_iota(jnp.int32, shape, 0)
    kc = lane & (GRID_W - 1)
    cs = jnp.clip(c - NA_WIN_COLS // 2, 0, GRID_W - NA_WIN_COLS)
    col_ok = jnp.logical_and(kc >= cs, kc < cs + NA_WIN_COLS)
    low = lane < GRID_W
    neg = jnp.full(shape, NEG_BIG, F32)
    toep = []
    for a in range(2 * NA_WIN_ROWS - 1):
        x = jnp.broadcast_to(rpb_ref[a:a + 1, :], shape)
        lo = pltpu.roll(x, LANES - (NA_WIN_COLS - 1), 1, stride=1, stride_axis=0)
        hi = pltpu.roll(x, GRID_W - (NA_WIN_COLS - 1), 1, stride=1, stride_axis=0)
        toep.append(jnp.where(col_ok, jnp.where(low, lo, hi), neg))
    for k, plan in enumerate(plans):
        for i in range(NA_QROWS):
            for jp in range(NA_KROWS // 2):
                a0, a1 = plan[i][2 * jp], plan[i][2 * jp + 1]
                b0 = neg if a0 is None else toep[a0]
                b1 = neg if a1 is None else toep[a1]
                o_ref[k, i * GRID_W:(i + 1) * GRID_W, jp * LANES:(jp + 1) * LANES] = (
                    jnp.where(low, b0, b1))


def _na_bias_tables(rpb, rows):
    depth, heads, nr, nc = rpb.shape
    rp = jnp.pad(rpb, ((0, 0), (0, 0), (0, 16 - nr), (0, LANES - nc)))
    tq, tk = NA_QROWS * GRID_W, NA_KROWS * GRID_W
    return pl.pallas_call(
        functools.partial(_bias_kernel, plans=_na_block_plan(rows)),
        grid=(depth, heads),
        in_specs=[pl.BlockSpec((None, None, 16, LANES), lambda l, h: (l, h, 0, 0))],
        out_specs=pl.BlockSpec((None, 3, None, tq, tk), lambda l, h: (l, 0, h, 0, 0)),
        out_shape=jax.ShapeDtypeStruct((depth, 3, heads, tq, tk), F32),
        compiler_params=_params("parallel", "parallel"),
        name="na_bias",
    )(rp)


def _lane_mask(width, lo, hi):
    lane = lax.broadcasted_iota(jnp.int32, (1, width), 1)
    return jnp.logical_and(lane >= lo, lane < hi)


def _na_kernel(q_ref, k_ref, v_ref, kc_ref, vc_ref, bias_ref, o_ref, *, n_qb, n_sub):
    tq = q_ref.shape[0] // n_sub
    tk = bias_ref.shape[-1]
    width = q_ref.shape[1]
    heads = width // NA_HEAD_DIM
    kc = kc_ref[...]
    vc = vc_ref[...]
    inside = [_lane_mask(width, h * NA_HEAD_DIM, (h + 1) * NA_HEAD_DIM) for h in range(heads)]
    ones_row = [_lane_mask(width, ((h + 1) % heads) * NA_HEAD_DIM, ((h + 1) % heads) * NA_HEAD_DIM + 1)
                for h in range(heads)]
    starts = []
    for sub in range(n_sub):
        qb = pl.program_id(2) * n_sub + sub
        t0 = jnp.clip(qb - 1, 0, n_qb - tk // tq)
        starts.append((qb, pl.multiple_of(t0 * tq, tq)))
    units = [(sub, h) for sub in range(n_sub) for h in range(heads)]

    def scores(sub, h):
        qb, start = starts[sub]
        kind = jnp.where(qb == 0, 0, jnp.where(qb == n_qb - 1, 2, 1))
        qh = q_ref[sub * tq:(sub + 1) * tq, :] * inside[h].astype(BF16)
        return _dot_nt(qh, k_ref[pl.ds(start, tk), :]) + bias_ref[kind, h], _dot_nt(qh, kc)

    pending = scores(*units[0])
    acc = None
    for idx, (sub, h) in enumerate(units):
        s, sc = pending
        if idx + 1 < len(units):
            pending = scores(*units[idx + 1])
        keep = inside[h].astype(BF16)
        fill = ones_row[h].astype(BF16)
        m = jnp.maximum(jnp.max(s, axis=-1, keepdims=True), jnp.max(sc, axis=-1, keepdims=True))
        p = jnp.exp((s - m).astype(BF16))
        pc = jnp.exp((sc - m).astype(BF16))
        v3 = v_ref[pl.ds(starts[sub][1], tk), :]
        oh = _dot(p, v3 * keep + fill) + _dot(pc, vc * keep + fill)
        l = jnp.sum(jnp.where(ones_row[h], oh, 0.0), axis=-1, keepdims=True)
        acc = oh * (1.0 / l) if h == 0 else jnp.where(inside[h], oh * (1.0 / l), acc)
        if h == heads - 1:
            o_ref[sub * tq:(sub + 1) * tq, :] = acc.astype(BF16)


def _na_attention(p_lat, p_ctx, bias, *, layer, batch, seq, ctx_len, n_sub):
    tq = NA_QROWS * GRID_W
    tk = NA_KROWS * GRID_W
    n_qb = seq // tq
    n_steps = n_qb // n_sub
    hw = 2 * LANES
    n_hg = BRANCH_W // hw
    sec_blk = SEC // hw
    heads = hw // NA_HEAD_DIM
    return pl.pallas_call(
        functools.partial(_na_kernel, n_qb=n_qb, n_sub=n_sub),
        grid=(batch, n_hg, n_steps),
        in_specs=[
            pl.BlockSpec((n_sub * tq, hw), lambda b, g, i: (b * n_steps + i, SEC_NA_Q * sec_blk + g)),
            pl.BlockSpec((seq, hw), lambda b, g, i: (b, SEC_NA_K * sec_blk + g)),
            pl.BlockSpec((seq, hw), lambda b, g, i: (b, SEC_NA_V * sec_blk + g)),
            pl.BlockSpec((ctx_len, hw), lambda b, g, i: (b, SEC_NA_K * sec_blk + g)),
            pl.BlockSpec((ctx_len, hw), lambda b, g, i: (b, SEC_NA_V * sec_blk + g)),
            pl.BlockSpec((None, bias.shape[1], heads, tq, tk), lambda b, g, i: (layer, 0, g, 0, 0)),
        ],
        out_specs=pl.BlockSpec((n_sub * tq, hw), lambda b, g, i: (b * n_steps + i, g)),
        out_shape=jax.ShapeDtypeStruct((batch * seq, BRANCH_W), BF16),
        compiler_params=_params("parallel", "parallel", "arbitrary"),
        name="na_attention",
    )(p_lat, p_lat, p_lat, p_ctx, p_ctx, bias)


def _ctx_attn_kernel(q_ref, k_ref, v_ref, o_ref):
    first = _lane_mask(LANES, 0, NA_HEAD_DIM)
    sel = (first.astype(BF16), jnp.logical_not(first).astype(BF16))
    outs = []
    for pair in range(q_ref.shape[1] // LANES):
        lanes = slice(pair * LANES, (pair + 1) * LANES)
        q2, k2, v2 = q_ref[:, lanes], k_ref[:, lanes], v_ref[:, lanes]
        o_pair = []
        for hh in range(2):
            s = _dot_nt(q2 * sel[hh], k2)
            m = jnp.max(s, axis=-1, keepdims=True)
            p = jnp.exp(s - m)
            l = jnp.sum(p, axis=-1, keepdims=True)
            o_pair.append(_dot(p.astype(BF16), v2) * (1.0 / l))
        outs.append(jnp.where(first, o_pair[0], o_pair[1]))
    o_ref[...] = jnp.concatenate(outs, axis=1).astype(BF16)


def _ctx_attention(p_ctx, *, batch, ctx_len):
    hw = 2 * LANES
    n_hg = BRANCH_W // hw
    sec_blk = SEC // hw
    spec = lambda sec: pl.BlockSpec((ctx_len, hw), lambda b, g: (b, sec * sec_blk + g))
    return pl.pallas_call(
        _ctx_attn_kernel,
        grid=(batch, n_hg),
        in_specs=[spec(SEC_NA_Q), spec(SEC_NA_K), spec(SEC_NA_V)],
        out_specs=pl.BlockSpec((ctx_len, hw), lambda b, g: (b, g)),
        out_shape=jax.ShapeDtypeStruct((batch * ctx_len, BRANCH_W), BF16),
        compiler_params=_params("parallel", "parallel"),
        name="ctx_attention",
    )(p_ctx, p_ctx, p_ctx)


def _scan_head_of_lane(gla):
    lane = lax.broadcasted_iota(jnp.int32, (1, SCAN_HEADS * SCAN_DK), 1)
    if gla:
        return lax.shift_right_logical(lane, SCAN_DK.bit_length() - 1)
    return lax.shift_right_logical(lane & (LANES - 1), (SCAN_DK // 2).bit_length() - 1)


def _decay_factors(cum, ref_row, total_row):
    cm = cum[ref_row:ref_row + 1, :]
    tot = cum[total_row:total_row + 1, :]
    eq = jnp.exp(cum - cm)
    ek = jnp.exp(cm - cum)
    return eq, ek, jnp.exp(cm), jnp.exp(tot - cm), jnp.exp(tot)


def _chunk_prepare(q, k, v, factors, head_sel):
    eq, ek, _, e_tm, _ = factors
    qtb = (q * eq).astype(BF16)
    kt = k * ek
    khb = (kt * e_tm).astype(BF16)
    hmul = [sel.astype(BF16) for sel in head_sel]
    q_stack = jnp.concatenate([qtb * m for m in hmul], axis=0)
    v_rows = jnp.concatenate([v[:, h * SCAN_DV:(h + 1) * SCAN_DV] for h in range(SCAN_HEADS)], axis=0)
    k_rows = jnp.concatenate([khb * m for m in hmul], axis=0)
    return q_stack, kt.astype(BF16), _dot_tn(v_rows, k_rows)


def _chunk_scores(q_stack, ktb, st, e_m, tri_stack):
    rhs = jnp.concatenate([(st * e_m).astype(BF16), ktb], axis=0)
    both = _dot_nt(q_stack, rhs)
    return both[:, :SCAN_DV], jnp.where(tri_stack, both[:, SCAN_DV:], 0.0).astype(BF16)


def _chunk_output(inter, att, v):
    outs = []
    for h in range(SCAN_HEADS):
        rows = slice(h * CHUNK, (h + 1) * CHUNK)
        outs.append(_dot(att[rows], v[:, h * SCAN_DV:(h + 1) * SCAN_DV]) + inter[rows])
    return jnp.concatenate(outs, axis=1)


def _scan_masks(gla):
    head = _scan_head_of_lane(gla)
    head_sel = [head == h for h in range(SCAN_HEADS)]
    t = lax.broadcasted_iota(jnp.int32, (SCAN_HEADS * CHUNK, CHUNK), 0) & (CHUNK - 1)
    s = lax.broadcasted_iota(jnp.int32, (SCAN_HEADS * CHUNK, CHUNK), 1)
    return head_sel, s <= t, s >= t


def _ret_cum(lg_row, forward):
    t = lax.broadcasted_iota(jnp.int32, (CHUNK, lg_row.shape[1]), 0).astype(F32)
    steps = (t + 1.0) if forward else (float(CHUNK) - t)
    return steps * lg_row


def _scan_kernel(*refs, gla, n_chunks, from_zero):
    if gla:
        (qkf_ref, vf_ref, cf_ref, qkb_ref, vb_ref, cb_ref, s0_ref,
         of_ref, ob_ref, sfin_ref, stf, stb) = refs
    else:
        (qkf_ref, vf_ref, qkb_ref, vb_ref, lg_ref, s0_ref,
         of_ref, ob_ref, sfin_ref, stf, stb) = refs
    step = pl.program_id(1)
    hk = SCAN_HEADS * SCAN_DK

    @pl.when(step == 0)
    def _():
        if from_zero:
            stf[...] = jnp.zeros_like(stf)
            stb[...] = jnp.zeros_like(stb)
        else:
            stf[...] = s0_ref[0]
            stb[...] = s0_ref[1]

    head_sel, tril, triu = _scan_masks(gla)
    if not gla:
        fac_f = _decay_factors(_ret_cum(lg_ref[0:1, :], True), CHUNK // 2 - 1, CHUNK - 1)
        fac_b = _decay_factors(_ret_cum(lg_ref[1:2, :], False), CHUNK // 2, 0)

    chains = (
        (qkf_ref, vf_ref, cf_ref if gla else None, of_ref, stf, tril, (CHUNK // 2 - 1, CHUNK - 1),
         [slice(c * CHUNK, (c + 1) * CHUNK) for c in range(n_chunks)]),
        (qkb_ref, vb_ref, cb_ref if gla else None, ob_ref, stb, triu, (CHUNK // 2, 0),
         [slice(c * CHUNK, (c + 1) * CHUNK) for c in reversed(range(n_chunks))]),
    )
    prepared = []
    for qk_ref, v_ref, cum_ref, _, _, _, (ref_row, total_row), order in chains:
        items = []
        for rows in order:
            if gla:
                fac = _decay_factors(cum_ref[rows, :], ref_row, total_row)
            else:
                fac = fac_f if ref_row < total_row else fac_b
            qk = qk_ref[rows, :].astype(F32)
            items.append((fac, _chunk_prepare(qk[:, :hk], qk[:, hk:], v_ref[rows, :], fac, head_sel)))
        prepared.append(items)
    finals = []
    scored = []
    for (_, _, _, _, st_ref, tri, _, _), items in zip(chains, prepared):
        st = st_ref[...]
        states = []
        for fac, (_, _, inc) in items:
            states.append(st)
            st = st * fac[4] + inc
        st_ref[...] = st
        finals.append(st)
        scored.append([_chunk_scores(q_stack, ktb, st_c, fac[2], tri)
                       for st_c, (fac, (q_stack, ktb, _)) in zip(states, items)])
    for (_, v_ref, _, o_ref, _, _, _, order), results in zip(chains, scored):
        for rows, (inter, att) in zip(order, results):
            o_ref[rows, :] = _chunk_output(inter, att, v_ref[rows, :]).astype(BF16)
    sf, sb = finals

    @pl.when(step == pl.num_programs(1) - 1)
    def _():
        sfin_ref[0] = sf
        sfin_ref[1] = sb


def _scan(p, cum, lg, s0, *, gla, batch, rows_per_batch, tb):
    nb = rows_per_batch // tb
    hk = SCAN_HEADS * SCAN_DK
    hv = SCAN_HEADS * SCAN_DV
    sec_qk = SEC_GLA_QK if gla else SEC_RET_QK
    sec_v = SEC_GLA_V if gla else SEC_RET_V
    fwd = lambda b, s: b * nb + s
    bwd = lambda b, s: b * nb + nb - 1 - s
    from_zero = s0 is None
    if from_zero:
        s0 = jnp.zeros((batch, 2, SCAN_DV, hk), F32)
    in_specs = [
        pl.BlockSpec((tb, SEC), lambda b, s: (fwd(b, s), sec_qk)),
        pl.BlockSpec((tb, SEC), lambda b, s: (fwd(b, s), sec_v)),
    ]
    args = [p, p]
    if gla:
        in_specs.append(pl.BlockSpec((tb, hk), lambda b, s: (fwd(b, s), 0)))
        args.append(cum)
    in_specs += [
        pl.BlockSpec((tb, SEC), lambda b, s: (bwd(b, s), sec_qk)),
        pl.BlockSpec((tb, SEC), lambda b, s: (bwd(b, s), sec_v)),
    ]
    args += [p, p]
    if gla:
        in_specs.append(pl.BlockSpec((tb, hk), lambda b, s: (bwd(b, s), 1)))
        args.append(cum)
    else:
        in_specs.append(pl.BlockSpec((2, hk), lambda b, s: (0, 0)))
        args.append(lg)
    in_specs.append(pl.BlockSpec((None, 2, SCAN_DV, hk), lambda b, s: (b, 0, 0, 0)))
    args.append(s0)
    return pl.pallas_call(
        functools.partial(_scan_kernel, gla=gla, n_chunks=tb // CHUNK, from_zero=from_zero),
        grid=(batch, nb),
        in_specs=in_specs,
        out_specs=[
            pl.BlockSpec((tb, hv), lambda b, s: (fwd(b, s), 0)),
            pl.BlockSpec((tb, hv), lambda b, s: (bwd(b, s), 0)),
            pl.BlockSpec((None, 2, SCAN_DV, hk), lambda b, s: (b, 0, 0, 0)),
        ],
        out_shape=[
            jax.ShapeDtypeStruct((batch * rows_per_batch, hv), BF16),
            jax.ShapeDtypeStruct((batch * rows_per_batch, hv), BF16),
            jax.ShapeDtypeStruct((batch, 2, SCAN_DV, hk), F32),
        ],
        scratch_shapes=[pltpu.VMEM((SCAN_DV, hk), F32), pltpu.VMEM((SCAN_DV, hk), F32)],
        compiler_params=_params("parallel", "arbitrary"),
        name=("gla" if gla else "ret") + ("_ctx_scan" if from_zero else "_scan"),
    )(*args)


def _head_rms(o, gain):
    parts = []
    for h in range(SCAN_HEADS):
        sl = o[:, h * SCAN_DV:(h + 1) * SCAN_DV]
        ms = jnp.mean(sl * sl, axis=-1, keepdims=True)
        parts.append(sl * lax.rsqrt(ms + NORM_EPS))
    return jnp.concatenate(parts, axis=1) * gain


def _merge_kernel(x_ref, nw_ref, sh_ref, sc_ref, gate_ref, na_ref, gf_ref, gb_ref, rf_ref, rb_ref,
                  gg_ref, rg_ref, wz0_ref, wz1_ref, wz2_ref, wg0_ref, wg1_ref, wg2_ref, wb_ref, wo_ref,
                  o_ref):
    x = x_ref[...]
    wz_refs = (wz0_ref, wz1_ref, wz2_ref)
    wg_refs = (wg0_ref, wg1_ref, wg2_ref)
    hb = _ada_norm(x, nw_ref[...], sc_ref[...], sh_ref[...]).astype(BF16)
    outs = (
        na_ref[...].astype(F32),
        _head_rms(gf_ref[...].astype(F32) + gb_ref[...].astype(F32), gg_ref[...]),
        _head_rms(rf_ref[...].astype(F32) + rb_ref[...].astype(F32), rg_ref[...]),
    )
    zs = [_dot(hb, wz_refs[i][...]) for i in range(N_BRANCH)]
    us = [(outs[i] * _silu(zs[i])).astype(BF16) for i in range(N_BRANCH)]
    gs = [_sigmoid(_dot(hb, wg_refs[i][...])) for i in range(N_BRANCH)]
    y = None
    for i in range(N_BRANCH):
        t = gs[i] * _dot(us[i], wb_ref[i])
        y = t if y is None else y + t
    r = _dot(y.astype(BF16), wo_ref[...])
    o_ref[...] = x + gate_ref[...] * r


def _merge(x2d, mod, prm, na_o, gla_f, gla_b, ret_f, ret_b, *, layer, tm, row_of):
    t, d = x2d.shape
    row = lambda w: pl.BlockSpec((tm, w), lambda i: (i, 0))
    sh_spec, sc_spec, gate_spec = _mod_specs(mod, layer, row_of)
    lconst3 = lambda i: (layer, 0, 0)
    return pl.pallas_call(
        _merge_kernel,
        grid=(t // tm,),
        in_specs=[
            row(d),
            pl.BlockSpec((None, 1, d), lconst3),
            sh_spec, sc_spec, gate_spec,
            row(BRANCH_W), row(BRANCH_W), row(BRANCH_W), row(BRANCH_W), row(BRANCH_W),
            pl.BlockSpec((None, 1, BRANCH_W), lconst3),
            pl.BlockSpec((None, 1, BRANCH_W), lconst3),
            *[_resident((None, d, BRANCH_W), lambda i, n=n: (layer, 0, N_SEC * SEC // BRANCH_W + n))
              for n in range(N_BRANCH)],
            *[_resident((None, d, d), lambda i, n=n: (layer, 0, (N_SEC * SEC + N_BRANCH * BRANCH_W) // d + n))
              for n in range(N_BRANCH)],
            _resident((None, N_BRANCH, BRANCH_W, d), lambda i: (layer, 0, 0, 0)),
            _resident((None, d, d), lconst3),
        ],
        out_specs=row(d),
        out_shape=jax.ShapeDtypeStruct((t, d), F32),
        compiler_params=_params("parallel"),
        name="merge",
    )(x2d, prm["nw"], mod, mod, mod, na_o, gla_f, gla_b, ret_f, ret_b, prm["gla_gain"],
      prm["ret_gain"], *([prm["wpack"]] * (2 * N_BRANCH)), prm["wb"], prm["wo"])


def _in_splits():
    na = NA_HEADS * NA_HEAD_DIM
    qk = SCAN_HEADS * SCAN_DK
    vv = SCAN_HEADS * SCAN_DV
    sizes = (na, na, na, qk, qk, vv, 2 * GLA_GATE_RANK, qk, qk, vv, N_BRANCH * BRANCH_W,
             N_BRANCH * D_MODEL)
    return [int(v) for v in np.concatenate([[0], np.cumsum(sizes)])]


def _rotary_pack_matrix():
    hk = SCAN_HEADS * SCAN_DK
    quarter = SCAN_DK // 4
    p = np.zeros((SEC, SEC), np.float32)
    for blk in range(SEC // hk):
        for h in range(SCAN_HEADS):
            for half in range(2):
                for partner in range(2):
                    for i in range(quarter):
                        old = h * SCAN_DK + half * 2 * quarter + partner * quarter + i
                        new = partner * (hk // 2) + h * 2 * quarter + half * quarter + i
                        p[blk * hk + old, blk * hk + new] = 1.0
    return jnp.asarray(p).astype(BF16)


def _pack_kernel(a_ref, b_ref, perm_ref, o_ref, *, first_shifted, perm_block, shift):
    j = pl.program_id(1)
    sec = a_ref.shape[0]

    def shifted():
        x = jnp.concatenate([a_ref[...], b_ref[...]], axis=0)
        return x[shift:shift + sec]

    @pl.when(j < first_shifted)
    def _():
        o_ref[...] = a_ref[...].T.astype(BF16)

    @pl.when(j == perm_block)
    def _():
        o_ref[...] = _dot(shifted().T.astype(BF16), perm_ref[...]).astype(BF16)

    @pl.when(jnp.logical_and(j >= first_shifted, j != perm_block))
    def _():
        o_ref[...] = shifted().T.astype(BF16)


def _pack_w_in(w_in):
    depth, d, n_in = w_in.shape
    o = _in_splits()
    g0, g1 = o[6], o[7]
    assert g0 % SEC == 0 and (n_in - (g1 - g0)) % SEC == 0
    n_blocks = (n_in - (g1 - g0)) // SEC
    tail_rows = LANES
    w_t = jnp.swapaxes(w_in, 1, 2)
    return pl.pallas_call(
        functools.partial(_pack_kernel, first_shifted=g0 // SEC, perm_block=SEC_RET_QK,
                          shift=g1 - g0),
        grid=(depth, n_blocks),
        in_specs=[
            pl.BlockSpec((None, SEC, d), lambda l, j: (l, j, 0)),
            pl.BlockSpec((None, tail_rows, d), lambda l, j: (l, (j + 1) * (SEC // tail_rows), 0)),
            _resident((SEC, SEC), lambda l, j: (0, 0)),
        ],
        out_specs=pl.BlockSpec((None, d, SEC), lambda l, j: (l, 0, j)),
        out_shape=jax.ShapeDtypeStruct((depth, d, n_blocks * SEC), BF16),
        compiler_params=_params("parallel", "parallel"),
        name="pack_w_in",
    )(w_t, w_t, _rotary_pack_matrix())


def _prep_params(norm_w, w_in, q_norm, k_norm, w_gate, b_gate, gla_out_norm, ret_out_norm,
                 w_branch, w_out):
    depth, d, _ = w_in.shape
    o = _in_splits()
    g0, g1 = o[6], o[7]
    hk = SCAN_HEADS * SCAN_DK
    wg = jnp.pad(w_in[:, :, g0:g1], ((0, 0), (0, 0), (0, GATE_PAD - (g1 - g0)))).astype(BF16)
    wgate = jnp.zeros((depth, GATE_PAD, 2 * hk), F32)
    wgate = wgate.at[:, :GLA_GATE_RANK, :hk].set(w_gate[:, 0])
    wgate = wgate.at[:, GLA_GATE_RANK:2 * GLA_GATE_RANK, hk:].set(w_gate[:, 1])
    bgate = b_gate.reshape(depth, 1, 2 * hk)
    base = np.ones((N_SEC, 1, SEC), np.float32)
    base[SEC_NA_Q] = NA_HEAD_DIM ** -0.5
    base[SEC_GLA_QK, 0, :hk] = SCAN_DK ** -0.5
    base[SEC_RET_QK, 0, :hk] = SCAN_DK ** -0.5
    reps = SEC // NA_HEAD_DIM
    ones = jnp.ones((depth, N_SEC - 2, 1, SEC), F32)
    gains = jnp.concatenate([jnp.tile(q_norm, (1, reps))[:, None, None, :],
                             jnp.tile(k_norm, (1, reps))[:, None, None, :], ones], axis=1)
    return dict(
        nw=norm_w.reshape(depth, 1, d),
        wpack=_pack_w_in(w_in), wg=wg, wgate=wgate.astype(BF16), bgate=bgate,
        colscale=gains * jnp.asarray(base)[None],
        wb=w_branch.astype(BF16), wo=w_out.astype(BF16),
        gla_gain=gla_out_norm.reshape(depth, 1, BRANCH_W),
        ret_gain=ret_out_norm.reshape(depth, 1, BRANCH_W),
    )


def _tables(seq, ctx_len):
    i = np.arange(TRI_ROWS)
    same = (i[:, None] // CHUNK) == (i[None, :] // CHUNK)
    ltri = jnp.asarray((same & (i[None, :] <= i[:, None])).astype(np.float32)).astype(BF16)
    utri = jnp.asarray((same & (i[None, :] >= i[:, None])).astype(np.float32)).astype(BF16)
    grp = np.arange(MXU_DIM) // NA_HEAD_DIM
    gmat = jnp.asarray((grp[:, None] == grp[None, :]).astype(np.float32)).astype(BF16)
    quarter = SCAN_DK // 4
    n_rows = seq // GRID_W
    inv = ROPE_BASE ** (-jnp.arange(quarter, dtype=F32) / quarter)
    ang_r = jnp.arange(n_rows, dtype=F32)[:, None] * inv[None, :]
    ang_c = jnp.arange(GRID_W, dtype=F32)[:, None] * inv[None, :]
    zr, zc = jnp.zeros_like(ang_r), jnp.zeros_like(ang_c)
    lanes = lambda a, b: jnp.concatenate([a, b] * SCAN_HEADS, axis=1)
    rot_rows = jnp.stack([lanes(jnp.cos(ang_r), zr), lanes(jnp.sin(ang_r), zr)])
    rot_cols = jnp.stack([lanes(zc, jnp.cos(ang_c)), lanes(zc, jnp.sin(ang_c))])
    lat = dict(ltri=ltri, utri=utri, gmat=gmat, rot_rows=rot_rows, rot_cols=rot_cols)
    ctx = dict(ltri=ltri, utri=utri, gmat=gmat, rot_cols=jnp.zeros_like(rot_cols),
               rot_rows=jnp.zeros((2, ctx_len // GRID_W, LANES), F32))
    return lat, ctx


def _tile_plan(seq):
    tm = 2 * MXU_DIM
    tb_scan = min(seq, 16 * CHUNK)
    na_blocks = min(4, seq // (NA_QROWS * GRID_W))
    assert seq % tm == 0 and seq % tb_scan == 0 and (seq // (NA_QROWS * GRID_W)) % na_blocks == 0
    return tm, tm, tb_scan, na_blocks


def _log_gamma(offset):
    g = jnp.log1p(-jnp.exp2(-(offset + jnp.arange(SCAN_HEADS, dtype=F32))))
    return jnp.tile(jnp.repeat(g, SCAN_DK // 2), 2)


def kernel(x, c, ctx, c_ctx, w_mod, b_mod, norm_w, w_in, na_q_norm, na_k_norm, na_rpb, gla_w_gate,
           gla_b_gate, gla_out_norm, ret_out_norm, w_branch, w_out):
    batch, seq, d = x.shape
    ctx_len = ctx.shape[1]
    depth = w_mod.shape[0]
    rows = seq // GRID_W
    assert d == D_MODEL and x.dtype == F32 and seq % GRID_W == 0 and rows >= NA_KROWS
    assert ctx_len % TRI_ROWS == 0 and batch + 1 <= 8
    tiles = _tile_plan(seq)
    tm_proj, tm_merge, tb_scan, na_blocks_per_step = tiles

    c_rows = jnp.zeros((8, d), F32).at[:batch].set(c).at[batch].set(c_ctx)
    mod = _modulation(c_rows, w_mod, b_mod).reshape(depth, 8, 1, 3 * d)
    prm = _prep_params(norm_w, w_in, na_q_norm, na_k_norm, gla_w_gate, gla_b_gate, gla_out_norm,
                       ret_out_norm, w_branch, w_out)
    lat_tabs, ctx_tabs = _tables(seq, ctx_len)
    bias = _na_bias_tables(na_rpb, rows)
    lg = jnp.stack([_log_gamma(RET_DECAY_FWD), _log_gamma(RET_DECAY_BWD)])
    ctx_row = lambda i: batch

    xl = x.reshape(batch * seq, d)
    xc = ctx.reshape(batch * ctx_len, d)
    for layer in range(depth):
        with_ctx = layer < depth - 1
        p_lat, cum_lat = _in_proj(xl, mod, prm, lat_tabs, layer=layer, tm=tm_proj,
                                  row_of=lambda i: i // (seq // tm_proj),
                                  blocks_per_batch=seq // tm_proj, rope=True)
        p_ctx, cum_ctx = _in_proj(xc, mod, prm, ctx_tabs, layer=layer, tm=ctx_len, row_of=ctx_row,
                                  blocks_per_batch=1, rope=False)

        na_l = _na_attention(p_lat, p_ctx, bias, layer=layer, batch=batch, seq=seq, ctx_len=ctx_len,
                             n_sub=na_blocks_per_step)

        gla_cf, gla_cb, gla_s = _scan(p_ctx, cum_ctx, None, None, gla=True, batch=batch,
                                      rows_per_batch=ctx_len, tb=ctx_len)
        gla_f, gla_b, _ = _scan(p_lat, cum_lat, None, gla_s, gla=True, batch=batch,
                                rows_per_batch=seq, tb=tb_scan)
        ret_cf, ret_cb, ret_s = _scan(p_ctx, None, lg, None, gla=False, batch=batch,
                                      rows_per_batch=ctx_len, tb=ctx_len)
        ret_f, ret_b, _ = _scan(p_lat, None, lg, ret_s, gla=False, batch=batch,
                                rows_per_batch=seq, tb=tb_scan)

        xl_new = _merge(xl, mod, prm, na_l, gla_f, gla_b, ret_f, ret_b, layer=layer, tm=tm_merge,
                        row_of=lambda i: i // (seq // tm_merge))
        if with_ctx:
            na_c = _ctx_attention(p_ctx, batch=batch, ctx_len=ctx_len)
            xc = _merge(xc, mod, prm, na_c, gla_cf, gla_cb, ret_cf, ret_cb, layer=layer, tm=ctx_len,
                        row_of=ctx_row)
        xl = xl_new
    return xl.reshape(batch, seq, d)
```

```python
import functools

import numpy as np
import jax
import jax.numpy as jnp
from jax import lax
from jax.experimental import pallas as pl
from jax.experimental.pallas import tpu as pltpu

D_MODEL = 1024
GRID_W = 64
NORM_EPS = 1e-6
N_BRANCH = 3
BRANCH_W = D_MODEL // 2
NA_HEAD_DIM = 64
NA_HEADS = BRANCH_W // NA_HEAD_DIM
NA_WIN_ROWS = 8
NA_WIN_COLS = 16
SCAN_HEADS = 4
SCAN_DV = BRANCH_W // SCAN_HEADS
SCAN_DK = SCAN_DV // 2
GLA_GATE_RANK = 16
GLA_GATE_TAU = 16.0
RET_DECAY_FWD = 5.0
RET_DECAY_BWD = 5.5
CHUNK = 64
ROPE_BASE = 10000.0

LANES = 128
MXU_DIM = 256

SEC = 512
SEC_NA_Q, SEC_NA_K, SEC_NA_V, SEC_GLA_QK, SEC_GLA_V, SEC_RET_QK, SEC_RET_V = range(7)
N_SEC = 7
GATE_PAD = LANES
TRI_ROWS = MXU_DIM
NA_QROWS = 4
NA_KROWS = 12
NEG_BIG = -1e30

VMEM_LIMIT = 56 * 1024 * 1024

F32 = jnp.float32
BF16 = jnp.bfloat16


def _dot(a, b):
    return jnp.dot(a, b, preferred_element_type=F32)


def _dot_nt(a, b):
    return lax.dot_general(a, b, (((1,), (1,)), ((), ())), preferred_element_type=F32)


def _dot_tn(a, b):
    return lax.dot_general(a, b, (((0,), (0,)), ((), ())), preferred_element_type=F32)


def _split2(x):
    hi = x.astype(BF16)
    lo = (x - hi.astype(F32)).astype(BF16)
    return hi, lo


def _log_sigmoid(x):
    return jnp.minimum(x, 0.0) - jnp.log(1.0 + jnp.exp(-jnp.abs(x)))


def _sigmoid(x):
    return 1.0 / (1.0 + jnp.exp(-x))


def _silu(x):
    return x * _sigmoid(x)


def _params(*sem):
    return pltpu.CompilerParams(dimension_semantics=sem, vmem_limit_bytes=VMEM_LIMIT)


def _resident(block_shape, index_map):
    return pl.BlockSpec(block_shape, index_map, pipeline_mode=pl.Buffered(1))


def _ada_norm(x, nw, sc, sh):
    ms = jnp.mean(x * x, axis=-1, keepdims=True)
    return (x * lax.rsqrt(ms + NORM_EPS)) * (nw * (1.0 + sc)) + sh


def _mod_kernel(c_ref, w_ref, b_ref, o_ref):
    a = _silu(c_ref[...])
    o_ref[...] = jnp.dot(a, w_ref[...], preferred_element_type=F32,
                         precision=lax.Precision.HIGHEST) + b_ref[...]


def _modulation(c_rows, w_mod, b_mod):
    depth, d, d3 = w_mod.shape
    rows = c_rows.shape[0]
    tn = d
    return pl.pallas_call(
        _mod_kernel,
        grid=(depth, d3 // tn),
        in_specs=[
            pl.BlockSpec((rows, d), lambda l, j: (0, 0)),
            pl.BlockSpec((None, d, tn), lambda l, j: (l, 0, j)),
            pl.BlockSpec((None, 1, tn), lambda l, j: (l, 0, j)),
        ],
        out_specs=pl.BlockSpec((None, rows, tn), lambda l, j: (l, 0, j)),
        out_shape=jax.ShapeDtypeStruct((depth, rows, d3), F32),
        compiler_params=_params("parallel", "parallel"),
        name="modulation",
    )(c_rows, w_mod, b_mod.reshape(depth, 1, d3))


def _mod_specs(mod, layer, row_of):
    d = mod.shape[-1] // 3
    spec = lambda third: pl.BlockSpec((None, None, 1, d), lambda i: (layer, row_of(i), 0, third))
    return spec(0), spec(1), spec(2)


def _in_proj_kernel(x_ref, nw_ref, sh_ref, sc_ref, w_ref, cs_ref, gm_ref, wg_ref, wgate_ref,
                    bgate_ref, ltri_ref, utri_ref, rt_ref, ct_ref, p_ref, cum_ref, *, rope):
    tm = x_ref.shape[0]
    hb = _ada_norm(x_ref[...], nw_ref[...], sc_ref[...], sh_ref[...]).astype(BF16)

    g = _dot(hb, wg_ref[...])
    normed = {}
    for sec in range(N_SEC):
        base = sec * SEC
        raw = _dot(hb, w_ref[:, base:base + SEC])
        acc = raw * cs_ref[sec]
        if sec <= SEC_NA_K:
            normed[sec] = (raw, acc)
        elif rope and sec == SEC_RET_QK:
            n_grid_rows = tm // GRID_W

            def table(t):
                by_row = jnp.concatenate(
                    [jnp.broadcast_to(rt_ref[t, g:g + 1, :], (GRID_W, LANES)) for g in range(n_grid_rows)],
                    axis=0)
                return by_row + jnp.concatenate([ct_ref[t]] * n_grid_rows, axis=0)

            cos = table(0)
            sin = table(1)
            for c in range(0, SEC, 2 * LANES):
                u1 = acc[:, c:c + LANES]
                u2 = acc[:, c + LANES:c + 2 * LANES]
                p_ref[:, base + c:base + c + LANES] = (u1 * cos - u2 * sin).astype(BF16)
                p_ref[:, base + c + LANES:base + c + 2 * LANES] = (u1 * sin + u2 * cos).astype(BF16)
        else:
            p_ref[:, base:base + SEC] = acc.astype(BF16)
        if sec == SEC_NA_V:
            xg = _dot(g.astype(BF16), wgate_ref[...]) + bgate_ref[...]
            la = _log_sigmoid(xg) * (1.0 / GLA_GATE_TAU)

    for sec, (raw, acc) in normed.items():
        base = sec * SEC
        for c in range(0, SEC, MXU_DIM):
            sq = raw[:, c:c + MXU_DIM] * raw[:, c:c + MXU_DIM]
            ss = _dot(sq.astype(BF16), gm_ref[...])
            p_ref[:, base + c:base + c + MXU_DIM] = (
                acc[:, c:c + MXU_DIM] * lax.rsqrt(ss * (1.0 / NA_HEAD_DIM) + NORM_EPS)).astype(BF16)

    half = la.shape[1] // 2
    for r in range(tm // TRI_ROWS):
        rows = slice(r * TRI_ROWS, (r + 1) * TRI_ROWS)
        fh, fl = _split2(la[rows, :half])
        bh, bl = _split2(la[rows, half:])
        cum_ref[rows, :half] = _dot(ltri_ref[...], fh) + _dot(ltri_ref[...], fl)
        cum_ref[rows, half:] = _dot(utri_ref[...], bh) + _dot(utri_ref[...], bl)


def _in_proj(x2d, mod, prm, tabs, *, layer, tm, row_of, blocks_per_batch, rope):
    t, d = x2d.shape
    const2 = lambda i: (0, 0)
    lconst = lambda i: (layer, 0, 0)
    row_tab = pl.BlockSpec((2, tm // GRID_W, LANES), lambda i: (0, i % blocks_per_batch, 0))
    col_tab = _resident((2, GRID_W, LANES), lambda i: (0, 0, 0))
    sh_spec, sc_spec, _ = _mod_specs(mod, layer, row_of)
    return pl.pallas_call(
        functools.partial(_in_proj_kernel, rope=rope),
        grid=(t // tm,),
        in_specs=[
            pl.BlockSpec((tm, d), lambda i: (i, 0)),
            pl.BlockSpec((None, 1, d), lconst),
            sh_spec, sc_spec,
            _resident((None, d, N_SEC * SEC), lconst),
            _resident((None, N_SEC, 1, SEC), lambda i: (layer, 0, 0, 0)),
            _resident((MXU_DIM, MXU_DIM), const2),
            _resident((None, d, GATE_PAD), lconst),
            _resident((None, GATE_PAD, SEC), lconst),
            _resident((None, 1, SEC), lconst),
            _resident((TRI_ROWS, TRI_ROWS), const2),
            _resident((TRI_ROWS, TRI_ROWS), const2),
            row_tab, col_tab,
        ],
        out_specs=[
            pl.BlockSpec((tm, N_SEC * SEC), lambda i: (i, 0)),
            pl.BlockSpec((tm, SEC), lambda i: (i, 0)),
        ],
        out_shape=[
            jax.ShapeDtypeStruct((t, N_SEC * SEC), BF16),
            jax.ShapeDtypeStruct((t, SEC), F32),
        ],
        compiler_params=_params("parallel"),
        name="in_proj_rope" if rope else "in_proj",
    )(x2d, prm["nw"], mod, mod, prm["wpack"], prm["colscale"], tabs["gmat"], prm["wg"],
      prm["wgate"], prm["bgate"], tabs["ltri"], tabs["utri"], tabs["rot_rows"], tabs["rot_cols"])


def _na_block_plan(rows):
    n_qb = rows // NA_QROWS
    kr = min(NA_WIN_ROWS, rows)
    plans = []
    for qb in (0, min(1, n_qb - 1), n_qb - 1):
        t0 = int(np.clip(qb - 1, 0, n_qb - NA_KROWS // NA_QROWS))
        plan = []
        for i in range(NA_QROWS):
            r = qb * NA_QROWS + i
            rs = int(np.clip(r - kr // 2, 0, rows - kr))
            plan.append([(t0 * NA_QROWS + j) - r + NA_WIN_ROWS - 1
                         if rs <= t0 * NA_QROWS + j < rs + kr else None
                         for j in range(NA_KROWS)])
        plans.append(plan)
    return plans


def _bias_kernel(rpb_ref, o_ref, *, plans):
    shape = (GRID_W, LANES)
    lane = lax.broadcasted_iota(jnp.int32, shape, 1)
    c = lax.broadcasted_iota(jnp.int32, shape, 0)
    kc = lane & (GRID_W - 1)
    cs = jnp.clip(c - NA_WIN_COLS // 2, 0, GRID_W - NA_WIN_COLS)
    col_ok = jnp.logical_and(kc >= cs, kc < cs + NA_WIN_COLS)
    low = lane < GRID_W
    neg = jnp.full(shape, NEG_BIG, F32)
    toep = []
    for a in range(2 * NA_WIN_ROWS - 1):
        x = jnp.broadcast_to(rpb_ref[a:a + 1, :], shape)
        lo = pltpu.roll(x, LANES - (NA_WIN_COLS - 1), 1, stride=1, stride_axis=0)
        hi = pltpu.roll(x, GRID_W - (NA_WIN_COLS - 1), 1, stride=1, stride_axis=0)
        toep.append(jnp.where(col_ok, jnp.where(low, lo, hi), neg))
    for k, plan in enumerate(plans):
        for i in range(NA_QROWS):
            for jp in range(NA_KROWS // 2):
                a0, a1 = plan[i][2 * jp], plan[i][2 * jp + 1]
                b0 = neg if a0 is None else toep[a0]
                b1 = neg if a1 is None else toep[a1]
                o_ref[k, i * GRID_W:(i + 1) * GRID_W, jp * LANES:(jp + 1) * LANES] = (
                    jnp.where(low, b0, b1))


def _na_bias_tables(rpb, rows):
    depth, heads, nr, nc = rpb.shape
    rp = jnp.pad(rpb, ((0, 0), (0, 0), (0, 16 - nr), (0, LANES - nc)))
    tq, tk = NA_QROWS * GRID_W, NA_KROWS * GRID_W
    return pl.pallas_call(
        functools.partial(_bias_kernel, plans=_na_block_plan(rows)),
        grid=(depth, heads),
        in_specs=[pl.BlockSpec((None, None, 16, LANES), lambda l, h: (l, h, 0, 0))],
        out_specs=pl.BlockSpec((None, 3, None, tq, tk), lambda l, h: (l, 0, h, 0, 0)),
        out_shape=jax.ShapeDtypeStruct((depth, 3, heads, tq, tk), F32),
        compiler_params=_params("parallel", "parallel"),
        name="na_bias",
    )(rp)


def _lane_mask(width, lo, hi):
    lane = lax.broadcasted_iota(jnp.int32, (1, width), 1)
    return jnp.logical_and(lane >= lo, lane < hi)


def _na_kernel(q_ref, k_ref, v_ref, kc_ref, vc_ref, bias_ref, o_ref, *, n_qb, n_sub):
    tq = q_ref.shape[0] // n_sub
    tk = bias_ref.shape[-1]
    width = q_ref.shape[1]
    heads = width // NA_HEAD_DIM
    kc = kc_ref[...]
    vc = vc_ref[...]
    inside = [_lane_mask(width, h * NA_HEAD_DIM, (h + 1) * NA_HEAD_DIM) for h in range(heads)]
    ones_row = [_lane_mask(width, ((h + 1) % heads) * NA_HEAD_DIM, ((h + 1) % heads) * NA_HEAD_DIM + 1)
                for h in range(heads)]
    starts = []
    for sub in range(n_sub):
        qb = pl.program_id(2) * n_sub + sub
        t0 = jnp.clip(qb - 1, 0, n_qb - tk // tq)
        starts.append((qb, pl.multiple_of(t0 * tq, tq)))
    units = [(sub, h) for sub in range(n_sub) for h in range(heads)]

    def scores(sub, h):
        qb, start = starts[sub]
        kind = jnp.where(qb == 0, 0, jnp.where(qb == n_qb - 1, 2, 1))
        qh = q_ref[sub * tq:(sub + 1) * tq, :] * inside[h].astype(BF16)
        return _dot_nt(qh, k_ref[pl.ds(start, tk), :]) + bias_ref[kind, h], _dot_nt(qh, kc)

    pending = scores(*units[0])
    acc = None
    for idx, (sub, h) in enumerate(units):
        s, sc = pending
        if idx + 1 < len(units):
            pending = scores(*units[idx + 1])
        keep = inside[h].astype(BF16)
        fill = ones_row[h].astype(BF16)
        m = jnp.maximum(jnp.max(s, axis=-1, keepdims=True), jnp.max(sc, axis=-1, keepdims=True))
        p = jnp.exp((s - m).astype(BF16))
        pc = jnp.exp((sc - m).astype(BF16))
        v3 = v_ref[pl.ds(starts[sub][1], tk), :]
        oh = _dot(p, v3 * keep + fill) + _dot(pc, vc * keep + fill)
        l = jnp.sum(jnp.where(ones_row[h], oh, 0.0), axis=-1, keepdims=True)
        acc = oh * (1.0 / l) if h == 0 else jnp.where(inside[h], oh * (1.0 / l), acc)
        if h == heads - 1:
            o_ref[sub * tq:(sub + 1) * tq, :] = acc.astype(BF16)


def _na_attention(p_lat, p_ctx, bias, *, layer, batch, seq, ctx_len, n_sub):
    tq = NA_QROWS * GRID_W
    tk = NA_KROWS * GRID_W
    n_qb = seq // tq
    n_steps = n_qb // n_sub
    hw = 2 * LANES
    n_hg = BRANCH_W // hw
    sec_blk = SEC // hw
    heads = hw // NA_HEAD_DIM
    return pl.pallas_call(
        functools.partial(_na_kernel, n_qb=n_qb, n_sub=n_sub),
        grid=(batch, n_hg, n_steps),
        in_specs=[
            pl.BlockSpec((n_sub * tq, hw), lambda b, g, i: (b * n_steps + i, SEC_NA_Q * sec_blk + g)),
            pl.BlockSpec((seq, hw), lambda b, g, i: (b, SEC_NA_K * sec_blk + g)),
            pl.BlockSpec((seq, hw), lambda b, g, i: (b, SEC_NA_V * sec_blk + g)),
            pl.BlockSpec((ctx_len, hw), lambda b, g, i: (b, SEC_NA_K * sec_blk + g)),
            pl.BlockSpec((ctx_len, hw), lambda b, g, i: (b, SEC_NA_V * sec_blk + g)),
            pl.BlockSpec((None, bias.shape[1], heads, tq, tk), lambda b, g, i: (layer, 0, g, 0, 0)),
        ],
        out_specs=pl.BlockSpec((n_sub * tq, hw), lambda b, g, i: (b * n_steps + i, g)),
        out_shape=jax.ShapeDtypeStruct((batch * seq, BRANCH_W), BF16),
        compiler_params=_params("parallel", "parallel", "arbitrary"),
        name="na_attention",
    )(p_lat, p_lat, p_lat, p_ctx, p_ctx, bias)


def _ctx_attn_kernel(q_ref, k_ref, v_ref, o_ref):
    first = _lane_mask(LANES, 0, NA_HEAD_DIM)
    sel = (first.astype(BF16), jnp.logical_not(first).astype(BF16))
    outs = []
    for pair in range(q_ref.shape[1] // LANES):
        lanes = slice(pair * LANES, (pair + 1) * LANES)
        q2, k2, v2 = q_ref[:, lanes], k_ref[:, lanes], v_ref[:, lanes]
        o_pair = []
        for hh in range(2):
            s = _dot_nt(q2 * sel[hh], k2)
            m = jnp.max(s, axis=-1, keepdims=True)
            p = jnp.exp(s - m)
            l = jnp.sum(p, axis=-1, keepdims=True)
            o_pair.append(_dot(p.astype(BF16), v2) * (1.0 / l))
        outs.append(jnp.where(first, o_pair[0], o_pair[1]))
    o_ref[...] = jnp.concatenate(outs, axis=1).astype(BF16)


def _ctx_attention(p_ctx, *, batch, ctx_len):
    hw = 2 * LANES
    n_hg = BRANCH_W // hw
    sec_blk = SEC // hw
    spec = lambda sec: pl.BlockSpec((ctx_len, hw), lambda b, g: (b, sec * sec_blk + g))
    return pl.pallas_call(
        _ctx_attn_kernel,
        grid=(batch, n_hg),
        in_specs=[spec(SEC_NA_Q), spec(SEC_NA_K), spec(SEC_NA_V)],
        out_specs=pl.BlockSpec((ctx_len, hw), lambda b, g: (b, g)),
        out_shape=jax.ShapeDtypeStruct((batch * ctx_len, BRANCH_W), BF16),
        compiler_params=_params("parallel", "parallel"),
        name="ctx_attention",
    )(p_ctx, p_ctx, p_ctx)


def _scan_head_of_lane(gla):
    lane = lax.broadcasted_iota(jnp.int32, (1, SCAN_HEADS * SCAN_DK), 1)
    if gla:
        return lax.shift_right_logical(lane, SCAN_DK.bit_length() - 1)
    return lax.shift_right_logical(lane & (LANES - 1), (SCAN_DK // 2).bit_length() - 1)


def _decay_factors(cum, ref_row, total_row):
    cm = cum[ref_row:ref_row + 1, :]
    tot = cum[total_row:total_row + 1, :]
    eq = jnp.exp(cum - cm)
    ek = jnp.exp(cm - cum)
    return eq, ek, jnp.exp(cm), jnp.exp(tot - cm), jnp.exp(tot)


def _chunk_prepare(q, k, v, factors, head_sel):
    eq, ek, _, e_tm, _ = factors
    qtb = (q * eq).astype(BF16)
    kt = k * ek
    khb = (kt * e_tm).astype(BF16)
    hmul = [sel.astype(BF16) for sel in head_sel]
    q_stack = jnp.concatenate([qtb * m for m in hmul], axis=0)
    v_rows = jnp.concatenate([v[:, h * SCAN_DV:(h + 1) * SCAN_DV] for h in range(SCAN_HEADS)], axis=0)
    k_rows = jnp.concatenate([khb * m for m in hmul], axis=0)
    return q_stack, kt.astype(BF16), _dot_tn(v_rows, k_rows)


def _chunk_scores(q_stack, ktb, st, e_m, tri_stack):
    rhs = jnp.concatenate([(st * e_m).astype(BF16), ktb], axis=0)
    both = _dot_nt(q_stack, rhs)
    return both[:, :SCAN_DV], jnp.where(tri_stack, both[:, SCAN_DV:], 0.0).astype(BF16)


def _chunk_output(inter, att, v):
    outs = []
    for h in range(SCAN_HEADS):
        rows = slice(h * CHUNK, (h + 1) * CHUNK)
        outs.append(_dot(att[rows], v[:, h * SCAN_DV:(h + 1) * SCAN_DV]) + inter[rows])
    return jnp.concatenate(outs, axis=1)


def _scan_masks(gla):
    head = _scan_head_of_lane(gla)
    head_sel = [head == h for h in range(SCAN_HEADS)]
    t = lax.broadcasted_iota(jnp.int32, (SCAN_HEADS * CHUNK, CHUNK), 0) & (CHUNK - 1)
    s = lax.broadcasted_iota(jnp.int32, (SCAN_HEADS * CHUNK, CHUNK), 1)
    return head_sel, s <= t, s >= t


def _ret_cum(lg_row, forward):
    t = lax.broadcasted_iota(jnp.int32, (CHUNK, lg_row.shape[1]), 0).astype(F32)
    steps = (t + 1.0) if forward else (float(CHUNK) - t)
    return steps * lg_row


def _scan_block(fwd_refs, bwd_refs, n_chunks, gla, ret_factors, masks):
    head_sel, tril, triu = masks
    hk = SCAN_HEADS * SCAN_DK
    chains = (
        (*fwd_refs, tril, (CHUNK // 2 - 1, CHUNK - 1), 0,
         [slice(c * CHUNK, (c + 1) * CHUNK) for c in range(n_chunks)]),
        (*bwd_refs, triu, (CHUNK // 2, 0), 1,
         [slice(c * CHUNK, (c + 1) * CHUNK) for c in reversed(range(n_chunks))]),
    )
    prepared = []
    for qk_ref, v_ref, cum_ref, _, _, _, (ref_row, total_row), direction, order in chains:
        items = []
        for rows in order:
            fac = (_decay_factors(cum_ref[rows, :], ref_row, total_row) if gla
                   else ret_factors[direction])
            qk = qk_ref[rows, :].astype(F32)
            items.append((fac, _chunk_prepare(qk[:, :hk], qk[:, hk:], v_ref[rows, :], fac, head_sel)))
        prepared.append(items)
    scored = []
    for (_, _, _, _, st_ref, tri, _, _, _), items in zip(chains, prepared):
        st = st_ref[...]
        states = []
        for fac, (_, _, inc) in items:
            states.append(st)
            st = st * fac[4] + inc
        st_ref[...] = st
        scored.append([_chunk_scores(q_stack, ktb, st_c, fac[2], tri)
                       for st_c, (fac, (q_stack, ktb, _)) in zip(states, items)])
    for (_, v_ref, _, o_ref, _, _, _, _, order), results in zip(chains, scored):
        for rows, (inter, att) in zip(order, results):
            o_ref[rows, :] = _chunk_output(inter, att, v_ref[rows, :]).astype(BF16)


def _scan_kernel(*refs, gla):
    if gla:
        (qkf_ref, vf_ref, cf_ref, qkb_ref, vb_ref, cb_ref, qkc_ref, vc_ref, ccf_ref, ccb_ref,
         of_ref, ob_ref, ocf_ref, ocb_ref, stf, stb) = refs
        ret_factors = None
    else:
        (qkf_ref, vf_ref, qkb_ref, vb_ref, qkc_ref, vc_ref, lg_ref,
         of_ref, ob_ref, ocf_ref, ocb_ref, stf, stb) = refs
        cf_ref = cb_ref = ccf_ref = ccb_ref = None
        ret_factors = (_decay_factors(_ret_cum(lg_ref[0:1, :], True), CHUNK // 2 - 1, CHUNK - 1),
                       _decay_factors(_ret_cum(lg_ref[1:2, :], False), CHUNK // 2, 0))
    masks = _scan_masks(gla)

    @pl.when(pl.program_id(1) == 0)
    def _():
        stf[...] = jnp.zeros_like(stf)
        stb[...] = jnp.zeros_like(stb)
        _scan_block((qkc_ref, vc_ref, ccf_ref, ocf_ref, stf), (qkc_ref, vc_ref, ccb_ref, ocb_ref, stb),
                    qkc_ref.shape[0] // CHUNK, gla, ret_factors, masks)

    _scan_block((qkf_ref, vf_ref, cf_ref, of_ref, stf), (qkb_ref, vb_ref, cb_ref, ob_ref, stb),
                qkf_ref.shape[0] // CHUNK, gla, ret_factors, masks)


def _scan(p_lat, p_ctx, cum_lat, cum_ctx, lg, *, gla, batch, seq, ctx_len, tb):
    nb = seq // tb
    hk = SCAN_HEADS * SCAN_DK
    hv = SCAN_HEADS * SCAN_DV
    sec_qk = SEC_GLA_QK if gla else SEC_RET_QK
    sec_v = SEC_GLA_V if gla else SEC_RET_V
    fwd = lambda b, s: b * nb + s
    bwd = lambda b, s: b * nb + nb - 1 - s

    def stream(order):
        specs = [pl.BlockSpec((tb, SEC), lambda b, s: (order(b, s), sec_qk)),
                 pl.BlockSpec((tb, SEC), lambda b, s: (order(b, s), sec_v))]
        return specs, [p_lat, p_lat]

    in_specs, args = stream(fwd)
    if gla:
        in_specs.append(pl.BlockSpec((tb, hk), lambda b, s: (fwd(b, s), 0)))
        args.append(cum_lat)
    specs_b, args_b = stream(bwd)
    in_specs += specs_b
    args += args_b
    if gla:
        in_specs.append(pl.BlockSpec((tb, hk), lambda b, s: (bwd(b, s), 1)))
        args.append(cum_lat)
    in_specs += [pl.BlockSpec((ctx_len, SEC), lambda b, s: (b, sec_qk)),
                 pl.BlockSpec((ctx_len, SEC), lambda b, s: (b, sec_v))]
    args += [p_ctx, p_ctx]
    if gla:
        in_specs += [pl.BlockSpec((ctx_len, hk), lambda b, s: (b, 0)),
                     pl.BlockSpec((ctx_len, hk), lambda b, s: (b, 1))]
        args += [cum_ctx, cum_ctx]
    else:
        in_specs.append(pl.BlockSpec((2, hk), lambda b, s: (0, 0)))
        args.append(lg)
    ctx_out = pl.BlockSpec((ctx_len, hv), lambda b, s: (b, 0))
    o_f, o_b, oc_f, oc_b = pl.pallas_call(
        functools.partial(_scan_kernel, gla=gla),
        grid=(batch, nb),
        in_specs=in_specs,
        out_specs=[
            pl.BlockSpec((tb, hv), lambda b, s: (fwd(b, s), 0)),
            pl.BlockSpec((tb, hv), lambda b, s: (bwd(b, s), 0)),
            ctx_out, ctx_out,
        ],
        out_shape=[
            jax.ShapeDtypeStruct((batch * seq, hv), BF16),
            jax.ShapeDtypeStruct((batch * seq, hv), BF16),
            jax.ShapeDtypeStruct((batch * ctx_len, hv), BF16),
            jax.ShapeDtypeStruct((batch * ctx_len, hv), BF16),
        ],
        scratch_shapes=[pltpu.VMEM((SCAN_DV, hk), F32), pltpu.VMEM((SCAN_DV, hk), F32)],
        compiler_params=_params("parallel", "arbitrary"),
        name="gla_scan" if gla else "ret_scan",
    )(*args)
    return (o_f, o_b), (oc_f, oc_b)


def _head_rms(o, gain):
    parts = []
    for h in range(SCAN_HEADS):
        sl = o[:, h * SCAN_DV:(h + 1) * SCAN_DV]
        ms = jnp.mean(sl * sl, axis=-1, keepdims=True)
        parts.append(sl * lax.rsqrt(ms + NORM_EPS))
    return jnp.concatenate(parts, axis=1) * gain


def _merge_kernel(x_ref, nw_ref, sh_ref, sc_ref, gate_ref, na_ref, gf_ref, gb_ref, rf_ref, rb_ref,
                  gg_ref, rg_ref, wz0_ref, wz1_ref, wz2_ref, wg0_ref, wg1_ref, wg2_ref, wb_ref, wo_ref,
                  o_ref):
    x = x_ref[...]
    wz_refs = (wz0_ref, wz1_ref, wz2_ref)
    wg_refs = (wg0_ref, wg1_ref, wg2_ref)
    hb = _ada_norm(x, nw_ref[...], sc_ref[...], sh_ref[...]).astype(BF16)
    outs = (
        na_ref[...].astype(F32),
        _head_rms(gf_ref[...].astype(F32) + gb_ref[...].astype(F32), gg_ref[...]),
        _head_rms(rf_ref[...].astype(F32) + rb_ref[...].astype(F32), rg_ref[...]),
    )
    zs = [_dot(hb, wz_refs[i][...]) for i in range(N_BRANCH)]
    us = [(outs[i] * _silu(zs[i])).astype(BF16) for i in range(N_BRANCH)]
    gs = [_sigmoid(_dot(hb, wg_refs[i][...])) for i in range(N_BRANCH)]
    y = None
    for i in range(N_BRANCH):
        t = gs[i] * _dot(us[i], wb_ref[i])
        y = t if y is None else y + t
    r = _dot(y.astype(BF16), wo_ref[...])
    o_ref[...] = x + gate_ref[...] * r


def _merge(x2d, mod, prm, na_o, gla_f, gla_b, ret_f, ret_b, *, layer, tm, row_of):
    t, d = x2d.shape
    row = lambda w: pl.BlockSpec((tm, w), lambda i: (i, 0))
    sh_spec, sc_spec, gate_spec = _mod_specs(mod, layer, row_of)
    lconst3 = lambda i: (layer, 0, 0)
    return pl.pallas_call(
        _merge_kernel,
        grid=(t // tm,),
        in_specs=[
            row(d),
            pl.BlockSpec((None, 1, d), lconst3),
            sh_spec, sc_spec, gate_spec,
            row(BRANCH_W), row(BRANCH_W), row(BRANCH_W), row(BRANCH_W), row(BRANCH_W),
            pl.BlockSpec((None, 1, BRANCH_W), lconst3),
            pl.BlockSpec((None, 1, BRANCH_W), lconst3),
            *[_resident((None, d, BRANCH_W), lambda i, n=n: (layer, 0, N_SEC * SEC // BRANCH_W + n))
              for n in range(N_BRANCH)],
            *[_resident((None, d, d), lambda i, n=n: (layer, 0, (N_SEC * SEC + N_BRANCH * BRANCH_W) // d + n))
              for n in range(N_BRANCH)],
            _resident((None, N_BRANCH, BRANCH_W, d), lambda i: (layer, 0, 0, 0)),
            _resident((None, d, d), lconst3),
        ],
        out_specs=row(d),
        out_shape=jax.ShapeDtypeStruct((t, d), F32),
        compiler_params=_params("parallel"),
        name="merge",
    )(x2d, prm["nw"], mod, mod, mod, na_o, gla_f, gla_b, ret_f, ret_b, prm["gla_gain"],
      prm["ret_gain"], *([prm["wpack"]] * (2 * N_BRANCH)), prm["wb"], prm["wo"])


def _in_splits():
    na = NA_HEADS * NA_HEAD_DIM
    qk = SCAN_HEADS * SCAN_DK
    vv = SCAN_HEADS * SCAN_DV
    sizes = (na, na, na, qk, qk, vv, 2 * GLA_GATE_RANK, qk, qk, vv, N_BRANCH * BRANCH_W,
             N_BRANCH * D_MODEL)
    return [int(v) for v in np.concatenate([[0], np.cumsum(sizes)])]


def _rotary_pack_matrix():
    hk = SCAN_HEADS * SCAN_DK
    quarter = SCAN_DK // 4
    p = np.zeros((SEC, SEC), np.float32)
    for blk in range(SEC // hk):
        for h in range(SCAN_HEADS):
            for half in range(2):
                for partner in range(2):
                    for i in range(quarter):
                        old = h * SCAN_DK + half * 2 * quarter + partner * quarter + i
                        new = partner * (hk // 2) + h * 2 * quarter + half * quarter + i
                        p[blk * hk + old, blk * hk + new] = 1.0
    return jnp.asarray(p).astype(BF16)


def _pack_kernel(a_ref, b_ref, perm_ref, o_ref, *, first_shifted, perm_block, shift):
    j = pl.program_id(1)
    sec = a_ref.shape[0]

    def shifted():
        x = jnp.concatenate([a_ref[...], b_ref[...]], axis=0)
        return x[shift:shift + sec]

    @pl.when(j < first_shifted)
    def _():
        o_ref[...] = a_ref[...].T.astype(BF16)

    @pl.when(j == perm_block)
    def _():
        o_ref[...] = _dot(shifted().T.astype(BF16), perm_ref[...]).astype(BF16)

    @pl.when(jnp.logical_and(j >= first_shifted, j != perm_block))
    def _():
        o_ref[...] = shifted().T.astype(BF16)


def _pack_w_in(w_in):
    depth, d, n_in = w_in.shape
    o = _in_splits()
    g0, g1 = o[6], o[7]
    assert g0 % SEC == 0 and (n_in - (g1 - g0)) % SEC == 0
    n_blocks = (n_in - (g1 - g0)) // SEC
    tail_rows = LANES
    w_t = jnp.swapaxes(w_in, 1, 2)
    return pl.pallas_call(
        functools.partial(_pack_kernel, first_shifted=g0 // SEC, perm_block=SEC_RET_QK,
                          shift=g1 - g0),
        grid=(depth, n_blocks),
        in_specs=[
            pl.BlockSpec((None, SEC, d), lambda l, j: (l, j, 0)),
            pl.BlockSpec((None, tail_rows, d), lambda l, j: (l, (j + 1) * (SEC // tail_rows), 0)),
            _resident((SEC, SEC), lambda l, j: (0, 0)),
        ],
        out_specs=pl.BlockSpec((None, d, SEC), lambda l, j: (l, 0, j)),
        out_shape=jax.ShapeDtypeStruct((depth, d, n_blocks * SEC), BF16),
        compiler_params=_params("parallel", "parallel"),
        name="pack_w_in",
    )(w_t, w_t, _rotary_pack_matrix())


def _prep_params(norm_w, w_in, q_norm, k_norm, w_gate, b_gate, gla_out_norm, ret_out_norm,
                 w_branch, w_out):
    depth, d, _ = w_in.shape
    o = _in_splits()
    g0, g1 = o[6], o[7]
    hk = SCAN_HEADS * SCAN_DK
    wg = jnp.pad(w_in[:, :, g0:g1], ((0, 0), (0, 0), (0, GATE_PAD - (g1 - g0)))).astype(BF16)
    wgate = jnp.zeros((depth, GATE_PAD, 2 * hk), F32)
    wgate = wgate.at[:, :GLA_GATE_RANK, :hk].set(w_gate[:, 0])
    wgate = wgate.at[:, GLA_GATE_RANK:2 * GLA_GATE_RANK, hk:].set(w_gate[:, 1])
    bgate = b_gate.reshape(depth, 1, 2 * hk)
    base = np.ones((N_SEC, 1, SEC), np.float32)
    base[SEC_NA_Q] = NA_HEAD_DIM ** -0.5
    base[SEC_GLA_QK, 0, :hk] = SCAN_DK ** -0.5
    base[SEC_RET_QK, 0, :hk] = SCAN_DK ** -0.5
    reps = SEC // NA_HEAD_DIM
    ones = jnp.ones((depth, N_SEC - 2, 1, SEC), F32)
    gains = jnp.concatenate([jnp.tile(q_norm, (1, reps))[:, None, None, :],
                             jnp.tile(k_norm, (1, reps))[:, None, None, :], ones], axis=1)
    return dict(
        nw=norm_w.reshape(depth, 1, d),
        wpack=_pack_w_in(w_in), wg=wg, wgate=wgate.astype(BF16), bgate=bgate,
        colscale=gains * jnp.asarray(base)[None],
        wb=w_branch.astype(BF16), wo=w_out.astype(BF16),
        gla_gain=gla_out_norm.reshape(depth, 1, BRANCH_W),
        ret_gain=ret_out_norm.reshape(depth, 1, BRANCH_W),
    )


def _tables(seq, ctx_len):
    i = np.arange(TRI_ROWS)
    same = (i[:, None] // CHUNK) == (i[None, :] // CHUNK)
    ltri = jnp.asarray((same & (i[None, :] <= i[:, None])).astype(np.float32)).astype(BF16)
    utri = jnp.asarray((same & (i[None, :] >= i[:, None])).astype(np.float32)).astype(BF16)
    grp = np.arange(MXU_DIM) // NA_HEAD_DIM
    gmat = jnp.asarray((grp[:, None] == grp[None, :]).astype(np.float32)).astype(BF16)
    quarter = SCAN_DK // 4
    n_rows = seq // GRID_W
    inv = ROPE_BASE ** (-jnp.arange(quarter, dtype=F32) / quarter)
    ang_r = jnp.arange(n_rows, dtype=F32)[:, None] * inv[None, :]
    ang_c = jnp.arange(GRID_W, dtype=F32)[:, None] * inv[None, :]
    zr, zc = jnp.zeros_like(ang_r), jnp.zeros_like(ang_c)
    lanes = lambda a, b: jnp.concatenate([a, b] * SCAN_HEADS, axis=1)
    rot_rows = jnp.stack([lanes(jnp.cos(ang_r), zr), lanes(jnp.sin(ang_r), zr)])
    rot_cols = jnp.stack([lanes(zc, jnp.cos(ang_c)), lanes(zc, jnp.sin(ang_c))])
    lat = dict(ltri=ltri, utri=utri, gmat=gmat, rot_rows=rot_rows, rot_cols=rot_cols)
    ctx = dict(ltri=ltri, utri=utri, gmat=gmat, rot_cols=jnp.zeros_like(rot_cols),
               rot_rows=jnp.zeros((2, ctx_len // GRID_W, LANES), F32))
    return lat, ctx


def _tile_plan(seq):
    tm = 2 * MXU_DIM
    tb_scan = min(seq, 16 * CHUNK)
    na_blocks = min(4, seq // (NA_QROWS * GRID_W))
    assert seq % tm == 0 and seq % tb_scan == 0 and (seq // (NA_QROWS * GRID_W)) % na_blocks == 0
    return 2 * tm, tm, tb_scan, na_blocks


def _log_gamma(offset):
    g = jnp.log1p(-jnp.exp2(-(offset + jnp.arange(SCAN_HEADS, dtype=F32))))
    return jnp.tile(jnp.repeat(g, SCAN_DK // 2), 2)


def kernel(x, c, ctx, c_ctx, w_mod, b_mod, norm_w, w_in, na_q_norm, na_k_norm, na_rpb, gla_w_gate,
           gla_b_gate, gla_out_norm, ret_out_norm, w_branch, w_out):
    batch, seq, d = x.shape
    ctx_len = ctx.shape[1]
    depth = w_mod.shape[0]
    rows = seq // GRID_W
    assert d == D_MODEL and x.dtype == F32 and seq % GRID_W == 0 and rows >= NA_KROWS
    assert ctx_len % TRI_ROWS == 0 and batch + 1 <= 8
    tiles = _tile_plan(seq)
    tm_proj, tm_merge, tb_scan, na_blocks_per_step = tiles

    c_rows = jnp.zeros((8, d), F32).at[:batch].set(c).at[batch].set(c_ctx)
    mod = _modulation(c_rows, w_mod, b_mod).reshape(depth, 8, 1, 3 * d)
    prm = _prep_params(norm_w, w_in, na_q_norm, na_k_norm, gla_w_gate, gla_b_gate, gla_out_norm,
                       ret_out_norm, w_branch, w_out)
    lat_tabs, ctx_tabs = _tables(seq, ctx_len)
    bias = _na_bias_tables(na_rpb, rows)
    lg = jnp.stack([_log_gamma(RET_DECAY_FWD), _log_gamma(RET_DECAY_BWD)])
    ctx_row = lambda i: batch

    xl = x.reshape(batch * seq, d)
    xc = ctx.reshape(batch * ctx_len, d)
    for layer in range(depth):
        with_ctx = layer < depth - 1
        p_lat, cum_lat = _in_proj(xl, mod, prm, lat_tabs, layer=layer, tm=tm_proj,
                                  row_of=lambda i: i // (seq // tm_proj),
                                  blocks_per_batch=seq // tm_proj, rope=True)
        p_ctx, cum_ctx = _in_proj(xc, mod, prm, ctx_tabs, layer=layer, tm=ctx_len, row_of=ctx_row,
                                  blocks_per_batch=1, rope=False)

        na_l = _na_attention(p_lat, p_ctx, bias, layer=layer, batch=batch, seq=seq, ctx_len=ctx_len,
                             n_sub=na_blocks_per_step)

        (gla_f, gla_b), (gla_cf, gla_cb) = _scan(p_lat, p_ctx, cum_lat, cum_ctx, None, gla=True,
                                                 batch=batch, seq=seq, ctx_len=ctx_len, tb=tb_scan)
        (ret_f, ret_b), (ret_cf, ret_cb) = _scan(p_lat, p_ctx, None, None, lg, gla=False,
                                                 batch=batch, seq=seq, ctx_len=ctx_len, tb=tb_scan)

        xl_new = _merge(xl, mod, prm, na_l, gla_f, gla_b, ret_f, ret_b, layer=layer, tm=tm_merge,
                        row_of=lambda i: i // (seq // tm_merge))
        if with_ctx:
            na_c = _ctx_attention(p_ctx, batch=batch, ctx_len=ctx_len)
            xc = _merge(xc, mod, prm, na_c, gla_cf, gla_cb, ret_cf, ret_cb, layer=layer, tm=ctx_len,
                        row_of=ctx_row)
        xl = xl_new
    return xl.reshape(batch, seq, d)
```

```python
import functools

import numpy as np
import jax
import jax.numpy as jnp
from jax import lax
from jax.experimental import pallas as pl
from jax.experimental.pallas import tpu as pltpu

D_MODEL = 1024
GRID_W = 64
NORM_EPS = 1e-6
N_BRANCH = 3
BRANCH_W = D_MODEL // 2
NA_HEAD_DIM = 64
NA_HEADS = BRANCH_W // NA_HEAD_DIM
NA_WIN_ROWS = 8
NA_WIN_COLS = 16
SCAN_HEADS = 4
SCAN_DV = BRANCH_W // SCAN_HEADS
SCAN_DK = SCAN_DV // 2
GLA_GATE_RANK = 16
GLA_GATE_TAU = 16.0
RET_DECAY_FWD = 5.0
RET_DECAY_BWD = 5.5
CHUNK = 64
ROPE_BASE = 10000.0

LANES = 128
MXU_DIM = 256

SEC = 512
SEC_NA_Q, SEC_NA_K, SEC_NA_V, SEC_GLA_QK, SEC_GLA_V, SEC_RET_QK, SEC_RET_V = range(7)
N_SEC = 7
GATE_PAD = LANES
TRI_ROWS = MXU_DIM
NA_QROWS = 4
NA_KROWS = 12
NEG_BIG = -1e30

VMEM_LIMIT = 56 * 1024 * 1024

F32 = jnp.float32
BF16 = jnp.bfloat16


def _dot(a, b):
    return jnp.dot(a, b, preferred_element_type=F32)


def _dot_nt(a, b):
    return lax.dot_general(a, b, (((1,), (1,)), ((), ())), preferred_element_type=F32)


def _dot_tn(a, b):
    return lax.dot_general(a, b, (((0,), (0,)), ((), ())), preferred_element_type=F32)


def _split2(x):
    hi = x.astype(BF16)
    lo = (x - hi.astype(F32)).astype(BF16)
    return hi, lo


def _log_sigmoid(x):
    return jnp.minimum(x, 0.0) - jnp.log(1.0 + jnp.exp(-jnp.abs(x)))


def _sigmoid(x):
    return 1.0 / (1.0 + jnp.exp(-x))


def _silu(x):
    return x * _sigmoid(x)


def _params(*sem):
    return pltpu.CompilerParams(dimension_semantics=sem, vmem_limit_bytes=VMEM_LIMIT)


def _resident(block_shape, index_map):
    return pl.BlockSpec(block_shape, index_map, pipeline_mode=pl.Buffered(1))


def _ada_norm(x, nw, sc, sh):
    ms = jnp.mean(x * x, axis=-1, keepdims=True)
    return (x * lax.rsqrt(ms + NORM_EPS)) * (nw * (1.0 + sc)) + sh


def _mod_kernel(c_ref, w_ref, b_ref, o_ref):
    a = _silu(c_ref[...])
    o_ref[...] = jnp.dot(a, w_ref[...], preferred_element_type=F32,
                         precision=lax.Precision.HIGHEST) + b_ref[...]


def _modulation(c_rows, w_mod, b_mod):
    depth, d, d3 = w_mod.shape
    rows = c_rows.shape[0]
    tn = d
    return pl.pallas_call(
        _mod_kernel,
        grid=(depth, d3 // tn),
        in_specs=[
            pl.BlockSpec((rows, d), lambda l, j: (0, 0)),
            pl.BlockSpec((None, d, tn), lambda l, j: (l, 0, j)),
            pl.BlockSpec((None, 1, tn), lambda l, j: (l, 0, j)),
        ],
        out_specs=pl.BlockSpec((None, rows, tn), lambda l, j: (l, 0, j)),
        out_shape=jax.ShapeDtypeStruct((depth, rows, d3), F32),
        compiler_params=_params("parallel", "parallel"),
        name="modulation",
    )(c_rows, w_mod, b_mod.reshape(depth, 1, d3))


def _mod_specs(mod, layer, row_of):
    d = mod.shape[-1] // 3
    spec = lambda third: pl.BlockSpec((None, None, 1, d), lambda i: (layer, row_of(i), 0, third))
    return spec(0), spec(1), spec(2)


def _in_proj_kernel(x_ref, nw_ref, sh_ref, sc_ref, w_ref, cs_ref, gm_ref, wg_ref, wgate_ref,
                    bgate_ref, ltri_ref, utri_ref, rt_ref, ct_ref, p_ref, cum_ref, *, rope):
    tm = x_ref.shape[0]
    hb = _ada_norm(x_ref[...], nw_ref[...], sc_ref[...], sh_ref[...]).astype(BF16)

    g = _dot(hb, wg_ref[...])
    normed = {}
    for sec in range(N_SEC):
        base = sec * SEC
        raw = _dot(hb, w_ref[:, base:base + SEC])
        acc = raw * cs_ref[sec]
        if sec <= SEC_NA_K:
            normed[sec] = (raw, acc)
        elif rope and sec == SEC_RET_QK:
            n_grid_rows = tm // GRID_W

            def table(t):
                by_row = jnp.concatenate(
                    [jnp.broadcast_to(rt_ref[t, g:g + 1, :], (GRID_W, LANES)) for g in range(n_grid_rows)],
                    axis=0)
                return by_row + jnp.concatenate([ct_ref[t]] * n_grid_rows, axis=0)

            cos = table(0)
            sin = table(1)
            for c in range(0, SEC, 2 * LANES):
                u1 = acc[:, c:c + LANES]
                u2 = acc[:, c + LANES:c + 2 * LANES]
                p_ref[:, base + c:base + c + LANES] = (u1 * cos - u2 * sin).astype(BF16)
                p_ref[:, base + c + LANES:base + c + 2 * LANES] = (u1 * sin + u2 * cos).astype(BF16)
        else:
            p_ref[:, base:base + SEC] = acc.astype(BF16)
        if sec == SEC_NA_V:
            xg = _dot(g.astype(BF16), wgate_ref[...]) + bgate_ref[...]
            la = _log_sigmoid(xg) * (1.0 / GLA_GATE_TAU)

    for sec, (raw, acc) in normed.items():
        base = sec * SEC
        for c in range(0, SEC, MXU_DIM):
            sq = raw[:, c:c + MXU_DIM] * raw[:, c:c + MXU_DIM]
            ss = _dot(sq.astype(BF16), gm_ref[...])
            p_ref[:, base + c:base + c + MXU_DIM] = (
                acc[:, c:c + MXU_DIM] * lax.rsqrt(ss * (1.0 / NA_HEAD_DIM) + NORM_EPS)).astype(BF16)

    half = la.shape[1] // 2
    for r in range(tm // TRI_ROWS):
        rows = slice(r * TRI_ROWS, (r + 1) * TRI_ROWS)
        fh, fl = _split2(la[rows, :half])
        bh, bl = _split2(la[rows, half:])
        cum_ref[rows, :half] = _dot(ltri_ref[...], fh) + _dot(ltri_ref[...], fl)
        cum_ref[rows, half:] = _dot(utri_ref[...], bh) + _dot(utri_ref[...], bl)


def _in_proj(x2d, mod, prm, tabs, *, layer, tm, row_of, blocks_per_batch, rope):
    t, d = x2d.shape
    const2 = lambda i: (0, 0)
    lconst = lambda i: (layer, 0, 0)
    row_tab = pl.BlockSpec((2, tm // GRID_W, LANES), lambda i: (0, i % blocks_per_batch, 0))
    col_tab = _resident((2, GRID_W, LANES), lambda i: (0, 0, 0))
    sh_spec, sc_spec, _ = _mod_specs(mod, layer, row_of)
    return pl.pallas_call(
        functools.partial(_in_proj_kernel, rope=rope),
        grid=(t // tm,),
        in_specs=[
            pl.BlockSpec((tm, d), lambda i: (i, 0)),
            pl.BlockSpec((None, 1, d), lconst),
            sh_spec, sc_spec,
            _resident((None, d, N_SEC * SEC), lconst),
            _resident((None, N_SEC, 1, SEC), lambda i: (layer, 0, 0, 0)),
            _resident((MXU_DIM, MXU_DIM), const2),
            _resident((None, d, GATE_PAD), lconst),
            _resident((None, GATE_PAD, SEC), lconst),
            _resident((None, 1, SEC), lconst),
            _resident((TRI_ROWS, TRI_ROWS), const2),
            _resident((TRI_ROWS, TRI_ROWS), const2),
            row_tab, col_tab,
        ],
        out_specs=[
            pl.BlockSpec((tm, N_SEC * SEC), lambda i: (i, 0)),
            pl.BlockSpec((tm, SEC), lambda i: (i, 0)),
        ],
        out_shape=[
            jax.ShapeDtypeStruct((t, N_SEC * SEC), BF16),
            jax.ShapeDtypeStruct((t, SEC), F32),
        ],
        compiler_params=_params("parallel"),
        name="in_proj_rope" if rope else "in_proj",
    )(x2d, prm["nw"], mod, mod, prm["wpack"], prm["colscale"], tabs["gmat"], prm["wg"],
      prm["wgate"], prm["bgate"], tabs["ltri"], tabs["utri"], tabs["rot_rows"], tabs["rot_cols"])


def _na_block_plan(rows):
    n_qb = rows // NA_QROWS
    kr = min(NA_WIN_ROWS, rows)
    plans = []
    for qb in (0, min(1, n_qb - 1), n_qb - 1):
        t0 = int(np.clip(qb - 1, 0, n_qb - NA_KROWS // NA_QROWS))
        plan = []
        for i in range(NA_QROWS):
            r = qb * NA_QROWS + i
            rs = int(np.clip(r - kr // 2, 0, rows - kr))
            plan.append([(t0 * NA_QROWS + j) - r + NA_WIN_ROWS - 1
                         if rs <= t0 * NA_QROWS + j < rs + kr else None
                         for j in range(NA_KROWS)])
        plans.append(plan)
    return plans


def _bias_kernel(rpb_ref, o_ref, *, plans):
    shape = (GRID_W, LANES)
    lane = lax.broadcasted_iota(jnp.int32, shape, 1)
    c = lax.broadcasted_iota(jnp.int32, shape, 0)
    kc = lane & (GRID_W - 1)
    cs = jnp.clip(c - NA_WIN_COLS // 2, 0, GRID_W - NA_WIN_COLS)
    col_ok = jnp.logical_and(kc >= cs, kc < cs + NA_WIN_COLS)
    low = lane < GRID_W
    neg = jnp.full(shape, NEG_BIG, F32)
    toep = []
    for a in range(2 * NA_WIN_ROWS - 1):
        x = jnp.broadcast_to(rpb_ref[a:a + 1, :], shape)
        lo = pltpu.roll(x, LANES - (NA_WIN_COLS - 1), 1, stride=1, stride_axis=0)
        hi = pltpu.roll(x, GRID_W - (NA_WIN_COLS - 1), 1, stride=1, stride_axis=0)
        toep.append(jnp.where(col_ok, jnp.where(low, lo, hi), neg))
    for k, plan in enumerate(plans):
        for i in range(NA_QROWS):
            for jp in range(NA_KROWS // 2):
                a0, a1 = plan[i][2 * jp], plan[i][2 * jp + 1]
                b0 = neg if a0 is None else toep[a0]
                b1 = neg if a1 is None else toep[a1]
                o_ref[k, i * GRID_W:(i + 1) * GRID_W, jp * LANES:(jp + 1) * LANES] = (
                    jnp.where(low, b0, b1))


def _na_bias_tables(rpb, rows):
    depth, heads, nr, nc = rpb.shape
    rp = jnp.pad(rpb, ((0, 0), (0, 0), (0, 16 - nr), (0, LANES - nc)))
    tq, tk = NA_QROWS * GRID_W, NA_KROWS * GRID_W
    return pl.pallas_call(
        functools.partial(_bias_kernel, plans=_na_block_plan(rows)),
        grid=(depth, heads),
        in_specs=[pl.BlockSpec((None, None, 16, LANES), lambda l, h: (l, h, 0, 0))],
        out_specs=pl.BlockSpec((None, 3, None, tq, tk), lambda l, h: (l, 0, h, 0, 0)),
        out_shape=jax.ShapeDtypeStruct((depth, 3, heads, tq, tk), F32),
        compiler_params=_params("parallel", "parallel"),
        name="na_bias",
    )(rp)


def _lane_mask(width, lo, hi):
    lane = lax.broadcasted_iota(jnp.int32, (1, width), 1)
    return jnp.logical_and(lane >= lo, lane < hi)


def _na_kernel(q_ref, k_ref, v_ref, kc_ref, vc_ref, bias_ref, o_ref, *, n_qb, n_sub):
    tq = q_ref.shape[0] // n_sub
    tk = bias_ref.shape[-1]
    width = q_ref.shape[1]
    heads = width // NA_HEAD_DIM
    kc = kc_ref[...]
    vc = vc_ref[...]
    inside = [_lane_mask(width, h * NA_HEAD_DIM, (h + 1) * NA_HEAD_DIM) for h in range(heads)]
    ones_row = [_lane_mask(width, ((h + 1) % heads) * NA_HEAD_DIM, ((h + 1) % heads) * NA_HEAD_DIM + 1)
                for h in range(heads)]
    starts = []
    for sub in range(n_sub):
        qb = pl.program_id(2) * n_sub + sub
        t0 = jnp.clip(qb - 1, 0, n_qb - tk // tq)
        starts.append((qb, pl.multiple_of(t0 * tq, tq)))
    units = [(sub, h) for sub in range(n_sub) for h in range(heads)]

    def scores(sub, h):
        qb, start = starts[sub]
        kind = jnp.where(qb == 0, 0, jnp.where(qb == n_qb - 1, 2, 1))
        qh = q_ref[sub * tq:(sub + 1) * tq, :] * inside[h].astype(BF16)
        return _dot_nt(qh, k_ref[pl.ds(start, tk), :]) + bias_ref[kind, h], _dot_nt(qh, kc)

    pending = scores(*units[0])
    acc = None
    for idx, (sub, h) in enumerate(units):
        s, sc = pending
        if idx + 1 < len(units):
            pending = scores(*units[idx + 1])
        keep = inside[h].astype(BF16)
        fill = ones_row[h].astype(BF16)
        m = jnp.maximum(jnp.max(s, axis=-1, keepdims=True), jnp.max(sc, axis=-1, keepdims=True))
        p = jnp.exp((s - m).astype(BF16))
        pc = jnp.exp((sc - m).astype(BF16))
        v3 = v_ref[pl.ds(starts[sub][1], tk), :]
        oh = _dot(p, v3 * keep + fill) + _dot(pc, vc * keep + fill)
        l = jnp.sum(jnp.where(ones_row[h], oh, 0.0), axis=-1, keepdims=True)
        acc = oh * (1.0 / l) if h == 0 else jnp.where(inside[h], oh * (1.0 / l), acc)
        if h == heads - 1:
            o_ref[sub * tq:(sub + 1) * tq, :] = acc.astype(BF16)


def _na_attention(p_lat, p_ctx, bias, *, layer, batch, seq, ctx_len, n_sub):
    tq = NA_QROWS * GRID_W
    tk = NA_KROWS * GRID_W
    n_qb = seq // tq
    n_steps = n_qb // n_sub
    hw = 2 * LANES
    n_hg = BRANCH_W // hw
    sec_blk = SEC // hw
    heads = hw // NA_HEAD_DIM
    return pl.pallas_call(
        functools.partial(_na_kernel, n_qb=n_qb, n_sub=n_sub),
        grid=(batch, n_hg, n_steps),
        in_specs=[
            pl.BlockSpec((n_sub * tq, hw), lambda b, g, i: (b * n_steps + i, SEC_NA_Q * sec_blk + g)),
            pl.BlockSpec((seq, hw), lambda b, g, i: (b, SEC_NA_K * sec_blk + g)),
            pl.BlockSpec((seq, hw), lambda b, g, i: (b, SEC_NA_V * sec_blk + g)),
            pl.BlockSpec((ctx_len, hw), lambda b, g, i: (b, SEC_NA_K * sec_blk + g)),
            pl.BlockSpec((ctx_len, hw), lambda b, g, i: (b, SEC_NA_V * sec_blk + g)),
            pl.BlockSpec((None, bias.shape[1], heads, tq, tk), lambda b, g, i: (layer, 0, g, 0, 0)),
        ],
        out_specs=pl.BlockSpec((n_sub * tq, hw), lambda b, g, i: (b * n_steps + i, g)),
        out_shape=jax.ShapeDtypeStruct((batch * seq, BRANCH_W), BF16),
        compiler_params=_params("parallel", "parallel", "arbitrary"),
        name="na_attention",
    )(p_lat, p_lat, p_lat, p_ctx, p_ctx, bias)


def _ctx_attn_kernel(q_ref, k_ref, v_ref, o_ref):
    first = _lane_mask(LANES, 0, NA_HEAD_DIM)
    sel = (first.astype(BF16), jnp.logical_not(first).astype(BF16))
    outs = []
    for pair in range(q_ref.shape[1] // LANES):
        lanes = slice(pair * LANES, (pair + 1) * LANES)
        q2, k2, v2 = q_ref[:, lanes], k_ref[:, lanes], v_ref[:, lanes]
        o_pair = []
        for hh in range(2):
            s = _dot_nt(q2 * sel[hh], k2)
            m = jnp.max(s, axis=-1, keepdims=True)
            p = jnp.exp(s - m)
            l = jnp.sum(p, axis=-1, keepdims=True)
            o_pair.append(_dot(p.astype(BF16), v2) * (1.0 / l))
        outs.append(jnp.where(first, o_pair[0], o_pair[1]))
    o_ref[...] = jnp.concatenate(outs, axis=1).astype(BF16)


def _ctx_attention(p_ctx, *, batch, ctx_len):
    hw = 2 * LANES
    n_hg = BRANCH_W // hw
    sec_blk = SEC // hw
    spec = lambda sec: pl.BlockSpec((ctx_len, hw), lambda b, g: (b, sec * sec_blk + g))
    return pl.pallas_call(
        _ctx_attn_kernel,
        grid=(batch, n_hg),
        in_specs=[spec(SEC_NA_Q), spec(SEC_NA_K), spec(SEC_NA_V)],
        out_specs=pl.BlockSpec((ctx_len, hw), lambda b, g: (b, g)),
        out_shape=jax.ShapeDtypeStruct((batch * ctx_len, BRANCH_W), BF16),
        compiler_params=_params("parallel", "parallel"),
        name="ctx_attention",
    )(p_ctx, p_ctx, p_ctx)


def _scan_head_of_lane(gla):
    lane = lax.broadcasted_iota(jnp.int32, (1, SCAN_HEADS * SCAN_DK), 1)
    if gla:
        return lax.shift_right_logical(lane, SCAN_DK.bit_length() - 1)
    return lax.shift_right_logical(lane & (LANES - 1), (SCAN_DK // 2).bit_length() - 1)


def _decay_factors(cum, ref_row, total_row):
    cm = cum[ref_row:ref_row + 1, :]
    tot = cum[total_row:total_row + 1, :]
    eq = jnp.exp(cum - cm)
    ek = jnp.exp(cm - cum)
    return eq, ek, jnp.exp(cm), jnp.exp(tot - cm), jnp.exp(tot)


def _chunk_prepare(q, k, v, factors, head_sel):
    eq, ek, _, e_tm, _ = factors
    qtb = (q * eq).astype(BF16)
    kt = k * ek
    khb = (kt * e_tm).astype(BF16)
    hmul = [sel.astype(BF16) for sel in head_sel]
    q_stack = jnp.concatenate([qtb * m for m in hmul], axis=0)
    v_rows = jnp.concatenate([v[:, h * SCAN_DV:(h + 1) * SCAN_DV] for h in range(SCAN_HEADS)], axis=0)
    k_rows = jnp.concatenate([khb * m for m in hmul], axis=0)
    return q_stack, kt.astype(BF16), _dot_tn(v_rows, k_rows)


def _chunk_scores(q_stack, ktb, st, e_m, tri_stack):
    rhs = jnp.concatenate([(st * e_m).astype(BF16), ktb], axis=0)
    both = _dot_nt(q_stack, rhs)
    return both[:, :SCAN_DV], jnp.where(tri_stack, both[:, SCAN_DV:], 0.0).astype(BF16)


def _chunk_output(inter, att, v):
    outs = []
    for h in range(SCAN_HEADS):
        rows = slice(h * CHUNK, (h + 1) * CHUNK)
        outs.append(_dot(att[rows], v[:, h * SCAN_DV:(h + 1) * SCAN_DV]) + inter[rows])
    return jnp.concatenate(outs, axis=1)


def _scan_masks(gla):
    head = _scan_head_of_lane(gla)
    head_sel = [head == h for h in range(SCAN_HEADS)]
    t = lax.broadcasted_iota(jnp.int32, (SCAN_HEADS * CHUNK, CHUNK), 0) & (CHUNK - 1)
    s = lax.broadcasted_iota(jnp.int32, (SCAN_HEADS * CHUNK, CHUNK), 1)
    return head_sel, s <= t, s >= t


def _ret_cum(lg_row, forward):
    t = lax.broadcasted_iota(jnp.int32, (CHUNK, lg_row.shape[1]), 0).astype(F32)
    steps = (t + 1.0) if forward else (float(CHUNK) - t)
    return steps * lg_row


def _scan_block(fwd_refs, bwd_refs, n_chunks, gla, ret_factors, masks):
    head_sel, tril, triu = masks
    hk = SCAN_HEADS * SCAN_DK
    chains = (
        (*fwd_refs, tril, (CHUNK // 2 - 1, CHUNK - 1), 0,
         [slice(c * CHUNK, (c + 1) * CHUNK) for c in range(n_chunks)]),
        (*bwd_refs, triu, (CHUNK // 2, 0), 1,
         [slice(c * CHUNK, (c + 1) * CHUNK) for c in reversed(range(n_chunks))]),
    )
    prepared = []
    for qk_ref, v_ref, cum_ref, _, _, _, (ref_row, total_row), direction, order in chains:
        items = []
        for rows in order:
            fac = (_decay_factors(cum_ref[rows, :], ref_row, total_row) if gla
                   else ret_factors[direction])
            qk = qk_ref[rows, :].astype(F32)
            items.append((fac, _chunk_prepare(qk[:, :hk], qk[:, hk:], v_ref[rows, :], fac, head_sel)))
        prepared.append(items)
    scored = []
    for (_, _, _, _, st_ref, tri, _, _, _), items in zip(chains, prepared):
        st = st_ref[...]
        states = []
        for fac, (_, _, inc) in items:
            states.append(st)
            st = st * fac[4] + inc
        st_ref[...] = st
        scored.append([_chunk_scores(q_stack, ktb, st_c, fac[2], tri)
                       for st_c, (fac, (q_stack, ktb, _)) in zip(states, items)])
    for (_, v_ref, _, o_ref, _, _, _, _, order), results in zip(chains, scored):
        for rows, (inter, att) in zip(order, results):
            o_ref[rows, :] = _chunk_output(inter, att, v_ref[rows, :]).astype(BF16)


def _scan_kernel(*refs, gla):
    if gla:
        (qkf_ref, vf_ref, cf_ref, qkb_ref, vb_ref, cb_ref, qkc_ref, vc_ref, ccf_ref, ccb_ref,
         of_ref, ob_ref, ocf_ref, ocb_ref, stf, stb) = refs
        ret_factors = None
    else:
        (qkf_ref, vf_ref, qkb_ref, vb_ref, qkc_ref, vc_ref, lg_ref,
         of_ref, ob_ref, ocf_ref, ocb_ref, stf, stb) = refs
        cf_ref = cb_ref = ccf_ref = ccb_ref = None
        ret_factors = (_decay_factors(_ret_cum(lg_ref[0:1, :], True), CHUNK // 2 - 1, CHUNK - 1),
                       _decay_factors(_ret_cum(lg_ref[1:2, :], False), CHUNK // 2, 0))
    masks = _scan_masks(gla)

    @pl.when(pl.program_id(1) == 0)
    def _():
        stf[...] = jnp.zeros_like(stf)
        stb[...] = jnp.zeros_like(stb)
        _scan_block((qkc_ref, vc_ref, ccf_ref, ocf_ref, stf), (qkc_ref, vc_ref, ccb_ref, ocb_ref, stb),
                    qkc_ref.shape[0] // CHUNK, gla, ret_factors, masks)

    _scan_block((qkf_ref, vf_ref, cf_ref, of_ref, stf), (qkb_ref, vb_ref, cb_ref, ob_ref, stb),
                qkf_ref.shape[0] // CHUNK, gla, ret_factors, masks)


def _scan(p_lat, p_ctx, cum_lat, cum_ctx, lg, *, gla, batch, seq, ctx_len, tb):
    nb = seq // tb
    hk = SCAN_HEADS * SCAN_DK
    hv = SCAN_HEADS * SCAN_DV
    sec_qk = SEC_GLA_QK if gla else SEC_RET_QK
    sec_v = SEC_GLA_V if gla else SEC_RET_V
    fwd = lambda b, s: b * nb + s
    bwd = lambda b, s: b * nb + nb - 1 - s

    def stream(order):
        specs = [pl.BlockSpec((tb, SEC), lambda b, s: (order(b, s), sec_qk)),
                 pl.BlockSpec((tb, SEC), lambda b, s: (order(b, s), sec_v))]
        return specs, [p_lat, p_lat]

    in_specs, args = stream(fwd)
    if gla:
        in_specs.append(pl.BlockSpec((tb, hk), lambda b, s: (fwd(b, s), 0)))
        args.append(cum_lat)
    specs_b, args_b = stream(bwd)
    in_specs += specs_b
    args += args_b
    if gla:
        in_specs.append(pl.BlockSpec((tb, hk), lambda b, s: (bwd(b, s), 1)))
        args.append(cum_lat)
    in_specs += [pl.BlockSpec((ctx_len, SEC), lambda b, s: (b, sec_qk)),
                 pl.BlockSpec((ctx_len, SEC), lambda b, s: (b, sec_v))]
    args += [p_ctx, p_ctx]
    if gla:
        in_specs += [pl.BlockSpec((ctx_len, hk), lambda b, s: (b, 0)),
                     pl.BlockSpec((ctx_len, hk), lambda b, s: (b, 1))]
        args += [cum_ctx, cum_ctx]
    else:
        in_specs.append(pl.BlockSpec((2, hk), lambda b, s: (0, 0)))
        args.append(lg)
    ctx_out = pl.BlockSpec((ctx_len, hv), lambda b, s: (b, 0))
    o_f, o_b, oc_f, oc_b = pl.pallas_call(
        functools.partial(_scan_kernel, gla=gla),
        grid=(batch, nb),
        in_specs=in_specs,
        out_specs=[
            pl.BlockSpec((tb, hv), lambda b, s: (fwd(b, s), 0)),
            pl.BlockSpec((tb, hv), lambda b, s: (bwd(b, s), 0)),
            ctx_out, ctx_out,
        ],
        out_shape=[
            jax.ShapeDtypeStruct((batch * seq, hv), BF16),
            jax.ShapeDtypeStruct((batch * seq, hv), BF16),
            jax.ShapeDtypeStruct((batch * ctx_len, hv), BF16),
            jax.ShapeDtypeStruct((batch * ctx_len, hv), BF16),
        ],
        scratch_shapes=[pltpu.VMEM((SCAN_DV, hk), F32), pltpu.VMEM((SCAN_DV, hk), F32)],
        compiler_params=_params("parallel", "arbitrary"),
        name="gla_scan" if gla else "ret_scan",
    )(*args)
    return (o_f, o_b), (oc_f, oc_b)


def _head_rms(o, gain):
    parts = []
    for h in range(SCAN_HEADS):
        sl = o[:, h * SCAN_DV:(h + 1) * SCAN_DV]
        ms = jnp.mean(sl * sl, axis=-1, keepdims=True)
        parts.append(sl * lax.rsqrt(ms + NORM_EPS))
    return jnp.concatenate(parts, axis=1) * gain


def _merge_kernel(x_ref, nw_ref, sh_ref, sc_ref, gate_ref, na_ref, gf_ref, gb_ref, rf_ref, rb_ref,
                  gg_ref, rg_ref, wz0_ref, wz1_ref, wz2_ref, wg0_ref, wg1_ref, wg2_ref, wb_ref, wo_ref,
                  o_ref):
    x = x_ref[...]
    wz_refs = (wz0_ref, wz1_ref, wz2_ref)
    wg_refs = (wg0_ref, wg1_ref, wg2_ref)
    hb = _ada_norm(x, nw_ref[...], sc_ref[...], sh_ref[...]).astype(BF16)
    outs = (
        na_ref[...].astype(F32),
        _head_rms(gf_ref[...].astype(F32) + gb_ref[...].astype(F32), gg_ref[...]),
        _head_rms(rf_ref[...].astype(F32) + rb_ref[...].astype(F32), rg_ref[...]),
    )
    zs = [_dot(hb, wz_refs[i][...]) for i in range(N_BRANCH)]
    us = [(outs[i] * _silu(zs[i])).astype(BF16) for i in range(N_BRANCH)]
    gs = [_sigmoid(_dot(hb, wg_refs[i][...])) for i in range(N_BRANCH)]
    y = None
    for i in range(N_BRANCH):
        t = gs[i] * _dot(us[i], wb_ref[i])
        y = t if y is None else y + t
    r = _dot(y.astype(BF16), wo_ref[...])
    o_ref[...] = x + gate_ref[...] * r


def _merge(x2d, mod, prm, na_o, gla_f, gla_b, ret_f, ret_b, *, layer, tm, row_of):
    t, d = x2d.shape
    row = lambda w: pl.BlockSpec((tm, w), lambda i: (i, 0))
    sh_spec, sc_spec, gate_spec = _mod_specs(mod, layer, row_of)
    lconst3 = lambda i: (layer, 0, 0)
    return pl.pallas_call(
        _merge_kernel,
        grid=(t // tm,),
        in_specs=[
            row(d),
            pl.BlockSpec((None, 1, d), lconst3),
            sh_spec, sc_spec, gate_spec,
            row(BRANCH_W), row(BRANCH_W), row(BRANCH_W), row(BRANCH_W), row(BRANCH_W),
            pl.BlockSpec((None, 1, BRANCH_W), lconst3),
            pl.BlockSpec((None, 1, BRANCH_W), lconst3),
            *[_resident((None, d, BRANCH_W), lambda i, n=n: (layer, 0, N_SEC * SEC // BRANCH_W + n))
              for n in range(N_BRANCH)],
            *[_resident((None, d, d), lambda i, n=n: (layer, 0, (N_SEC * SEC + N_BRANCH * BRANCH_W) // d + n))
              for n in range(N_BRANCH)],
            _resident((None, N_BRANCH, BRANCH_W, d), lambda i: (layer, 0, 0, 0)),
            _resident((None, d, d), lconst3),
        ],
        out_specs=row(d),
        out_shape=jax.ShapeDtypeStruct((t, d), F32),
        compiler_params=_params("parallel"),
        name="merge",
    )(x2d, prm["nw"], mod, mod, mod, na_o, gla_f, gla_b, ret_f, ret_b, prm["gla_gain"],
      prm["ret_gain"], *([prm["wpack"]] * (2 * N_BRANCH)), prm["wb"], prm["wo"])


def _in_splits():
    na = NA_HEADS * NA_HEAD_DIM
    qk = SCAN_HEADS * SCAN_DK
    vv = SCAN_HEADS * SCAN_DV
    sizes = (na, na, na, qk, qk, vv, 2 * GLA_GATE_RANK, qk, qk, vv, N_BRANCH * BRANCH_W,
             N_BRANCH * D_MODEL)
    return [int(v) for v in np.concatenate([[0], np.cumsum(sizes)])]


def _rotary_pack_matrix():
    hk = SCAN_HEADS * SCAN_DK
    quarter = SCAN_DK // 4
    p = np.zeros((SEC, SEC), np.float32)
    for blk in range(SEC // hk):
        for h in range(SCAN_HEADS):
            for half in range(2):
                for partner in range(2):
                    for i in range(quarter):
                        old = h * SCAN_DK + half * 2 * quarter + partner * quarter + i
                        new = partner * (hk // 2) + h * 2 * quarter + half * quarter + i
                        p[blk * hk + old, blk * hk + new] = 1.0
    return jnp.asarray(p).astype(BF16)


def _pack_kernel(a_ref, b_ref, perm_ref, o_ref, *, first_shifted, perm_block, shift):
    j = pl.program_id(1)
    sec = a_ref.shape[0]

    def shifted():
        x = jnp.concatenate([a_ref[...], b_ref[...]], axis=0)
        return x[shift:shift + sec]

    @pl.when(j < first_shifted)
    def _():
        o_ref[...] = a_ref[...].T.astype(BF16)

    @pl.when(j == perm_block)
    def _():
        o_ref[...] = _dot(shifted().T.astype(BF16), perm_ref[...]).astype(BF16)

    @pl.when(jnp.logical_and(j >= first_shifted, j != perm_block))
    def _():
        o_ref[...] = shifted().T.astype(BF16)


def _pack_w_in(w_in):
    depth, d, n_in = w_in.shape
    o = _in_splits()
    g0, g1 = o[6], o[7]
    assert g0 % SEC == 0 and (n_in - (g1 - g0)) % SEC == 0
    n_blocks = (n_in - (g1 - g0)) // SEC
    tail_rows = g1 - g0
    w_t = jnp.swapaxes(w_in, 1, 2)
    return pl.pallas_call(
        functools.partial(_pack_kernel, first_shifted=g0 // SEC, perm_block=SEC_RET_QK,
                          shift=g1 - g0),
        grid=(depth, n_blocks),
        in_specs=[
            pl.BlockSpec((None, SEC, d), lambda l, j: (l, j, 0)),
            pl.BlockSpec((None, tail_rows, d), lambda l, j: (l, (j + 1) * (SEC // tail_rows), 0)),
            _resident((SEC, SEC), lambda l, j: (0, 0)),
        ],
        out_specs=pl.BlockSpec((None, d, SEC), lambda l, j: (l, 0, j)),
        out_shape=jax.ShapeDtypeStruct((depth, d, n_blocks * SEC), BF16),
        compiler_params=_params("parallel", "parallel"),
        name="pack_w_in",
    )(w_t, w_t, _rotary_pack_matrix())


def _prep_params(norm_w, w_in, q_norm, k_norm, w_gate, b_gate, gla_out_norm, ret_out_norm,
                 w_branch, w_out):
    depth, d, _ = w_in.shape
    o = _in_splits()
    g0, g1 = o[6], o[7]
    hk = SCAN_HEADS * SCAN_DK
    wg = jnp.pad(w_in[:, :, g0:g1], ((0, 0), (0, 0), (0, GATE_PAD - (g1 - g0)))).astype(BF16)
    wgate = jnp.zeros((depth, GATE_PAD, 2 * hk), F32)
    wgate = wgate.at[:, :GLA_GATE_RANK, :hk].set(w_gate[:, 0])
    wgate = wgate.at[:, GLA_GATE_RANK:2 * GLA_GATE_RANK, hk:].set(w_gate[:, 1])
    bgate = b_gate.reshape(depth, 1, 2 * hk)
    base = np.ones((N_SEC, 1, SEC), np.float32)
    base[SEC_NA_Q] = NA_HEAD_DIM ** -0.5
    base[SEC_GLA_QK, 0, :hk] = SCAN_DK ** -0.5
    base[SEC_RET_QK, 0, :hk] = SCAN_DK ** -0.5
    reps = SEC // NA_HEAD_DIM
    ones = jnp.ones((depth, N_SEC - 2, 1, SEC), F32)
    gains = jnp.concatenate([jnp.tile(q_norm, (1, reps))[:, None, None, :],
                             jnp.tile(k_norm, (1, reps))[:, None, None, :], ones], axis=1)
    return dict(
        nw=norm_w.reshape(depth, 1, d),
        wpack=_pack_w_in(w_in), wg=wg, wgate=wgate.astype(BF16), bgate=bgate,
        colscale=gains * jnp.asarray(base)[None],
        wb=w_branch.astype(BF16), wo=w_out.astype(BF16),
        gla_gain=gla_out_norm.reshape(depth, 1, BRANCH_W),
        ret_gain=ret_out_norm.reshape(depth, 1, BRANCH_W),
    )


def _tables(seq, ctx_len):
    i = np.arange(TRI_ROWS)
    same = (i[:, None] // CHUNK) == (i[None, :] // CHUNK)
    ltri = jnp.asarray((same & (i[None, :] <= i[:, None])).astype(np.float32)).astype(BF16)
    utri = jnp.asarray((same & (i[None, :] >= i[:, None])).astype(np.float32)).astype(BF16)
    grp = np.arange(MXU_DIM) // NA_HEAD_DIM
    gmat = jnp.asarray((grp[:, None] == grp[None, :]).astype(np.float32)).astype(BF16)
    quarter = SCAN_DK // 4
    n_rows = seq // GRID_W
    inv = ROPE_BASE ** (-jnp.arange(quarter, dtype=F32) / quarter)
    ang_r = jnp.arange(n_rows, dtype=F32)[:, None] * inv[None, :]
    ang_c = jnp.arange(GRID_W, dtype=F32)[:, None] * inv[None, :]
    zr, zc = jnp.zeros_like(ang_r), jnp.zeros_like(ang_c)
    lanes = lambda a, b: jnp.concatenate([a, b] * SCAN_HEADS, axis=1)
    rot_rows = jnp.stack([lanes(jnp.cos(ang_r), zr), lanes(jnp.sin(ang_r), zr)])
    rot_cols = jnp.stack([lanes(zc, jnp.cos(ang_c)), lanes(zc, jnp.sin(ang_c))])
    lat = dict(ltri=ltri, utri=utri, gmat=gmat, rot_rows=rot_rows, rot_cols=rot_cols)
    ctx = dict(ltri=ltri, utri=utri, gmat=gmat, rot_cols=jnp.zeros_like(rot_cols),
               rot_rows=jnp.zeros((2, ctx_len // GRID_W, LANES), F32))
    return lat, ctx


def _tile_plan(seq):
    tm = 2 * MXU_DIM
    tb_scan = min(seq, 16 * CHUNK)
    na_blocks = min(8, seq // (NA_QROWS * GRID_W))
    assert seq % tm == 0 and seq % tb_scan == 0 and (seq // (NA_QROWS * GRID_W)) % na_blocks == 0
    return 2 * tm, tm, tb_scan, na_blocks


def _log_gamma(offset):
    g = jnp.log1p(-jnp.exp2(-(offset + jnp.arange(SCAN_HEADS, dtype=F32))))
    return jnp.tile(jnp.repeat(g, SCAN_DK // 2), 2)


def kernel(x, c, ctx, c_ctx, w_mod, b_mod, norm_w, w_in, na_q_norm, na_k_norm, na_rpb, gla_w_gate,
           gla_b_gate, gla_out_norm, ret_out_norm, w_branch, w_out):
    batch, seq, d = x.shape
    ctx_len = ctx.shape[1]
    depth = w_mod.shape[0]
    rows = seq // GRID_W
    assert d == D_MODEL and x.dtype == F32 and seq % GRID_W == 0 and rows >= NA_KROWS
    assert ctx_len % TRI_ROWS == 0 and batch + 1 <= 8
    tiles = _tile_plan(seq)
    tm_proj, tm_merge, tb_scan, na_blocks_per_step = tiles

    c_rows = jnp.zeros((8, d), F32).at[:batch].set(c).at[batch].set(c_ctx)
    mod = _modulation(c_rows, w_mod, b_mod).reshape(depth, 8, 1, 3 * d)
    prm = _prep_params(norm_w, w_in, na_q_norm, na_k_norm, gla_w_gate, gla_b_gate, gla_out_norm,
                       ret_out_norm, w_branch, w_out)
    lat_tabs, ctx_tabs = _tables(seq, ctx_len)
    bias = _na_bias_tables(na_rpb, rows)
    lg = jnp.stack([_log_gamma(RET_DECAY_FWD), _log_gamma(RET_DECAY_BWD)])
    ctx_row = lambda i: batch

    xl = x.reshape(batch * seq, d)
    xc = ctx.reshape(batch * ctx_len, d)
    for layer in range(depth):
        with_ctx = layer < depth - 1
        p_lat, cum_lat = _in_proj(xl, mod, prm, lat_tabs, layer=layer, tm=tm_proj,
                                  row_of=lambda i: i // (seq // tm_proj),
                                  blocks_per_batch=seq // tm_proj, rope=True)
        p_ctx, cum_ctx = _in_proj(xc, mod, prm, ctx_tabs, layer=layer, tm=ctx_len, row_of=ctx_row,
                                  blocks_per_batch=1, rope=False)

        na_l = _na_attention(p_lat, p_ctx, bias, layer=layer, batch=batch, seq=seq, ctx_len=ctx_len,
                             n_sub=na_blocks_per_step)

        (gla_f, gla_b), (gla_cf, gla_cb) = _scan(p_lat, p_ctx, cum_lat, cum_ctx, None, gla=True,
                                                 batch=batch, seq=seq, ctx_len=ctx_len, tb=tb_scan)
        (ret_f, ret_b), (ret_cf, ret_cb) = _scan(p_lat, p_ctx, None, None, lg, gla=False,
                                                 batch=batch, seq=seq, ctx_len=ctx_len, tb=tb_scan)

        xl_new = _merge(xl, mod, prm, na_l, gla_f, gla_b, ret_f, ret_b, layer=layer, tm=tm_merge,
                        row_of=lambda i: i // (seq // tm_merge))
        if with_ctx:
            na_c = _ctx_attention(p_ctx, batch=batch, ctx_len=ctx_len)
            xc = _merge(xc, mod, prm, na_c, gla_cf, gla_cb, ret_cf, ret_cb, layer=layer, tm=ctx_len,
                        row_of=ctx_row)
        xl = xl_new
    return xl.reshape(batch, seq, d)
```

```python
import functools

import numpy as np
import jax
import jax.numpy as jnp
from jax import lax
from jax.experimental import pallas as pl
from jax.experimental.pallas import tpu as pltpu

D_MODEL = 1024
GRID_W = 64
NORM_EPS = 1e-6
N_BRANCH = 3
BRANCH_W = D_MODEL // 2
NA_HEAD_DIM = 64
NA_HEADS = BRANCH_W // NA_HEAD_DIM
NA_WIN_ROWS = 8
NA_WIN_COLS = 16
SCAN_HEADS = 4
SCAN_DV = BRANCH_W // SCAN_HEADS
SCAN_DK = SCAN_DV // 2
GLA_GATE_RANK = 16
GLA_GATE_TAU = 16.0
RET_DECAY_FWD = 5.0
RET_DECAY_BWD = 5.5
CHUNK = 64
ROPE_BASE = 10000.0

LANES = 128
MXU_DIM = 256

SEC = 512
SEC_NA_Q, SEC_NA_K, SEC_NA_V, SEC_GLA_QK, SEC_GLA_V, SEC_RET_QK, SEC_RET_V = range(7)
N_SEC = 7
GATE_PAD = LANES
TRI_ROWS = MXU_DIM
NA_QROWS = 4
NA_KROWS = 12
NEG_BIG = -1e30

VMEM_LIMIT = 56 * 1024 * 1024

F32 = jnp.float32
BF16 = jnp.bfloat16


def _dot(a, b):
    return jnp.dot(a, b, preferred_element_type=F32)


def _dot_nt(a, b):
    return lax.dot_general(a, b, (((1,), (1,)), ((), ())), preferred_element_type=F32)


def _dot_tn(a, b):
    return lax.dot_general(a, b, (((0,), (0,)), ((), ())), preferred_element_type=F32)


def _split2(x):
    hi = x.astype(BF16)
    lo = (x - hi.astype(F32)).astype(BF16)
    return hi, lo


def _log_sigmoid(x):
    return jnp.minimum(x, 0.0) - jnp.log(1.0 + jnp.exp(-jnp.abs(x)))


def _sigmoid(x):
    return 1.0 / (1.0 + jnp.exp(-x))


def _silu(x):
    return x * _sigmoid(x)


def _params(*sem):
    return pltpu.CompilerParams(dimension_semantics=sem, vmem_limit_bytes=VMEM_LIMIT)


def _resident(block_shape, index_map):
    return pl.BlockSpec(block_shape, index_map, pipeline_mode=pl.Buffered(1))


def _ada_norm(x, nw, sc, sh):
    ms = jnp.mean(x * x, axis=-1, keepdims=True)
    return (x * lax.rsqrt(ms + NORM_EPS)) * (nw * (1.0 + sc)) + sh


def _mod_kernel(c_ref, w_ref, b_ref, o_ref):
    a = _silu(c_ref[...])
    o_ref[...] = jnp.dot(a, w_ref[...], preferred_element_type=F32,
                         precision=lax.Precision.HIGHEST) + b_ref[...]


def _modulation(c_rows, w_mod, b_mod):
    depth, d, d3 = w_mod.shape
    rows = c_rows.shape[0]
    tn = d
    return pl.pallas_call(
        _mod_kernel,
        grid=(depth, d3 // tn),
        in_specs=[
            pl.BlockSpec((rows, d), lambda l, j: (0, 0)),
            pl.BlockSpec((None, d, tn), lambda l, j: (l, 0, j)),
            pl.BlockSpec((None, 1, tn), lambda l, j: (l, 0, j)),
        ],
        out_specs=pl.BlockSpec((None, rows, tn), lambda l, j: (l, 0, j)),
        out_shape=jax.ShapeDtypeStruct((depth, rows, d3), F32),
        compiler_params=_params("parallel", "parallel"),
        name="modulation",
    )(c_rows, w_mod, b_mod.reshape(depth, 1, d3))


def _mod_specs(mod, layer, row_of):
    d = mod.shape[-1] // 3
    spec = lambda third: pl.BlockSpec((None, None, 1, d), lambda i: (layer, row_of(i), 0, third))
    return spec(0), spec(1), spec(2)


def _in_proj_kernel(x_ref, nw_ref, sh_ref, sc_ref, w_ref, cs_ref, gm_ref, wg_ref, wgate_ref,
                    bgate_ref, ltri_ref, utri_ref, rt_ref, ct_ref, p_ref, cum_ref, *, rope):
    tm = x_ref.shape[0]
    hb = _ada_norm(x_ref[...], nw_ref[...], sc_ref[...], sh_ref[...]).astype(BF16)

    g = _dot(hb, wg_ref[...])
    normed = {}
    for sec in range(N_SEC):
        base = sec * SEC
        raw = _dot(hb, w_ref[:, base:base + SEC])
        acc = raw * cs_ref[sec]
        if sec <= SEC_NA_K:
            normed[sec] = (raw, acc)
        elif rope and sec == SEC_RET_QK:
            n_grid_rows = tm // GRID_W

            def table(t):
                by_row = jnp.concatenate(
                    [jnp.broadcast_to(rt_ref[t, g:g + 1, :], (GRID_W, LANES)) for g in range(n_grid_rows)],
                    axis=0)
                return by_row + jnp.concatenate([ct_ref[t]] * n_grid_rows, axis=0)

            cos = table(0)
            sin = table(1)
            for c in range(0, SEC, 2 * LANES):
                u1 = acc[:, c:c + LANES]
                u2 = acc[:, c + LANES:c + 2 * LANES]
                p_ref[:, base + c:base + c + LANES] = (u1 * cos - u2 * sin).astype(BF16)
                p_ref[:, base + c + LANES:base + c + 2 * LANES] = (u1 * sin + u2 * cos).astype(BF16)
        else:
            p_ref[:, base:base + SEC] = acc.astype(BF16)
        if sec == SEC_NA_V:
            xg = _dot(g.astype(BF16), wgate_ref[...]) + bgate_ref[...]
            la = _log_sigmoid(xg) * (1.0 / GLA_GATE_TAU)

    for sec, (raw, acc) in normed.items():
        base = sec * SEC
        for c in range(0, SEC, MXU_DIM):
            sq = raw[:, c:c + MXU_DIM] * raw[:, c:c + MXU_DIM]
            ss = _dot(sq.astype(BF16), gm_ref[...])
            p_ref[:, base + c:base + c + MXU_DIM] = (
                acc[:, c:c + MXU_DIM] * lax.rsqrt(ss * (1.0 / NA_HEAD_DIM) + NORM_EPS)).astype(BF16)

    half = la.shape[1] // 2
    for r in range(tm // TRI_ROWS):
        rows = slice(r * TRI_ROWS, (r + 1) * TRI_ROWS)
        fh, fl = _split2(la[rows, :half])
        bh, bl = _split2(la[rows, half:])
        cum_ref[rows, :half] = _dot(ltri_ref[...], fh) + _dot(ltri_ref[...], fl)
        cum_ref[rows, half:] = _dot(utri_ref[...], bh) + _dot(utri_ref[...], bl)


def _in_proj(x2d, mod, prm, tabs, *, layer, tm, row_of, blocks_per_batch, rope):
    t, d = x2d.shape
    const2 = lambda i: (0, 0)
    lconst = lambda i: (layer, 0, 0)
    row_tab = pl.BlockSpec((2, tm // GRID_W, LANES), lambda i: (0, i % blocks_per_batch, 0))
    col_tab = _resident((2, GRID_W, LANES), lambda i: (0, 0, 0))
    sh_spec, sc_spec, _ = _mod_specs(mod, layer, row_of)
    return pl.pallas_call(
        functools.partial(_in_proj_kernel, rope=rope),
        grid=(t // tm,),
        in_specs=[
            pl.BlockSpec((tm, d), lambda i: (i, 0)),
            pl.BlockSpec((None, 1, d), lconst),
            sh_spec, sc_spec,
            _resident((None, d, N_SEC * SEC), lconst),
            _resident((None, N_SEC, 1, SEC), lambda i: (layer, 0, 0, 0)),
            _resident((MXU_DIM, MXU_DIM), const2),
            _resident((None, d, GATE_PAD), lconst),
            _resident((None, GATE_PAD, SEC), lconst),
            _resident((None, 1, SEC), lconst),
            _resident((TRI_ROWS, TRI_ROWS), const2),
            _resident((TRI_ROWS, TRI_ROWS), const2),
            row_tab, col_tab,
        ],
        out_specs=[
            pl.BlockSpec((tm, N_SEC * SEC), lambda i: (i, 0)),
            pl.BlockSpec((tm, SEC), lambda i: (i, 0)),
        ],
        out_shape=[
            jax.ShapeDtypeStruct((t, N_SEC * SEC), BF16),
            jax.ShapeDtypeStruct((t, SEC), F32),
        ],
        compiler_params=_params("parallel"),
        name="in_proj_rope" if rope else "in_proj",
    )(x2d, prm["nw"], mod, mod, prm["wpack"], prm["colscale"], tabs["gmat"], prm["wg"],
      prm["wgate"], prm["bgate"], tabs["ltri"], tabs["utri"], tabs["rot_rows"], tabs["rot_cols"])


def _na_block_plan(rows):
    n_qb = rows // NA_QROWS
    kr = min(NA_WIN_ROWS, rows)
    plans = []
    for qb in (0, min(1, n_qb - 1), n_qb - 1):
        t0 = int(np.clip(qb - 1, 0, n_qb - NA_KROWS // NA_QROWS))
        plan = []
        for i in range(NA_QROWS):
            r = qb * NA_QROWS + i
            rs = int(np.clip(r - kr // 2, 0, rows - kr))
            plan.append([(t0 * NA_QROWS + j) - r + NA_WIN_ROWS - 1
                         if rs <= t0 * NA_QROWS + j < rs + kr else None
                         for j in range(NA_KROWS)])
        plans.append(plan)
    return plans


def _bias_kernel(rpb_ref, o_ref, *, plans):
    shape = (GRID_W, LANES)
    lane = lax.broadcasted_iota(jnp.int32, shape, 1)
    c = lax.broadcasted_iota(jnp.int32, shape, 0)
    kc = lane & (GRID_W - 1)
    cs = jnp.clip(c - NA_WIN_COLS // 2, 0, GRID_W - NA_WIN_COLS)
    col_ok = jnp.logical_and(kc >= cs, kc < cs + NA_WIN_COLS)
    low = lane < GRID_W
    neg = jnp.full(shape, NEG_BIG, F32)
    toep = []
    for a in range(2 * NA_WIN_ROWS - 1):
        x = jnp.broadcast_to(rpb_ref[a:a + 1, :], shape)
        lo = pltpu.roll(x, LANES - (NA_WIN_COLS - 1), 1, stride=1, stride_axis=0)
        hi = pltpu.roll(x, GRID_W - (NA_WIN_COLS - 1), 1, stride=1, stride_axis=0)
        toep.append(jnp.where(col_ok, jnp.where(low, lo, hi), neg))
    for k, plan in enumerate(plans):
        for i in range(NA_QROWS):
            for jp in range(NA_KROWS // 2):
                a0, a1 = plan[i][2 * jp], plan[i][2 * jp + 1]
                b0 = neg if a0 is None else toep[a0]
                b1 = neg if a1 is None else toep[a1]
                o_ref[k, i * GRID_W:(i + 1) * GRID_W, jp * LANES:(jp + 1) * LANES] = (
                    jnp.where(low, b0, b1))


def _na_bias_tables(rpb, rows):
    depth, heads, nr, nc = rpb.shape
    rp = jnp.pad(rpb, ((0, 0), (0, 0), (0, 16 - nr), (0, LANES - nc)))
    tq, tk = NA_QROWS * GRID_W, NA_KROWS * GRID_W
    return pl.pallas_call(
        functools.partial(_bias_kernel, plans=_na_block_plan(rows)),
        grid=(depth, heads),
        in_specs=[pl.BlockSpec((None, None, 16, LANES), lambda l, h: (l, h, 0, 0))],
        out_specs=pl.BlockSpec((None, 3, None, tq, tk), lambda l, h: (l, 0, h, 0, 0)),
        out_shape=jax.ShapeDtypeStruct((depth, 3, heads, tq, tk), F32),
        compiler_params=_params("parallel", "parallel"),
        name="na_bias",
    )(rp)


def _lane_mask(width, lo, hi):
    lane = lax.broadcasted_iota(jnp.int32, (1, width), 1)
    return jnp.logical_and(lane >= lo, lane < hi)


def _na_kernel(q_ref, k_ref, v_ref, kc_ref, vc_ref, bias_ref, o_ref, *, n_qb, n_sub):
    tq = q_ref.shape[0] // n_sub
    tk = bias_ref.shape[-1]
    width = q_ref.shape[1]
    heads = width // NA_HEAD_DIM
    kc = kc_ref[...]
    vc = vc_ref[...]
    inside = [_lane_mask(width, h * NA_HEAD_DIM, (h + 1) * NA_HEAD_DIM) for h in range(heads)]
    ones_row = [_lane_mask(width, ((h + 1) % heads) * NA_HEAD_DIM, ((h + 1) % heads) * NA_HEAD_DIM + 1)
                for h in range(heads)]
    starts = []
    for sub in range(n_sub):
        qb = pl.program_id(2) * n_sub + sub
        t0 = jnp.clip(qb - 1, 0, n_qb - tk // tq)
        starts.append((qb, pl.multiple_of(t0 * tq, tq)))
    units = [(sub, h) for sub in range(n_sub) for h in range(heads)]
    vc_heads = [vc * inside[h].astype(BF16) + ones_row[h].astype(BF16) for h in range(heads)]

    def scores(sub, h):
        qb, start = starts[sub]
        kind = jnp.where(qb == 0, 0, jnp.where(qb == n_qb - 1, 2, 1))
        qh = q_ref[sub * tq:(sub + 1) * tq, :] * inside[h].astype(BF16)
        return _dot_nt(qh, k_ref[pl.ds(start, tk), :]) + bias_ref[kind, h], _dot_nt(qh, kc)

    pending = scores(*units[0])
    acc = None
    for idx, (sub, h) in enumerate(units):
        s, sc = pending
        if idx + 1 < len(units):
            pending = scores(*units[idx + 1])
        keep = inside[h].astype(BF16)
        fill = ones_row[h].astype(BF16)
        m = jnp.maximum(jnp.max(s, axis=-1, keepdims=True), jnp.max(sc, axis=-1, keepdims=True))
        p = jnp.exp((s - m).astype(BF16))
        pc = jnp.exp((sc - m).astype(BF16))
        v3 = v_ref[pl.ds(starts[sub][1], tk), :]
        oh = _dot(p, v3 * keep + fill) + _dot(pc, vc_heads[h])
        l = jnp.sum(jnp.where(ones_row[h], oh, 0.0), axis=-1, keepdims=True)
        acc = oh * (1.0 / l) if h == 0 else jnp.where(inside[h], oh * (1.0 / l), acc)
        if h == heads - 1:
            o_ref[sub * tq:(sub + 1) * tq, :] = acc.astype(BF16)


def _na_attention(p_lat, p_ctx, bias, *, layer, batch, seq, ctx_len, n_sub):
    tq = NA_QROWS * GRID_W
    tk = NA_KROWS * GRID_W
    n_qb = seq // tq
    n_steps = n_qb // n_sub
    hw = 2 * LANES
    n_hg = BRANCH_W // hw
    sec_blk = SEC // hw
    heads = hw // NA_HEAD_DIM
    return pl.pallas_call(
        functools.partial(_na_kernel, n_qb=n_qb, n_sub=n_sub),
        grid=(batch, n_hg, n_steps),
        in_specs=[
            pl.BlockSpec((n_sub * tq, hw), lambda b, g, i: (b * n_steps + i, SEC_NA_Q * sec_blk + g)),
            pl.BlockSpec((seq, hw), lambda b, g, i: (b, SEC_NA_K * sec_blk + g)),
            pl.BlockSpec((seq, hw), lambda b, g, i: (b, SEC_NA_V * sec_blk + g)),
            pl.BlockSpec((ctx_len, hw), lambda b, g, i: (b, SEC_NA_K * sec_blk + g)),
            pl.BlockSpec((ctx_len, hw), lambda b, g, i: (b, SEC_NA_V * sec_blk + g)),
            pl.BlockSpec((None, bias.shape[1], heads, tq, tk), lambda b, g, i: (layer, 0, g, 0, 0)),
        ],
        out_specs=pl.BlockSpec((n_sub * tq, hw), lambda b, g, i: (b * n_steps + i, g)),
        out_shape=jax.ShapeDtypeStruct((batch * seq, BRANCH_W), BF16),
        compiler_params=_params("parallel", "parallel", "arbitrary"),
        name="na_attention",
    )(p_lat, p_lat, p_lat, p_ctx, p_ctx, bias)


def _ctx_attn_kernel(q_ref, k_ref, v_ref, o_ref):
    first = _lane_mask(LANES, 0, NA_HEAD_DIM)
    sel = (first.astype(BF16), jnp.logical_not(first).astype(BF16))
    outs = []
    for pair in range(q_ref.shape[1] // LANES):
        lanes = slice(pair * LANES, (pair + 1) * LANES)
        q2, k2, v2 = q_ref[:, lanes], k_ref[:, lanes], v_ref[:, lanes]
        o_pair = []
        for hh in range(2):
            s = _dot_nt(q2 * sel[hh], k2)
            m = jnp.max(s, axis=-1, keepdims=True)
            p = jnp.exp(s - m)
            l = jnp.sum(p, axis=-1, keepdims=True)
            o_pair.append(_dot(p.astype(BF16), v2) * (1.0 / l))
        outs.append(jnp.where(first, o_pair[0], o_pair[1]))
    o_ref[...] = jnp.concatenate(outs, axis=1).astype(BF16)


def _ctx_attention(p_ctx, *, batch, ctx_len):
    hw = 2 * LANES
    n_hg = BRANCH_W // hw
    sec_blk = SEC // hw
    spec = lambda sec: pl.BlockSpec((ctx_len, hw), lambda b, g: (b, sec * sec_blk + g))
    return pl.pallas_call(
        _ctx_attn_kernel,
        grid=(batch, n_hg),
        in_specs=[spec(SEC_NA_Q), spec(SEC_NA_K), spec(SEC_NA_V)],
        out_specs=pl.BlockSpec((ctx_len, hw), lambda b, g: (b, g)),
        out_shape=jax.ShapeDtypeStruct((batch * ctx_len, BRANCH_W), BF16),
        compiler_params=_params("parallel", "parallel"),
        name="ctx_attention",
    )(p_ctx, p_ctx, p_ctx)


def _scan_head_of_lane(gla):
    lane = lax.broadcasted_iota(jnp.int32, (1, SCAN_HEADS * SCAN_DK), 1)
    if gla:
        return lax.shift_right_logical(lane, SCAN_DK.bit_length() - 1)
    return lax.shift_right_logical(lane & (LANES - 1), (SCAN_DK // 2).bit_length() - 1)


def _decay_factors(cum, ref_row, total_row):
    cm = cum[ref_row:ref_row + 1, :]
    tot = cum[total_row:total_row + 1, :]
    eq = jnp.exp(cum - cm)
    ek = jnp.exp(cm - cum)
    return eq, ek, jnp.exp(cm), jnp.exp(tot - cm), jnp.exp(tot)


def _chunk_prepare(q, k, v, factors, head_sel):
    eq, ek, _, e_tm, _ = factors
    qtb = (q * eq).astype(BF16)
    kt = k * ek
    khb = (kt * e_tm).astype(BF16)
    hmul = [sel.astype(BF16) for sel in head_sel]
    q_stack = jnp.concatenate([qtb * m for m in hmul], axis=0)
    v_rows = jnp.concatenate([v[:, h * SCAN_DV:(h + 1) * SCAN_DV] for h in range(SCAN_HEADS)], axis=0)
    k_rows = jnp.concatenate([khb * m for m in hmul], axis=0)
    return q_stack, kt.astype(BF16), _dot_tn(v_rows, k_rows)


def _chunk_scores(q_stack, ktb, st, e_m, tri_stack):
    rhs = jnp.concatenate([(st * e_m).astype(BF16), ktb], axis=0)
    both = _dot_nt(q_stack, rhs)
    return both[:, :SCAN_DV], jnp.where(tri_stack, both[:, SCAN_DV:], 0.0).astype(BF16)


def _chunk_output(inter, att, v):
    outs = []
    for h in range(SCAN_HEADS):
        rows = slice(h * CHUNK, (h + 1) * CHUNK)
        outs.append(_dot(att[rows], v[:, h * SCAN_DV:(h + 1) * SCAN_DV]) + inter[rows])
    return jnp.concatenate(outs, axis=1)


def _scan_masks(gla):
    head = _scan_head_of_lane(gla)
    head_sel = [head == h for h in range(SCAN_HEADS)]
    t = lax.broadcasted_iota(jnp.int32, (SCAN_HEADS * CHUNK, CHUNK), 0) & (CHUNK - 1)
    s = lax.broadcasted_iota(jnp.int32, (SCAN_HEADS * CHUNK, CHUNK), 1)
    return head_sel, s <= t, s >= t


def _ret_cum(lg_row, forward):
    t = lax.broadcasted_iota(jnp.int32, (CHUNK, lg_row.shape[1]), 0).astype(F32)
    steps = (t + 1.0) if forward else (float(CHUNK) - t)
    return steps * lg_row


def _scan_block(fwd_refs, bwd_refs, n_chunks, gla, ret_factors, masks):
    head_sel, tril, triu = masks
    hk = SCAN_HEADS * SCAN_DK
    chains = (
        (*fwd_refs, tril, (CHUNK // 2 - 1, CHUNK - 1), 0,
         [slice(c * CHUNK, (c + 1) * CHUNK) for c in range(n_chunks)]),
        (*bwd_refs, triu, (CHUNK // 2, 0), 1,
         [slice(c * CHUNK, (c + 1) * CHUNK) for c in reversed(range(n_chunks))]),
    )
    def prepare(chain):
        qk_ref, v_ref, cum_ref, _, _, _, (ref_row, total_row), direction, order = chain
        items = []
        for rows in order:
            fac = (_decay_factors(cum_ref[rows, :], ref_row, total_row) if gla
                   else ret_factors[direction])
            qk = qk_ref[rows, :].astype(F32)
            items.append((fac, _chunk_prepare(qk[:, :hk], qk[:, hk:], v_ref[rows, :], fac, head_sel)))
        return items

    def score(chain, items):
        st_ref, tri = chain[4], chain[5]
        st = st_ref[...]
        states = []
        for fac, (_, _, inc) in items:
            states.append(st)
            st = st * fac[4] + inc
        st_ref[...] = st
        return [_chunk_scores(q_stack, ktb, st_c, fac[2], tri)
                for st_c, (fac, (q_stack, ktb, _)) in zip(states, items)]

    def output(chain, results):
        v_ref, o_ref, order = chain[1], chain[3], chain[8]
        for rows, (inter, att) in zip(order, results):
            o_ref[rows, :] = _chunk_output(inter, att, v_ref[rows, :]).astype(BF16)

    prepared = [prepare(chain) for chain in chains]
    scored = [score(chain, items) for chain, items in zip(chains, prepared)]
    for chain, results in zip(chains, scored):
        output(chain, results)


def _scan_kernel(*refs, gla):
    if gla:
        (qkf_ref, vf_ref, cf_ref, qkb_ref, vb_ref, cb_ref, qkc_ref, vc_ref, ccf_ref, ccb_ref,
         of_ref, ob_ref, ocf_ref, ocb_ref, stf, stb) = refs
        ret_factors = None
    else:
        (qkf_ref, vf_ref, qkb_ref, vb_ref, qkc_ref, vc_ref, lg_ref,
         of_ref, ob_ref, ocf_ref, ocb_ref, stf, stb) = refs
        cf_ref = cb_ref = ccf_ref = ccb_ref = None
        ret_factors = (_decay_factors(_ret_cum(lg_ref[0:1, :], True), CHUNK // 2 - 1, CHUNK - 1),
                       _decay_factors(_ret_cum(lg_ref[1:2, :], False), CHUNK // 2, 0))
    masks = _scan_masks(gla)

    @pl.when(pl.program_id(1) == 0)
    def _():
        stf[...] = jnp.zeros_like(stf)
        stb[...] = jnp.zeros_like(stb)
        _scan_block((qkc_ref, vc_ref, ccf_ref, ocf_ref, stf), (qkc_ref, vc_ref, ccb_ref, ocb_ref, stb),
                    qkc_ref.shape[0] // CHUNK, gla, ret_factors, masks)

    _scan_block((qkf_ref, vf_ref, cf_ref, of_ref, stf), (qkb_ref, vb_ref, cb_ref, ob_ref, stb),
                qkf_ref.shape[0] // CHUNK, gla, ret_factors, masks)


def _scan(p_lat, p_ctx, cum_lat, cum_ctx, lg, *, gla, batch, seq, ctx_len, tb):
    nb = seq // tb
    hk = SCAN_HEADS * SCAN_DK
    hv = SCAN_HEADS * SCAN_DV
    sec_qk = SEC_GLA_QK if gla else SEC_RET_QK
    sec_v = SEC_GLA_V if gla else SEC_RET_V
    fwd = lambda b, s: b * nb + s
    bwd = lambda b, s: b * nb + nb - 1 - s

    def stream(order):
        specs = [pl.BlockSpec((tb, SEC), lambda b, s: (order(b, s), sec_qk)),
                 pl.BlockSpec((tb, SEC), lambda b, s: (order(b, s), sec_v))]
        return specs, [p_lat, p_lat]

    in_specs, args = stream(fwd)
    if gla:
        in_specs.append(pl.BlockSpec((tb, hk), lambda b, s: (fwd(b, s), 0)))
        args.append(cum_lat)
    specs_b, args_b = stream(bwd)
    in_specs += specs_b
    args += args_b
    if gla:
        in_specs.append(pl.BlockSpec((tb, hk), lambda b, s: (bwd(b, s), 1)))
        args.append(cum_lat)
    in_specs += [pl.BlockSpec((ctx_len, SEC), lambda b, s: (b, sec_qk)),
                 pl.BlockSpec((ctx_len, SEC), lambda b, s: (b, sec_v))]
    args += [p_ctx, p_ctx]
    if gla:
        in_specs += [pl.BlockSpec((ctx_len, hk), lambda b, s: (b, 0)),
                     pl.BlockSpec((ctx_len, hk), lambda b, s: (b, 1))]
        args += [cum_ctx, cum_ctx]
    else:
        in_specs.append(pl.BlockSpec((2, hk), lambda b, s: (0, 0)))
        args.append(lg)
    ctx_out = pl.BlockSpec((ctx_len, hv), lambda b, s: (b, 0))
    o_f, o_b, oc_f, oc_b = pl.pallas_call(
        functools.partial(_scan_kernel, gla=gla),
        grid=(batch, nb),
        in_specs=in_specs,
        out_specs=[
            pl.BlockSpec((tb, hv), lambda b, s: (fwd(b, s), 0)),
            pl.BlockSpec((tb, hv), lambda b, s: (bwd(b, s), 0)),
            ctx_out, ctx_out,
        ],
        out_shape=[
            jax.ShapeDtypeStruct((batch * seq, hv), BF16),
            jax.ShapeDtypeStruct((batch * seq, hv), BF16),
            jax.ShapeDtypeStruct((batch * ctx_len, hv), BF16),
            jax.ShapeDtypeStruct((batch * ctx_len, hv), BF16),
        ],
        scratch_shapes=[pltpu.VMEM((SCAN_DV, hk), F32), pltpu.VMEM((SCAN_DV, hk), F32)],
        compiler_params=_params("parallel", "arbitrary"),
        name="gla_scan" if gla else "ret_scan",
    )(*args)
    return (o_f, o_b), (oc_f, oc_b)


def _head_rms(o, gain):
    parts = []
    for h in range(SCAN_HEADS):
        sl = o[:, h * SCAN_DV:(h + 1) * SCAN_DV]
        ms = jnp.mean(sl * sl, axis=-1, keepdims=True)
        parts.append(sl * lax.rsqrt(ms + NORM_EPS))
    return jnp.concatenate(parts, axis=1) * gain


def _merge_kernel(x_ref, nw_ref, sh_ref, sc_ref, gate_ref, na_ref, gf_ref, gb_ref, rf_ref, rb_ref,
                  gg_ref, rg_ref, wz0_ref, wz1_ref, wz2_ref, wg0_ref, wg1_ref, wg2_ref, wb_ref, wo_ref,
                  o_ref):
    x = x_ref[...]
    wz_refs = (wz0_ref, wz1_ref, wz2_ref)
    wg_refs = (wg0_ref, wg1_ref, wg2_ref)
    hb = _ada_norm(x, nw_ref[...], sc_ref[...], sh_ref[...]).astype(BF16)
    outs = (
        na_ref[...].astype(F32),
        _head_rms(gf_ref[...].astype(F32) + gb_ref[...].astype(F32), gg_ref[...]),
        _head_rms(rf_ref[...].astype(F32) + rb_ref[...].astype(F32), rg_ref[...]),
    )
    zs = [_dot(hb, wz_refs[i][...]) for i in range(N_BRANCH)]
    us = [(outs[i] * _silu(zs[i])).astype(BF16) for i in range(N_BRANCH)]
    gs = [_sigmoid(_dot(hb, wg_refs[i][...])) for i in range(N_BRANCH)]
    y = None
    for i in range(N_BRANCH):
        t = gs[i] * _dot(us[i], wb_ref[i])
        y = t if y is None else y + t
    r = _dot(y.astype(BF16), wo_ref[...])
    o_ref[...] = x + gate_ref[...] * r


def _merge(x2d, mod, prm, na_o, gla_f, gla_b, ret_f, ret_b, *, layer, tm, row_of):
    t, d = x2d.shape
    row = lambda w: pl.BlockSpec((tm, w), lambda i: (i, 0))
    sh_spec, sc_spec, gate_spec = _mod_specs(mod, layer, row_of)
    lconst3 = lambda i: (layer, 0, 0)
    return pl.pallas_call(
        _merge_kernel,
        grid=(t // tm,),
        in_specs=[
            row(d),
            pl.BlockSpec((None, 1, d), lconst3),
            sh_spec, sc_spec, gate_spec,
            row(BRANCH_W), row(BRANCH_W), row(BRANCH_W), row(BRANCH_W), row(BRANCH_W),
            pl.BlockSpec((None, 1, BRANCH_W), lconst3),
            pl.BlockSpec((None, 1, BRANCH_W), lconst3),
            *[_resident((None, d, BRANCH_W), lambda i, n=n: (layer, 0, N_SEC * SEC // BRANCH_W + n))
              for n in range(N_BRANCH)],
            *[_resident((None, d, d), lambda i, n=n: (layer, 0, (N_SEC * SEC + N_BRANCH * BRANCH_W) // d + n))
              for n in range(N_BRANCH)],
            _resident((None, N_BRANCH, BRANCH_W, d), lambda i: (layer, 0, 0, 0)),
            _resident((None, d, d), lconst3),
        ],
        out_specs=row(d),
        out_shape=jax.ShapeDtypeStruct((t, d), F32),
        compiler_params=_params("parallel"),
        name="merge",
    )(x2d, prm["nw"], mod, mod, mod, na_o, gla_f, gla_b, ret_f, ret_b, prm["gla_gain"],
      prm["ret_gain"], *([prm["wpack"]] * (2 * N_BRANCH)), prm["wb"], prm["wo"])


def _in_splits():
    na = NA_HEADS * NA_HEAD_DIM
    qk = SCAN_HEADS * SCAN_DK
    vv = SCAN_HEADS * SCAN_DV
    sizes = (na, na, na, qk, qk, vv, 2 * GLA_GATE_RANK, qk, qk, vv, N_BRANCH * BRANCH_W,
             N_BRANCH * D_MODEL)
    return [int(v) for v in np.concatenate([[0], np.cumsum(sizes)])]


def _rotary_pack_matrix():
    hk = SCAN_HEADS * SCAN_DK
    quarter = SCAN_DK // 4
    p = np.zeros((SEC, SEC), np.float32)
    for blk in range(SEC // hk):
        for h in range(SCAN_HEADS):
            for half in range(2):
                for partner in range(2):
                    for i in range(quarter):
                        old = h * SCAN_DK + half * 2 * quarter + partner * quarter + i
                        new = partner * (hk // 2) + h * 2 * quarter + half * quarter + i
                        p[blk * hk + old, blk * hk + new] = 1.0
    return jnp.asarray(p).astype(BF16)


def _pack_kernel(a_ref, b_ref, perm_ref, o_ref, *, first_shifted, perm_block, shift):
    j = pl.program_id(1)
    sec = a_ref.shape[0]

    def shifted():
        x = jnp.concatenate([a_ref[...], b_ref[...]], axis=0)
        return x[shift:shift + sec]

    @pl.when(j < first_shifted)
    def _():
        o_ref[...] = a_ref[...].T.astype(BF16)

    @pl.when(j == perm_block)
    def _():
        o_ref[...] = _dot(shifted().T.astype(BF16), perm_ref[...]).astype(BF16)

    @pl.when(jnp.logical_and(j >= first_shifted, j != perm_block))
    def _():
        o_ref[...] = shifted().T.astype(BF16)


def _pack_w_in(w_in):
    depth, d, n_in = w_in.shape
    o = _in_splits()
    g0, g1 = o[6], o[7]
    assert g0 % SEC == 0 and (n_in - (g1 - g0)) % SEC == 0
    n_blocks = (n_in - (g1 - g0)) // SEC
    tail_rows = g1 - g0
    w_t = jnp.swapaxes(w_in, 1, 2)
    return pl.pallas_call(
        functools.partial(_pack_kernel, first_shifted=g0 // SEC, perm_block=SEC_RET_QK,
                          shift=g1 - g0),
        grid=(depth, n_blocks),
        in_specs=[
            pl.BlockSpec((None, SEC, d), lambda l, j: (l, j, 0)),
            pl.BlockSpec((None, tail_rows, d), lambda l, j: (l, (j + 1) * (SEC // tail_rows), 0)),
            _resident((SEC, SEC), lambda l, j: (0, 0)),
        ],
        out_specs=pl.BlockSpec((None, d, SEC), lambda l, j: (l, 0, j)),
        out_shape=jax.ShapeDtypeStruct((depth, d, n_blocks * SEC), BF16),
        compiler_params=_params("parallel", "parallel"),
        name="pack_w_in",
    )(w_t, w_t, _rotary_pack_matrix())


def _prep_params(norm_w, w_in, q_norm, k_norm, w_gate, b_gate, gla_out_norm, ret_out_norm,
                 w_branch, w_out):
    depth, d, _ = w_in.shape
    o = _in_splits()
    g0, g1 = o[6], o[7]
    hk = SCAN_HEADS * SCAN_DK
    wg = jnp.pad(w_in[:, :, g0:g1], ((0, 0), (0, 0), (0, GATE_PAD - (g1 - g0)))).astype(BF16)
    wgate = jnp.zeros((depth, GATE_PAD, 2 * hk), F32)
    wgate = wgate.at[:, :GLA_GATE_RANK, :hk].set(w_gate[:, 0])
    wgate = wgate.at[:, GLA_GATE_RANK:2 * GLA_GATE_RANK, hk:].set(w_gate[:, 1])
    bgate = b_gate.reshape(depth, 1, 2 * hk)
    base = np.ones((N_SEC, 1, SEC), np.float32)
    base[SEC_NA_Q] = NA_HEAD_DIM ** -0.5
    base[SEC_GLA_QK, 0, :hk] = SCAN_DK ** -0.5
    base[SEC_RET_QK, 0, :hk] = SCAN_DK ** -0.5
    reps = SEC // NA_HEAD_DIM
    ones = jnp.ones((depth, N_SEC - 2, 1, SEC), F32)
    gains = jnp.concatenate([jnp.tile(q_norm, (1, reps))[:, None, None, :],
                             jnp.tile(k_norm, (1, reps))[:, None, None, :], ones], axis=1)
    return dict(
        nw=norm_w.reshape(depth, 1, d),
        wpack=_pack_w_in(w_in), wg=wg, wgate=wgate.astype(BF16), bgate=bgate,
        colscale=gains * jnp.asarray(base)[None],
        wb=w_branch.astype(BF16), wo=w_out.astype(BF16),
        gla_gain=gla_out_norm.reshape(depth, 1, BRANCH_W),
        ret_gain=ret_out_norm.reshape(depth, 1, BRANCH_W),
    )


def _tables(seq, ctx_rows):
    i = np.arange(TRI_ROWS)
    same = (i[:, None] // CHUNK) == (i[None, :] // CHUNK)
    ltri = jnp.asarray((same & (i[None, :] <= i[:, None])).astype(np.float32)).astype(BF16)
    utri = jnp.asarray((same & (i[None, :] >= i[:, None])).astype(np.float32)).astype(BF16)
    grp = np.arange(MXU_DIM) // NA_HEAD_DIM
    gmat = jnp.asarray((grp[:, None] == grp[None, :]).astype(np.float32)).astype(BF16)
    quarter = SCAN_DK // 4
    n_rows = seq // GRID_W
    inv = ROPE_BASE ** (-jnp.arange(quarter, dtype=F32) / quarter)
    ang_r = jnp.arange(n_rows, dtype=F32)[:, None] * inv[None, :]
    ang_c = jnp.arange(GRID_W, dtype=F32)[:, None] * inv[None, :]
    zr, zc = jnp.zeros_like(ang_r), jnp.zeros_like(ang_c)
    lanes = lambda a, b: jnp.concatenate([a, b] * SCAN_HEADS, axis=1)
    rot_rows = jnp.stack([lanes(jnp.cos(ang_r), zr), lanes(jnp.sin(ang_r), zr)])
    rot_cols = jnp.stack([lanes(zc, jnp.cos(ang_c)), lanes(zc, jnp.sin(ang_c))])
    lat = dict(ltri=ltri, utri=utri, gmat=gmat, rot_rows=rot_rows, rot_cols=rot_cols)
    ctx = dict(ltri=ltri, utri=utri, gmat=gmat, rot_cols=jnp.zeros_like(rot_cols),
               rot_rows=jnp.zeros((2, ctx_rows // GRID_W, LANES), F32))
    return lat, ctx


def _tile_plan(seq):
    tm = 2 * MXU_DIM
    tb_scan = min(seq, 16 * CHUNK)
    na_blocks = min(8, seq // (NA_QROWS * GRID_W))
    assert seq % tm == 0 and seq % tb_scan == 0 and (seq // (NA_QROWS * GRID_W)) % na_blocks == 0
    return 2 * tm, tm, tb_scan, na_blocks


def _log_gamma(offset):
    g = jnp.log1p(-jnp.exp2(-(offset + jnp.arange(SCAN_HEADS, dtype=F32))))
    return jnp.tile(jnp.repeat(g, SCAN_DK // 2), 2)


def kernel(x, c, ctx, c_ctx, w_mod, b_mod, norm_w, w_in, na_q_norm, na_k_norm, na_rpb, gla_w_gate,
           gla_b_gate, gla_out_norm, ret_out_norm, w_branch, w_out):
    batch, seq, d = x.shape
    ctx_len = ctx.shape[1]
    depth = w_mod.shape[0]
    rows = seq // GRID_W
    assert d == D_MODEL and x.dtype == F32 and seq % GRID_W == 0 and rows >= NA_KROWS
    assert ctx_len % TRI_ROWS == 0 and batch + 1 <= 8
    tiles = _tile_plan(seq)
    tm_proj, tm_merge, tb_scan, na_blocks_per_step = tiles

    c_rows = jnp.zeros((8, d), F32).at[:batch].set(c).at[batch].set(c_ctx)
    mod = _modulation(c_rows, w_mod, b_mod).reshape(depth, 8, 1, 3 * d)
    prm = _prep_params(norm_w, w_in, na_q_norm, na_k_norm, gla_w_gate, gla_b_gate, gla_out_norm,
                       ret_out_norm, w_branch, w_out)
    lat_tabs, ctx_tabs = _tables(seq, batch * ctx_len)
    bias = _na_bias_tables(na_rpb, rows)
    lg = jnp.stack([_log_gamma(RET_DECAY_FWD), _log_gamma(RET_DECAY_BWD)])
    ctx_row = lambda i: batch

    xl = x.reshape(batch * seq, d)
    xc = ctx.reshape(batch * ctx_len, d)
    for layer in range(depth):
        with_ctx = layer < depth - 1
        p_lat, cum_lat = _in_proj(xl, mod, prm, lat_tabs, layer=layer, tm=tm_proj,
                                  row_of=lambda i: i // (seq // tm_proj),
                                  blocks_per_batch=seq // tm_proj, rope=True)
        p_ctx, cum_ctx = _in_proj(xc, mod, prm, ctx_tabs, layer=layer, tm=batch * ctx_len,
                                  row_of=ctx_row, blocks_per_batch=1, rope=False)

        na_l = _na_attention(p_lat, p_ctx, bias, layer=layer, batch=batch, seq=seq, ctx_len=ctx_len,
                             n_sub=na_blocks_per_step)

        (gla_f, gla_b), (gla_cf, gla_cb) = _scan(p_lat, p_ctx, cum_lat, cum_ctx, None, gla=True,
                                                 batch=batch, seq=seq, ctx_len=ctx_len, tb=tb_scan)
        (ret_f, ret_b), (ret_cf, ret_cb) = _scan(p_lat, p_ctx, None, None, lg, gla=False,
                                                 batch=batch, seq=seq, ctx_len=ctx_len, tb=tb_scan)

        xl_new = _merge(xl, mod, prm, na_l, gla_f, gla_b, ret_f, ret_b, layer=layer, tm=tm_merge,
                        row_of=lambda i: i // (seq // tm_merge))
        if with_ctx:
            na_c = _ctx_attention(p_ctx, batch=batch, ctx_len=ctx_len)
            xc = _merge(xc, mod, prm, na_c, gla_cf, gla_cb, ret_cf, ret_cb, layer=layer,
                        tm=batch * ctx_len, row_of=ctx_row)
        xl = xl_new
    return xl.reshape(batch, seq, d)
```

```python
import functools

import numpy as np
import jax
import jax.numpy as jnp
from jax import lax
from jax.experimental import pallas as pl
from jax.experimental.pallas import tpu as pltpu

D_MODEL = 1024
GRID_W = 64
NORM_EPS = 1e-6
N_BRANCH = 3
BRANCH_W = D_MODEL // 2
NA_HEAD_DIM = 64
NA_HEADS = BRANCH_W // NA_HEAD_DIM
NA_WIN_ROWS = 8
NA_WIN_COLS = 16
SCAN_HEADS = 4
SCAN_DV = BRANCH_W // SCAN_HEADS
SCAN_DK = SCAN_DV // 2
GLA_GATE_RANK = 16
GLA_GATE_TAU = 16.0
RET_DECAY_FWD = 5.0
RET_DECAY_BWD = 5.5
CHUNK = 64
ROPE_BASE = 10000.0

LANES = 128
MXU_DIM = 256

SEC = 512
SEC_NA_Q, SEC_NA_K, SEC_NA_V, SEC_GLA_QK, SEC_GLA_V, SEC_RET_QK, SEC_RET_V = range(7)
N_SEC = 7
GATE_PAD = LANES
TRI_ROWS = MXU_DIM
NA_QROWS = 4
NA_KROWS = 12
NEG_BIG = -1e30

VMEM_LIMIT = 56 * 1024 * 1024

F32 = jnp.float32
BF16 = jnp.bfloat16


def _dot(a, b):
    return jnp.dot(a, b, preferred_element_type=F32)


def _dot_nt(a, b):
    return lax.dot_general(a, b, (((1,), (1,)), ((), ())), preferred_element_type=F32)


def _dot_tn(a, b):
    return lax.dot_general(a, b, (((0,), (0,)), ((), ())), preferred_element_type=F32)


def _split2(x):
    hi = x.astype(BF16)
    lo = (x - hi.astype(F32)).astype(BF16)
    return hi, lo


def _log_sigmoid(x):
    return jnp.minimum(x, 0.0) - jnp.log(1.0 + jnp.exp(-jnp.abs(x)))


def _sigmoid(x):
    return 1.0 / (1.0 + jnp.exp(-x))


def _silu(x):
    return x * _sigmoid(x)


def _params(*sem):
    return pltpu.CompilerParams(dimension_semantics=sem, vmem_limit_bytes=VMEM_LIMIT)


def _resident(block_shape, index_map):
    return pl.BlockSpec(block_shape, index_map, pipeline_mode=pl.Buffered(1))


def _ada_norm(x, nw, sc, sh):
    ms = jnp.mean(x * x, axis=-1, keepdims=True)
    return (x * lax.rsqrt(ms + NORM_EPS)) * (nw * (1.0 + sc)) + sh


def _mod_kernel(c_ref, w_ref, b_ref, o_ref):
    a = _silu(c_ref[...])
    o_ref[...] = jnp.dot(a, w_ref[...], preferred_element_type=F32,
                         precision=lax.Precision.HIGHEST) + b_ref[...]


def _modulation(c_rows, w_mod, b_mod):
    depth, d, d3 = w_mod.shape
    rows = c_rows.shape[0]
    tn = d
    return pl.pallas_call(
        _mod_kernel,
        grid=(depth, d3 // tn),
        in_specs=[
            pl.BlockSpec((rows, d), lambda l, j: (0, 0)),
            pl.BlockSpec((None, d, tn), lambda l, j: (l, 0, j)),
            pl.BlockSpec((None, 1, tn), lambda l, j: (l, 0, j)),
        ],
        out_specs=pl.BlockSpec((None, rows, tn), lambda l, j: (l, 0, j)),
        out_shape=jax.ShapeDtypeStruct((depth, rows, d3), F32),
        compiler_params=_params("parallel", "parallel"),
        name="modulation",
    )(c_rows, w_mod, b_mod.reshape(depth, 1, d3))


def _mod_specs(mod, layer, row_of):
    d = mod.shape[-1] // 3
    spec = lambda third: pl.BlockSpec((None, None, 1, d), lambda i: (layer, row_of(i), 0, third))
    return spec(0), spec(1), spec(2)


def _in_proj_kernel(x_ref, nw_ref, sh_ref, sc_ref, w_ref, cs_ref, gm_ref, wg_ref, wgate_ref,
                    bgate_ref, ltri_ref, utri_ref, rt_ref, ct_ref, p_ref, cum_ref, *, rope):
    tm = x_ref.shape[0]
    hb = _ada_norm(x_ref[...], nw_ref[...], sc_ref[...], sh_ref[...]).astype(BF16)

    g = _dot(hb, wg_ref[...])
    normed = {}
    for sec in range(N_SEC):
        base = sec * SEC
        raw = _dot(hb, w_ref[:, base:base + SEC])
        acc = raw * cs_ref[sec]
        if sec <= SEC_NA_K:
            normed[sec] = (raw, acc)
        elif rope and sec == SEC_RET_QK:
            n_grid_rows = tm // GRID_W

            def table(t):
                by_row = jnp.concatenate(
                    [jnp.broadcast_to(rt_ref[t, g:g + 1, :], (GRID_W, LANES)) for g in range(n_grid_rows)],
                    axis=0)
                return by_row + jnp.concatenate([ct_ref[t]] * n_grid_rows, axis=0)

            cos = table(0)
            sin = table(1)
            for c in range(0, SEC, 2 * LANES):
                u1 = acc[:, c:c + LANES]
                u2 = acc[:, c + LANES:c + 2 * LANES]
                p_ref[:, base + c:base + c + LANES] = (u1 * cos - u2 * sin).astype(BF16)
                p_ref[:, base + c + LANES:base + c + 2 * LANES] = (u1 * sin + u2 * cos).astype(BF16)
        else:
            p_ref[:, base:base + SEC] = acc.astype(BF16)
        if sec == SEC_NA_V:
            xg = _dot(g.astype(BF16), wgate_ref[...]) + bgate_ref[...]
            la = _log_sigmoid(xg) * (1.0 / GLA_GATE_TAU)

    for sec, (raw, acc) in normed.items():
        base = sec * SEC
        for c in range(0, SEC, MXU_DIM):
            sq = raw[:, c:c + MXU_DIM] * raw[:, c:c + MXU_DIM]
            ss = _dot(sq.astype(BF16), gm_ref[...])
            p_ref[:, base + c:base + c + MXU_DIM] = (
                acc[:, c:c + MXU_DIM] * lax.rsqrt(ss * (1.0 / NA_HEAD_DIM) + NORM_EPS)).astype(BF16)

    half = la.shape[1] // 2
    for r in range(tm // TRI_ROWS):
        rows = slice(r * TRI_ROWS, (r + 1) * TRI_ROWS)
        fh, fl = _split2(la[rows, :half])
        bh, bl = _split2(la[rows, half:])
        cum_ref[rows, :half] = _dot(ltri_ref[...], fh) + _dot(ltri_ref[...], fl)
        cum_ref[rows, half:] = _dot(utri_ref[...], bh) + _dot(utri_ref[...], bl)


def _in_proj(x2d, mod, prm, tabs, *, layer, tm, row_of, blocks_per_batch, rope):
    t, d = x2d.shape
    const2 = lambda i: (0, 0)
    lconst = lambda i: (layer, 0, 0)
    row_tab = pl.BlockSpec((2, tm // GRID_W, LANES), lambda i: (0, i % blocks_per_batch, 0))
    col_tab = _resident((2, GRID_W, LANES), lambda i: (0, 0, 0))
    sh_spec, sc_spec, _ = _mod_specs(mod, layer, row_of)
    return pl.pallas_call(
        functools.partial(_in_proj_kernel, rope=rope),
        grid=(t // tm,),
        in_specs=[
            pl.BlockSpec((tm, d), lambda i: (i, 0)),
            pl.BlockSpec((None, 1, d), lconst),
            sh_spec, sc_spec,
            _resident((None, d, N_SEC * SEC), lconst),
            _resident((None, N_SEC, 1, SEC), lambda i: (layer, 0, 0, 0)),
            _resident((MXU_DIM, MXU_DIM), const2),
            _resident((None, d, GATE_PAD), lconst),
            _resident((None, GATE_PAD, SEC), lconst),
            _resident((None, 1, SEC), lconst),
            _resident((TRI_ROWS, TRI_ROWS), const2),
            _resident((TRI_ROWS, TRI_ROWS), const2),
            row_tab, col_tab,
        ],
        out_specs=[
            pl.BlockSpec((tm, N_SEC * SEC), lambda i: (i, 0)),
            pl.BlockSpec((tm, SEC), lambda i: (i, 0)),
        ],
        out_shape=[
            jax.ShapeDtypeStruct((t, N_SEC * SEC), BF16),
            jax.ShapeDtypeStruct((t, SEC), F32),
        ],
        compiler_params=_params("parallel"),
        name="in_proj_rope" if rope else "in_proj",
    )(x2d, prm["nw"], mod, mod, prm["wpack"], prm["colscale"], tabs["gmat"], prm["wg"],
      prm["wgate"], prm["bgate"], tabs["ltri"], tabs["utri"], tabs["rot_rows"], tabs["rot_cols"])


def _na_block_plan(rows):
    n_qb = rows // NA_QROWS
    kr = min(NA_WIN_ROWS, rows)
    plans = []
    for qb in (0, min(1, n_qb - 1), n_qb - 1):
        t0 = int(np.clip(qb - 1, 0, n_qb - NA_KROWS // NA_QROWS))
        plan = []
        for i in range(NA_QROWS):
            r = qb * NA_QROWS + i
            rs = int(np.clip(r - kr // 2, 0, rows - kr))
            plan.append([(t0 * NA_QROWS + j) - r + NA_WIN_ROWS - 1
                         if rs <= t0 * NA_QROWS + j < rs + kr else None
                         for j in range(NA_KROWS)])
        plans.append(plan)
    return plans


def _bias_kernel(rpb_ref, o_ref, *, plans):
    shape = (GRID_W, LANES)
    lane = lax.broadcasted_iota(jnp.int32, shape, 1)
    c = lax.broadcasted_iota(jnp.int32, shape, 0)
    kc = lane & (GRID_W - 1)
    cs = jnp.clip(c - NA_WIN_COLS // 2, 0, GRID_W - NA_WIN_COLS)
    col_ok = jnp.logical_and(kc >= cs, kc < cs + NA_WIN_COLS)
    low = lane < GRID_W
    neg = jnp.full(shape, NEG_BIG, F32)
    toep = []
    for a in range(2 * NA_WIN_ROWS - 1):
        x = jnp.broadcast_to(rpb_ref[a:a + 1, :], shape)
        lo = pltpu.roll(x, LANES - (NA_WIN_COLS - 1), 1, stride=1, stride_axis=0)
        hi = pltpu.roll(x, GRID_W - (NA_WIN_COLS - 1), 1, stride=1, stride_axis=0)
        toep.append(jnp.where(col_ok, jnp.where(low, lo, hi), neg))
    for k, plan in enumerate(plans):
        for i in range(NA_QROWS):
            for jp in range(NA_KROWS // 2):
                a0, a1 = plan[i][2 * jp], plan[i][2 * jp + 1]
                b0 = neg if a0 is None else toep[a0]
                b1 = neg if a1 is None else toep[a1]
                o_ref[k, i * GRID_W:(i + 1) * GRID_W, jp * LANES:(jp + 1) * LANES] = (
                    jnp.where(low, b0, b1))


def _na_bias_tables(rpb, rows):
    depth, heads, nr, nc = rpb.shape
    rp = jnp.pad(rpb, ((0, 0), (0, 0), (0, 16 - nr), (0, LANES - nc)))
    tq, tk = NA_QROWS * GRID_W, NA_KROWS * GRID_W
    return pl.pallas_call(
        functools.partial(_bias_kernel, plans=_na_block_plan(rows)),
        grid=(depth, heads),
        in_specs=[pl.BlockSpec((None, None, 16, LANES), lambda l, h: (l, h, 0, 0))],
        out_specs=pl.BlockSpec((None, 3, None, tq, tk), lambda l, h: (l, 0, h, 0, 0)),
        out_shape=jax.ShapeDtypeStruct((depth, 3, heads, tq, tk), F32),
        compiler_params=_params("parallel", "parallel"),
        name="na_bias",
    )(rp)


def _lane_mask(width, lo, hi):
    lane = lax.broadcasted_iota(jnp.int32, (1, width), 1)
    return jnp.logical_and(lane >= lo, lane < hi)


def _na_kernel(q_ref, k_ref, v_ref, kc_ref, vc_ref, bias_ref, o_ref, *, n_qb, n_sub):
    tq = q_ref.shape[0] // n_sub
    tk = bias_ref.shape[-1]
    width = q_ref.shape[1]
    heads = width // NA_HEAD_DIM
    kc = kc_ref[...]
    vc = vc_ref[...]
    inside = [_lane_mask(width, h * NA_HEAD_DIM, (h + 1) * NA_HEAD_DIM) for h in range(heads)]
    ones_row = [_lane_mask(width, ((h + 1) % heads) * NA_HEAD_DIM, ((h + 1) % heads) * NA_HEAD_DIM + 1)
                for h in range(heads)]
    starts = []
    for sub in range(n_sub):
        qb = pl.program_id(2) * n_sub + sub
        t0 = jnp.clip(qb - 1, 0, n_qb - tk // tq)
        starts.append((qb, pl.multiple_of(t0 * tq, tq)))
    units = [(sub, h) for sub in range(n_sub) for h in range(heads)]
    vc_heads = [vc * inside[h].astype(BF16) + ones_row[h].astype(BF16) for h in range(heads)]

    def scores(sub, h):
        qb, start = starts[sub]
        kind = jnp.where(qb == 0, 0, jnp.where(qb == n_qb - 1, 2, 1))
        qh = q_ref[sub * tq:(sub + 1) * tq, :] * inside[h].astype(BF16)
        s = _dot_nt(qh, k_ref[pl.ds(start, tk), :]) + bias_ref[kind, h]
        return s.astype(BF16), _dot_nt(qh, kc).astype(BF16)

    pending = scores(*units[0])
    acc = None
    for idx, (sub, h) in enumerate(units):
        s, sc = pending
        if idx + 1 < len(units):
            pending = scores(*units[idx + 1])
        keep = inside[h].astype(BF16)
        fill = ones_row[h].astype(BF16)
        m = jnp.maximum(jnp.max(s, axis=-1, keepdims=True), jnp.max(sc, axis=-1, keepdims=True))
        p = jnp.exp(s - m)
        pc = jnp.exp(sc - m)
        v3 = v_ref[pl.ds(starts[sub][1], tk), :]
        oh = _dot(p, v3 * keep + fill) + _dot(pc, vc_heads[h])
        l = jnp.sum(jnp.where(ones_row[h], oh, 0.0), axis=-1, keepdims=True)
        acc = oh * (1.0 / l) if h == 0 else jnp.where(inside[h], oh * (1.0 / l), acc)
        if h == heads - 1:
            o_ref[sub * tq:(sub + 1) * tq, :] = acc.astype(BF16)


def _na_attention(p_lat, p_ctx, bias, *, layer, batch, seq, ctx_len, n_sub):
    tq = NA_QROWS * GRID_W
    tk = NA_KROWS * GRID_W
    n_qb = seq // tq
    n_steps = n_qb // n_sub
    hw = 2 * LANES
    n_hg = BRANCH_W // hw
    sec_blk = SEC // hw
    heads = hw // NA_HEAD_DIM
    return pl.pallas_call(
        functools.partial(_na_kernel, n_qb=n_qb, n_sub=n_sub),
        grid=(batch, n_hg, n_steps),
        in_specs=[
            pl.BlockSpec((n_sub * tq, hw), lambda b, g, i: (b * n_steps + i, SEC_NA_Q * sec_blk + g)),
            pl.BlockSpec((seq, hw), lambda b, g, i: (b, SEC_NA_K * sec_blk + g)),
            pl.BlockSpec((seq, hw), lambda b, g, i: (b, SEC_NA_V * sec_blk + g)),
            pl.BlockSpec((ctx_len, hw), lambda b, g, i: (b, SEC_NA_K * sec_blk + g)),
            pl.BlockSpec((ctx_len, hw), lambda b, g, i: (b, SEC_NA_V * sec_blk + g)),
            pl.BlockSpec((None, bias.shape[1], heads, tq, tk), lambda b, g, i: (layer, 0, g, 0, 0)),
        ],
        out_specs=pl.BlockSpec((n_sub * tq, hw), lambda b, g, i: (b * n_steps + i, g)),
        out_shape=jax.ShapeDtypeStruct((batch * seq, BRANCH_W), BF16),
        compiler_params=_params("parallel", "parallel", "arbitrary"),
        name="na_attention",
    )(p_lat, p_lat, p_lat, p_ctx, p_ctx, bias)


def _ctx_attn_kernel(q_ref, k_ref, v_ref, o_ref):
    first = _lane_mask(LANES, 0, NA_HEAD_DIM)
    sel = (first.astype(BF16), jnp.logical_not(first).astype(BF16))
    outs = []
    for pair in range(q_ref.shape[1] // LANES):
        lanes = slice(pair * LANES, (pair + 1) * LANES)
        q2, k2, v2 = q_ref[:, lanes], k_ref[:, lanes], v_ref[:, lanes]
        o_pair = []
        for hh in range(2):
            s = _dot_nt(q2 * sel[hh], k2)
            m = jnp.max(s, axis=-1, keepdims=True)
            p = jnp.exp(s - m)
            l = jnp.sum(p, axis=-1, keepdims=True)
            o_pair.append(_dot(p.astype(BF16), v2) * (1.0 / l))
        outs.append(jnp.where(first, o_pair[0], o_pair[1]))
    o_ref[...] = jnp.concatenate(outs, axis=1).astype(BF16)


def _ctx_attention(p_ctx, *, batch, ctx_len):
    hw = 2 * LANES
    n_hg = BRANCH_W // hw
    sec_blk = SEC // hw
    spec = lambda sec: pl.BlockSpec((ctx_len, hw), lambda b, g: (b, sec * sec_blk + g))
    return pl.pallas_call(
        _ctx_attn_kernel,
        grid=(batch, n_hg),
        in_specs=[spec(SEC_NA_Q), spec(SEC_NA_K), spec(SEC_NA_V)],
        out_specs=pl.BlockSpec((ctx_len, hw), lambda b, g: (b, g)),
        out_shape=jax.ShapeDtypeStruct((batch * ctx_len, BRANCH_W), BF16),
        compiler_params=_params("parallel", "parallel"),
        name="ctx_attention",
    )(p_ctx, p_ctx, p_ctx)


def _scan_head_of_lane(gla):
    lane = lax.broadcasted_iota(jnp.int32, (1, SCAN_HEADS * SCAN_DK), 1)
    if gla:
        return lax.shift_right_logical(lane, SCAN_DK.bit_length() - 1)
    return lax.shift_right_logical(lane & (LANES - 1), (SCAN_DK // 2).bit_length() - 1)


def _decay_factors(cum, ref_row, total_row):
    cm = cum[ref_row:ref_row + 1, :]
    tot = cum[total_row:total_row + 1, :]
    eq = jnp.exp(cum - cm)
    ek = jnp.exp(cm - cum)
    return eq, ek, jnp.exp(cm), jnp.exp(tot - cm), jnp.exp(tot)


def _chunk_prepare(q, k, v, factors, head_sel):
    eq, ek, _, e_tm, _ = factors
    qtb = (q * eq).astype(BF16)
    kt = k * ek
    khb = (kt * e_tm).astype(BF16)
    hmul = [sel.astype(BF16) for sel in head_sel]
    q_stack = jnp.concatenate([qtb * m for m in hmul], axis=0)
    v_rows = jnp.concatenate([v[:, h * SCAN_DV:(h + 1) * SCAN_DV] for h in range(SCAN_HEADS)], axis=0)
    k_rows = jnp.concatenate([khb * m for m in hmul], axis=0)
    return q_stack, kt.astype(BF16), _dot_tn(v_rows, k_rows)


def _chunk_scores(q_stack, ktb, st, e_m, tri_stack):
    rhs = jnp.concatenate([(st * e_m).astype(BF16), ktb], axis=0)
    both = _dot_nt(q_stack, rhs)
    return both[:, :SCAN_DV], jnp.where(tri_stack, both[:, SCAN_DV:], 0.0).astype(BF16)


def _chunk_output(inter, att, v):
    outs = []
    for h in range(SCAN_HEADS):
        rows = slice(h * CHUNK, (h + 1) * CHUNK)
        outs.append(_dot(att[rows], v[:, h * SCAN_DV:(h + 1) * SCAN_DV]) + inter[rows])
    return jnp.concatenate(outs, axis=1)


def _scan_masks(gla):
    head = _scan_head_of_lane(gla)
    head_sel = [head == h for h in range(SCAN_HEADS)]
    t = lax.broadcasted_iota(jnp.int32, (SCAN_HEADS * CHUNK, CHUNK), 0) & (CHUNK - 1)
    s = lax.broadcasted_iota(jnp.int32, (SCAN_HEADS * CHUNK, CHUNK), 1)
    return head_sel, s <= t, s >= t


def _ret_cum(lg_row, forward):
    t = lax.broadcasted_iota(jnp.int32, (CHUNK, lg_row.shape[1]), 0).astype(F32)
    steps = (t + 1.0) if forward else (float(CHUNK) - t)
    return steps * lg_row


def _scan_block(fwd_refs, bwd_refs, n_chunks, gla, ret_factors, masks):
    head_sel, tril, triu = masks
    hk = SCAN_HEADS * SCAN_DK
    chains = (
        (*fwd_refs, tril, (CHUNK // 2 - 1, CHUNK - 1), 0,
         [slice(c * CHUNK, (c + 1) * CHUNK) for c in range(n_chunks)]),
        (*bwd_refs, triu, (CHUNK // 2, 0), 1,
         [slice(c * CHUNK, (c + 1) * CHUNK) for c in reversed(range(n_chunks))]),
    )
    def prepare(chain):
        qk_ref, v_ref, cum_ref, _, _, _, (ref_row, total_row), direction, order = chain
        items = []
        for rows in order:
            fac = (_decay_factors(cum_ref[rows, :], ref_row, total_row) if gla
                   else ret_factors[direction])
            qk = qk_ref[rows, :].astype(F32)
            items.append((fac, _chunk_prepare(qk[:, :hk], qk[:, hk:], v_ref[rows, :], fac, head_sel)))
        return items

    def score(chain, items):
        st_ref, tri = chain[4], chain[5]
        st = st_ref[...]
        states = []
        for fac, (_, _, inc) in items:
            states.append(st)
            st = st * fac[4] + inc
        st_ref[...] = st
        return [_chunk_scores(q_stack, ktb, st_c, fac[2], tri)
                for st_c, (fac, (q_stack, ktb, _)) in zip(states, items)]

    def output(chain, results):
        v_ref, o_ref, order = chain[1], chain[3], chain[8]
        for rows, (inter, att) in zip(order, results):
            o_ref[rows, :] = _chunk_output(inter, att, v_ref[rows, :]).astype(BF16)

    prepared = [prepare(chain) for chain in chains]
    scored = [score(chain, items) for chain, items in zip(chains, prepared)]
    for chain, results in zip(chains, scored):
        output(chain, results)


def _scan_kernel(*refs, gla):
    if gla:
        (qkf_ref, vf_ref, cf_ref, qkb_ref, vb_ref, cb_ref, qkc_ref, vc_ref, ccf_ref, ccb_ref,
         of_ref, ob_ref, ocf_ref, ocb_ref, stf, stb) = refs
        ret_factors = None
    else:
        (qkf_ref, vf_ref, qkb_ref, vb_ref, qkc_ref, vc_ref, lg_ref,
         of_ref, ob_ref, ocf_ref, ocb_ref, stf, stb) = refs
        cf_ref = cb_ref = ccf_ref = ccb_ref = None
        ret_factors = (_decay_factors(_ret_cum(lg_ref[0:1, :], True), CHUNK // 2 - 1, CHUNK - 1),
                       _decay_factors(_ret_cum(lg_ref[1:2, :], False), CHUNK // 2, 0))
    masks = _scan_masks(gla)

    @pl.when(pl.program_id(1) == 0)
    def _():
        stf[...] = jnp.zeros_like(stf)
        stb[...] = jnp.zeros_like(stb)
        _scan_block((qkc_ref, vc_ref, ccf_ref, ocf_ref, stf), (qkc_ref, vc_ref, ccb_ref, ocb_ref, stb),
                    qkc_ref.shape[0] // CHUNK, gla, ret_factors, masks)

    _scan_block((qkf_ref, vf_ref, cf_ref, of_ref, stf), (qkb_ref, vb_ref, cb_ref, ob_ref, stb),
                qkf_ref.shape[0] // CHUNK, gla, ret_factors, masks)


def _scan(p_lat, p_ctx, cum_lat, cum_ctx, lg, *, gla, batch, seq, ctx_len, tb):
    nb = seq // tb
    hk = SCAN_HEADS * SCAN_DK
    hv = SCAN_HEADS * SCAN_DV
    sec_qk = SEC_GLA_QK if gla else SEC_RET_QK
    sec_v = SEC_GLA_V if gla else SEC_RET_V
    fwd = lambda b, s: b * nb + s
    bwd = lambda b, s: b * nb + nb - 1 - s

    def stream(order):
        specs = [pl.BlockSpec((tb, SEC), lambda b, s: (order(b, s), sec_qk)),
                 pl.BlockSpec((tb, SEC), lambda b, s: (order(b, s), sec_v))]
        return specs, [p_lat, p_lat]

    in_specs, args = stream(fwd)
    if gla:
        in_specs.append(pl.BlockSpec((tb, hk), lambda b, s: (fwd(b, s), 0)))
        args.append(cum_lat)
    specs_b, args_b = stream(bwd)
    in_specs += specs_b
    args += args_b
    if gla:
        in_specs.append(pl.BlockSpec((tb, hk), lambda b, s: (bwd(b, s), 1)))
        args.append(cum_lat)
    in_specs += [pl.BlockSpec((ctx_len, SEC), lambda b, s: (b, sec_qk)),
                 pl.BlockSpec((ctx_len, SEC), lambda b, s: (b, sec_v))]
    args += [p_ctx, p_ctx]
    if gla:
        in_specs += [pl.BlockSpec((ctx_len, hk), lambda b, s: (b, 0)),
                     pl.BlockSpec((ctx_len, hk), lambda b, s: (b, 1))]
        args += [cum_ctx, cum_ctx]
    else:
        in_specs.append(pl.BlockSpec((2, hk), lambda b, s: (0, 0)))
        args.append(lg)
    ctx_out = pl.BlockSpec((ctx_len, hv), lambda b, s: (b, 0))
    o_f, o_b, oc_f, oc_b = pl.pallas_call(
        functools.partial(_scan_kernel, gla=gla),
        grid=(batch, nb),
        in_specs=in_specs,
        out_specs=[
            pl.BlockSpec((tb, hv), lambda b, s: (fwd(b, s), 0)),
            pl.BlockSpec((tb, hv), lambda b, s: (bwd(b, s), 0)),
            ctx_out, ctx_out,
        ],
        out_shape=[
            jax.ShapeDtypeStruct((batch * seq, hv), BF16),
            jax.ShapeDtypeStruct((batch * seq, hv), BF16),
            jax.ShapeDtypeStruct((batch * ctx_len, hv), BF16),
            jax.ShapeDtypeStruct((batch * ctx_len, hv), BF16),
        ],
        scratch_shapes=[pltpu.VMEM((SCAN_DV, hk), F32), pltpu.VMEM((SCAN_DV, hk), F32)],
        compiler_params=_params("parallel", "arbitrary"),
        name="gla_scan" if gla else "ret_scan",
    )(*args)
    return (o_f, o_b), (oc_f, oc_b)


def _head_rms(o, gain):
    parts = []
    for h in range(SCAN_HEADS):
        sl = o[:, h * SCAN_DV:(h + 1) * SCAN_DV]
        ms = jnp.mean(sl * sl, axis=-1, keepdims=True)
        parts.append(sl * lax.rsqrt(ms + NORM_EPS))
    return jnp.concatenate(parts, axis=1) * gain


def _merge_kernel(x_ref, nw_ref, sh_ref, sc_ref, gate_ref, na_ref, gf_ref, gb_ref, rf_ref, rb_ref,
                  gg_ref, rg_ref, wz0_ref, wz1_ref, wz2_ref, wg0_ref, wg1_ref, wg2_ref, wb_ref, wo_ref,
                  o_ref):
    x = x_ref[...]
    wz_refs = (wz0_ref, wz1_ref, wz2_ref)
    wg_refs = (wg0_ref, wg1_ref, wg2_ref)
    hb = _ada_norm(x, nw_ref[...], sc_ref[...], sh_ref[...]).astype(BF16)
    outs = (
        na_ref[...].astype(F32),
        _head_rms(gf_ref[...].astype(F32) + gb_ref[...].astype(F32), gg_ref[...]),
        _head_rms(rf_ref[...].astype(F32) + rb_ref[...].astype(F32), rg_ref[...]),
    )
    zs = [_dot(hb, wz_refs[i][...]) for i in range(N_BRANCH)]
    us = [(outs[i] * _silu(zs[i])).astype(BF16) for i in range(N_BRANCH)]
    gs = [_sigmoid(_dot(hb, wg_refs[i][...])) for i in range(N_BRANCH)]
    y = None
    for i in range(N_BRANCH):
        t = gs[i] * _dot(us[i], wb_ref[i])
        y = t if y is None else y + t
    r = _dot(y.astype(BF16), wo_ref[...])
    o_ref[...] = x + gate_ref[...] * r


def _merge(x2d, mod, prm, na_o, gla_f, gla_b, ret_f, ret_b, *, layer, tm, row_of):
    t, d = x2d.shape
    row = lambda w: pl.BlockSpec((tm, w), lambda i: (i, 0))
    sh_spec, sc_spec, gate_spec = _mod_specs(mod, layer, row_of)
    lconst3 = lambda i: (layer, 0, 0)
    return pl.pallas_call(
        _merge_kernel,
        grid=(t // tm,),
        in_specs=[
            row(d),
            pl.BlockSpec((None, 1, d), lconst3),
            sh_spec, sc_spec, gate_spec,
            row(BRANCH_W), row(BRANCH_W), row(BRANCH_W), row(BRANCH_W), row(BRANCH_W),
            pl.BlockSpec((None, 1, BRANCH_W), lconst3),
            pl.BlockSpec((None, 1, BRANCH_W), lconst3),
            *[_resident((None, d, BRANCH_W), lambda i, n=n: (layer, 0, N_SEC * SEC // BRANCH_W + n))
              for n in range(N_BRANCH)],
            *[_resident((None, d, d), lambda i, n=n: (layer, 0, (N_SEC * SEC + N_BRANCH * BRANCH_W) // d + n))
              for n in range(N_BRANCH)],
            _resident((None, N_BRANCH, BRANCH_W, d), lambda i: (layer, 0, 0, 0)),
            _resident((None, d, d), lconst3),
        ],
        out_specs=row(d),
        out_shape=jax.ShapeDtypeStruct((t, d), F32),
        compiler_params=_params("parallel"),
        name="merge",
    )(x2d, prm["nw"], mod, mod, mod, na_o, gla_f, gla_b, ret_f, ret_b, prm["gla_gain"],
      prm["ret_gain"], *([prm["wpack"]] * (2 * N_BRANCH)), prm["wb"], prm["wo"])


def _in_splits():
    na = NA_HEADS * NA_HEAD_DIM
    qk = SCAN_HEADS * SCAN_DK
    vv = SCAN_HEADS * SCAN_DV
    sizes = (na, na, na, qk, qk, vv, 2 * GLA_GATE_RANK, qk, qk, vv, N_BRANCH * BRANCH_W,
             N_BRANCH * D_MODEL)
    return [int(v) for v in np.concatenate([[0], np.cumsum(sizes)])]


def _rotary_pack_matrix():
    hk = SCAN_HEADS * SCAN_DK
    quarter = SCAN_DK // 4
    p = np.zeros((SEC, SEC), np.float32)
    for blk in range(SEC // hk):
        for h in range(SCAN_HEADS):
            for half in range(2):
                for partner in range(2):
                    for i in range(quarter):
                        old = h * SCAN_DK + half * 2 * quarter + partner * quarter + i
                        new = partner * (hk // 2) + h * 2 * quarter + half * quarter + i
                        p[blk * hk + old, blk * hk + new] = 1.0
    return jnp.asarray(p).astype(BF16)


def _pack_kernel(a_ref, b_ref, perm_ref, o_ref, *, first_shifted, perm_block, shift):
    j = pl.program_id(1)
    sec = a_ref.shape[0]

    def shifted():
        x = jnp.concatenate([a_ref[...], b_ref[...]], axis=0)
        return x[shift:shift + sec]

    @pl.when(j < first_shifted)
    def _():
        o_ref[...] = a_ref[...].T.astype(BF16)

    @pl.when(j == perm_block)
    def _():
        o_ref[...] = _dot(shifted().T.astype(BF16), perm_ref[...]).astype(BF16)

    @pl.when(jnp.logical_and(j >= first_shifted, j != perm_block))
    def _():
        o_ref[...] = shifted().T.astype(BF16)


def _pack_w_in(w_in):
    depth, d, n_in = w_in.shape
    o = _in_splits()
    g0, g1 = o[6], o[7]
    assert g0 % SEC == 0 and (n_in - (g1 - g0)) % SEC == 0
    n_blocks = (n_in - (g1 - g0)) // SEC
    tail_rows = g1 - g0
    w_t = jnp.swapaxes(w_in, 1, 2)
    return pl.pallas_call(
        functools.partial(_pack_kernel, first_shifted=g0 // SEC, perm_block=SEC_RET_QK,
                          shift=g1 - g0),
        grid=(depth, n_blocks),
        in_specs=[
            pl.BlockSpec((None, SEC, d), lambda l, j: (l, j, 0)),
            pl.BlockSpec((None, tail_rows, d), lambda l, j: (l, (j + 1) * (SEC // tail_rows), 0)),
            _resident((SEC, SEC), lambda l, j: (0, 0)),
        ],
        out_specs=pl.BlockSpec((None, d, SEC), lambda l, j: (l, 0, j)),
        out_shape=jax.ShapeDtypeStruct((depth, d, n_blocks * SEC), BF16),
        compiler_params=_params("parallel", "parallel"),
        name="pack_w_in",
    )(w_t, w_t, _rotary_pack_matrix())


def _prep_params(norm_w, w_in, q_norm, k_norm, w_gate, b_gate, gla_out_norm, ret_out_norm,
                 w_branch, w_out):
    depth, d, _ = w_in.shape
    o = _in_splits()
    g0, g1 = o[6], o[7]
    hk = SCAN_HEADS * SCAN_DK
    wg = jnp.pad(w_in[:, :, g0:g1], ((0, 0), (0, 0), (0, GATE_PAD - (g1 - g0)))).astype(BF16)
    wgate = jnp.zeros((depth, GATE_PAD, 2 * hk), F32)
    wgate = wgate.at[:, :GLA_GATE_RANK, :hk].set(w_gate[:, 0])
    wgate = wgate.at[:, GLA_GATE_RANK:2 * GLA_GATE_RANK, hk:].set(w_gate[:, 1])
    bgate = b_gate.reshape(depth, 1, 2 * hk)
    base = np.ones((N_SEC, 1, SEC), np.float32)
    base[SEC_NA_Q] = NA_HEAD_DIM ** -0.5
    base[SEC_GLA_QK, 0, :hk] = SCAN_DK ** -0.5
    base[SEC_RET_QK, 0, :hk] = SCAN_DK ** -0.5
    reps = SEC // NA_HEAD_DIM
    ones = jnp.ones((depth, N_SEC - 2, 1, SEC), F32)
    gains = jnp.concatenate([jnp.tile(q_norm, (1, reps))[:, None, None, :],
                             jnp.tile(k_norm, (1, reps))[:, None, None, :], ones], axis=1)
    return dict(
        nw=norm_w.reshape(depth, 1, d),
        wpack=_pack_w_in(w_in), wg=wg, wgate=wgate.astype(BF16), bgate=bgate,
        colscale=gains * jnp.asarray(base)[None],
        wb=w_branch.astype(BF16), wo=w_out.astype(BF16),
        gla_gain=gla_out_norm.reshape(depth, 1, BRANCH_W),
        ret_gain=ret_out_norm.reshape(depth, 1, BRANCH_W),
    )


def _tables(seq, ctx_rows):
    i = np.arange(TRI_ROWS)
    same = (i[:, None] // CHUNK) == (i[None, :] // CHUNK)
    ltri = jnp.asarray((same & (i[None, :] <= i[:, None])).astype(np.float32)).astype(BF16)
    utri = jnp.asarray((same & (i[None, :] >= i[:, None])).astype(np.float32)).astype(BF16)
    grp = np.arange(MXU_DIM) // NA_HEAD_DIM
    gmat = jnp.asarray((grp[:, None] == grp[None, :]).astype(np.float32)).astype(BF16)
    quarter = SCAN_DK // 4
    n_rows = seq // GRID_W
    inv = ROPE_BASE ** (-jnp.arange(quarter, dtype=F32) / quarter)
    ang_r = jnp.arange(n_rows, dtype=F32)[:, None] * inv[None, :]
    ang_c = jnp.arange(GRID_W, dtype=F32)[:, None] * inv[None, :]
    zr, zc = jnp.zeros_like(ang_r), jnp.zeros_like(ang_c)
    lanes = lambda a, b: jnp.concatenate([a, b] * SCAN_HEADS, axis=1)
    rot_rows = jnp.stack([lanes(jnp.cos(ang_r), zr), lanes(jnp.sin(ang_r), zr)])
    rot_cols = jnp.stack([lanes(zc, jnp.cos(ang_c)), lanes(zc, jnp.sin(ang_c))])
    lat = dict(ltri=ltri, utri=utri, gmat=gmat, rot_rows=rot_rows, rot_cols=rot_cols)
    ctx = dict(ltri=ltri, utri=utri, gmat=gmat, rot_cols=jnp.zeros_like(rot_cols),
               rot_rows=jnp.zeros((2, ctx_rows // GRID_W, LANES), F32))
    return lat, ctx


def _tile_plan(seq):
    tm = 2 * MXU_DIM
    tb_scan = min(seq, 16 * CHUNK)
    na_blocks = min(8, seq // (NA_QROWS * GRID_W))
    assert seq % tm == 0 and seq % tb_scan == 0 and (seq // (NA_QROWS * GRID_W)) % na_blocks == 0
    return 2 * tm, tm, tb_scan, na_blocks


def _log_gamma(offset):
    g = jnp.log1p(-jnp.exp2(-(offset + jnp.arange(SCAN_HEADS, dtype=F32))))
    return jnp.tile(jnp.repeat(g, SCAN_DK // 2), 2)


def kernel(x, c, ctx, c_ctx, w_mod, b_mod, norm_w, w_in, na_q_norm, na_k_norm, na_rpb, gla_w_gate,
           gla_b_gate, gla_out_norm, ret_out_norm, w_branch, w_out):
    batch, seq, d = x.shape
    ctx_len = ctx.shape[1]
    depth = w_mod.shape[0]
    rows = seq // GRID_W
    assert d == D_MODEL and x.dtype == F32 and seq % GRID_W == 0 and rows >= NA_KROWS
    assert ctx_len % TRI_ROWS == 0 and batch + 1 <= 8
    tiles = _tile_plan(seq)
    tm_proj, tm_merge, tb_scan, na_blocks_per_step = tiles

    c_rows = jnp.zeros((8, d), F32).at[:batch].set(c).at[batch].set(c_ctx)
    mod = _modulation(c_rows, w_mod, b_mod).reshape(depth, 8, 1, 3 * d)
    prm = _prep_params(norm_w, w_in, na_q_norm, na_k_norm, gla_w_gate, gla_b_gate, gla_out_norm,
                       ret_out_norm, w_branch, w_out)
    lat_tabs, ctx_tabs = _tables(seq, batch * ctx_len)
    bias = _na_bias_tables(na_rpb, rows)
    lg = jnp.stack([_log_gamma(RET_DECAY_FWD), _log_gamma(RET_DECAY_BWD)])
    ctx_row = lambda i: batch

    xl = x.reshape(batch * seq, d)
    xc = ctx.reshape(batch * ctx_len, d)
    for layer in range(depth):
        with_ctx = layer < depth - 1
        p_lat, cum_lat = _in_proj(xl, mod, prm, lat_tabs, layer=layer, tm=tm_proj,
                                  row_of=lambda i: i // (seq // tm_proj),
                                  blocks_per_batch=seq // tm_proj, rope=True)
        p_ctx, cum_ctx = _in_proj(xc, mod, prm, ctx_tabs, layer=layer, tm=batch * ctx_len,
                                  row_of=ctx_row, blocks_per_batch=1, rope=False)

        na_l = _na_attention(p_lat, p_ctx, bias, layer=layer, batch=batch, seq=seq, ctx_len=ctx_len,
                             n_sub=na_blocks_per_step)

        (gla_f, gla_b), (gla_cf, gla_cb) = _scan(p_lat, p_ctx, cum_lat, cum_ctx, None, gla=True,
                                                 batch=batch, seq=seq, ctx_len=ctx_len, tb=tb_scan)
        (ret_f, ret_b), (ret_cf, ret_cb) = _scan(p_lat, p_ctx, None, None, lg, gla=False,
                                                 batch=batch, seq=seq, ctx_len=ctx_len, tb=tb_scan)

        xl_new = _merge(xl, mod, prm, na_l, gla_f, gla_b, ret_f, ret_b, layer=layer, tm=tm_merge,
                        row_of=lambda i: i // (seq // tm_merge))
        if with_ctx:
            na_c = _ctx_attention(p_ctx, batch=batch, ctx_len=ctx_len)
            xc = _merge(xc, mod, prm, na_c, gla_cf, gla_cb, ret_cf, ret_cb, layer=layer,
                        tm=batch * ctx_len, row_of=ctx_row)
        xl = xl_new
    return xl.reshape(batch, seq, d)
```

```python
import functools

import numpy as np
import jax
import jax.numpy as jnp
from jax import lax
from jax.experimental import pallas as pl
from jax.experimental.pallas import tpu as pltpu

D_MODEL = 1024
GRID_W = 64
NORM_EPS = 1e-6
N_BRANCH = 3
BRANCH_W = D_MODEL // 2
NA_HEAD_DIM = 64
NA_HEADS = BRANCH_W // NA_HEAD_DIM
NA_WIN_ROWS = 8
NA_WIN_COLS = 16
SCAN_HEADS = 4
SCAN_DV = BRANCH_W // SCAN_HEADS
SCAN_DK = SCAN_DV // 2
GLA_GATE_RANK = 16
GLA_GATE_TAU = 16.0
RET_DECAY_FWD = 5.0
RET_DECAY_BWD = 5.5
CHUNK = 64
ROPE_BASE = 10000.0

LANES = 128
MXU_DIM = 256

SEC = 512
SEC_NA_Q, SEC_NA_K, SEC_NA_V, SEC_GLA_QK, SEC_GLA_V, SEC_RET_QK, SEC_RET_V = range(7)
N_SEC = 7
GATE_PAD = LANES
TRI_ROWS = MXU_DIM
NA_QROWS = 4
NA_KROWS = 12
NEG_BIG = -1e30

VMEM_LIMIT = 56 * 1024 * 1024

F32 = jnp.float32
BF16 = jnp.bfloat16


def _dot(a, b):
    return jnp.dot(a, b, preferred_element_type=F32)


def _dot_nt(a, b):
    return lax.dot_general(a, b, (((1,), (1,)), ((), ())), preferred_element_type=F32)


def _dot_tn(a, b):
    return lax.dot_general(a, b, (((0,), (0,)), ((), ())), preferred_element_type=F32)


def _split2(x):
    hi = x.astype(BF16)
    lo = (x - hi.astype(F32)).astype(BF16)
    return hi, lo


def _log_sigmoid(x):
    return jnp.minimum(x, 0.0) - jnp.log(1.0 + jnp.exp(-jnp.abs(x)))


def _sigmoid(x):
    return 1.0 / (1.0 + jnp.exp(-x))


def _silu(x):
    return x * _sigmoid(x)


def _params(*sem):
    return pltpu.CompilerParams(dimension_semantics=sem, vmem_limit_bytes=VMEM_LIMIT)


def _resident(block_shape, index_map):
    return pl.BlockSpec(block_shape, index_map, pipeline_mode=pl.Buffered(1))


def _ada_norm(x, nw, sc, sh):
    ms = jnp.mean(x * x, axis=-1, keepdims=True)
    return (x * lax.rsqrt(ms + NORM_EPS)) * (nw * (1.0 + sc)) + sh


def _mod_kernel(c_ref, w_ref, b_ref, o_ref):
    a = _silu(c_ref[...])
    o_ref[...] = jnp.dot(a, w_ref[...], preferred_element_type=F32,
                         precision=lax.Precision.HIGHEST) + b_ref[...]


def _modulation(c_rows, w_mod, b_mod):
    depth, d, d3 = w_mod.shape
    rows = c_rows.shape[0]
    tn = d
    return pl.pallas_call(
        _mod_kernel,
        grid=(depth, d3 // tn),
        in_specs=[
            pl.BlockSpec((rows, d), lambda l, j: (0, 0)),
            pl.BlockSpec((None, d, tn), lambda l, j: (l, 0, j)),
            pl.BlockSpec((None, 1, tn), lambda l, j: (l, 0, j)),
        ],
        out_specs=pl.BlockSpec((None, rows, tn), lambda l, j: (l, 0, j)),
        out_shape=jax.ShapeDtypeStruct((depth, rows, d3), F32),
        compiler_params=_params("parallel", "parallel"),
        name="modulation",
    )(c_rows, w_mod, b_mod.reshape(depth, 1, d3))


def _mod_specs(mod, layer, row_of):
    d = mod.shape[-1] // 3
    spec = lambda third: pl.BlockSpec((None, None, 1, d), lambda i: (layer, row_of(i), 0, third))
    return spec(0), spec(1), spec(2)


def _in_proj_kernel(x_ref, nw_ref, sh_ref, sc_ref, w_ref, cs_ref, gm_ref, wg_ref, wgate_ref,
                    bgate_ref, ltri_ref, utri_ref, rt_ref, ct_ref, p_ref, cum_ref, *, rope):
    tm = x_ref.shape[0]
    hb = _ada_norm(x_ref[...], nw_ref[...], sc_ref[...], sh_ref[...]).astype(BF16)

    g = _dot(hb, wg_ref[...])
    normed = {}
    for sec in range(N_SEC):
        base = sec * SEC
        raw = _dot(hb, w_ref[:, base:base + SEC])
        acc = raw * cs_ref[sec]
        if sec <= SEC_NA_K:
            normed[sec] = (raw, acc)
        elif rope and sec == SEC_RET_QK:
            n_grid_rows = tm // GRID_W

            def table(t):
                by_row = jnp.concatenate(
                    [jnp.broadcast_to(rt_ref[t, g:g + 1, :], (GRID_W, LANES)) for g in range(n_grid_rows)],
                    axis=0)
                return by_row + jnp.concatenate([ct_ref[t]] * n_grid_rows, axis=0)

            cos = table(0)
            sin = table(1)
            for c in range(0, SEC, 2 * LANES):
                u1 = acc[:, c:c + LANES]
                u2 = acc[:, c + LANES:c + 2 * LANES]
                p_ref[:, base + c:base + c + LANES] = (u1 * cos - u2 * sin).astype(BF16)
                p_ref[:, base + c + LANES:base + c + 2 * LANES] = (u1 * sin + u2 * cos).astype(BF16)
        else:
            p_ref[:, base:base + SEC] = acc.astype(BF16)
        if sec == SEC_NA_V:
            xg = _dot(g.astype(BF16), wgate_ref[...]) + bgate_ref[...]
            la = _log_sigmoid(xg) * (1.0 / GLA_GATE_TAU)

    for sec, (raw, acc) in normed.items():
        base = sec * SEC
        for c in range(0, SEC, MXU_DIM):
            sq = raw[:, c:c + MXU_DIM] * raw[:, c:c + MXU_DIM]
            ss = _dot(sq.astype(BF16), gm_ref[...])
            p_ref[:, base + c:base + c + MXU_DIM] = (
                acc[:, c:c + MXU_DIM] * lax.rsqrt(ss * (1.0 / NA_HEAD_DIM) + NORM_EPS)).astype(BF16)

    half = la.shape[1] // 2
    for r in range(tm // TRI_ROWS):
        rows = slice(r * TRI_ROWS, (r + 1) * TRI_ROWS)
        fh, fl = _split2(la[rows, :half])
        bh, bl = _split2(la[rows, half:])
        cum_ref[rows, :half] = _dot(ltri_ref[...], fh) + _dot(ltri_ref[...], fl)
        cum_ref[rows, half:] = _dot(utri_ref[...], bh) + _dot(utri_ref[...], bl)


def _in_proj(x2d, mod, prm, tabs, *, layer, tm, row_of, blocks_per_batch, rope):
    t, d = x2d.shape
    const2 = lambda i: (0, 0)
    lconst = lambda i: (layer, 0, 0)
    row_tab = pl.BlockSpec((2, tm // GRID_W, LANES), lambda i: (0, i % blocks_per_batch, 0))
    col_tab = _resident((2, GRID_W, LANES), lambda i: (0, 0, 0))
    sh_spec, sc_spec, _ = _mod_specs(mod, layer, row_of)
    return pl.pallas_call(
        functools.partial(_in_proj_kernel, rope=rope),
        grid=(t // tm,),
        in_specs=[
            pl.BlockSpec((tm, d), lambda i: (i, 0)),
            pl.BlockSpec((None, 1, d), lconst),
            sh_spec, sc_spec,
            _resident((None, d, N_SEC * SEC), lconst),
            _resident((None, N_SEC, 1, SEC), lambda i: (layer, 0, 0, 0)),
            _resident((MXU_DIM, MXU_DIM), const2),
            _resident((None, d, GATE_PAD), lconst),
            _resident((None, GATE_PAD, SEC), lconst),
            _resident((None, 1, SEC), lconst),
            _resident((TRI_ROWS, TRI_ROWS), const2),
            _resident((TRI_ROWS, TRI_ROWS), const2),
            row_tab, col_tab,
        ],
        out_specs=[
            pl.BlockSpec((tm, N_SEC * SEC), lambda i: (i, 0)),
            pl.BlockSpec((tm, SEC), lambda i: (i, 0)),
        ],
        out_shape=[
            jax.ShapeDtypeStruct((t, N_SEC * SEC), BF16),
            jax.ShapeDtypeStruct((t, SEC), F32),
        ],
        compiler_params=_params("parallel"),
        name="in_proj_rope" if rope else "in_proj",
    )(x2d, prm["nw"], mod, mod, prm["wpack"], prm["colscale"], tabs["gmat"], prm["wg"],
      prm["wgate"], prm["bgate"], tabs["ltri"], tabs["utri"], tabs["rot_rows"], tabs["rot_cols"])


def _na_block_plan(rows):
    n_qb = rows // NA_QROWS
    kr = min(NA_WIN_ROWS, rows)
    plans = []
    for qb in (0, min(1, n_qb - 1), n_qb - 1):
        t0 = int(np.clip(qb - 1, 0, n_qb - NA_KROWS // NA_QROWS))
        plan = []
        for i in range(NA_QROWS):
            r = qb * NA_QROWS + i
            rs = int(np.clip(r - kr // 2, 0, rows - kr))
            plan.append([(t0 * NA_QROWS + j) - r + NA_WIN_ROWS - 1
                         if rs <= t0 * NA_QROWS + j < rs + kr else None
                         for j in range(NA_KROWS)])
        plans.append(plan)
    return plans


def _bias_kernel(rpb_ref, o_ref, *, plans):
    shape = (GRID_W, LANES)
    lane = lax.broadcasted_iota(jnp.int32, shape, 1)
    c = lax.broadcasted_iota(jnp.int32, shape, 0)
    kc = lane & (GRID_W - 1)
    cs = jnp.clip(c - NA_WIN_COLS // 2, 0, GRID_W - NA_WIN_COLS)
    col_ok = jnp.logical_and(kc >= cs, kc < cs + NA_WIN_COLS)
    low = lane < GRID_W
    neg = jnp.full(shape, NEG_BIG, F32)
    toep = []
    for a in range(2 * NA_WIN_ROWS - 1):
        x = jnp.broadcast_to(rpb_ref[a:a + 1, :], shape)
        lo = pltpu.roll(x, LANES - (NA_WIN_COLS - 1), 1, stride=1, stride_axis=0)
        hi = pltpu.roll(x, GRID_W - (NA_WIN_COLS - 1), 1, stride=1, stride_axis=0)
        toep.append(jnp.where(col_ok, jnp.where(low, lo, hi), neg))
    for k, plan in enumerate(plans):
        for i in range(NA_QROWS):
            for jp in range(NA_KROWS // 2):
                a0, a1 = plan[i][2 * jp], plan[i][2 * jp + 1]
                b0 = neg if a0 is None else toep[a0]
                b1 = neg if a1 is None else toep[a1]
                o_ref[k, i * GRID_W:(i + 1) * GRID_W, jp * LANES:(jp + 1) * LANES] = (
                    jnp.where(low, b0, b1))


def _na_bias_tables(rpb, rows):
    depth, heads, nr, nc = rpb.shape
    rp = jnp.pad(rpb, ((0, 0), (0, 0), (0, 16 - nr), (0, LANES - nc)))
    tq, tk = NA_QROWS * GRID_W, NA_KROWS * GRID_W
    return pl.pallas_call(
        functools.partial(_bias_kernel, plans=_na_block_plan(rows)),
        grid=(depth, heads),
        in_specs=[pl.BlockSpec((None, None, 16, LANES), lambda l, h: (l, h, 0, 0))],
        out_specs=pl.BlockSpec((None, 3, None, tq, tk), lambda l, h: (l, 0, h, 0, 0)),
        out_shape=jax.ShapeDtypeStruct((depth, 3, heads, tq, tk), F32),
        compiler_params=_params("parallel", "parallel"),
        name="na_bias",
    )(rp)


def _lane_mask(width, lo, hi):
    lane = lax.broadcasted_iota(jnp.int32, (1, width), 1)
    return jnp.logical_and(lane >= lo, lane < hi)


def _na_kernel(q_ref, k_ref, v_ref, kc_ref, vc_ref, bias_ref, o_ref, *, n_qb, n_sub):
    tq = q_ref.shape[0] // n_sub
    tk = bias_ref.shape[-1]
    width = q_ref.shape[1]
    heads = width // NA_HEAD_DIM
    kc = kc_ref[...]
    vc = vc_ref[...]
    inside = [_lane_mask(width, h * NA_HEAD_DIM, (h + 1) * NA_HEAD_DIM) for h in range(heads)]
    ones_row = [_lane_mask(width, ((h + 1) % heads) * NA_HEAD_DIM, ((h + 1) % heads) * NA_HEAD_DIM + 1)
                for h in range(heads)]
    starts = []
    for sub in range(n_sub):
        qb = pl.program_id(2) * n_sub + sub
        t0 = jnp.clip(qb - 1, 0, n_qb - tk // tq)
        starts.append((qb, pl.multiple_of(t0 * tq, tq)))
    units = [(sub, h) for sub in range(n_sub) for h in range(heads)]
    vc_heads = [vc * inside[h].astype(BF16) + ones_row[h].astype(BF16) for h in range(heads)]

    def scores(sub, h):
        qb, start = starts[sub]
        kind = jnp.where(qb == 0, 0, jnp.where(qb == n_qb - 1, 2, 1))
        qh = q_ref[sub * tq:(sub + 1) * tq, :] * inside[h].astype(BF16)
        s = _dot_nt(qh, k_ref[pl.ds(start, tk), :]) + bias_ref[kind, h]
        return s.astype(BF16), _dot_nt(qh, kc).astype(BF16)

    pending = scores(*units[0])
    acc = None
    for idx, (sub, h) in enumerate(units):
        s, sc = pending
        if idx + 1 < len(units):
            pending = scores(*units[idx + 1])
        keep = inside[h].astype(BF16)
        fill = ones_row[h].astype(BF16)
        m = jnp.maximum(jnp.max(s, axis=-1, keepdims=True), jnp.max(sc, axis=-1, keepdims=True))
        p = jnp.exp(s - m)
        pc = jnp.exp(sc - m)
        v3 = v_ref[pl.ds(starts[sub][1], tk), :]
        oh = _dot(p, v3 * keep + fill) + _dot(pc, vc_heads[h])
        l = jnp.sum(jnp.where(ones_row[h], oh, 0.0), axis=-1, keepdims=True)
        acc = oh * (1.0 / l) if h == 0 else jnp.where(inside[h], oh * (1.0 / l), acc)
        if h == heads - 1:
            o_ref[sub * tq:(sub + 1) * tq, :] = acc.astype(BF16)


def _na_attention(p_lat, p_ctx, bias, *, layer, batch, seq, ctx_len, n_sub):
    tq = NA_QROWS * GRID_W
    tk = NA_KROWS * GRID_W
    n_qb = seq // tq
    n_steps = n_qb // n_sub
    hw = 2 * LANES
    n_hg = BRANCH_W // hw
    sec_blk = SEC // hw
    heads = hw // NA_HEAD_DIM
    return pl.pallas_call(
        functools.partial(_na_kernel, n_qb=n_qb, n_sub=n_sub),
        grid=(batch, n_hg, n_steps),
        in_specs=[
            pl.BlockSpec((n_sub * tq, hw), lambda b, g, i: (b * n_steps + i, SEC_NA_Q * sec_blk + g)),
            pl.BlockSpec((seq, hw), lambda b, g, i: (b, SEC_NA_K * sec_blk + g)),
            pl.BlockSpec((seq, hw), lambda b, g, i: (b, SEC_NA_V * sec_blk + g)),
            pl.BlockSpec((ctx_len, hw), lambda b, g, i: (b, SEC_NA_K * sec_blk + g)),
            pl.BlockSpec((ctx_len, hw), lambda b, g, i: (b, SEC_NA_V * sec_blk + g)),
            pl.BlockSpec((None, bias.shape[1], heads, tq, tk), lambda b, g, i: (layer, 0, g, 0, 0)),
        ],
        out_specs=pl.BlockSpec((n_sub * tq, hw), lambda b, g, i: (b * n_steps + i, g)),
        out_shape=jax.ShapeDtypeStruct((batch * seq, BRANCH_W), BF16),
        compiler_params=_params("parallel", "parallel", "arbitrary"),
        name="na_attention",
    )(p_lat, p_lat, p_lat, p_ctx, p_ctx, bias)


def _ctx_attn_kernel(q_ref, k_ref, v_ref, o_ref):
    first = _lane_mask(LANES, 0, NA_HEAD_DIM)
    sel = (first.astype(BF16), jnp.logical_not(first).astype(BF16))
    outs = []
    for pair in range(q_ref.shape[1] // LANES):
        lanes = slice(pair * LANES, (pair + 1) * LANES)
        q2, k2, v2 = q_ref[:, lanes], k_ref[:, lanes], v_ref[:, lanes]
        o_pair = []
        for hh in range(2):
            s = _dot_nt(q2 * sel[hh], k2)
            m = jnp.max(s, axis=-1, keepdims=True)
            p = jnp.exp(s - m)
            l = jnp.sum(p, axis=-1, keepdims=True)
            o_pair.append(_dot(p.astype(BF16), v2) * (1.0 / l))
        outs.append(jnp.where(first, o_pair[0], o_pair[1]))
    o_ref[...] = jnp.concatenate(outs, axis=1).astype(BF16)


def _ctx_attention(p_ctx, *, batch, ctx_len):
    hw = 2 * LANES
    n_hg = BRANCH_W // hw
    sec_blk = SEC // hw
    spec = lambda sec: pl.BlockSpec((ctx_len, hw), lambda b, g: (b, sec * sec_blk + g))
    return pl.pallas_call(
        _ctx_attn_kernel,
        grid=(batch, n_hg),
        in_specs=[spec(SEC_NA_Q), spec(SEC_NA_K), spec(SEC_NA_V)],
        out_specs=pl.BlockSpec((ctx_len, hw), lambda b, g: (b, g)),
        out_shape=jax.ShapeDtypeStruct((batch * ctx_len, BRANCH_W), BF16),
        compiler_params=_params("parallel", "parallel"),
        name="ctx_attention",
    )(p_ctx, p_ctx, p_ctx)


def _scan_head_of_lane(gla):
    lane = lax.broadcasted_iota(jnp.int32, (1, SCAN_HEADS * SCAN_DK), 1)
    if gla:
        return lax.shift_right_logical(lane, SCAN_DK.bit_length() - 1)
    return lax.shift_right_logical(lane & (LANES - 1), (SCAN_DK // 2).bit_length() - 1)


def _decay_factors(cum, ref_row, total_row):
    cm = cum[ref_row:ref_row + 1, :]
    tot = cum[total_row:total_row + 1, :]
    eq = jnp.exp(cum - cm)
    ek = jnp.exp(cm - cum)
    return eq, ek, jnp.exp(cm), jnp.exp(tot - cm), jnp.exp(tot)


def _chunk_prepare(q, k, v, factors, head_sel):
    eq, ek, _, e_tm, _ = factors
    qtb = (q * eq).astype(BF16)
    kt = k * ek
    khb = (kt * e_tm).astype(BF16)
    hmul = [sel.astype(BF16) for sel in head_sel]
    q_stack = jnp.concatenate([qtb * m for m in hmul], axis=0)
    v_rows = jnp.concatenate([v[:, h * SCAN_DV:(h + 1) * SCAN_DV] for h in range(SCAN_HEADS)], axis=0)
    k_rows = jnp.concatenate([khb * m for m in hmul], axis=0)
    return q_stack, kt.astype(BF16), _dot_tn(v_rows, k_rows)


def _chunk_scores(q_stack, ktb, st, e_m, tri_stack):
    rhs = jnp.concatenate([(st * e_m).astype(BF16), ktb], axis=0)
    both = _dot_nt(q_stack, rhs)
    return both[:, :SCAN_DV], jnp.where(tri_stack, both[:, SCAN_DV:], 0.0).astype(BF16)


def _chunk_output(inter, att, v):
    outs = []
    for h in range(SCAN_HEADS):
        rows = slice(h * CHUNK, (h + 1) * CHUNK)
        outs.append(_dot(att[rows], v[:, h * SCAN_DV:(h + 1) * SCAN_DV]) + inter[rows])
    return jnp.concatenate(outs, axis=1)


def _scan_masks(gla):
    head = _scan_head_of_lane(gla)
    head_sel = [head == h for h in range(SCAN_HEADS)]
    t = lax.broadcasted_iota(jnp.int32, (SCAN_HEADS * CHUNK, CHUNK), 0) & (CHUNK - 1)
    s = lax.broadcasted_iota(jnp.int32, (SCAN_HEADS * CHUNK, CHUNK), 1)
    return head_sel, s <= t, s >= t


def _ret_cum(lg_row, forward):
    t = lax.broadcasted_iota(jnp.int32, (CHUNK, lg_row.shape[1]), 0).astype(F32)
    steps = (t + 1.0) if forward else (float(CHUNK) - t)
    return steps * lg_row


def _scan_block(fwd_refs, bwd_refs, n_chunks, gla, ret_factors, masks):
    head_sel, tril, triu = masks
    hk = SCAN_HEADS * SCAN_DK
    chains = (
        (*fwd_refs, tril, (CHUNK // 2 - 1, CHUNK - 1), 0,
         [slice(c * CHUNK, (c + 1) * CHUNK) for c in range(n_chunks)]),
        (*bwd_refs, triu, (CHUNK // 2, 0), 1,
         [slice(c * CHUNK, (c + 1) * CHUNK) for c in reversed(range(n_chunks))]),
    )
    def prepare(chain):
        qk_ref, v_ref, cum_ref, _, _, _, (ref_row, total_row), direction, order = chain
        items = []
        for rows in order:
            fac = (_decay_factors(cum_ref[rows, :], ref_row, total_row) if gla
                   else ret_factors[direction])
            qk = qk_ref[rows, :].astype(F32)
            items.append((fac, _chunk_prepare(qk[:, :hk], qk[:, hk:], v_ref[rows, :], fac, head_sel)))
        return items

    def score(chain, items):
        st_ref, tri = chain[4], chain[5]
        st = st_ref[...]
        states = []
        for fac, (_, _, inc) in items:
            states.append(st)
            st = st * fac[4] + inc
        st_ref[...] = st
        return [_chunk_scores(q_stack, ktb, st_c, fac[2], tri)
                for st_c, (fac, (q_stack, ktb, _)) in zip(states, items)]

    def output(chain, results):
        v_ref, o_ref, order = chain[1], chain[3], chain[8]
        for rows, (inter, att) in zip(order, results):
            o_ref[rows, :] = _chunk_output(inter, att, v_ref[rows, :]).astype(BF16)

    prepared = [prepare(chain) for chain in chains]
    scored = [score(chain, items) for chain, items in zip(chains, prepared)]
    for chain, results in zip(chains, scored):
        output(chain, results)


def _scan_kernel(*refs, gla):
    if gla:
        (qkf_ref, vf_ref, cf_ref, qkb_ref, vb_ref, cb_ref, qkc_ref, vc_ref, ccf_ref, ccb_ref,
         of_ref, ob_ref, ocf_ref, ocb_ref, stf, stb) = refs
        ret_factors = None
    else:
        (qkf_ref, vf_ref, qkb_ref, vb_ref, qkc_ref, vc_ref, lg_ref,
         of_ref, ob_ref, ocf_ref, ocb_ref, stf, stb) = refs
        cf_ref = cb_ref = ccf_ref = ccb_ref = None
        ret_factors = (_decay_factors(_ret_cum(lg_ref[0:1, :], True), CHUNK // 2 - 1, CHUNK - 1),
                       _decay_factors(_ret_cum(lg_ref[1:2, :], False), CHUNK // 2, 0))
    masks = _scan_masks(gla)

    @pl.when(pl.program_id(1) == 0)
    def _():
        stf[...] = jnp.zeros_like(stf)
        stb[...] = jnp.zeros_like(stb)
        _scan_block((qkc_ref, vc_ref, ccf_ref, ocf_ref, stf), (qkc_ref, vc_ref, ccb_ref, ocb_ref, stb),
                    qkc_ref.shape[0] // CHUNK, gla, ret_factors, masks)

    _scan_block((qkf_ref, vf_ref, cf_ref, of_ref, stf), (qkb_ref, vb_ref, cb_ref, ob_ref, stb),
                qkf_ref.shape[0] // CHUNK, gla, ret_factors, masks)


def _scan(p_lat, p_ctx, cum_lat, cum_ctx, lg, *, gla, batch, seq, ctx_len, tb):
    nb = seq // tb
    hk = SCAN_HEADS * SCAN_DK
    hv = SCAN_HEADS * SCAN_DV
    sec_qk = SEC_GLA_QK if gla else SEC_RET_QK
    sec_v = SEC_GLA_V if gla else SEC_RET_V
    fwd = lambda b, s: b * nb + s
    bwd = lambda b, s: b * nb + nb - 1 - s

    def stream(order):
        specs = [pl.BlockSpec((tb, SEC), lambda b, s: (order(b, s), sec_qk)),
                 pl.BlockSpec((tb, SEC), lambda b, s: (order(b, s), sec_v))]
        return specs, [p_lat, p_lat]

    in_specs, args = stream(fwd)
    if gla:
        in_specs.append(pl.BlockSpec((tb, hk), lambda b, s: (fwd(b, s), 0)))
        args.append(cum_lat)
    specs_b, args_b = stream(bwd)
    in_specs += specs_b
    args += args_b
    if gla:
        in_specs.append(pl.BlockSpec((tb, hk), lambda b, s: (bwd(b, s), 1)))
        args.append(cum_lat)
    in_specs += [pl.BlockSpec((ctx_len, SEC), lambda b, s: (b, sec_qk)),
                 pl.BlockSpec((ctx_len, SEC), lambda b, s: (b, sec_v))]
    args += [p_ctx, p_ctx]
    if gla:
        in_specs += [pl.BlockSpec((ctx_len, hk), lambda b, s: (b, 0)),
                     pl.BlockSpec((ctx_len, hk), lambda b, s: (b, 1))]
        args += [cum_ctx, cum_ctx]
    else:
        in_specs.append(pl.BlockSpec((2, hk), lambda b, s: (0, 0)))
        args.append(lg)
    ctx_out = pl.BlockSpec((ctx_len, hv), lambda b, s: (b, 0))
    o_f, o_b, oc_f, oc_b = pl.pallas_call(
        functools.partial(_scan_kernel, gla=gla),
        grid=(batch, nb),
        in_specs=in_specs,
        out_specs=[
            pl.BlockSpec((tb, hv), lambda b, s: (fwd(b, s), 0)),
            pl.BlockSpec((tb, hv), lambda b, s: (bwd(b, s), 0)),
            ctx_out, ctx_out,
        ],
        out_shape=[
            jax.ShapeDtypeStruct((batch * seq, hv), BF16),
            jax.ShapeDtypeStruct((batch * seq, hv), BF16),
            jax.ShapeDtypeStruct((batch * ctx_len, hv), BF16),
            jax.ShapeDtypeStruct((batch * ctx_len, hv), BF16),
        ],
        scratch_shapes=[pltpu.VMEM((SCAN_DV, hk), F32), pltpu.VMEM((SCAN_DV, hk), F32)],
        compiler_params=_params("parallel", "arbitrary"),
        name="gla_scan" if gla else "ret_scan",
    )(*args)
    return (o_f, o_b), (oc_f, oc_b)


def _head_rms(o, gain):
    parts = []
    for h in range(SCAN_HEADS):
        sl = o[:, h * SCAN_DV:(h + 1) * SCAN_DV]
        ms = jnp.mean(sl * sl, axis=-1, keepdims=True)
        parts.append(sl * lax.rsqrt(ms + NORM_EPS))
    return jnp.concatenate(parts, axis=1) * gain


def _merge_kernel(x_ref, nw_ref, sh_ref, sc_ref, gate_ref, na_ref, gf_ref, gb_ref, rf_ref, rb_ref,
                  gg_ref, rg_ref, wz0_ref, wz1_ref, wz2_ref, wg0_ref, wg1_ref, wg2_ref, wb_ref, wo_ref,
                  o_ref):
    x = x_ref[...]
    wz_refs = (wz0_ref, wz1_ref, wz2_ref)
    wg_refs = (wg0_ref, wg1_ref, wg2_ref)
    hb = _ada_norm(x, nw_ref[...], sc_ref[...], sh_ref[...]).astype(BF16)
    outs = (
        na_ref[...].astype(F32),
        _head_rms(gf_ref[...].astype(F32) + gb_ref[...].astype(F32), gg_ref[...]),
        _head_rms(rf_ref[...].astype(F32) + rb_ref[...].astype(F32), rg_ref[...]),
    )
    zs = [_dot(hb, wz_refs[i][...]) for i in range(N_BRANCH)]
    us = [(outs[i] * _silu(zs[i])).astype(BF16) for i in range(N_BRANCH)]
    gs = [_sigmoid(_dot(hb, wg_refs[i][...])) for i in range(N_BRANCH)]
    y = None
    for i in range(N_BRANCH):
        t = gs[i] * _dot(us[i], wb_ref[i])
        y = t if y is None else y + t
    r = _dot(y.astype(BF16), wo_ref[...])
    o_ref[...] = x + gate_ref[...] * r


def _merge(x2d, mod, prm, na_o, gla_f, gla_b, ret_f, ret_b, *, layer, tm, row_of):
    t, d = x2d.shape
    row = lambda w: pl.BlockSpec((tm, w), lambda i: (i, 0))
    sh_spec, sc_spec, gate_spec = _mod_specs(mod, layer, row_of)
    lconst3 = lambda i: (layer, 0, 0)
    return pl.pallas_call(
        _merge_kernel,
        grid=(t // tm,),
        in_specs=[
            row(d),
            pl.BlockSpec((None, 1, d), lconst3),
            sh_spec, sc_spec, gate_spec,
            row(BRANCH_W), row(BRANCH_W), row(BRANCH_W), row(BRANCH_W), row(BRANCH_W),
            pl.BlockSpec((None, 1, BRANCH_W), lconst3),
            pl.BlockSpec((None, 1, BRANCH_W), lconst3),
            *[_resident((None, d, BRANCH_W), lambda i, n=n: (layer, 0, N_SEC * SEC // BRANCH_W + n))
              for n in range(N_BRANCH)],
            *[_resident((None, d, d), lambda i, n=n: (layer, 0, (N_SEC * SEC + N_BRANCH * BRANCH_W) // d + n))
              for n in range(N_BRANCH)],
            _resident((None, N_BRANCH, BRANCH_W, d), lambda i: (layer, 0, 0, 0)),
            _resident((None, d, d), lconst3),
        ],
        out_specs=row(d),
        out_shape=jax.ShapeDtypeStruct((t, d), F32),
        compiler_params=_params("parallel"),
        name="merge",
    )(x2d, prm["nw"], mod, mod, mod, na_o, gla_f, gla_b, ret_f, ret_b, prm["gla_gain"],
      prm["ret_gain"], *([prm["wpack"]] * (2 * N_BRANCH)), prm["wb"], prm["wo"])


def _in_splits():
    na = NA_HEADS * NA_HEAD_DIM
    qk = SCAN_HEADS * SCAN_DK
    vv = SCAN_HEADS * SCAN_DV
    sizes = (na, na, na, qk, qk, vv, 2 * GLA_GATE_RANK, qk, qk, vv, N_BRANCH * BRANCH_W,
             N_BRANCH * D_MODEL)
    return [int(v) for v in np.concatenate([[0], np.cumsum(sizes)])]


def _rotary_pack_matrix():
    hk = SCAN_HEADS * SCAN_DK
    quarter = SCAN_DK // 4
    p = np.zeros((SEC, SEC), np.float32)
    for blk in range(SEC // hk):
        for h in range(SCAN_HEADS):
            for half in range(2):
                for partner in range(2):
                    for i in range(quarter):
                        old = h * SCAN_DK + half * 2 * quarter + partner * quarter + i
                        new = partner * (hk // 2) + h * 2 * quarter + half * quarter + i
                        p[blk * hk + old, blk * hk + new] = 1.0
    return jnp.asarray(p).astype(BF16)


def _pack_kernel(a_ref, b_ref, perm_ref, o_ref, *, first_shifted, perm_block, shift):
    j = pl.program_id(1)
    sec = a_ref.shape[0]

    def shifted():
        x = jnp.concatenate([a_ref[...], b_ref[...]], axis=0)
        return x[shift:shift + sec]

    @pl.when(j < first_shifted)
    def _():
        o_ref[...] = a_ref[...].T.astype(BF16)

    @pl.when(j == perm_block)
    def _():
        o_ref[...] = _dot(shifted().T.astype(BF16), perm_ref[...]).astype(BF16)

    @pl.when(jnp.logical_and(j >= first_shifted, j != perm_block))
    def _():
        o_ref[...] = shifted().T.astype(BF16)


def _pack_w_in(w_in):
    depth, d, n_in = w_in.shape
    o = _in_splits()
    g0, g1 = o[6], o[7]
    assert g0 % SEC == 0 and (n_in - (g1 - g0)) % SEC == 0
    n_blocks = (n_in - (g1 - g0)) // SEC
    tail_rows = g1 - g0
    w_t = jnp.swapaxes(w_in, 1, 2)
    return pl.pallas_call(
        functools.partial(_pack_kernel, first_shifted=g0 // SEC, perm_block=SEC_RET_QK,
                          shift=g1 - g0),
        grid=(depth, n_blocks),
        in_specs=[
            pl.BlockSpec((None, SEC, d), lambda l, j: (l, j, 0)),
            pl.BlockSpec((None, tail_rows, d), lambda l, j: (l, (j + 1) * (SEC // tail_rows), 0)),
            _resident((SEC, SEC), lambda l, j: (0, 0)),
        ],
        out_specs=pl.BlockSpec((None, d, SEC), lambda l, j: (l, 0, j)),
        out_shape=jax.ShapeDtypeStruct((depth, d, n_blocks * SEC), BF16),
        compiler_params=_params("parallel", "parallel"),
        name="pack_w_in",
    )(w_t, w_t, _rotary_pack_matrix())


def _prep_params(norm_w, w_in, q_norm, k_norm, w_gate, b_gate, gla_out_norm, ret_out_norm,
                 w_branch, w_out):
    depth, d, _ = w_in.shape
    o = _in_splits()
    g0, g1 = o[6], o[7]
    hk = SCAN_HEADS * SCAN_DK
    wg = jnp.pad(w_in[:, :, g0:g1], ((0, 0), (0, 0), (0, GATE_PAD - (g1 - g0)))).astype(BF16)
    wgate = jnp.zeros((depth, GATE_PAD, 2 * hk), F32)
    wgate = wgate.at[:, :GLA_GATE_RANK, :hk].set(w_gate[:, 0])
    wgate = wgate.at[:, GLA_GATE_RANK:2 * GLA_GATE_RANK, hk:].set(w_gate[:, 1])
    bgate = b_gate.reshape(depth, 1, 2 * hk)
    base = np.ones((N_SEC, 1, SEC), np.float32)
    base[SEC_NA_Q] = NA_HEAD_DIM ** -0.5
    base[SEC_GLA_QK, 0, :hk] = SCAN_DK ** -0.5
    base[SEC_RET_QK, 0, :hk] = SCAN_DK ** -0.5
    reps = SEC // NA_HEAD_DIM
    ones = jnp.ones((depth, N_SEC - 2, 1, SEC), F32)
    gains = jnp.concatenate([jnp.tile(q_norm, (1, reps))[:, None, None, :],
                             jnp.tile(k_norm, (1, reps))[:, None, None, :], ones], axis=1)
    return dict(
        nw=norm_w.reshape(depth, 1, d),
        wpack=_pack_w_in(w_in), wg=wg, wgate=wgate.astype(BF16), bgate=bgate,
        colscale=gains * jnp.asarray(base)[None],
        wb=w_branch.astype(BF16), wo=w_out.astype(BF16),
        gla_gain=gla_out_norm.reshape(depth, 1, BRANCH_W),
        ret_gain=ret_out_norm.reshape(depth, 1, BRANCH_W),
    )


def _tables(seq, ctx_rows):
    i = np.arange(TRI_ROWS)
    same = (i[:, None] // CHUNK) == (i[None, :] // CHUNK)
    ltri = jnp.asarray((same & (i[None, :] <= i[:, None])).astype(np.float32)).astype(BF16)
    utri = jnp.asarray((same & (i[None, :] >= i[:, None])).astype(np.float32)).astype(BF16)
    grp = np.arange(MXU_DIM) // NA_HEAD_DIM
    gmat = jnp.asarray((grp[:, None] == grp[None, :]).astype(np.float32)).astype(BF16)
    quarter = SCAN_DK // 4
    n_rows = seq // GRID_W
    inv = ROPE_BASE ** (-jnp.arange(quarter, dtype=F32) / quarter)
    ang_r = jnp.arange(n_rows, dtype=F32)[:, None] * inv[None, :]
    ang_c = jnp.arange(GRID_W, dtype=F32)[:, None] * inv[None, :]
    zr, zc = jnp.zeros_like(ang_r), jnp.zeros_like(ang_c)
    lanes = lambda a, b: jnp.concatenate([a, b] * SCAN_HEADS, axis=1)
    rot_rows = jnp.stack([lanes(jnp.cos(ang_r), zr), lanes(jnp.sin(ang_r), zr)])
    rot_cols = jnp.stack([lanes(zc, jnp.cos(ang_c)), lanes(zc, jnp.sin(ang_c))])
    lat = dict(ltri=ltri, utri=utri, gmat=gmat, rot_rows=rot_rows, rot_cols=rot_cols)
    ctx = dict(ltri=ltri, utri=utri, gmat=gmat, rot_cols=jnp.zeros_like(rot_cols),
               rot_rows=jnp.zeros((2, ctx_rows // GRID_W, LANES), F32))
    return lat, ctx


def _tile_plan(seq):
    tm = 2 * MXU_DIM
    tb_scan = min(seq, 32 * CHUNK)
    na_blocks = min(8, seq // (NA_QROWS * GRID_W))
    assert seq % tm == 0 and seq % tb_scan == 0 and (seq // (NA_QROWS * GRID_W)) % na_blocks == 0
    return 2 * tm, tm, tb_scan, na_blocks


def _log_gamma(offset):
    g = jnp.log1p(-jnp.exp2(-(offset + jnp.arange(SCAN_HEADS, dtype=F32))))
    return jnp.tile(jnp.repeat(g, SCAN_DK // 2), 2)


def kernel(x, c, ctx, c_ctx, w_mod, b_mod, norm_w, w_in, na_q_norm, na_k_norm, na_rpb, gla_w_gate,
           gla_b_gate, gla_out_norm, ret_out_norm, w_branch, w_out):
    batch, seq, d = x.shape
    ctx_len = ctx.shape[1]
    depth = w_mod.shape[0]
    rows = seq // GRID_W
    assert d == D_MODEL and x.dtype == F32 and seq % GRID_W == 0 and rows >= NA_KROWS
    assert ctx_len % TRI_ROWS == 0 and batch + 1 <= 8
    tiles = _tile_plan(seq)
    tm_proj, tm_merge, tb_scan, na_blocks_per_step = tiles

    c_rows = jnp.zeros((8, d), F32).at[:batch].set(c).at[batch].set(c_ctx)
    mod = _modulation(c_rows, w_mod, b_mod).reshape(depth, 8, 1, 3 * d)
    prm = _prep_params(norm_w, w_in, na_q_norm, na_k_norm, gla_w_gate, gla_b_gate, gla_out_norm,
                       ret_out_norm, w_branch, w_out)
    lat_tabs, ctx_tabs = _tables(seq, batch * ctx_len)
    bias = _na_bias_tables(na_rpb, rows)
    lg = jnp.stack([_log_gamma(RET_DECAY_FWD), _log_gamma(RET_DECAY_BWD)])
    ctx_row = lambda i: batch

    xl = x.reshape(batch * seq, d)
    xc = ctx.reshape(batch * ctx_len, d)
    for layer in range(depth):
        with_ctx = layer < depth - 1
        p_lat, cum_lat = _in_proj(xl, mod, prm, lat_tabs, layer=layer, tm=tm_proj,
                                  row_of=lambda i: i // (seq // tm_proj),
                                  blocks_per_batch=seq // tm_proj, rope=True)
        p_ctx, cum_ctx = _in_proj(xc, mod, prm, ctx_tabs, layer=layer, tm=batch * ctx_len,
                                  row_of=ctx_row, blocks_per_batch=1, rope=False)

        na_l = _na_attention(p_lat, p_ctx, bias, layer=layer, batch=batch, seq=seq, ctx_len=ctx_len,
                             n_sub=na_blocks_per_step)

        (gla_f, gla_b), (gla_cf, gla_cb) = _scan(p_lat, p_ctx, cum_lat, cum_ctx, None, gla=True,
                                                 batch=batch, seq=seq, ctx_len=ctx_len, tb=tb_scan)
        (ret_f, ret_b), (ret_cf, ret_cb) = _scan(p_lat, p_ctx, None, None, lg, gla=False,
                                                 batch=batch, seq=seq, ctx_len=ctx_len, tb=tb_scan)

        xl_new = _merge(xl, mod, prm, na_l, gla_f, gla_b, ret_f, ret_b, layer=layer, tm=tm_merge,
                        row_of=lambda i: i // (seq // tm_merge))
        if with_ctx:
            na_c = _ctx_attention(p_ctx, batch=batch, ctx_len=ctx_len)
            xc = _merge(xc, mod, prm, na_c, gla_cf, gla_cb, ret_cf, ret_cb, layer=layer,
                        tm=batch * ctx_len, row_of=ctx_row)
        xl = xl_new
    return xl.reshape(batch, seq, d)
```

```python
import functools

import numpy as np
import jax
import jax.numpy as jnp
from jax import lax
from jax.experimental import pallas as pl
from jax.experimental.pallas import tpu as pltpu

D_MODEL = 1024
GRID_W = 64
NORM_EPS = 1e-6
N_BRANCH = 3
BRANCH_W = D_MODEL // 2
NA_HEAD_DIM = 64
NA_HEADS = BRANCH_W // NA_HEAD_DIM
NA_WIN_ROWS = 8
NA_WIN_COLS = 16
SCAN_HEADS = 4
SCAN_DV = BRANCH_W // SCAN_HEADS
SCAN_DK = SCAN_DV // 2
GLA_GATE_RANK = 16
GLA_GATE_TAU = 16.0
RET_DECAY_FWD = 5.0
RET_DECAY_BWD = 5.5
CHUNK = 64
ROPE_BASE = 10000.0

LANES = 128
MXU_DIM = 256

SEC = 512
SEC_NA_Q, SEC_NA_K, SEC_NA_V, SEC_GLA_QK, SEC_GLA_V, SEC_RET_QK, SEC_RET_V = range(7)
N_SEC = 7
GATE_PAD = LANES
TRI_ROWS = MXU_DIM
NA_QROWS = 4
NA_KROWS = 12
NEG_BIG = -1e30

VMEM_LIMIT = 56 * 1024 * 1024

F32 = jnp.float32
BF16 = jnp.bfloat16


def _dot(a, b):
    return jnp.dot(a, b, preferred_element_type=F32)


def _dot_nt(a, b):
    return lax.dot_general(a, b, (((1,), (1,)), ((), ())), preferred_element_type=F32)


def _dot_tn(a, b):
    return lax.dot_general(a, b, (((0,), (0,)), ((), ())), preferred_element_type=F32)


def _split2(x):
    hi = x.astype(BF16)
    lo = (x - hi.astype(F32)).astype(BF16)
    return hi, lo


def _log_sigmoid(x):
    return jnp.minimum(x, 0.0) - jnp.log(1.0 + jnp.exp(-jnp.abs(x)))


def _sigmoid(x):
    return 1.0 / (1.0 + jnp.exp(-x))


def _silu(x):
    return x * _sigmoid(x)


def _params(*sem):
    return pltpu.CompilerParams(dimension_semantics=sem, vmem_limit_bytes=VMEM_LIMIT)


def _resident(block_shape, index_map):
    return pl.BlockSpec(block_shape, index_map, pipeline_mode=pl.Buffered(1))


def _ada_norm(x, nw, sc, sh):
    ms = jnp.mean(x * x, axis=-1, keepdims=True)
    return (x * lax.rsqrt(ms + NORM_EPS)) * (nw * (1.0 + sc)) + sh


def _mod_kernel(c_ref, w_ref, b_ref, o_ref):
    a = _silu(c_ref[...])
    o_ref[...] = jnp.dot(a, w_ref[...], preferred_element_type=F32,
                         precision=lax.Precision.HIGHEST) + b_ref[...]


def _modulation(c_rows, w_mod, b_mod):
    depth, d, d3 = w_mod.shape
    rows = c_rows.shape[0]
    tn = d
    return pl.pallas_call(
        _mod_kernel,
        grid=(depth, d3 // tn),
        in_specs=[
            pl.BlockSpec((rows, d), lambda l, j: (0, 0)),
            pl.BlockSpec((None, d, tn), lambda l, j: (l, 0, j)),
            pl.BlockSpec((None, 1, tn), lambda l, j: (l, 0, j)),
        ],
        out_specs=pl.BlockSpec((None, rows, tn), lambda l, j: (l, 0, j)),
        out_shape=jax.ShapeDtypeStruct((depth, rows, d3), F32),
        compiler_params=_params("parallel", "parallel"),
        name="modulation",
    )(c_rows, w_mod, b_mod.reshape(depth, 1, d3))


def _mod_specs(mod, layer, row_of):
    d = mod.shape[-1] // 3
    spec = lambda third: pl.BlockSpec((None, None, 1, d), lambda i: (layer, row_of(i), 0, third))
    return spec(0), spec(1), spec(2)


def _in_proj_kernel(x_ref, nw_ref, sh_ref, sc_ref, w_ref, cs_ref, gm_ref, wg_ref, wgate_ref,
                    bgate_ref, ltri_ref, utri_ref, rt_ref, ct_ref, p_ref, cum_ref, *, rope):
    tm = x_ref.shape[0]
    hb = _ada_norm(x_ref[...], nw_ref[...], sc_ref[...], sh_ref[...]).astype(BF16)

    g = _dot(hb, wg_ref[...])
    normed = {}
    for sec in range(N_SEC):
        base = sec * SEC
        raw = _dot(hb, w_ref[:, base:base + SEC])
        acc = raw * cs_ref[sec]
        if sec <= SEC_NA_K:
            normed[sec] = (raw, acc)
        elif rope and sec == SEC_RET_QK:
            n_grid_rows = tm // GRID_W

            def table(t):
                by_row = jnp.concatenate(
                    [jnp.broadcast_to(rt_ref[t, g:g + 1, :], (GRID_W, LANES)) for g in range(n_grid_rows)],
                    axis=0)
                return by_row + jnp.concatenate([ct_ref[t]] * n_grid_rows, axis=0)

            cos = table(0)
            sin = table(1)
            for c in range(0, SEC, 2 * LANES):
                u1 = acc[:, c:c + LANES]
                u2 = acc[:, c + LANES:c + 2 * LANES]
                p_ref[:, base + c:base + c + LANES] = (u1 * cos - u2 * sin).astype(BF16)
                p_ref[:, base + c + LANES:base + c + 2 * LANES] = (u1 * sin + u2 * cos).astype(BF16)
        else:
            p_ref[:, base:base + SEC] = acc.astype(BF16)
        if sec == SEC_NA_V:
            xg = _dot(g.astype(BF16), wgate_ref[...]) + bgate_ref[...]
            la = _log_sigmoid(xg) * (1.0 / GLA_GATE_TAU)

    for sec, (raw, acc) in normed.items():
        base = sec * SEC
        for c in range(0, SEC, MXU_DIM):
            sq = raw[:, c:c + MXU_DIM] * raw[:, c:c + MXU_DIM]
            ss = _dot(sq.astype(BF16), gm_ref[...])
            p_ref[:, base + c:base + c + MXU_DIM] = (
                acc[:, c:c + MXU_DIM] * lax.rsqrt(ss * (1.0 / NA_HEAD_DIM) + NORM_EPS)).astype(BF16)

    half = la.shape[1] // 2
    for r in range(tm // TRI_ROWS):
        rows = slice(r * TRI_ROWS, (r + 1) * TRI_ROWS)
        fh, fl = _split2(la[rows, :half])
        bh, bl = _split2(la[rows, half:])
        cum_ref[rows, :half] = _dot(ltri_ref[...], fh) + _dot(ltri_ref[...], fl)
        cum_ref[rows, half:] = _dot(utri_ref[...], bh) + _dot(utri_ref[...], bl)


def _in_proj(x2d, mod, prm, tabs, *, layer, tm, row_of, blocks_per_batch, rope):
    t, d = x2d.shape
    const2 = lambda i: (0, 0)
    lconst = lambda i: (layer, 0, 0)
    row_tab = pl.BlockSpec((2, tm // GRID_W, LANES), lambda i: (0, i % blocks_per_batch, 0))
    col_tab = _resident((2, GRID_W, LANES), lambda i: (0, 0, 0))
    sh_spec, sc_spec, _ = _mod_specs(mod, layer, row_of)
    return pl.pallas_call(
        functools.partial(_in_proj_kernel, rope=rope),
        grid=(t // tm,),
        in_specs=[
            pl.BlockSpec((tm, d), lambda i: (i, 0)),
            pl.BlockSpec((None, 1, d), lconst),
            sh_spec, sc_spec,
            _resident((None, d, N_SEC * SEC), lconst),
            _resident((None, N_SEC, 1, SEC), lambda i: (layer, 0, 0, 0)),
            _resident((MXU_DIM, MXU_DIM), const2),
            _resident((None, d, GATE_PAD), lconst),
            _resident((None, GATE_PAD, SEC), lconst),
            _resident((None, 1, SEC), lconst),
            _resident((TRI_ROWS, TRI_ROWS), const2),
            _resident((TRI_ROWS, TRI_ROWS), const2),
            row_tab, col_tab,
        ],
        out_specs=[
            pl.BlockSpec((tm, N_SEC * SEC), lambda i: (i, 0)),
            pl.BlockSpec((tm, SEC), lambda i: (i, 0)),
        ],
        out_shape=[
            jax.ShapeDtypeStruct((t, N_SEC * SEC), BF16),
            jax.ShapeDtypeStruct((t, SEC), F32),
        ],
        compiler_params=_params("parallel"),
        name="in_proj_rope" if rope else "in_proj",
    )(x2d, prm["nw"], mod, mod, prm["wpack"], prm["colscale"], tabs["gmat"], prm["wg"],
      prm["wgate"], prm["bgate"], tabs["ltri"], tabs["utri"], tabs["rot_rows"], tabs["rot_cols"])


def _na_block_plan(rows):
    n_qb = rows // NA_QROWS
    kr = min(NA_WIN_ROWS, rows)
    plans = []
    for qb in (0, min(1, n_qb - 1), n_qb - 1):
        t0 = int(np.clip(qb - 1, 0, n_qb - NA_KROWS // NA_QROWS))
        plan = []
        for i in range(NA_QROWS):
            r = qb * NA_QROWS + i
            rs = int(np.clip(r - kr // 2, 0, rows - kr))
            plan.append([(t0 * NA_QROWS + j) - r + NA_WIN_ROWS - 1
                         if rs <= t0 * NA_QROWS + j < rs + kr else None
                         for j in range(NA_KROWS)])
        plans.append(plan)
    return plans


def _bias_kernel(rpb_ref, o_ref, *, plans):
    shape = (GRID_W, LANES)
    lane = lax.broadcasted_iota(jnp.int32, shape, 1)
    c = lax.broadcasted_iota(jnp.int32, shape, 0)
    kc = lane & (GRID_W - 1)
    cs = jnp.clip(c - NA_WIN_COLS // 2, 0, GRID_W - NA_WIN_COLS)
    col_ok = jnp.logical_and(kc >= cs, kc < cs + NA_WIN_COLS)
    low = lane < GRID_W
    neg = jnp.full(shape, NEG_BIG, F32)
    toep = []
    for a in range(2 * NA_WIN_ROWS - 1):
        x = jnp.broadcast_to(rpb_ref[a:a + 1, :], shape)
        lo = pltpu.roll(x, LANES - (NA_WIN_COLS - 1), 1, stride=1, stride_axis=0)
        hi = pltpu.roll(x, GRID_W - (NA_WIN_COLS - 1), 1, stride=1, stride_axis=0)
        toep.append(jnp.where(col_ok, jnp.where(low, lo, hi), neg))
    for k, plan in enumerate(plans):
        for i in range(NA_QROWS):
            for jp in range(NA_KROWS // 2):
                a0, a1 = plan[i][2 * jp], plan[i][2 * jp + 1]
                b0 = neg if a0 is None else toep[a0]
                b1 = neg if a1 is None else toep[a1]
                o_ref[k, i * GRID_W:(i + 1) * GRID_W, jp * LANES:(jp + 1) * LANES] = (
                    jnp.where(low, b0, b1))


def _na_bias_tables(rpb, rows):
    depth, heads, nr, nc = rpb.shape
    rp = jnp.pad(rpb, ((0, 0), (0, 0), (0, 16 - nr), (0, LANES - nc)))
    tq, tk = NA_QROWS * GRID_W, NA_KROWS * GRID_W
    return pl.pallas_call(
        functools.partial(_bias_kernel, plans=_na_block_plan(rows)),
        grid=(depth, heads),
        in_specs=[pl.BlockSpec((None, None, 16, LANES), lambda l, h: (l, h, 0, 0))],
        out_specs=pl.BlockSpec((None, 3, None, tq, tk), lambda l, h: (l, 0, h, 0, 0)),
        out_shape=jax.ShapeDtypeStruct((depth, 3, heads, tq, tk), F32),
        compiler_params=_params("parallel", "parallel"),
        name="na_bias",
    )(rp)


def _lane_mask(width, lo, hi):
    lane = lax.broadcasted_iota(jnp.int32, (1, width), 1)
    return jnp.logical_and(lane >= lo, lane < hi)


def _na_kernel(q_ref, k_ref, v_ref, kc_ref, vc_ref, bias_ref, o_ref, *, n_qb, n_sub):
    tq = q_ref.shape[0] // n_sub
    tk = bias_ref.shape[-1]
    width = q_ref.shape[1]
    heads = width // NA_HEAD_DIM
    kc = kc_ref[...]
    vc = vc_ref[...]
    inside = [_lane_mask(width, h * NA_HEAD_DIM, (h + 1) * NA_HEAD_DIM) for h in range(heads)]
    ones_row = [_lane_mask(width, ((h + 1) % heads) * NA_HEAD_DIM, ((h + 1) % heads) * NA_HEAD_DIM + 1)
                for h in range(heads)]
    starts = []
    for sub in range(n_sub):
        qb = pl.program_id(2) * n_sub + sub
        t0 = jnp.clip(qb - 1, 0, n_qb - tk // tq)
        starts.append((qb, pl.multiple_of(t0 * tq, tq)))
    units = [(sub, h) for sub in range(n_sub) for h in range(heads)]
    vc_heads = [vc * inside[h].astype(BF16) + ones_row[h].astype(BF16) for h in range(heads)]

    def scores(sub, h):
        qb, start = starts[sub]
        kind = jnp.where(qb == 0, 0, jnp.where(qb == n_qb - 1, 2, 1))
        qh = q_ref[sub * tq:(sub + 1) * tq, :] * inside[h].astype(BF16)
        s = _dot_nt(qh, k_ref[pl.ds(start, tk), :]) + bias_ref[kind, h]
        return s.astype(BF16), _dot_nt(qh, kc).astype(BF16)

    pending = scores(*units[0])
    acc = None
    for idx, (sub, h) in enumerate(units):
        s, sc = pending
        if idx + 1 < len(units):
            pending = scores(*units[idx + 1])
        keep = inside[h].astype(BF16)
        fill = ones_row[h].astype(BF16)
        m = jnp.maximum(jnp.max(s, axis=-1, keepdims=True), jnp.max(sc, axis=-1, keepdims=True))
        p = jnp.exp(s - m)
        pc = jnp.exp(sc - m)
        v3 = v_ref[pl.ds(starts[sub][1], tk), :]
        oh = _dot(p, v3 * keep + fill) + _dot(pc, vc_heads[h])
        l = jnp.sum(jnp.where(ones_row[h], oh, 0.0), axis=-1, keepdims=True)
        acc = oh * (1.0 / l) if h == 0 else jnp.where(inside[h], oh * (1.0 / l), acc)
        if h == heads - 1:
            o_ref[sub * tq:(sub + 1) * tq, :] = acc.astype(BF16)


def _na_attention(p_lat, p_ctx, bias, *, layer, batch, seq, ctx_len, n_sub):
    tq = NA_QROWS * GRID_W
    tk = NA_KROWS * GRID_W
    n_qb = seq // tq
    n_steps = n_qb // n_sub
    hw = 2 * LANES
    n_hg = BRANCH_W // hw
    sec_blk = SEC // hw
    heads = hw // NA_HEAD_DIM
    return pl.pallas_call(
        functools.partial(_na_kernel, n_qb=n_qb, n_sub=n_sub),
        grid=(batch, n_hg, n_steps),
        in_specs=[
            pl.BlockSpec((n_sub * tq, hw), lambda b, g, i: (b * n_steps + i, SEC_NA_Q * sec_blk + g)),
            pl.BlockSpec((seq, hw), lambda b, g, i: (b, SEC_NA_K * sec_blk + g)),
            pl.BlockSpec((seq, hw), lambda b, g, i: (b, SEC_NA_V * sec_blk + g)),
            pl.BlockSpec((ctx_len, hw), lambda b, g, i: (b, SEC_NA_K * sec_blk + g)),
            pl.BlockSpec((ctx_len, hw), lambda b, g, i: (b, SEC_NA_V * sec_blk + g)),
            _resident((None, bias.shape[1], heads, tq, tk), lambda b, g, i: (layer, 0, g, 0, 0)),
        ],
        out_specs=pl.BlockSpec((n_sub * tq, hw), lambda b, g, i: (b * n_steps + i, g)),
        out_shape=jax.ShapeDtypeStruct((batch * seq, BRANCH_W), BF16),
        compiler_params=_params("parallel", "parallel", "arbitrary"),
        name="na_attention",
    )(p_lat, p_lat, p_lat, p_ctx, p_ctx, bias)


def _ctx_attn_kernel(q_ref, k_ref, v_ref, o_ref):
    first = _lane_mask(LANES, 0, NA_HEAD_DIM)
    sel = (first.astype(BF16), jnp.logical_not(first).astype(BF16))
    outs = []
    for pair in range(q_ref.shape[1] // LANES):
        lanes = slice(pair * LANES, (pair + 1) * LANES)
        q2, k2, v2 = q_ref[:, lanes], k_ref[:, lanes], v_ref[:, lanes]
        o_pair = []
        for hh in range(2):
            s = _dot_nt(q2 * sel[hh], k2)
            m = jnp.max(s, axis=-1, keepdims=True)
            p = jnp.exp(s - m)
            l = jnp.sum(p, axis=-1, keepdims=True)
            o_pair.append(_dot(p.astype(BF16), v2) * (1.0 / l))
        outs.append(jnp.where(first, o_pair[0], o_pair[1]))
    o_ref[...] = jnp.concatenate(outs, axis=1).astype(BF16)


def _ctx_attention(p_ctx, *, batch, ctx_len):
    hw = 2 * LANES
    n_hg = BRANCH_W // hw
    sec_blk = SEC // hw
    spec = lambda sec: pl.BlockSpec((ctx_len, hw), lambda b, g: (b, sec * sec_blk + g))
    return pl.pallas_call(
        _ctx_attn_kernel,
        grid=(batch, n_hg),
        in_specs=[spec(SEC_NA_Q), spec(SEC_NA_K), spec(SEC_NA_V)],
        out_specs=pl.BlockSpec((ctx_len, hw), lambda b, g: (b, g)),
        out_shape=jax.ShapeDtypeStruct((batch * ctx_len, BRANCH_W), BF16),
        compiler_params=_params("parallel", "parallel"),
        name="ctx_attention",
    )(p_ctx, p_ctx, p_ctx)


def _scan_head_of_lane(gla):
    lane = lax.broadcasted_iota(jnp.int32, (1, SCAN_HEADS * SCAN_DK), 1)
    if gla:
        return lax.shift_right_logical(lane, SCAN_DK.bit_length() - 1)
    return lax.shift_right_logical(lane & (LANES - 1), (SCAN_DK // 2).bit_length() - 1)


def _decay_factors(cum, ref_row, total_row):
    cm = cum[ref_row:ref_row + 1, :]
    tot = cum[total_row:total_row + 1, :]
    eq = jnp.exp(cum - cm)
    ek = jnp.exp(cm - cum)
    return eq, ek, jnp.exp(cm), jnp.exp(tot - cm), jnp.exp(tot)


def _chunk_prepare(q, k, v, factors, head_sel):
    eq, ek, _, e_tm, _ = factors
    qtb = (q * eq).astype(BF16)
    kt = k * ek
    khb = (kt * e_tm).astype(BF16)
    hmul = [sel.astype(BF16) for sel in head_sel]
    q_stack = jnp.concatenate([qtb * m for m in hmul], axis=0)
    v_rows = jnp.concatenate([v[:, h * SCAN_DV:(h + 1) * SCAN_DV] for h in range(SCAN_HEADS)], axis=0)
    k_rows = jnp.concatenate([khb * m for m in hmul], axis=0)
    return q_stack, kt.astype(BF16), _dot_tn(v_rows, k_rows)


def _chunk_scores(q_stack, ktb, st, e_m, tri_stack):
    rhs = jnp.concatenate([(st * e_m).astype(BF16), ktb], axis=0)
    both = _dot_nt(q_stack, rhs)
    return both[:, :SCAN_DV], jnp.where(tri_stack, both[:, SCAN_DV:], 0.0).astype(BF16)


def _chunk_output(inter, att, v):
    outs = []
    for h in range(SCAN_HEADS):
        rows = slice(h * CHUNK, (h + 1) * CHUNK)
        outs.append(_dot(att[rows], v[:, h * SCAN_DV:(h + 1) * SCAN_DV]) + inter[rows])
    return jnp.concatenate(outs, axis=1)


def _scan_masks(gla):
    head = _scan_head_of_lane(gla)
    head_sel = [head == h for h in range(SCAN_HEADS)]
    t = lax.broadcasted_iota(jnp.int32, (SCAN_HEADS * CHUNK, CHUNK), 0) & (CHUNK - 1)
    s = lax.broadcasted_iota(jnp.int32, (SCAN_HEADS * CHUNK, CHUNK), 1)
    return head_sel, s <= t, s >= t


def _ret_cum(lg_row, forward):
    t = lax.broadcasted_iota(jnp.int32, (CHUNK, lg_row.shape[1]), 0).astype(F32)
    steps = (t + 1.0) if forward else (float(CHUNK) - t)
    return steps * lg_row


def _scan_block(fwd_refs, bwd_refs, n_chunks, gla, ret_factors, masks):
    head_sel, tril, triu = masks
    hk = SCAN_HEADS * SCAN_DK
    chains = (
        (*fwd_refs, tril, (CHUNK // 2 - 1, CHUNK - 1), 0,
         [slice(c * CHUNK, (c + 1) * CHUNK) for c in range(n_chunks)]),
        (*bwd_refs, triu, (CHUNK // 2, 0), 1,
         [slice(c * CHUNK, (c + 1) * CHUNK) for c in reversed(range(n_chunks))]),
    )
    def prepare(chain):
        qk_ref, v_ref, cum_ref, _, _, _, (ref_row, total_row), direction, order = chain
        items = []
        for rows in order:
            fac = (_decay_factors(cum_ref[rows, :], ref_row, total_row) if gla
                   else ret_factors[direction])
            qk = qk_ref[rows, :].astype(F32)
            items.append((fac, _chunk_prepare(qk[:, :hk], qk[:, hk:], v_ref[rows, :], fac, head_sel)))
        return items

    def score(chain, items):
        st_ref, tri = chain[4], chain[5]
        st = st_ref[...]
        states = []
        for fac, (_, _, inc) in items:
            states.append(st)
            st = st * fac[4] + inc
        st_ref[...] = st
        return [_chunk_scores(q_stack, ktb, st_c, fac[2], tri)
                for st_c, (fac, (q_stack, ktb, _)) in zip(states, items)]

    def output(chain, results):
        v_ref, o_ref, order = chain[1], chain[3], chain[8]
        for rows, (inter, att) in zip(order, results):
            o_ref[rows, :] = _chunk_output(inter, att, v_ref[rows, :]).astype(BF16)

    prepared = [prepare(chain) for chain in chains]
    scored = [score(chain, items) for chain, items in zip(chains, prepared)]
    for chain, results in zip(chains, scored):
        output(chain, results)


def _scan_kernel(*refs, gla):
    if gla:
        (qkf_ref, vf_ref, cf_ref, qkb_ref, vb_ref, cb_ref, qkc_ref, vc_ref, ccf_ref, ccb_ref,
         of_ref, ob_ref, ocf_ref, ocb_ref, stf, stb) = refs
        ret_factors = None
    else:
        (qkf_ref, vf_ref, qkb_ref, vb_ref, qkc_ref, vc_ref, lg_ref,
         of_ref, ob_ref, ocf_ref, ocb_ref, stf, stb) = refs
        cf_ref = cb_ref = ccf_ref = ccb_ref = None
        ret_factors = (_decay_factors(_ret_cum(lg_ref[0:1, :], True), CHUNK // 2 - 1, CHUNK - 1),
                       _decay_factors(_ret_cum(lg_ref[1:2, :], False), CHUNK // 2, 0))
    masks = _scan_masks(gla)

    @pl.when(pl.program_id(1) == 0)
    def _():
        stf[...] = jnp.zeros_like(stf)
        stb[...] = jnp.zeros_like(stb)
        _scan_block((qkc_ref, vc_ref, ccf_ref, ocf_ref, stf), (qkc_ref, vc_ref, ccb_ref, ocb_ref, stb),
                    qkc_ref.shape[0] // CHUNK, gla, ret_factors, masks)

    _scan_block((qkf_ref, vf_ref, cf_ref, of_ref, stf), (qkb_ref, vb_ref, cb_ref, ob_ref, stb),
                qkf_ref.shape[0] // CHUNK, gla, ret_factors, masks)


def _scan(p_lat, p_ctx, cum_lat, cum_ctx, lg, *, gla, batch, seq, ctx_len, tb):
    nb = seq // tb
    hk = SCAN_HEADS * SCAN_DK
    hv = SCAN_HEADS * SCAN_DV
    sec_qk = SEC_GLA_QK if gla else SEC_RET_QK
    sec_v = SEC_GLA_V if gla else SEC_RET_V
    fwd = lambda b, s: b * nb + s
    bwd = lambda b, s: b * nb + nb - 1 - s

    def stream(order):
        specs = [pl.BlockSpec((tb, SEC), lambda b, s: (order(b, s), sec_qk)),
                 pl.BlockSpec((tb, SEC), lambda b, s: (order(b, s), sec_v))]
        return specs, [p_lat, p_lat]

    in_specs, args = stream(fwd)
    if gla:
        in_specs.append(pl.BlockSpec((tb, hk), lambda b, s: (fwd(b, s), 0)))
        args.append(cum_lat)
    specs_b, args_b = stream(bwd)
    in_specs += specs_b
    args += args_b
    if gla:
        in_specs.append(pl.BlockSpec((tb, hk), lambda b, s: (bwd(b, s), 1)))
        args.append(cum_lat)
    in_specs += [pl.BlockSpec((ctx_len, SEC), lambda b, s: (b, sec_qk)),
                 pl.BlockSpec((ctx_len, SEC), lambda b, s: (b, sec_v))]
    args += [p_ctx, p_ctx]
    if gla:
        in_specs += [pl.BlockSpec((ctx_len, hk), lambda b, s: (b, 0)),
                     pl.BlockSpec((ctx_len, hk), lambda b, s: (b, 1))]
        args += [cum_ctx, cum_ctx]
    else:
        in_specs.append(pl.BlockSpec((2, hk), lambda b, s: (0, 0)))
        args.append(lg)
    ctx_out = pl.BlockSpec((ctx_len, hv), lambda b, s: (b, 0))
    o_f, o_b, oc_f, oc_b = pl.pallas_call(
        functools.partial(_scan_kernel, gla=gla),
        grid=(batch, nb),
        in_specs=in_specs,
        out_specs=[
            pl.BlockSpec((tb, hv), lambda b, s: (fwd(b, s), 0)),
            pl.BlockSpec((tb, hv), lambda b, s: (bwd(b, s), 0)),
            ctx_out, ctx_out,
        ],
        out_shape=[
            jax.ShapeDtypeStruct((batch * seq, hv), BF16),
            jax.ShapeDtypeStruct((batch * seq, hv), BF16),
            jax.ShapeDtypeStruct((batch * ctx_len, hv), BF16),
            jax.ShapeDtypeStruct((batch * ctx_len, hv), BF16),
        ],
        scratch_shapes=[pltpu.VMEM((SCAN_DV, hk), F32), pltpu.VMEM((SCAN_DV, hk), F32)],
        compiler_params=_params("parallel", "arbitrary"),
        name="gla_scan" if gla else "ret_scan",
    )(*args)
    return (o_f, o_b), (oc_f, oc_b)


def _head_rms(o, gain):
    parts = []
    for h in range(SCAN_HEADS):
        sl = o[:, h * SCAN_DV:(h + 1) * SCAN_DV]
        ms = jnp.mean(sl * sl, axis=-1, keepdims=True)
        parts.append(sl * lax.rsqrt(ms + NORM_EPS))
    return jnp.concatenate(parts, axis=1) * gain


def _merge_kernel(x_ref, nw_ref, sh_ref, sc_ref, gate_ref, na_ref, gf_ref, gb_ref, rf_ref, rb_ref,
                  gg_ref, rg_ref, wz0_ref, wz1_ref, wz2_ref, wg0_ref, wg1_ref, wg2_ref, wb_ref, wo_ref,
                  o_ref):
    x = x_ref[...]
    wz_refs = (wz0_ref, wz1_ref, wz2_ref)
    wg_refs = (wg0_ref, wg1_ref, wg2_ref)
    hb = _ada_norm(x, nw_ref[...], sc_ref[...], sh_ref[...]).astype(BF16)
    outs = (
        na_ref[...].astype(F32),
        _head_rms(gf_ref[...].astype(F32) + gb_ref[...].astype(F32), gg_ref[...]),
        _head_rms(rf_ref[...].astype(F32) + rb_ref[...].astype(F32), rg_ref[...]),
    )
    zs = [_dot(hb, wz_refs[i][...]) for i in range(N_BRANCH)]
    us = [(outs[i] * _silu(zs[i])).astype(BF16) for i in range(N_BRANCH)]
    gs = [_sigmoid(_dot(hb, wg_refs[i][...])) for i in range(N_BRANCH)]
    y = None
    for i in range(N_BRANCH):
        t = gs[i] * _dot(us[i], wb_ref[i])
        y = t if y is None else y + t
    r = _dot(y.astype(BF16), wo_ref[...])
    o_ref[...] = x + gate_ref[...] * r


def _merge(x2d, mod, prm, na_o, gla_f, gla_b, ret_f, ret_b, *, layer, tm, row_of):
    t, d = x2d.shape
    row = lambda w: pl.BlockSpec((tm, w), lambda i: (i, 0))
    sh_spec, sc_spec, gate_spec = _mod_specs(mod, layer, row_of)
    lconst3 = lambda i: (layer, 0, 0)
    return pl.pallas_call(
        _merge_kernel,
        grid=(t // tm,),
        in_specs=[
            row(d),
            pl.BlockSpec((None, 1, d), lconst3),
            sh_spec, sc_spec, gate_spec,
            row(BRANCH_W), row(BRANCH_W), row(BRANCH_W), row(BRANCH_W), row(BRANCH_W),
            pl.BlockSpec((None, 1, BRANCH_W), lconst3),
            pl.BlockSpec((None, 1, BRANCH_W), lconst3),
            *[_resident((None, d, BRANCH_W), lambda i, n=n: (layer, 0, N_SEC * SEC // BRANCH_W + n))
              for n in range(N_BRANCH)],
            *[_resident((None, d, d), lambda i, n=n: (layer, 0, (N_SEC * SEC + N_BRANCH * BRANCH_W) // d + n))
              for n in range(N_BRANCH)],
            _resident((None, N_BRANCH, BRANCH_W, d), lambda i: (layer, 0, 0, 0)),
            _resident((None, d, d), lconst3),
        ],
        out_specs=row(d),
        out_shape=jax.ShapeDtypeStruct((t, d), F32),
        compiler_params=_params("parallel"),
        name="merge",
    )(x2d, prm["nw"], mod, mod, mod, na_o, gla_f, gla_b, ret_f, ret_b, prm["gla_gain"],
      prm["ret_gain"], *([prm["wpack"]] * (2 * N_BRANCH)), prm["wb"], prm["wo"])


def _in_splits():
    na = NA_HEADS * NA_HEAD_DIM
    qk = SCAN_HEADS * SCAN_DK
    vv = SCAN_HEADS * SCAN_DV
    sizes = (na, na, na, qk, qk, vv, 2 * GLA_GATE_RANK, qk, qk, vv, N_BRANCH * BRANCH_W,
             N_BRANCH * D_MODEL)
    return [int(v) for v in np.concatenate([[0], np.cumsum(sizes)])]


def _rotary_pack_matrix():
    hk = SCAN_HEADS * SCAN_DK
    quarter = SCAN_DK // 4
    p = np.zeros((SEC, SEC), np.float32)
    for blk in range(SEC // hk):
        for h in range(SCAN_HEADS):
            for half in range(2):
                for partner in range(2):
                    for i in range(quarter):
                        old = h * SCAN_DK + half * 2 * quarter + partner * quarter + i
                        new = partner * (hk // 2) + h * 2 * quarter + half * quarter + i
                        p[blk * hk + old, blk * hk + new] = 1.0
    return jnp.asarray(p).astype(BF16)


def _pack_kernel(a_ref, b_ref, perm_ref, o_ref, *, first_shifted, perm_block, shift):
    j = pl.program_id(1)
    sec = a_ref.shape[0]

    def shifted():
        x = jnp.concatenate([a_ref[...], b_ref[...]], axis=0)
        return x[shift:shift + sec]

    @pl.when(j < first_shifted)
    def _():
        o_ref[...] = a_ref[...].T.astype(BF16)

    @pl.when(j == perm_block)
    def _():
        o_ref[...] = _dot(shifted().T.astype(BF16), perm_ref[...]).astype(BF16)

    @pl.when(jnp.logical_and(j >= first_shifted, j != perm_block))
    def _():
        o_ref[...] = shifted().T.astype(BF16)


def _pack_w_in(w_in):
    depth, d, n_in = w_in.shape
    o = _in_splits()
    g0, g1 = o[6], o[7]
    assert g0 % SEC == 0 and (n_in - (g1 - g0)) % SEC == 0
    n_blocks = (n_in - (g1 - g0)) // SEC
    tail_rows = g1 - g0
    w_t = jnp.swapaxes(w_in, 1, 2)
    return pl.pallas_call(
        functools.partial(_pack_kernel, first_shifted=g0 // SEC, perm_block=SEC_RET_QK,
                          shift=g1 - g0),
        grid=(depth, n_blocks),
        in_specs=[
            pl.BlockSpec((None, SEC, d), lambda l, j: (l, j, 0)),
            pl.BlockSpec((None, tail_rows, d), lambda l, j: (l, (j + 1) * (SEC // tail_rows), 0)),
            _resident((SEC, SEC), lambda l, j: (0, 0)),
        ],
        out_specs=pl.BlockSpec((None, d, SEC), lambda l, j: (l, 0, j)),
        out_shape=jax.ShapeDtypeStruct((depth, d, n_blocks * SEC), BF16),
        compiler_params=_params("parallel", "parallel"),
        name="pack_w_in",
    )(w_t, w_t, _rotary_pack_matrix())


def _prep_params(norm_w, w_in, q_norm, k_norm, w_gate, b_gate, gla_out_norm, ret_out_norm,
                 w_branch, w_out):
    depth, d, _ = w_in.shape
    o = _in_splits()
    g0, g1 = o[6], o[7]
    hk = SCAN_HEADS * SCAN_DK
    wg = jnp.pad(w_in[:, :, g0:g1], ((0, 0), (0, 0), (0, GATE_PAD - (g1 - g0)))).astype(BF16)
    wgate = jnp.zeros((depth, GATE_PAD, 2 * hk), F32)
    wgate = wgate.at[:, :GLA_GATE_RANK, :hk].set(w_gate[:, 0])
    wgate = wgate.at[:, GLA_GATE_RANK:2 * GLA_GATE_RANK, hk:].set(w_gate[:, 1])
    bgate = b_gate.reshape(depth, 1, 2 * hk)
    base = np.ones((N_SEC, 1, SEC), np.float32)
    base[SEC_NA_Q] = NA_HEAD_DIM ** -0.5
    base[SEC_GLA_QK, 0, :hk] = SCAN_DK ** -0.5
    base[SEC_RET_QK, 0, :hk] = SCAN_DK ** -0.5
    reps = SEC // NA_HEAD_DIM
    ones = jnp.ones((depth, N_SEC - 2, 1, SEC), F32)
    gains = jnp.concatenate([jnp.tile(q_norm, (1, reps))[:, None, None, :],
                             jnp.tile(k_norm, (1, reps))[:, None, None, :], ones], axis=1)
    return dict(
        nw=norm_w.reshape(depth, 1, d),
        wpack=_pack_w_in(w_in), wg=wg, wgate=wgate.astype(BF16), bgate=bgate,
        colscale=gains * jnp.asarray(base)[None],
        wb=w_branch.astype(BF16), wo=w_out.astype(BF16),
        gla_gain=gla_out_norm.reshape(depth, 1, BRANCH_W),
        ret_gain=ret_out_norm.reshape(depth, 1, BRANCH_W),
    )


def _tables(seq, ctx_rows):
    i = np.arange(TRI_ROWS)
    same = (i[:, None] // CHUNK) == (i[None, :] // CHUNK)
    ltri = jnp.asarray((same & (i[None, :] <= i[:, None])).astype(np.float32)).astype(BF16)
    utri = jnp.asarray((same & (i[None, :] >= i[:, None])).astype(np.float32)).astype(BF16)
    grp = np.arange(MXU_DIM) // NA_HEAD_DIM
    gmat = jnp.asarray((grp[:, None] == grp[None, :]).astype(np.float32)).astype(BF16)
    quarter = SCAN_DK // 4
    n_rows = seq // GRID_W
    inv = ROPE_BASE ** (-jnp.arange(quarter, dtype=F32) / quarter)
    ang_r = jnp.arange(n_rows, dtype=F32)[:, None] * inv[None, :]
    ang_c = jnp.arange(GRID_W, dtype=F32)[:, None] * inv[None, :]
    zr, zc = jnp.zeros_like(ang_r), jnp.zeros_like(ang_c)
    lanes = lambda a, b: jnp.concatenate([a, b] * SCAN_HEADS, axis=1)
    rot_rows = jnp.stack([lanes(jnp.cos(ang_r), zr), lanes(jnp.sin(ang_r), zr)])
    rot_cols = jnp.stack([lanes(zc, jnp.cos(ang_c)), lanes(zc, jnp.sin(ang_c))])
    lat = dict(ltri=ltri, utri=utri, gmat=gmat, rot_rows=rot_rows, rot_cols=rot_cols)
    ctx = dict(ltri=ltri, utri=utri, gmat=gmat, rot_cols=jnp.zeros_like(rot_cols),
               rot_rows=jnp.zeros((2, ctx_rows // GRID_W, LANES), F32))
    return lat, ctx


def _tile_plan(seq):
    tm = 2 * MXU_DIM
    tb_scan = min(seq, 16 * CHUNK)
    na_blocks = min(8, seq // (NA_QROWS * GRID_W))
    assert seq % tm == 0 and seq % tb_scan == 0 and (seq // (NA_QROWS * GRID_W)) % na_blocks == 0
    return 2 * tm, tm, tb_scan, na_blocks


def _log_gamma(offset):
    g = jnp.log1p(-jnp.exp2(-(offset + jnp.arange(SCAN_HEADS, dtype=F32))))
    return jnp.tile(jnp.repeat(g, SCAN_DK // 2), 2)


def kernel(x, c, ctx, c_ctx, w_mod, b_mod, norm_w, w_in, na_q_norm, na_k_norm, na_rpb, gla_w_gate,
           gla_b_gate, gla_out_norm, ret_out_norm, w_branch, w_out):
    batch, seq, d = x.shape
    ctx_len = ctx.shape[1]
    depth = w_mod.shape[0]
    rows = seq // GRID_W
    assert d == D_MODEL and x.dtype == F32 and seq % GRID_W == 0 and rows >= NA_KROWS
    assert ctx_len % TRI_ROWS == 0 and batch + 1 <= 8
    tiles = _tile_plan(seq)
    tm_proj, tm_merge, tb_scan, na_blocks_per_step = tiles

    c_rows = jnp.zeros((8, d), F32).at[:batch].set(c).at[batch].set(c_ctx)
    mod = _modulation(c_rows, w_mod, b_mod).reshape(depth, 8, 1, 3 * d)
    prm = _prep_params(norm_w, w_in, na_q_norm, na_k_norm, gla_w_gate, gla_b_gate, gla_out_norm,
                       ret_out_norm, w_branch, w_out)
    lat_tabs, ctx_tabs = _tables(seq, batch * ctx_len)
    bias = _na_bias_tables(na_rpb, rows)
    lg = jnp.stack([_log_gamma(RET_DECAY_FWD), _log_gamma(RET_DECAY_BWD)])
    ctx_row = lambda i: batch

    xl = x.reshape(batch * seq, d)
    xc = ctx.reshape(batch * ctx_len, d)
    for layer in range(depth):
        with_ctx = layer < depth - 1
        p_lat, cum_lat = _in_proj(xl, mod, prm, lat_tabs, layer=layer, tm=tm_proj,
                                  row_of=lambda i: i // (seq // tm_proj),
                                  blocks_per_batch=seq // tm_proj, rope=True)
        p_ctx, cum_ctx = _in_proj(xc, mod, prm, ctx_tabs, layer=layer, tm=batch * ctx_len,
                                  row_of=ctx_row, blocks_per_batch=1, rope=False)

        na_l = _na_attention(p_lat, p_ctx, bias, layer=layer, batch=batch, seq=seq, ctx_len=ctx_len,
                             n_sub=na_blocks_per_step)

        (gla_f, gla_b), (gla_cf, gla_cb) = _scan(p_lat, p_ctx, cum_lat, cum_ctx, None, gla=True,
                                                 batch=batch, seq=seq, ctx_len=ctx_len, tb=tb_scan)
        (ret_f, ret_b), (ret_cf, ret_cb) = _scan(p_lat, p_ctx, None, None, lg, gla=False,
                                                 batch=batch, seq=seq, ctx_len=ctx_len, tb=tb_scan)

        xl_new = _merge(xl, mod, prm, na_l, gla_f, gla_b, ret_f, ret_b, layer=layer, tm=tm_merge,
                        row_of=lambda i: i // (seq // tm_merge))
        if with_ctx:
            na_c = _ctx_attention(p_ctx, batch=batch, ctx_len=ctx_len)
            xc = _merge(xc, mod, prm, na_c, gla_cf, gla_cb, ret_cf, ret_cb, layer=layer,
                        tm=batch * ctx_len, row_of=ctx_row)
        xl = xl_new
    return xl.reshape(batch, seq, d)
```
